```python
import math
import jax
import jax.numpy as jnp
from jax import lax
import numpy as np

D_MODEL = 1024
BATCH = 8
SEQ = 4096
DEPTH = 2

MIX_W = D_MODEL // 4
N_BRANCH = 4
S5_W = MIX_W
S5_GROUP = 16
S5_GROUPS = S5_W // S5_GROUP
S5_STATE = 64
HEAD_DIM = 64
NSA_HEADS = MIX_W // HEAD_DIM
L_CMP = 32
CMP_STRIDE = 16
CMP_HIDDEN = 2 * HEAD_DIM
L_SEL = 64
TOP_N = 16
WINDOW = 512
Q_BLOCK = 128
ROPE_THETA = 10000.0
FORCED_SCORE = 1.0e4
NEG_INF = -1.0e30
LRU_W = MIX_W
LRU_BLOCKS = 8
LRU_BLOCK_W = LRU_W // LRU_BLOCKS
LRU_CONV = 4
LRU_C = 8.0
SC_W = MIX_W
SC_CONV = 3
FFN_DIM = 2816
N_EXPERTS = 8
TOP_K = 2
EXPERT_DIM = FFN_DIM // 2
RMS_EPS = 1e-6
IN_SPLITS = (S5_W, NSA_HEADS * HEAD_DIM, HEAD_DIM, HEAD_DIM, HEAD_DIM, HEAD_DIM, HEAD_DIM, HEAD_DIM, NSA_HEADS * 3, LRU_W, LRU_W, SC_W, SC_W, SC_W, N_BRANCH * D_MODEL)
IN_COLS = sum(IN_SPLITS)

kernel_name = 'hybrid_s5_nsa_rglru_shortconv_moe_adaln'


def rms_norm(x, g):
    xf = x.astype(jnp.float32)
    y = xf * lax.rsqrt(jnp.mean(xf * xf, axis=-1, keepdims=True) + RMS_EPS)
    return (y * g.astype(jnp.float32)).astype(x.dtype)


def causal_depthwise_conv(x, w):
    width, ch = w.shape
    return lax.conv_general_dilated(x, w[:, None, :].astype(x.dtype), window_strides=(1,), padding=[(width - 1, 0)], dimension_numbers=('NWC', 'WIO', 'NWC'), feature_group_count=ch)


def linear_combine(left, right):
    a_l, b_l = left
    a_r, b_r = right
    return a_l * a_r, a_r * b_l + b_r


def masked_softmax(s, mask):
    p = jax.nn.softmax(jnp.where(mask, s, NEG_INF), axis=-1)
    return jnp.where(mask, p, 0.0)


def rope_tables(seq):
    inv = ROPE_THETA ** (-jnp.arange(0, HEAD_DIM, 2, dtype=jnp.float32) / HEAD_DIM)
    ang = jnp.arange(seq, dtype=jnp.float32)[:, None] * inv[None, :]
    return jnp.cos(ang), jnp.sin(ang)


def apply_rope(x, cos, sin):
    half = HEAD_DIM // 2
    x1, x2 = x[..., :half], x[..., half:]
    return jnp.concatenate([x1 * cos - x2 * sin, x2 * cos + x1 * sin], axis=-1)


def swiglu(h, wg, wu, wd):
    return (jax.nn.silu(h @ wg) * (h @ wu)) @ wd


def moe_swiglu(h, router_w, router_b, wg, wu, wd):
    bsz, seq, d = h.shape
    tok = h.reshape(bsz * seq, d)
    logits = (tok @ router_w + router_b).astype(jnp.float32)
    top_val, top_idx = lax.top_k(logits, TOP_K)
    top_w = jax.nn.softmax(top_val, axis=-1)
    combine = jnp.sum(jax.nn.one_hot(top_idx, N_EXPERTS, dtype=jnp.float32) * top_w[..., None], axis=1)
    out = jnp.zeros((bsz * seq, d), jnp.float32)
    for e in range(N_EXPERTS):
        out = out + combine[:, e:e + 1] * swiglu(tok, wg[e], wu[e], wd[e]).astype(jnp.float32)
    return out.reshape(bsz, seq, d).astype(h.dtype)


def s5_mixer(u, lam_re, lam_im, log_step, b_re, b_im, c_re, c_im, d_skip, w_glu):
    bsz, seq, _ = u.shape
    f32 = jnp.float32
    uf = u.astype(f32).reshape(bsz, seq, S5_GROUPS, S5_GROUP)
    lam = lax.complex(lam_re.astype(f32), lam_im.astype(f32))
    step = jnp.exp(log_step.astype(f32))[:, None]
    a_bar = jnp.exp(lam * step)
    b_bar = ((a_bar - 1.0) / lam)[..., None] * lax.complex(b_re.astype(f32), b_im.astype(f32))
    bu = lax.complex(jnp.einsum('bsgc,gnc->bsgn', uf, jnp.real(b_bar)), jnp.einsum('bsgc,gnc->bsgn', uf, jnp.imag(b_bar)))
    a = jnp.broadcast_to(a_bar, bu.shape)
    _, hs = lax.associative_scan(linear_combine, (a, bu), axis=1)
    y = jnp.einsum('bsgn,gcn->bsgc', jnp.real(hs), c_re.astype(f32)) - jnp.einsum('bsgn,gcn->bsgc', jnp.imag(hs), c_im.astype(f32))
    y = (y + d_skip.astype(f32).reshape(S5_GROUPS, S5_GROUP) * uf).reshape(bsz, seq, S5_W)
    z = jax.nn.gelu(y)
    return (z * jax.nn.sigmoid(z @ w_glu.astype(f32))).astype(u.dtype)


def nsa_mixer(q, kc_tok, vc_tok, ks_tok, vs_tok, kw_tok, vw_tok, gate_logits, pe_k, pe_v, wk1, wk2, wv1, wv2, cos, sin):
    bsz, seq, _ = q.shape
    f32 = jnp.float32
    scale = HEAD_DIM ** -0.5
    qf = apply_rope(q.astype(f32).reshape(bsz, seq, NSA_HEADS, HEAD_DIM), cos[:, None, :], sin[:, None, :])
    kcf = apply_rope(kc_tok.astype(f32), cos, sin)
    ksf = apply_rope(ks_tok.astype(f32), cos, sin)
    kwf = apply_rope(kw_tok.astype(f32), cos, sin)
    vcf = vc_tok.astype(f32)
    vsf = vs_tok.astype(f32)
    vwf = vw_tok.astype(f32)
    t = np.arange(seq)

    n_cmp = (seq - L_CMP) // CMP_STRIDE + 1
    cmp_start = np.arange(n_cmp) * CMP_STRIDE
    cmp_idx = cmp_start[:, None] + np.arange(L_CMP)[None, :]

    def compress(tok, pe, w1, w2):
        blk = (tok[:, cmp_idx] + pe.astype(f32)).reshape(bsz, n_cmp, L_CMP * HEAD_DIM)
        return jax.nn.gelu(blk @ w1.astype(f32)) @ w2.astype(f32)

    k_cmp = compress(kcf, pe_k, wk1, wk2)
    v_cmp = compress(vcf, pe_v, wv1, wv2)
    cmp_mask = (cmp_start + L_CMP - 1)[None, :] <= t[:, None]
    p_cmp = masked_softmax(jnp.einsum('bshd,bnd->bhsn', qf, k_cmp) * scale, cmp_mask)
    o_cmp = jnp.einsum('bhsn,bnd->bshd', p_cmp, v_cmp)

    n_sel = seq // L_SEL
    sel_start = np.arange(n_sel) * L_SEL
    overlap = np.clip(np.minimum(cmp_start[:, None] + L_CMP, sel_start[None, :] + L_SEL) - np.maximum(cmp_start[:, None], sel_start[None, :]), 0, None) / L_CMP
    imp = jnp.einsum('bhsn,nj->bsj', p_cmp, jnp.asarray(overlap, f32))
    cur = t // L_SEL
    blk_id = np.arange(n_sel)[None, :]
    forced = (blk_id == 0) | (blk_id == cur[:, None]) | (blk_id == cur[:, None] - 1)
    future = sel_start[None, :] > t[:, None]
    imp = jnp.where(future, -1.0, jnp.where(forced, FORCED_SCORE, imp))
    n_top = min(TOP_N, n_sel)
    _, sel_idx = lax.top_k(imp, n_top)
    n_key = n_top * L_SEL
    kpos = (sel_idx[..., None] * L_SEL + jnp.arange(L_SEL, dtype=jnp.int32)).reshape(bsz, seq, n_key)
    n_qb = seq // Q_BLOCK
    q_blocks = qf.reshape(bsz, n_qb, Q_BLOCK, NSA_HEADS, HEAD_DIM)
    gather_rows = jax.vmap(lambda tok, ii: tok[ii])

    def sel_block(args):
        qq, kp, tt = args
        kg = gather_rows(ksf, kp)
        vg = gather_rows(vsf, kp)
        s = jnp.einsum('bqhd,bqkd->bhqk', qq, kg) * scale
        p = masked_softmax(s, (kp <= tt[None, :, None])[:, None])
        return jnp.einsum('bhqk,bqkd->bqhd', p, vg)

    o_sel = lax.map(sel_block, (q_blocks.transpose(1, 0, 2, 3, 4), kpos.reshape(bsz, n_qb, Q_BLOCK, n_key).transpose(1, 0, 2, 3), jnp.asarray(t.reshape(n_qb, Q_BLOCK), dtype=jnp.int32)))
    o_sel = o_sel.transpose(1, 0, 2, 3, 4).reshape(bsz, seq, NSA_HEADS, HEAD_DIM)

    n_wb = WINDOW // Q_BLOCK + 1

    def band(tok):
        padded = jnp.pad(tok, ((0, 0), (WINDOW, 0), (0, 0))).reshape(bsz, n_qb + n_wb - 1, Q_BLOCK, HEAD_DIM)
        return jnp.concatenate([padded[:, j:j + n_qb] for j in range(n_wb)], axis=2)

    k_band = band(kwf)
    v_band = band(vwf)
    key_pos = (np.arange(n_qb) * Q_BLOCK - WINDOW)[:, None] + np.arange(n_wb * Q_BLOCK)[None, :]
    dist = t.reshape(n_qb, Q_BLOCK)[:, :, None] - key_pos[:, None, :]
    win_mask = (dist >= 0) & (dist < WINDOW) & (key_pos[:, None, :] >= 0)
    p_win = masked_softmax(jnp.einsum('bnqhd,bnkd->bhnqk', q_blocks, k_band) * scale, win_mask)
    o_win = jnp.einsum('bhnqk,bnkd->bnqhd', p_win, v_band).reshape(bsz, seq, NSA_HEADS, HEAD_DIM)

    g = jax.nn.sigmoid(gate_logits.astype(f32)).reshape(bsz, seq, NSA_HEADS, 3)
    o = g[..., 0:1] * o_cmp + g[..., 1:2] * o_sel + g[..., 2:3] * o_win
    return o.reshape(bsz, seq, NSA_HEADS * HEAD_DIM).astype(q.dtype)


def rglru_mixer(x_in, gate_in, conv_w, conv_b, w_a, b_a, w_x, b_x, lam):
    bsz, seq, _ = x_in.shape
    f32 = jnp.float32
    xc = (causal_depthwise_conv(x_in, conv_w) + conv_b).astype(f32)
    xb = xc.reshape(bsz, seq, LRU_BLOCKS, LRU_BLOCK_W)
    r = jax.nn.sigmoid(jnp.einsum('bsnc,ncd->bsnd', xb, w_a.astype(f32)).reshape(bsz, seq, LRU_W) + b_a.astype(f32))
    i = jax.nn.sigmoid(jnp.einsum('bsnc,ncd->bsnd', xb, w_x.astype(f32)).reshape(bsz, seq, LRU_W) + b_x.astype(f32))
    log_a = -LRU_C * r * jax.nn.softplus(-lam.astype(f32))
    a = jnp.exp(log_a)
    b = jnp.sqrt(-jnp.expm1(2.0 * log_a)) * (i * xc)
    _, hs = lax.associative_scan(linear_combine, (a, b), axis=1)
    return (hs * jax.nn.gelu(gate_in.astype(f32))).astype(x_in.dtype)


def hybrid_mixer(h, w_in, s5_lambda_re, s5_lambda_im, s5_log_step, s5_b_re, s5_b_im, s5_c_re, s5_c_im, s5_d, s5_w_glu, nsa_pe_k, nsa_pe_v, nsa_cmp_k_w1, nsa_cmp_k_w2, nsa_cmp_v_w1, nsa_cmp_v_w2, lru_conv_w, lru_conv_b, lru_w_a, lru_b_a, lru_w_x, lru_b_x, lru_lambda, sc_conv_w, w_branch, w_out, cos, sin):
    z = h @ w_in
    points = np.cumsum(np.array(IN_SPLITS))[:-1].tolist()
    (u_s5, q, kc, vc, ks, vs, kw, vw, g_nsa, x_lru, g_lru, b_sc, c_sc, x_sc, g_merge) = jnp.split(z, points, axis=-1)
    y_a = s5_mixer(u_s5, s5_lambda_re, s5_lambda_im, s5_log_step, s5_b_re, s5_b_im, s5_c_re, s5_c_im, s5_d, s5_w_glu)
    y_b = nsa_mixer(q, kc, vc, ks, vs, kw, vw, g_nsa, nsa_pe_k, nsa_pe_v, nsa_cmp_k_w1, nsa_cmp_k_w2, nsa_cmp_v_w1, nsa_cmp_v_w2, cos, sin)
    y_c = rglru_mixer(x_lru, g_lru, lru_conv_w, lru_conv_b, lru_w_a, lru_b_a, lru_w_x, lru_b_x, lru_lambda)
    y_d = b_sc * causal_depthwise_conv(c_sc * x_sc, sc_conv_w)
    merged = None
    for m, y in enumerate((y_a, y_b, y_c, y_d)):
        term = jax.nn.sigmoid(g_merge[..., m * D_MODEL:(m + 1) * D_MODEL]) * (y @ w_branch[m])
        merged = term if merged is None else merged + term
    return merged @ w_out


def setup_inputs(seed: int = 0) -> dict:
    key = jax.random.key(seed)
    ks = iter(jax.random.split(key, 48))
    f32 = jnp.float32

    def nrm(shape, scale):
        return jax.random.normal(next(ks), shape, f32) * scale

    L = DEPTH
    D = D_MODEL
    n_dense = (DEPTH + 1) // 2
    n_moe = DEPTH // 2
    x = nrm((BATCH, SEQ, D), 1.0)
    c = nrm((BATCH, D), 1.0)
    mod_w = nrm((L, D, 6 * D), 0.5 * D ** -0.5)
    mod_b = nrm((L, 6 * D), 0.02)
    norm_mix_g = 1.0 + nrm((L, D), 0.05)
    norm_ffn_g = 1.0 + nrm((L, D), 0.05)
    w_in = nrm((L, D, IN_COLS), D ** -0.5)
    s5_lambda_re = -0.5 + nrm((L, S5_GROUPS, S5_STATE), 0.01)
    s5_lambda_im = math.pi * jnp.arange(S5_STATE, dtype=f32) + nrm((L, S5_GROUPS, S5_STATE), 0.01)
    s5_log_step = jax.random.uniform(next(ks), (L, S5_GROUPS), f32, math.log(1e-3), math.log(1e-1))
    s5_b_re = nrm((L, S5_GROUPS, S5_STATE, S5_GROUP), (2 * S5_GROUP) ** -0.5)
    s5_b_im = nrm((L, S5_GROUPS, S5_STATE, S5_GROUP), (2 * S5_GROUP) ** -0.5)
    s5_c_re = nrm((L, S5_GROUPS, S5_GROUP, S5_STATE), S5_STATE ** -0.5)
    s5_c_im = nrm((L, S5_GROUPS, S5_GROUP, S5_STATE), S5_STATE ** -0.5)
    s5_d = nrm((L, S5_W), 1.0)
    s5_w_glu = nrm((L, S5_W, S5_W), S5_W ** -0.5)
    nsa_pe_k = nrm((L, L_CMP, HEAD_DIM), 0.02)
    nsa_pe_v = nrm((L, L_CMP, HEAD_DIM), 0.02)
    nsa_cmp_k_w1 = nrm((L, L_CMP * HEAD_DIM, CMP_HIDDEN), (L_CMP * HEAD_DIM) ** -0.5)
    nsa_cmp_k_w2 = nrm((L, CMP_HIDDEN, HEAD_DIM), CMP_HIDDEN ** -0.5)
    nsa_cmp_v_w1 = nrm((L, L_CMP * HEAD_DIM, CMP_HIDDEN), (L_CMP * HEAD_DIM) ** -0.5)
    nsa_cmp_v_w2 = nrm((L, CMP_HIDDEN, HEAD_DIM), CMP_HIDDEN ** -0.5)
    lru_conv_w = nrm((L, LRU_CONV, LRU_W), LRU_CONV ** -0.5)
    lru_conv_b = nrm((L, LRU_W), 0.02)
    lru_w_a = nrm((L, LRU_BLOCKS, LRU_BLOCK_W, LRU_BLOCK_W), LRU_BLOCK_W ** -0.5)
    lru_b_a = nrm((L, LRU_W), 0.02)
    lru_w_x = nrm((L, LRU_BLOCKS, LRU_BLOCK_W, LRU_BLOCK_W), LRU_BLOCK_W ** -0.5)
    lru_b_x = nrm((L, LRU_W), 0.02)
    a_pow = jax.random.uniform(next(ks), (L, LRU_W), f32, 0.9, 0.999)
    a_base = a_pow ** (1.0 / LRU_C)
    lru_lambda = jnp.log(a_base) - jnp.log1p(-a_base)
    sc_conv_w = nrm((L, SC_CONV, SC_W), SC_CONV ** -0.5)
    w_branch = nrm((L, N_BRANCH, MIX_W, D), MIX_W ** -0.5)
    w_out = nrm((L, D, D), D ** -0.5)
    ffn_w_gate = nrm((n_dense, D, FFN_DIM), D ** -0.5)
    ffn_w_up = nrm((n_dense, D, FFN_DIM), D ** -0.5)
    ffn_w_down = nrm((n_dense, FFN_DIM, D), FFN_DIM ** -0.5)
    moe_router_w = nrm((n_moe, D, N_EXPERTS), D ** -0.5)
    moe_router_b = nrm((n_moe, N_EXPERTS), 0.01)
    moe_w_gate = nrm((n_moe, N_EXPERTS, D, EXPERT_DIM), D ** -0.5)
    moe_w_up = nrm((n_moe, N_EXPERTS, D, EXPERT_DIM), D ** -0.5)
    moe_w_down = nrm((n_moe, N_EXPERTS, EXPERT_DIM, D), EXPERT_DIM ** -0.5)
    final_norm_g = 1.0 + nrm((D,), 0.05)
    return {'x': x, 'c': c, 'mod_w': mod_w, 'mod_b': mod_b, 'norm_mix_g': norm_mix_g, 'norm_ffn_g': norm_ffn_g, 'w_in': w_in, 's5_lambda_re': s5_lambda_re, 's5_lambda_im': s5_lambda_im, 's5_log_step': s5_log_step, 's5_b_re': s5_b_re, 's5_b_im': s5_b_im, 's5_c_re': s5_c_re, 's5_c_im': s5_c_im, 's5_d': s5_d, 's5_w_glu': s5_w_glu, 'nsa_pe_k': nsa_pe_k, 'nsa_pe_v': nsa_pe_v, 'nsa_cmp_k_w1': nsa_cmp_k_w1, 'nsa_cmp_k_w2': nsa_cmp_k_w2, 'nsa_cmp_v_w1': nsa_cmp_v_w1, 'nsa_cmp_v_w2': nsa_cmp_v_w2, 'lru_conv_w': lru_conv_w, 'lru_conv_b': lru_conv_b, 'lru_w_a': lru_w_a, 'lru_b_a': lru_b_a, 'lru_w_x': lru_w_x, 'lru_b_x': lru_b_x, 'lru_lambda': lru_lambda, 'sc_conv_w': sc_conv_w, 'w_branch': w_branch, 'w_out': w_out, 'ffn_w_gate': ffn_w_gate, 'ffn_w_up': ffn_w_up, 'ffn_w_down': ffn_w_down, 'moe_router_w': moe_router_w, 'moe_router_b': moe_router_b, 'moe_w_gate': moe_w_gate, 'moe_w_up': moe_w_up, 'moe_w_down': moe_w_down, 'final_norm_g': final_norm_g}


def reference(x, c, mod_w, mod_b, norm_mix_g, norm_ffn_g, w_in, s5_lambda_re, s5_lambda_im, s5_log_step, s5_b_re, s5_b_im, s5_c_re, s5_c_im, s5_d, s5_w_glu, nsa_pe_k, nsa_pe_v, nsa_cmp_k_w1, nsa_cmp_k_w2, nsa_cmp_v_w1, nsa_cmp_v_w2, lru_conv_w, lru_conv_b, lru_w_a, lru_b_a, lru_w_x, lru_b_x, lru_lambda, sc_conv_w, w_branch, w_out, ffn_w_gate, ffn_w_up, ffn_w_down, moe_router_w, moe_router_b, moe_w_gate, moe_w_up, moe_w_down, final_norm_g):
    cos, sin = rope_tables(x.shape[1])
    c_act = jax.nn.silu(c)
    for i in range(DEPTH):
        mod = (c_act @ mod_w[i] + mod_b[i])[:, None, :]
        shift1, scale1, gate1, shift2, scale2, gate2 = jnp.split(mod, 6, axis=-1)
        h = rms_norm(x, norm_mix_g[i]) * (1.0 + scale1) + shift1
        mix = hybrid_mixer(h, w_in[i], s5_lambda_re[i], s5_lambda_im[i], s5_log_step[i], s5_b_re[i], s5_b_im[i], s5_c_re[i], s5_c_im[i], s5_d[i], s5_w_glu[i], nsa_pe_k[i], nsa_pe_v[i], nsa_cmp_k_w1[i], nsa_cmp_k_w2[i], nsa_cmp_v_w1[i], nsa_cmp_v_w2[i], lru_conv_w[i], lru_conv_b[i], lru_w_a[i], lru_b_a[i], lru_w_x[i], lru_b_x[i], lru_lambda[i], sc_conv_w[i], w_branch[i], w_out[i], cos, sin)
        x = (x + gate1 * mix).astype(x.dtype)
        h = rms_norm(x, norm_ffn_g[i]) * (1.0 + scale2) + shift2
        j = i // 2
        if i % 2 == 0:
            f = swiglu(h, ffn_w_gate[j], ffn_w_up[j], ffn_w_down[j])
        else:
            f = moe_swiglu(h, moe_router_w[j], moe_router_b[j], moe_w_gate[j], moe_w_up[j], moe_w_down[j])
        x = (x + gate2 * f).astype(x.dtype)
    return rms_norm(x, final_norm_g)
```

```python
import functools
import math

import jax
import jax.numpy as jnp
import numpy as np
from jax import lax
from jax.experimental import pallas as pl
from jax.experimental.pallas import tpu as pltpu

F32 = jnp.float32
BF16 = jnp.bfloat16

N_BRANCH = 4
S5_GROUP = 16
S5_STATE = 64
HEAD_DIM = 64
L_CMP = 32
CMP_STRIDE = 16
L_SEL = 64
TOP_N = 16
WINDOW = 512
ROPE_THETA = 10000.0
FORCED_SCORE = 1.0e4
NEG_INF = -1.0e30
LRU_CONV = 4
LRU_C = 8.0
SC_CONV = 3
N_EXPERTS = 8
RMS_EPS = 1e-6

LANES = 128
SUBLANES = 8
VMEM_LIMIT = 52 * 1024 * 1024

HI = lax.Precision.HIGHEST


def _cparams(sem):
    return pltpu.CompilerParams(dimension_semantics=sem, vmem_limit_bytes=VMEM_LIMIT)


def _dot(a, b, precision=None):
    return jnp.dot(a, b, preferred_element_type=F32, precision=precision)


def _dot_t(a, b, precision=None):
    return lax.dot_general(a, b, (((1,), (1,)), ((), ())), preferred_element_type=F32, precision=precision)


def _gelu(x):
    return 0.5 * x * (1.0 + jnp.tanh(math.sqrt(2.0 / math.pi) * (x + 0.044715 * (x * x * x))))


def _sigmoid(x):
    return 1.0 / (1.0 + jnp.exp(-x))


def _silu(x):
    return x * _sigmoid(x)


def _norm_mod(x, g, scale, shift):
    var = jnp.mean(x * x, axis=-1, keepdims=True)
    return (x * lax.rsqrt(var + RMS_EPS) * g) * (1.0 + scale) + shift


def _rot_half(z):
    n = z.shape[-1]
    lane = lax.broadcasted_iota(jnp.int32, z.shape, z.ndim - 1)
    first = (lane % HEAD_DIM) < (HEAD_DIM // 2)
    return jnp.where(first, pltpu.roll(z, n - HEAD_DIM // 2, z.ndim - 1), pltpu.roll(z, HEAD_DIM // 2, z.ndim - 1))


def _mod_kernel(c_ref, w_ref, b_ref, o_ref):
    c = c_ref[...]
    o_ref[0] = _dot(_silu(c), w_ref[0], HI) + b_ref[0]


def _modulation(c, mod_w, mod_b):
    depth, d, n = mod_w.shape
    bsz = c.shape[0]
    tn = 1024
    return pl.pallas_call(
        _mod_kernel,
        out_shape=jax.ShapeDtypeStruct((depth, bsz, n), F32),
        grid=(depth, n // tn),
        in_specs=[
            pl.BlockSpec((bsz, d), lambda l, j: (0, 0)),
            pl.BlockSpec((1, d, tn), lambda l, j: (l, 0, j)),
            pl.BlockSpec((1, 1, tn), lambda l, j: (l, 0, j)),
        ],
        out_specs=pl.BlockSpec((1, bsz, tn), lambda l, j: (l, 0, j)),
        compiler_params=_cparams(("arbitrary", "arbitrary")),
        name="modulation",
    )(c, mod_w, mod_b.reshape(depth, 1, n))


def _proj_kernel(x_ref, g_ref, sc_ref, sh_ref, w_ref, *rest, rope, nchunk):
    if rope:
        cos_ref, sin_ref, o_ref = rest
    else:
        (o_ref,) = rest
    h = _norm_mod(x_ref[0], g_ref[...], sc_ref[0], sh_ref[0]).astype(BF16)
    n = w_ref.shape[1]
    for n0 in range(0, n, nchunk):
        z = _dot(h, w_ref[:, n0:n0 + nchunk])
        if rope:
            z = z * cos_ref[:, n0:n0 + nchunk] + _rot_half(z) * sin_ref[:, n0:n0 + nchunk]
        if len(o_ref.shape) == 3:
            o_ref[0, :, n0:n0 + nchunk] = z.astype(o_ref.dtype)
        else:
            o_ref[:, n0:n0 + nchunk] = z.astype(o_ref.dtype)


def _project(x, g, scale, shift, w, *, time_major, out_dtype, tm, rope_tabs=None):
    bsz, seq, d = x.shape
    n = w.shape[1]
    rope = rope_tabs is not None
    in_specs = [
        pl.BlockSpec((1, tm, d), lambda b, i: (b, i, 0)),
        pl.BlockSpec((1, d), lambda b, i: (0, 0)),
        pl.BlockSpec((1, 1, d), lambda b, i: (b, 0, 0)),
        pl.BlockSpec((1, 1, d), lambda b, i: (b, 0, 0)),
        pl.BlockSpec((d, n), lambda b, i: (0, 0)),
    ]
    args = [x, g.reshape(1, d), scale, shift, w]
    if rope:
        in_specs += [pl.BlockSpec((tm, n), lambda b, i: (i, 0))] * 2
        args += list(rope_tabs)
    if time_major:
        out_shape = jax.ShapeDtypeStruct((seq, bsz * n), out_dtype)
        out_spec = pl.BlockSpec((tm, n), lambda b, i: (i, b))
    else:
        out_shape = jax.ShapeDtypeStruct((bsz, seq, n), out_dtype)
        out_spec = pl.BlockSpec((1, tm, n), lambda b, i: (b, i, 0))
    nchunk = 256 if rope else 512
    return pl.pallas_call(
        functools.partial(_proj_kernel, rope=rope, nchunk=nchunk),
        out_shape=out_shape,
        grid=(bsz, seq // tm),
        in_specs=in_specs,
        out_specs=out_spec,
        compiler_params=_cparams(("arbitrary", "arbitrary")),
        name="proj_rope" if rope else ("proj_tm" if time_major else "proj_bm"),
    )(*args)


def _s5_param_kernel(lre_ref, lim_ref, ls_ref, bre_ref, bim_ref, are_ref, aim_ref, bbre_ref, bbim_ref):
    lre = lre_ref[...]
    lim = lim_ref[...]
    step = jnp.exp(ls_ref[...])
    mag = jnp.exp(lre * step)
    ar = mag * jnp.cos(lim * step)
    ai = mag * jnp.sin(lim * step)
    xr = ar - 1.0
    den = lre * lre + lim * lim
    cr = (xr * lre + ai * lim) / den
    ci = (ai * lre - xr * lim) / den
    bre = bre_ref[...]
    bim = bim_ref[...]
    are_ref[...] = ar
    aim_ref[...] = ai
    bbre_ref[...] = cr * bre - ci * bim
    bbim_ref[...] = cr * bim + ci * bre


def _s5_params(lam_re, lam_im, log_step, b_re, b_im):
    g, n = lam_re.shape
    c = b_re.shape[-1]
    rows = g * n
    col = lambda a: a.reshape(rows, 1)
    ls = jnp.broadcast_to(log_step[:, None], (g, n))
    out = pl.pallas_call(
        _s5_param_kernel,
        out_shape=(
            jax.ShapeDtypeStruct((rows, 1), F32),
            jax.ShapeDtypeStruct((rows, 1), F32),
            jax.ShapeDtypeStruct((rows, c), F32),
            jax.ShapeDtypeStruct((rows, c), F32),
        ),
        name="s5_params",
    )(col(lam_re), col(lam_im), col(ls), b_re.reshape(rows, c), b_im.reshape(rows, c))
    return out


def _scan_kernel(z_ref, s5b_ref, s5c_ref, are_ref, aim_ref, d_ref, wglu_ref,
                 cw_ref, cb_ref, wa_ref, ba_ref, wx_ref, bx_ref, lam_ref, scw_ref,
                 o_ref,
                 bu_ref, hre_ref, him_ref, xe_ref, hl_ref, pe_ref, ab_ref, bb_ref, *, nb, steps):
    c = pl.program_id(0)
    rows = nb * steps
    w = d_ref.shape[1]
    ns = are_ref.shape[1]

    @pl.when(c == 0)
    def _():
        hre_ref[...] = jnp.zeros_like(hre_ref)
        him_ref[...] = jnp.zeros_like(him_ref)
        hl_ref[...] = jnp.zeros_like(hl_ref)
        xe_ref[0:(LRU_CONV - 1) * nb, :] = jnp.zeros(((LRU_CONV - 1) * nb, w), F32)
        pe_ref[0:(SC_CONV - 1) * nb, :] = jnp.zeros(((SC_CONV - 1) * nb, w), F32)

    u = z_ref[:, 0:w]
    bu_ref[...] = _dot(u.astype(BF16), s5b_ref[...])
    a_re = jnp.broadcast_to(are_ref[...], (nb, ns))
    a_im = jnp.broadcast_to(aim_ref[...], (nb, ns))

    def s5_step(t, carry):
        h_re, h_im = carry
        r0 = pl.multiple_of(t * nb, nb)
        b_re = bu_ref[pl.ds(r0, nb), 0:ns]
        b_im = bu_ref[pl.ds(r0, nb), ns:2 * ns]
        n_re = a_re * h_re - a_im * h_im + b_re
        n_im = a_re * h_im + a_im * h_re + b_im
        bu_ref[pl.ds(r0, nb), 0:ns] = n_re
        bu_ref[pl.ds(r0, nb), ns:2 * ns] = n_im
        return n_re, n_im

    h_re, h_im = lax.fori_loop(0, steps, s5_step, (hre_ref[...], him_ref[...]))
    hre_ref[...] = h_re
    him_ref[...] = h_im
    y = _dot(bu_ref[...].astype(BF16), s5c_ref[...]) + d_ref[...] * u
    zg = _gelu(y)
    o_ref[:, 0:w] = zg * _sigmoid(_dot(zg.astype(BF16), wglu_ref[...]))

    nt = (LRU_CONV - 1) * nb
    x = z_ref[:, w:2 * w]
    xe_ref[nt:nt + rows, :] = x
    xc = cb_ref[...] + cw_ref[LRU_CONV - 1:LRU_CONV, :] * x
    for k in range(LRU_CONV - 1):
        xc = xc + cw_ref[k:k + 1, :] * xe_ref[k * nb:k * nb + rows, :]
    xe_ref[0:nt, :] = xe_ref[rows:rows + nt, :]
    xcb = xc.astype(BF16)
    r = _sigmoid(_dot(xcb, wa_ref[...]) + ba_ref[...])
    gi = _sigmoid(_dot(xcb, wx_ref[...]) + bx_ref[...])
    nl = -lam_ref[...]
    softplus = jnp.maximum(nl, 0.0) + jnp.log(1.0 + jnp.exp(-jnp.abs(nl)))
    log_a = (-LRU_C) * r * softplus
    ab_ref[...] = jnp.exp(log_a)
    bb_ref[...] = jnp.sqrt(1.0 - jnp.exp(2.0 * log_a)) * (gi * xc)

    def lru_step(t, h):
        r0 = pl.multiple_of(t * nb, nb)
        h = ab_ref[pl.ds(r0, nb), :] * h + bb_ref[pl.ds(r0, nb), :]
        bb_ref[pl.ds(r0, nb), :] = h
        return h

    hl_ref[...] = lax.fori_loop(0, steps, lru_step, hl_ref[...])
    o_ref[:, w:2 * w] = bb_ref[...] * _gelu(z_ref[:, 2 * w:3 * w])

    npt = (SC_CONV - 1) * nb
    p = z_ref[:, 4 * w:5 * w] * z_ref[:, 5 * w:6 * w]
    pe_ref[npt:npt + rows, :] = p
    cv = scw_ref[SC_CONV - 1:SC_CONV, :] * p
    for k in range(SC_CONV - 1):
        cv = cv + scw_ref[k:k + 1, :] * pe_ref[k * nb:k * nb + rows, :]
    pe_ref[0:npt, :] = pe_ref[rows:rows + npt, :]
    o_ref[:, 2 * w:3 * w] = z_ref[:, 3 * w:4 * w] * cv


def _scan_mixers(z_tm, nb, s5b, s5c, a_re, a_im, d_skip, w_glu, conv_w, conv_b, wa, ba, wx, bx, lam, sc_w, *, steps):
    total, zc = z_tm.shape
    rows = nb * steps
    w = zc // 6
    ns = a_re.shape[1]
    full = lambda a: pl.BlockSpec(a.shape, lambda c: (0,) * a.ndim)
    params = [s5b, s5c, a_re, a_im, d_skip, w_glu, conv_w, conv_b, wa, ba, wx, bx, lam, sc_w]
    return pl.pallas_call(
        functools.partial(_scan_kernel, nb=nb, steps=steps),
        out_shape=jax.ShapeDtypeStruct((total, 3 * w), F32),
        grid=(total // rows,),
        in_specs=[pl.BlockSpec((rows, zc), lambda c: (c, 0))] + [full(a) for a in params],
        out_specs=pl.BlockSpec((rows, 3 * w), lambda c: (c, 0)),
        scratch_shapes=[
            pltpu.VMEM((rows, 2 * ns), F32),
            pltpu.VMEM((nb, ns), F32),
            pltpu.VMEM((nb, ns), F32),
            pltpu.VMEM((rows + (LRU_CONV - 1) * nb, w), F32),
            pltpu.VMEM((nb, w), F32),
            pltpu.VMEM((rows + (SC_CONV - 1) * nb, w), F32),
            pltpu.VMEM((rows, w), F32),
            pltpu.VMEM((rows, w), F32),
        ],
        compiler_params=_cparams(("arbitrary",)),
        name="scan_mixers",
    )(z_tm, *params)


def _compress_kernel(gk_ref, gv_ref, pek_ref, pev_ref, wk1_ref, wk2_ref, wv1_ref, wv2_ref, ko_ref, vo_ref):
    half = gk_ref.shape[2]
    nblk = gk_ref.shape[1]

    def phi(g, pe, w1_ref, w2_ref):
        top = _dot(g, w1_ref[0:half, :], HI)
        bot = _dot(g, w1_ref[half:2 * half, :], HI)
        hid = top + pltpu.roll(bot, nblk - 1, 0) + _dot(pe, w1_ref[...], HI)
        return _dot(_gelu(hid), w2_ref[...], HI)

    ko_ref[0] = phi(gk_ref[0], pek_ref[...], wk1_ref, wk2_ref)
    vo_ref[0] = phi(gv_ref[0], pev_ref[...], wv1_ref, wv2_ref)


def _compress(gk, gv, pe_k, pe_v, wk1, wk2, wv1, wv2):
    bsz, nblk, half = gk.shape
    hd = wk2.shape[1]
    full = lambda a: pl.BlockSpec(a.shape, lambda b: (0,) * a.ndim)
    pek = pe_k.reshape(1, -1)
    pev = pe_v.reshape(1, -1)
    params = [pek, pev, wk1, wk2, wv1, wv2]
    blk = pl.BlockSpec((1, nblk, half), lambda b: (b, 0, 0))
    oblk = pl.BlockSpec((1, nblk, hd), lambda b: (b, 0, 0))
    return pl.pallas_call(
        _compress_kernel,
        out_shape=(jax.ShapeDtypeStruct((bsz, nblk, hd), F32),) * 2,
        grid=(bsz,),
        in_specs=[blk, blk] + [full(a) for a in params],
        out_specs=(oblk, oblk),
        compiler_params=_cparams(("arbitrary",)),
        name="nsa_compress",
    )(gk, gv, *params)


def _cmp_kernel(q_ref, kc_ref, vc_ref, ov_ref, ocmp_ref, qa_ref, *, heads, tq):
    i = pl.program_id(1)
    q = q_ref[0]
    kc = kc_ref[0]
    vc = vc_ref[0]
    nblk = kc.shape[0]
    scale = HEAD_DIM ** -0.5
    t = i * tq + lax.broadcasted_iota(jnp.int32, (tq, nblk), 0)
    n_id = lax.broadcasted_iota(jnp.int32, (tq, nblk), 1)
    visible = (n_id * CMP_STRIDE + (L_CMP - 1)) <= t
    psum = jnp.zeros((tq, nblk), F32)
    outs = []
    for h in range(heads):
        qh = q[:, h * HEAD_DIM:(h + 1) * HEAD_DIM]
        s = jnp.where(visible, _dot_t(qh, kc, HI) * scale, NEG_INF)
        m = jnp.max(s, axis=-1, keepdims=True)
        e = jnp.where(visible, jnp.exp(s - m), 0.0)
        den = jnp.sum(e, axis=-1, keepdims=True)
        p = e / jnp.where(den > 0.0, den, 1.0)
        psum = psum + p
        outs.append(_dot(p, vc, HI))
    ocmp_ref[0] = jnp.concatenate(outs, axis=1)

    imp = _dot(psum, ov_ref[...], HI)
    nsel = ov_ref.shape[1]
    tt = i * tq + lax.broadcasted_iota(jnp.int32, (tq, nsel), 0)
    j = lax.broadcasted_iota(jnp.int32, (tq, nsel), 1)
    cur = tt // L_SEL
    forced = (j == 0) | (j == cur) | (j == cur - 1)
    future = (j * L_SEL) > tt
    imp = jnp.where(future, -1.0, jnp.where(forced, FORCED_SCORE, imp))
    imp_t = imp.T
    nreal = qa_ref.shape[3] - HEAD_DIM
    cand = imp_t[0:nreal, :]
    jrow = lax.broadcasted_iota(jnp.int32, (nreal, tq), 0)
    rank = jnp.zeros((nreal, tq), jnp.int32)
    for k in range(nreal):
        rk = imp_t[k:k + 1, :]
        beats = (rk > cand) | ((rk == cand) & (jrow > k))
        rank = rank + beats.astype(jnp.int32)
    bias_t = jnp.where(rank < TOP_N, 0.0, NEG_INF)
    pad = jnp.full((nsel - nreal, tq), NEG_INF, F32)
    bias = jnp.concatenate([bias_t, pad], axis=0).T[:, 0:nreal]
    for h in range(heads):
        qh = q[:, h * HEAD_DIM:(h + 1) * HEAD_DIM] * scale
        qa_ref[0, h] = jnp.concatenate([qh, bias], axis=1).astype(BF16)


def _cmp_select(q, k_cmp, v_cmp, overlap, *, heads, tq, nsel):
    bsz, seq, _ = q.shape
    qd = heads * HEAD_DIM
    nblk = k_cmp.shape[1]
    return pl.pallas_call(
        functools.partial(_cmp_kernel, heads=heads, tq=tq),
        out_shape=(
            jax.ShapeDtypeStruct((bsz, seq, qd), F32),
            jax.ShapeDtypeStruct((bsz, heads, seq, HEAD_DIM + nsel), BF16),
        ),
        grid=(bsz, seq // tq),
        in_specs=[
            pl.BlockSpec((1, tq, qd), lambda b, i: (b, i, 0)),
            pl.BlockSpec((1, nblk, HEAD_DIM), lambda b, i: (b, 0, 0)),
            pl.BlockSpec((1, nblk, HEAD_DIM), lambda b, i: (b, 0, 0)),
            pl.BlockSpec(overlap.shape, lambda b, i: (0, 0)),
        ],
        out_specs=(
            pl.BlockSpec((1, tq, qd), lambda b, i: (b, i, 0)),
            pl.BlockSpec((1, heads, tq, HEAD_DIM + nsel), lambda b, i: (b, 0, i, 0)),
        ),
        compiler_params=_cparams(("arbitrary", "arbitrary")),
        name="nsa_cmp_select",
    )(q, k_cmp, v_cmp, overlap)


def _sel_kernel(qa_ref, ka_ref, vs_ref, o_ref, m_ref, l_ref, acc_ref, *, heads, tq, tk):
    i = pl.program_id(1)
    rows = heads * tq
    qa = qa_ref[0].reshape(rows, qa_ref.shape[3])
    m_ref[...] = jnp.full(m_ref.shape, NEG_INF, F32)
    l_ref[...] = jnp.zeros(l_ref.shape, F32)
    acc_ref[...] = jnp.zeros(acc_ref.shape, F32)
    t = i * tq + (lax.broadcasted_iota(jnp.int32, (rows, tk), 0) % tq)
    lane = lax.broadcasted_iota(jnp.int32, (rows, tk), 1)
    nkt = (i * tq + tq + tk - 1) // tk

    def body(kt, carry):
        k0 = pl.multiple_of(kt * tk, tk)
        s = _dot_t(qa, ka_ref[0, pl.ds(k0, tk), :])
        s = jnp.where(k0 + lane <= t, s, NEG_INF)
        m_old = m_ref[...]
        m_new = jnp.maximum(m_old, jnp.max(s, axis=-1, keepdims=True))
        alpha = jnp.exp(m_old - m_new)
        p = jnp.exp(s - m_new)
        l_ref[...] = alpha * l_ref[...] + jnp.sum(p, axis=-1, keepdims=True)
        acc_ref[...] = alpha * acc_ref[...] + _dot(p.astype(BF16), vs_ref[0, pl.ds(k0, tk), :])
        m_ref[...] = m_new
        return carry

    lax.fori_loop(0, nkt, body, 0)
    o = acc_ref[...] / l_ref[...]
    o_ref[0] = jnp.concatenate([o[h * tq:(h + 1) * tq, :] for h in range(heads)], axis=1)


def _sel_attention(qa, ka, vs, *, tq, tk):
    bsz, heads, seq, ka_w = qa.shape
    return pl.pallas_call(
        functools.partial(_sel_kernel, heads=heads, tq=tq, tk=tk),
        out_shape=jax.ShapeDtypeStruct((bsz, seq, heads * HEAD_DIM), F32),
        grid=(bsz, seq // tq),
        in_specs=[
            pl.BlockSpec((1, heads, tq, ka_w), lambda b, i: (b, 0, i, 0)),
            pl.BlockSpec((1, seq, ka_w), lambda b, i: (b, 0, 0)),
            pl.BlockSpec((1, seq, HEAD_DIM), lambda b, i: (b, 0, 0)),
        ],
        out_specs=pl.BlockSpec((1, tq, heads * HEAD_DIM), lambda b, i: (b, i, 0)),
        scratch_shapes=[
            pltpu.VMEM((heads * tq, 1), F32),
            pltpu.VMEM((heads * tq, 1), F32),
            pltpu.VMEM((heads * tq, HEAD_DIM), F32),
        ],
        compiler_params=_cparams(("arbitrary", "arbitrary")),
        name="nsa_selected",
    )(qa, ka, vs)


def _win_kernel(qa_ref, kw_ref, vw_ref, gn_ref, ocmp_ref, osel_ref, o_ref, *, heads, tq, span):
    i = pl.program_id(1)
    rows = heads * tq
    qa = qa_ref[0].reshape(rows, qa_ref.shape[3])
    start = pl.multiple_of(jnp.maximum(i * tq + tq - span, 0), tq)
    kw = kw_ref[0, pl.ds(start, span), :]
    vw = vw_ref[0, pl.ds(start, span), :]
    s = _dot_t(qa, kw)
    t = i * tq + (lax.broadcasted_iota(jnp.int32, (rows, span), 0) % tq)
    dist = t - (start + lax.broadcasted_iota(jnp.int32, (rows, span), 1))
    mask = (dist >= 0) & (dist < WINDOW)
    s = jnp.where(mask, s, NEG_INF)
    m = jnp.max(s, axis=-1, keepdims=True)
    p = jnp.exp(s - m)
    den = jnp.sum(p, axis=-1, keepdims=True)
    ow = _dot(p.astype(BF16), vw) / den
    g = _sigmoid(gn_ref[0])
    ocmp = ocmp_ref[0]
    osel = osel_ref[0]
    outs = []
    for h in range(heads):
        sl = slice(h * HEAD_DIM, (h + 1) * HEAD_DIM)
        outs.append(g[:, 3 * h:3 * h + 1] * ocmp[:, sl] + g[:, 3 * h + 1:3 * h + 2] * osel[:, sl]
                    + g[:, 3 * h + 2:3 * h + 3] * ow[h * tq:(h + 1) * tq, :])
    o_ref[0] = jnp.concatenate(outs, axis=1)


def _win_combine(qa, kwa, vw, z_nsa, gn_block, o_cmp, o_sel, *, tq):
    bsz, heads, seq, ka_w = qa.shape
    qd = heads * HEAD_DIM
    span = WINDOW + tq
    return pl.pallas_call(
        functools.partial(_win_kernel, heads=heads, tq=tq, span=span),
        out_shape=jax.ShapeDtypeStruct((bsz, seq, qd), F32),
        grid=(bsz, seq // tq),
        in_specs=[
            pl.BlockSpec((1, heads, tq, ka_w), lambda b, i: (b, 0, i, 0)),
            pl.BlockSpec((1, seq, ka_w), lambda b, i: (b, 0, 0)),
            pl.BlockSpec((1, seq, HEAD_DIM), lambda b, i: (b, 0, 0)),
            pl.BlockSpec((1, tq, LANES), lambda b, i: (b, i, gn_block)),
            pl.BlockSpec((1, tq, qd), lambda b, i: (b, i, 0)),
            pl.BlockSpec((1, tq, qd), lambda b, i: (b, i, 0)),
        ],
        out_specs=pl.BlockSpec((1, tq, qd), lambda b, i: (b, i, 0)),
        compiler_params=_cparams(("arbitrary", "arbitrary")),
        name="nsa_window_combine",
    )(qa, kwa, vw, z_nsa, o_cmp, o_sel)


def _merge_kernel(ys_ref, yb_ref, gm_ref, x_ref, wb_ref, wo_ref, gate_ref, o_ref):
    w = yb_ref.shape[2]
    d = x_ref.shape[2]
    ys = ys_ref[...]
    branches = (ys[:, 0:w], yb_ref[0], ys[:, w:2 * w], ys[:, 2 * w:3 * w])
    merged = None
    for mi, y in enumerate(branches):
        term = _sigmoid(gm_ref[0, :, mi * d:(mi + 1) * d].astype(F32)) * _dot(y.astype(BF16), wb_ref[mi])
        merged = term if merged is None else merged + term
    mix = _dot(merged.astype(BF16), wo_ref[...])
    o_ref[0] = x_ref[0] + gate_ref[0] * mix


def _merge(y_scan_tm, y_b, g_merge, x, w_branch, w_out, gate, *, tm):
    bsz, seq, d = x.shape
    w = y_b.shape[2]
    ys = y_scan_tm.reshape(seq, bsz * 3 * w)
    return pl.pallas_call(
        _merge_kernel,
        out_shape=jax.ShapeDtypeStruct((bsz, seq, d), F32),
        grid=(bsz, seq // tm),
        in_specs=[
            pl.BlockSpec((tm, 3 * w), lambda b, i: (i, b)),
            pl.BlockSpec((1, tm, w), lambda b, i: (b, i, 0)),
            pl.BlockSpec((1, tm, N_BRANCH * d), lambda b, i: (b, i, 0)),
            pl.BlockSpec((1, tm, d), lambda b, i: (b, i, 0)),
            pl.BlockSpec(w_branch.shape, lambda b, i: (0, 0, 0)),
            pl.BlockSpec(w_out.shape, lambda b, i: (0, 0)),
            pl.BlockSpec((1, 1, d), lambda b, i: (b, 0, 0)),
        ],
        out_specs=pl.BlockSpec((1, tm, d), lambda b, i: (b, i, 0)),
        compiler_params=_cparams(("arbitrary", "arbitrary")),
        name="merge_out",
    )(ys, y_b, g_merge, x, w_branch, w_out, gate)


def _ffn_kernel(x_ref, g_ref, sc_ref, sh_ref, gate_ref, *rest, moe, final):
    if moe:
        rw_ref, rb_ref, wg_ref, wu_ref, wd_ref = rest[:5]
        rest = rest[5:]
    else:
        wg_ref, wu_ref, wd_ref = rest[:3]
        rest = rest[3:]
    if final:
        fg_ref = rest[0]
        rest = rest[1:]
    if moe:
        o_ref, h_ref, acc_ref, comb_ref = rest
    else:
        o_ref, h_ref, acc_ref = rest
    f = pl.program_id(2)
    nf = pl.num_programs(2)

    @pl.when(f == 0)
    def _():
        h = _norm_mod(x_ref[0], g_ref[...], sc_ref[0], sh_ref[0])
        h_ref[...] = h.astype(BF16)
        acc_ref[...] = jnp.zeros_like(acc_ref)
        if moe:
            logits = _dot(h, rw_ref[...], HI) + rb_ref[...]
            lane = lax.broadcasted_iota(jnp.int32, logits.shape, 1)
            big = logits.shape[1]
            v1 = jnp.max(logits, axis=-1, keepdims=True)
            i1 = jnp.min(jnp.where(logits == v1, lane, big), axis=-1, keepdims=True)
            rest_l = jnp.where(lane == i1, NEG_INF * 4.0, logits)
            v2 = jnp.max(rest_l, axis=-1, keepdims=True)
            i2 = jnp.min(jnp.where(rest_l == v2, lane, big), axis=-1, keepdims=True)
            e2 = jnp.exp(v2 - v1)
            w1 = 1.0 / (1.0 + e2)
            w2 = e2 / (1.0 + e2)
            comb_ref[...] = jnp.where(lane == i1, w1, 0.0) + jnp.where(lane == i2, w2, 0.0)

    hb = h_ref[...]
    if moe:
        wg, wu, wd = wg_ref[0], wu_ref[0], wd_ref[0]
    else:
        wg, wu, wd = wg_ref[...], wu_ref[...], wd_ref[...]
    a = (_silu(_dot(hb, wg)) * _dot(hb, wu)).astype(BF16)
    y = _dot(a, wd)
    if moe:
        lane = lax.broadcasted_iota(jnp.int32, comb_ref.shape, 1)
        y = y * jnp.sum(jnp.where(lane == f, comb_ref[...], 0.0), axis=-1, keepdims=True)
    acc_ref[...] += y

    @pl.when(f == nf - 1)
    def _():
        out = x_ref[0] + gate_ref[0] * acc_ref[...]
        if final:
            var = jnp.mean(out * out, axis=-1, keepdims=True)
            out = out * lax.rsqrt(var + RMS_EPS) * fg_ref[...]
        o_ref[0] = out


def _ffn(x, g, scale, shift, gate, weights, *, moe, final_g, tm, tf):
    bsz, seq, d = x.shape
    row = lambda: pl.BlockSpec((1, 1, d), lambda b, i, f: (b, 0, 0))
    in_specs = [
        pl.BlockSpec((1, tm, d), lambda b, i, f: (b, i, 0)),
        pl.BlockSpec((1, d), lambda b, i, f: (0, 0)),
        row(), row(), row(),
    ]
    args = [x, g.reshape(1, d), scale, shift, gate]
    if moe:
        rw, rb, wg, wu, wd = weights
        ne, _, fe = wg.shape
        in_specs += [
            pl.BlockSpec(rw.shape, lambda b, i, f: (0, 0)),
            pl.BlockSpec(rb.shape, lambda b, i, f: (0, 0)),
            pl.BlockSpec((1, d, fe), lambda b, i, f: (f, 0, 0)),
            pl.BlockSpec((1, d, fe), lambda b, i, f: (f, 0, 0)),
            pl.BlockSpec((1, fe, d), lambda b, i, f: (f, 0, 0)),
        ]
        args += [rw, rb, wg, wu, wd]
        nf = ne
    else:
        wg, wu, wd = weights
        ff = wg.shape[1]
        in_specs += [
            pl.BlockSpec((d, tf), lambda b, i, f: (0, f)),
            pl.BlockSpec((d, tf), lambda b, i, f: (0, f)),
            pl.BlockSpec((tf, d), lambda b, i, f: (f, 0)),
        ]
        args += [wg, wu, wd]
        nf = ff // tf
    final = final_g is not None
    if final:
        in_specs.append(pl.BlockSpec((1, d), lambda b, i, f: (0, 0)))
        args.append(final_g.reshape(1, d))
    scratch = [pltpu.VMEM((tm, d), BF16), pltpu.VMEM((tm, d), F32)]
    if moe:
        scratch.append(pltpu.VMEM((tm, LANES), F32))
    return pl.pallas_call(
        functools.partial(_ffn_kernel, moe=moe, final=final),
        out_shape=jax.ShapeDtypeStruct((bsz, seq, d), F32),
        grid=(bsz, seq // tm, nf),
        in_specs=in_specs,
        out_specs=pl.BlockSpec((1, tm, d), lambda b, i, f: (b, i, 0)),
        scratch_shapes=scratch,
        compiler_params=_cparams(("arbitrary", "arbitrary", "arbitrary")),
        name="moe_ffn" if moe else "dense_ffn",
    )(*args)


def _block_diag(blocks):
    n, r, c = blocks.shape
    eye = jnp.eye(n, dtype=blocks.dtype)
    return (eye[:, None, :, None] * blocks[:, :, None, :]).reshape(n * r, n * c)


def _rope_tables(seq, heads_q):
    inv = ROPE_THETA ** (-jnp.arange(0, HEAD_DIM, 2, dtype=F32) / HEAD_DIM)
    ang = jnp.arange(seq, dtype=F32)[:, None] * inv[None, :]
    cos, sin = jnp.cos(ang), jnp.sin(ang)
    cos_h = jnp.concatenate([cos, cos], axis=1)
    sin_h = jnp.concatenate([-sin, sin], axis=1)
    one_h = jnp.ones_like(cos_h)
    zero_h = jnp.zeros_like(cos_h)
    cos_t = jnp.concatenate([cos_h] * heads_q + [one_h, one_h] + [cos_h, one_h] * 3, axis=1)
    sin_t = jnp.concatenate([sin_h] * heads_q + [zero_h, zero_h] + [sin_h, zero_h] * 3, axis=1)
    return cos_t, sin_t


def kernel(x, c, mod_w, mod_b, norm_mix_g, norm_ffn_g, w_in, s5_lambda_re, s5_lambda_im, s5_log_step, s5_b_re, s5_b_im, s5_c_re, s5_c_im, s5_d, s5_w_glu, nsa_pe_k, nsa_pe_v, nsa_cmp_k_w1, nsa_cmp_k_w2, nsa_cmp_v_w1, nsa_cmp_v_w2, lru_conv_w, lru_conv_b, lru_w_a, lru_b_a, lru_w_x, lru_b_x, lru_lambda, sc_conv_w, w_branch, w_out, ffn_w_gate, ffn_w_up, ffn_w_down, moe_router_w, moe_router_b, moe_w_gate, moe_w_up, moe_w_down, final_norm_g):
    bsz, seq, d = x.shape
    depth = mod_w.shape[0]
    w = d // N_BRANCH
    heads = w // HEAD_DIM
    groups = w // S5_GROUP
    assert bsz == SUBLANES, "time-major scan blocks hold one timestep per sublane tile"
    tm = min(512, seq)
    tq = 128
    steps = min(64, seq)
    nsel = seq // L_SEL
    assert nsel <= LANES and seq % tm == 0 and seq >= WINDOW + tq

    mod = _modulation(c, mod_w, mod_b)
    mod = mod.reshape(depth, bsz, 6, 1, d)

    sizes = (w, w, HEAD_DIM, HEAD_DIM, HEAD_DIM, HEAD_DIM, HEAD_DIM, HEAD_DIM, heads * 3, w, w, w, w, w, N_BRANCH * d)
    offs = np.concatenate([[0], np.cumsum(sizes)])
    col = lambda wmat, k: wmat[:, offs[k]:offs[k + 1]]

    cos_t, sin_t = _rope_tables(seq, heads)
    n_cmp = (seq - L_CMP) // CMP_STRIDE + 1
    nblk = seq // CMP_STRIDE
    cmp_start = np.arange(nblk) * CMP_STRIDE
    sel_start = np.arange(nsel) * L_SEL
    ovl = np.clip(np.minimum(cmp_start[:, None] + L_CMP, sel_start[None, :] + L_SEL)
                  - np.maximum(cmp_start[:, None], sel_start[None, :]), 0, None) / L_CMP
    ovl[n_cmp:] = 0.0
    overlap = jnp.asarray(np.pad(ovl, ((0, 0), (0, LANES - nsel))), F32)
    onehot = jnp.asarray((np.arange(seq)[:, None] // L_SEL) == np.arange(nsel)[None, :], BF16)

    for li in range(depth):
        sh1, sc1, gt1, sh2, sc2, gt2 = (mod[li, :, k] for k in range(6))
        wl = w_in[li]
        w_scan = jnp.concatenate([col(wl, 0), col(wl, 9), col(wl, 10), col(wl, 11), col(wl, 12), col(wl, 13)], axis=1).astype(BF16)
        gpad = jnp.zeros((d, LANES - heads * 3), wl.dtype)
        w_nsa = jnp.concatenate([col(wl, 1), col(wl, 8), gpad] + [col(wl, k) for k in range(2, 8)], axis=1).astype(BF16)
        w_gm = col(wl, 14).astype(BF16)
        g1 = norm_mix_g[li]
        z_scan = _project(x, g1, sc1, sh1, w_scan, time_major=True, out_dtype=F32, tm=tm)
        z_nsa = _project(x, g1, sc1, sh1, w_nsa, time_major=False, out_dtype=F32, tm=tm, rope_tabs=(cos_t, sin_t))
        g_merge = _project(x, g1, sc1, sh1, w_gm, time_major=False, out_dtype=BF16, tm=tm)

        a_re, a_im, bb_re, bb_im = _s5_params(s5_lambda_re[li], s5_lambda_im[li], s5_log_step[li], s5_b_re[li], s5_b_im[li])
        to_bd = lambda m: _block_diag(jnp.swapaxes(m.reshape(groups, S5_STATE, S5_GROUP), 1, 2))
        s5b = jnp.concatenate([to_bd(bb_re), to_bd(bb_im)], axis=1).astype(BF16)
        c_bd = lambda m: _block_diag(jnp.swapaxes(m, 1, 2))
        s5c = jnp.concatenate([c_bd(s5_c_re[li]), -c_bd(s5_c_im[li])], axis=0).astype(BF16)
        r1 = lambda a: a.reshape(1, -1)
        y_scan = _scan_mixers(
            z_scan.reshape(seq * bsz, 6 * w), bsz, s5b, s5c, r1(a_re), r1(a_im), r1(s5_d[li]), s5_w_glu[li].astype(BF16),
            lru_conv_w[li], r1(lru_conv_b[li]), _block_diag(lru_w_a[li]).astype(BF16), r1(lru_b_a[li]),
            _block_diag(lru_w_x[li]).astype(BF16), r1(lru_b_x[li]), r1(lru_lambda[li]), sc_conv_w[li], steps=steps)

        kv0 = heads * HEAD_DIM + LANES
        kvs = [z_nsa[:, :, kv0 + k * HEAD_DIM:kv0 + (k + 1) * HEAD_DIM] for k in range(6)]
        kc, vc, ks, vs, kw, vw = kvs
        grp = lambda a: a.reshape(bsz, nblk, CMP_STRIDE * HEAD_DIM)
        k_cmp, v_cmp = _compress(grp(kc), grp(vc), nsa_pe_k[li], nsa_pe_v[li], nsa_cmp_k_w1[li], nsa_cmp_k_w2[li], nsa_cmp_v_w1[li], nsa_cmp_v_w2[li])
        o_cmp, qa = _cmp_select(z_nsa, k_cmp, v_cmp, overlap, heads=heads, tq=tq, nsel=nsel)
        ka = jnp.concatenate([ks.astype(BF16), jnp.broadcast_to(onehot, (bsz, seq, nsel))], axis=2)
        o_sel = _sel_attention(qa, ka, vs.astype(BF16), tq=tq, tk=min(256, seq))
        kwa = jnp.concatenate([kw.astype(BF16), jnp.zeros((bsz, seq, nsel), BF16)], axis=2)
        y_b = _win_combine(qa, kwa, vw.astype(BF16), z_nsa, (heads * HEAD_DIM) // LANES, o_cmp, o_sel, tq=tq)

        x = _merge(y_scan, y_b, g_merge, x, w_branch[li].astype(BF16), w_out[li].astype(BF16), gt1, tm=tm)

        final_g = final_norm_g if li == depth - 1 else None
        j = li // 2
        if li % 2 == 0:
            weights = (ffn_w_gate[j].astype(BF16), ffn_w_up[j].astype(BF16), ffn_w_down[j].astype(BF16))
            x = _ffn(x, norm_ffn_g[li], sc2, sh2, gt2, weights, moe=False, final_g=final_g, tm=tm, tf=ffn_w_gate.shape[2] // 2)
        else:
            rw = jnp.pad(moe_router_w[j], ((0, 0), (0, LANES - N_EXPERTS)))
            rb = jnp.pad(moe_router_b[j], (0, LANES - N_EXPERTS), constant_values=NEG_INF).reshape(1, LANES)
            weights = (rw, rb, moe_w_gate[j].astype(BF16), moe_w_up[j].astype(BF16), moe_w_down[j].astype(BF16))
            x = _ffn(x, norm_ffn_g[li], sc2, sh2, gt2, weights, moe=True, final_g=final_g, tm=tm, tf=None)
    return x
```

```python
import functools
import math

import jax
import jax.numpy as jnp
import numpy as np
from jax import lax
from jax.experimental import pallas as pl
from jax.experimental.pallas import tpu as pltpu

F32 = jnp.float32
BF16 = jnp.bfloat16

N_BRANCH = 4
S5_GROUP = 16
S5_STATE = 64
HEAD_DIM = 64
L_CMP = 32
CMP_STRIDE = 16
L_SEL = 64
TOP_N = 16
WINDOW = 512
ROPE_THETA = 10000.0
FORCED_SCORE = 1.0e4
NEG_INF = -1.0e30
LRU_CONV = 4
LRU_C = 8.0
SC_CONV = 3
N_EXPERTS = 8
RMS_EPS = 1e-6

LANES = 128
SUBLANES = 8
VMEM_LIMIT = 52 * 1024 * 1024

HI = lax.Precision.HIGHEST


def _cparams(sem):
    return pltpu.CompilerParams(dimension_semantics=sem, vmem_limit_bytes=VMEM_LIMIT)


def _dot(a, b, precision=None):
    return jnp.dot(a, b, preferred_element_type=F32, precision=precision)


def _dot_t(a, b, precision=None):
    return lax.dot_general(a, b, (((1,), (1,)), ((), ())), preferred_element_type=F32, precision=precision)


def _gelu(x):
    return 0.5 * x * (1.0 + jnp.tanh(math.sqrt(2.0 / math.pi) * (x + 0.044715 * (x * x * x))))


def _sigmoid(x):
    return 1.0 / (1.0 + jnp.exp(-x))


def _silu(x):
    return x * _sigmoid(x)


def _norm_mod(x, g, scale, shift):
    var = jnp.mean(x * x, axis=-1, keepdims=True)
    return (x * lax.rsqrt(var + RMS_EPS) * g) * (1.0 + scale) + shift


def _rot_half(z):
    n = z.shape[-1]
    lane = lax.broadcasted_iota(jnp.int32, z.shape, z.ndim - 1)
    first = (lane % HEAD_DIM) < (HEAD_DIM // 2)
    return jnp.where(first, pltpu.roll(z, n - HEAD_DIM // 2, z.ndim - 1), pltpu.roll(z, HEAD_DIM // 2, z.ndim - 1))


def _mod_kernel(c_ref, w_ref, b_ref, o_ref):
    c = c_ref[...]
    o_ref[0] = _dot(_silu(c), w_ref[0], HI) + b_ref[0]


def _modulation(c, mod_w, mod_b):
    depth, d, n = mod_w.shape
    bsz = c.shape[0]
    tn = 1024
    return pl.pallas_call(
        _mod_kernel,
        out_shape=jax.ShapeDtypeStruct((depth, bsz, n), F32),
        grid=(depth, n // tn),
        in_specs=[
            pl.BlockSpec((bsz, d), lambda l, j: (0, 0)),
            pl.BlockSpec((1, d, tn), lambda l, j: (l, 0, j)),
            pl.BlockSpec((1, 1, tn), lambda l, j: (l, 0, j)),
        ],
        out_specs=pl.BlockSpec((1, bsz, tn), lambda l, j: (l, 0, j)),
        compiler_params=_cparams(("arbitrary", "arbitrary")),
        name="modulation",
    )(c, mod_w, mod_b.reshape(depth, 1, n))


def _proj_kernel(x_ref, g_ref, sc_ref, sh_ref, w_ref, *rest, rope, nchunk):
    if rope:
        cos_ref, sin_ref, o_ref = rest
    else:
        (o_ref,) = rest
    h = _norm_mod(x_ref[0], g_ref[...], sc_ref[0], sh_ref[0]).astype(BF16)
    n = w_ref.shape[1]
    for n0 in range(0, n, nchunk):
        z = _dot(h, w_ref[:, n0:n0 + nchunk])
        if rope:
            z = z * cos_ref[:, n0:n0 + nchunk] + _rot_half(z) * sin_ref[:, n0:n0 + nchunk]
        if len(o_ref.shape) == 3:
            o_ref[0, :, n0:n0 + nchunk] = z.astype(o_ref.dtype)
        else:
            o_ref[:, n0:n0 + nchunk] = z.astype(o_ref.dtype)


def _project(x, g, scale, shift, w, *, time_major, out_dtype, tm, rope_tabs=None):
    bsz, seq, d = x.shape
    n = w.shape[1]
    rope = rope_tabs is not None
    in_specs = [
        pl.BlockSpec((1, tm, d), lambda b, i: (b, i, 0)),
        pl.BlockSpec((1, d), lambda b, i: (0, 0)),
        pl.BlockSpec((1, 1, d), lambda b, i: (b, 0, 0)),
        pl.BlockSpec((1, 1, d), lambda b, i: (b, 0, 0)),
        pl.BlockSpec((d, n), lambda b, i: (0, 0)),
    ]
    args = [x, g.reshape(1, d), scale, shift, w]
    if rope:
        in_specs += [pl.BlockSpec((tm, n), lambda b, i: (i, 0))] * 2
        args += list(rope_tabs)
    if time_major:
        out_shape = jax.ShapeDtypeStruct((seq, bsz * n), out_dtype)
        out_spec = pl.BlockSpec((tm, n), lambda b, i: (i, b))
    else:
        out_shape = jax.ShapeDtypeStruct((bsz, seq, n), out_dtype)
        out_spec = pl.BlockSpec((1, tm, n), lambda b, i: (b, i, 0))
    nchunk = 256 if rope else 512
    return pl.pallas_call(
        functools.partial(_proj_kernel, rope=rope, nchunk=nchunk),
        out_shape=out_shape,
        grid=(bsz, seq // tm),
        in_specs=in_specs,
        out_specs=out_spec,
        compiler_params=_cparams(("arbitrary", "arbitrary")),
        name="proj_rope" if rope else ("proj_tm" if time_major else "proj_bm"),
    )(*args)


def _s5_param_kernel(lre_ref, lim_ref, ls_ref, bre_ref, bim_ref, are_ref, aim_ref, bbre_ref, bbim_ref):
    lre = lre_ref[...]
    lim = lim_ref[...]
    step = jnp.exp(ls_ref[...])
    mag = jnp.exp(lre * step)
    ar = mag * jnp.cos(lim * step)
    ai = mag * jnp.sin(lim * step)
    xr = ar - 1.0
    den = lre * lre + lim * lim
    cr = (xr * lre + ai * lim) / den
    ci = (ai * lre - xr * lim) / den
    bre = bre_ref[...]
    bim = bim_ref[...]
    are_ref[...] = ar
    aim_ref[...] = ai
    bbre_ref[...] = cr * bre - ci * bim
    bbim_ref[...] = cr * bim + ci * bre


def _s5_params(lam_re, lam_im, log_step, b_re, b_im):
    g, n = lam_re.shape
    c = b_re.shape[-1]
    rows = g * n
    col = lambda a: a.reshape(rows, 1)
    ls = jnp.broadcast_to(log_step[:, None], (g, n))
    out = pl.pallas_call(
        _s5_param_kernel,
        out_shape=(
            jax.ShapeDtypeStruct((rows, 1), F32),
            jax.ShapeDtypeStruct((rows, 1), F32),
            jax.ShapeDtypeStruct((rows, c), F32),
            jax.ShapeDtypeStruct((rows, c), F32),
        ),
        name="s5_params",
    )(col(lam_re), col(lam_im), col(ls), b_re.reshape(rows, c), b_im.reshape(rows, c))
    return out


def _scan_kernel(z_ref, s5b_ref, s5c_ref, are_ref, aim_ref, d_ref, wglu_ref,
                 cw_ref, cb_ref, wa_ref, ba_ref, wx_ref, bx_ref, lam_ref, scw_ref,
                 o_ref,
                 bu_ref, hre_ref, him_ref, xe_ref, hl_ref, pe_ref, ab_ref, bb_ref, *, nb, steps):
    c = pl.program_id(0)
    rows = nb * steps
    w = d_ref.shape[1]
    ns = are_ref.shape[1]

    @pl.when(c == 0)
    def _():
        hre_ref[...] = jnp.zeros_like(hre_ref)
        him_ref[...] = jnp.zeros_like(him_ref)
        hl_ref[...] = jnp.zeros_like(hl_ref)
        xe_ref[0:(LRU_CONV - 1) * nb, :] = jnp.zeros(((LRU_CONV - 1) * nb, w), F32)
        pe_ref[0:(SC_CONV - 1) * nb, :] = jnp.zeros(((SC_CONV - 1) * nb, w), F32)

    u = z_ref[:, 0:w]
    bu_ref[...] = _dot(u.astype(BF16), s5b_ref[...])
    a_re = jnp.broadcast_to(are_ref[...], (nb, ns))
    a_im = jnp.broadcast_to(aim_ref[...], (nb, ns))

    def s5_step(t, carry):
        h_re, h_im = carry
        r0 = pl.multiple_of(t * nb, nb)
        b_re = bu_ref[pl.ds(r0, nb), 0:ns]
        b_im = bu_ref[pl.ds(r0, nb), ns:2 * ns]
        n_re = a_re * h_re - a_im * h_im + b_re
        n_im = a_re * h_im + a_im * h_re + b_im
        bu_ref[pl.ds(r0, nb), 0:ns] = n_re
        bu_ref[pl.ds(r0, nb), ns:2 * ns] = n_im
        return n_re, n_im

    h_re, h_im = lax.fori_loop(0, steps, s5_step, (hre_ref[...], him_ref[...]))
    hre_ref[...] = h_re
    him_ref[...] = h_im
    y = _dot(bu_ref[...].astype(BF16), s5c_ref[...]) + d_ref[...] * u
    zg = _gelu(y)
    o_ref[:, 0:w] = zg * _sigmoid(_dot(zg.astype(BF16), wglu_ref[...]))

    nt = (LRU_CONV - 1) * nb
    x = z_ref[:, w:2 * w]
    xe_ref[nt:nt + rows, :] = x
    xc = cb_ref[...] + cw_ref[LRU_CONV - 1:LRU_CONV, :] * x
    for k in range(LRU_CONV - 1):
        xc = xc + cw_ref[k:k + 1, :] * xe_ref[k * nb:k * nb + rows, :]
    xe_ref[0:nt, :] = xe_ref[rows:rows + nt, :]
    xcb = xc.astype(BF16)
    r = _sigmoid(_dot(xcb, wa_ref[...]) + ba_ref[...])
    gi = _sigmoid(_dot(xcb, wx_ref[...]) + bx_ref[...])
    nl = -lam_ref[...]
    softplus = jnp.maximum(nl, 0.0) + jnp.log(1.0 + jnp.exp(-jnp.abs(nl)))
    log_a = (-LRU_C) * r * softplus
    ab_ref[...] = jnp.exp(log_a)
    bb_ref[...] = jnp.sqrt(1.0 - jnp.exp(2.0 * log_a)) * (gi * xc)

    def lru_step(t, h):
        r0 = pl.multiple_of(t * nb, nb)
        h = ab_ref[pl.ds(r0, nb), :] * h + bb_ref[pl.ds(r0, nb), :]
        bb_ref[pl.ds(r0, nb), :] = h
        return h

    hl_ref[...] = lax.fori_loop(0, steps, lru_step, hl_ref[...])
    o_ref[:, w:2 * w] = bb_ref[...] * _gelu(z_ref[:, 2 * w:3 * w])

    npt = (SC_CONV - 1) * nb
    p = z_ref[:, 4 * w:5 * w] * z_ref[:, 5 * w:6 * w]
    pe_ref[npt:npt + rows, :] = p
    cv = scw_ref[SC_CONV - 1:SC_CONV, :] * p
    for k in range(SC_CONV - 1):
        cv = cv + scw_ref[k:k + 1, :] * pe_ref[k * nb:k * nb + rows, :]
    pe_ref[0:npt, :] = pe_ref[rows:rows + npt, :]
    o_ref[:, 2 * w:3 * w] = z_ref[:, 3 * w:4 * w] * cv


def _scan_mixers(z_tm, nb, s5b, s5c, a_re, a_im, d_skip, w_glu, conv_w, conv_b, wa, ba, wx, bx, lam, sc_w, *, steps):
    total, zc = z_tm.shape
    rows = nb * steps
    w = zc // 6
    ns = a_re.shape[1]
    full = lambda a: pl.BlockSpec(a.shape, lambda c: (0,) * a.ndim)
    params = [s5b, s5c, a_re, a_im, d_skip, w_glu, conv_w, conv_b, wa, ba, wx, bx, lam, sc_w]
    return pl.pallas_call(
        functools.partial(_scan_kernel, nb=nb, steps=steps),
        out_shape=jax.ShapeDtypeStruct((total, 3 * w), F32),
        grid=(total // rows,),
        in_specs=[pl.BlockSpec((rows, zc), lambda c: (c, 0))] + [full(a) for a in params],
        out_specs=pl.BlockSpec((rows, 3 * w), lambda c: (c, 0)),
        scratch_shapes=[
            pltpu.VMEM((rows, 2 * ns), F32),
            pltpu.VMEM((nb, ns), F32),
            pltpu.VMEM((nb, ns), F32),
            pltpu.VMEM((rows + (LRU_CONV - 1) * nb, w), F32),
            pltpu.VMEM((nb, w), F32),
            pltpu.VMEM((rows + (SC_CONV - 1) * nb, w), F32),
            pltpu.VMEM((rows, w), F32),
            pltpu.VMEM((rows, w), F32),
        ],
        compiler_params=_cparams(("arbitrary",)),
        name="scan_mixers",
    )(z_tm, *params)


def _compress_kernel(gk_ref, gv_ref, pek_ref, pev_ref, wk1_ref, wk2_ref, wv1_ref, wv2_ref, ko_ref, vo_ref):
    half = gk_ref.shape[2]
    nblk = gk_ref.shape[1]

    def phi(g, pe, w1_ref, w2_ref):
        top = _dot(g, w1_ref[0:half, :], HI)
        bot = _dot(g, w1_ref[half:2 * half, :], HI)
        hid = top + pltpu.roll(bot, nblk - 1, 0) + _dot(pe, w1_ref[...], HI)
        return _dot(_gelu(hid), w2_ref[...], HI)

    ko_ref[0] = phi(gk_ref[0], pek_ref[...], wk1_ref, wk2_ref)
    vo_ref[0] = phi(gv_ref[0], pev_ref[...], wv1_ref, wv2_ref)


def _compress(gk, gv, pe_k, pe_v, wk1, wk2, wv1, wv2):
    bsz, nblk, half = gk.shape
    hd = wk2.shape[1]
    full = lambda a: pl.BlockSpec(a.shape, lambda b: (0,) * a.ndim)
    pek = pe_k.reshape(1, -1)
    pev = pe_v.reshape(1, -1)
    params = [pek, pev, wk1, wk2, wv1, wv2]
    blk = pl.BlockSpec((1, nblk, half), lambda b: (b, 0, 0))
    oblk = pl.BlockSpec((1, nblk, hd), lambda b: (b, 0, 0))
    return pl.pallas_call(
        _compress_kernel,
        out_shape=(jax.ShapeDtypeStruct((bsz, nblk, hd), F32),) * 2,
        grid=(bsz,),
        in_specs=[blk, blk] + [full(a) for a in params],
        out_specs=(oblk, oblk),
        compiler_params=_cparams(("arbitrary",)),
        name="nsa_compress",
    )(gk, gv, *params)


def _cmp_kernel(q_ref, kc_ref, vc_ref, ov_ref, ocmp_ref, qa_ref, *, heads, tq):
    i = pl.program_id(1)
    q = q_ref[0]
    kc = kc_ref[0]
    vc = vc_ref[0]
    nblk = kc.shape[0]
    scale = HEAD_DIM ** -0.5
    t = i * tq + lax.broadcasted_iota(jnp.int32, (tq, nblk), 0)
    n_id = lax.broadcasted_iota(jnp.int32, (tq, nblk), 1)
    visible = (n_id * CMP_STRIDE + (L_CMP - 1)) <= t
    psum = jnp.zeros((tq, nblk), F32)
    outs = []
    for h in range(heads):
        qh = q[:, h * HEAD_DIM:(h + 1) * HEAD_DIM]
        s = jnp.where(visible, _dot_t(qh, kc, HI) * scale, NEG_INF)
        m = jnp.max(s, axis=-1, keepdims=True)
        e = jnp.where(visible, jnp.exp(s - m), 0.0)
        den = jnp.sum(e, axis=-1, keepdims=True)
        p = e / jnp.where(den > 0.0, den, 1.0)
        psum = psum + p
        outs.append(_dot(p, vc, HI))
    ocmp_ref[0] = jnp.concatenate(outs, axis=1).T

    imp = _dot(psum, ov_ref[...], HI)
    nsel = ov_ref.shape[1]
    tt = i * tq + lax.broadcasted_iota(jnp.int32, (tq, nsel), 0)
    j = lax.broadcasted_iota(jnp.int32, (tq, nsel), 1)
    cur = tt // L_SEL
    forced = (j == 0) | (j == cur) | (j == cur - 1)
    future = (j * L_SEL) > tt
    imp = jnp.where(future, -1.0, jnp.where(forced, FORCED_SCORE, imp))
    imp_t = imp.T
    nreal = qa_ref.shape[2] - HEAD_DIM
    cand = imp_t[0:nreal, :]
    jrow = lax.broadcasted_iota(jnp.int32, (nreal, tq), 0)
    rank = jnp.zeros((nreal, tq), jnp.int32)
    for k in range(nreal):
        rk = imp_t[k:k + 1, :]
        beats = (rk > cand) | ((rk == cand) & (jrow > k))
        rank = rank + beats.astype(jnp.int32)
    bias_t = jnp.where(rank < TOP_N, 0.0, NEG_INF)
    q_t = (q * scale).T
    for h in range(heads):
        qa_ref[0, h] = jnp.concatenate([q_t[h * HEAD_DIM:(h + 1) * HEAD_DIM, :], bias_t], axis=0).astype(BF16)


def _cmp_select(q, k_cmp, v_cmp, overlap, *, heads, tq, nsel):
    bsz, seq, _ = q.shape
    qd = heads * HEAD_DIM
    nblk = k_cmp.shape[1]
    return pl.pallas_call(
        functools.partial(_cmp_kernel, heads=heads, tq=tq),
        out_shape=(
            jax.ShapeDtypeStruct((bsz, qd, seq), F32),
            jax.ShapeDtypeStruct((bsz, heads, HEAD_DIM + nsel, seq), BF16),
        ),
        grid=(bsz, seq // tq),
        in_specs=[
            pl.BlockSpec((1, tq, qd), lambda b, i: (b, i, 0)),
            pl.BlockSpec((1, nblk, HEAD_DIM), lambda b, i: (b, 0, 0)),
            pl.BlockSpec((1, nblk, HEAD_DIM), lambda b, i: (b, 0, 0)),
            pl.BlockSpec(overlap.shape, lambda b, i: (0, 0)),
        ],
        out_specs=(
            pl.BlockSpec((1, qd, tq), lambda b, i: (b, 0, i)),
            pl.BlockSpec((1, heads, HEAD_DIM + nsel, tq), lambda b, i: (b, 0, 0, i)),
        ),
        compiler_params=_cparams(("arbitrary", "arbitrary")),
        name="nsa_cmp_select",
    )(q, k_cmp, v_cmp, overlap)


def _attend_kernel(qa_ref, ka_ref, vst_ref, kwa_ref, vwt_ref, gn_ref, ocmp_ref, o_ref,
                   m_ref, l_ref, acc_ref, *, heads, tq, tk, nwin):
    i = pl.program_id(1)
    cols = heads * tq
    qat = jnp.concatenate([qa_ref[0, h] for h in range(heads)], axis=1)
    t_row = i * tq + (lax.broadcasted_iota(jnp.int32, (1, cols), 1) % tq)

    m_ref[...] = jnp.full(m_ref.shape, NEG_INF, F32)
    l_ref[...] = jnp.zeros(l_ref.shape, F32)
    acc_ref[...] = jnp.zeros(acc_ref.shape, F32)
    key_id = lax.broadcasted_iota(jnp.int32, (tk, cols), 0)

    def tile(kt, masked):
        s = _dot(ka_ref[0, kt], qat)
        if masked:
            s = jnp.where(kt * tk + key_id <= t_row, s, NEG_INF)
        m_old = m_ref[...]
        m_new = jnp.maximum(m_old, jnp.max(s, axis=0, keepdims=True))
        alpha = jnp.exp(m_old - m_new)
        p = jnp.exp(s - m_new)
        l_ref[...] = alpha * l_ref[...] + jnp.sum(p, axis=0, keepdims=True)
        acc_ref[...] = alpha * acc_ref[...] + _dot(vst_ref[0, kt], p.astype(BF16))
        m_ref[...] = m_new

    n_full = (i * tq + 1) // tk
    n_all = (i * tq + tq + tk - 1) // tk

    def full_body(kt, carry):
        tile(kt, False)
        return carry

    def diag_body(kt, carry):
        tile(kt, True)
        return carry

    lax.fori_loop(0, n_full, full_body, 0)
    lax.fori_loop(n_full, n_all, diag_body, 0)
    osel = acc_ref[...] / l_ref[...]

    st = jnp.maximum(i + 1 - nwin, 0)
    kw = kwa_ref[0, pl.ds(st, nwin)].reshape(nwin * tq, kwa_ref.shape[3])
    s = _dot(kw, qat)
    dist = t_row - (st * tq + lax.broadcasted_iota(jnp.int32, (nwin * tq, cols), 0))
    s = jnp.where((dist >= 0) & (dist < WINDOW), s, NEG_INF)
    m = jnp.max(s, axis=0, keepdims=True)
    p = jnp.exp(s - m)
    den = jnp.sum(p, axis=0, keepdims=True)
    pb = p.astype(BF16)
    ow = _dot(vwt_ref[0, st], pb[0:tq, :])
    for j in range(1, nwin):
        ow = ow + _dot(vwt_ref[0, st + j], pb[j * tq:(j + 1) * tq, :])
    owin = ow / den

    g_t = _sigmoid(gn_ref[0]).T
    outs = []
    for h in range(heads):
        cs = slice(h * tq, (h + 1) * tq)
        outs.append(g_t[3 * h:3 * h + 1, :] * ocmp_ref[0, h * HEAD_DIM:(h + 1) * HEAD_DIM, :]
                    + g_t[3 * h + 1:3 * h + 2, :] * osel[:, cs]
                    + g_t[3 * h + 2:3 * h + 3, :] * owin[:, cs])
    o_ref[0] = jnp.concatenate(outs, axis=0).T


def _attend(qa, ka, vst, kwa, vwt, z_nsa, gn_block, o_cmp_t, *, tq, tk):
    bsz, heads, ka_w, seq = qa.shape
    qd = heads * HEAD_DIM
    nwin = WINDOW // tq + 1
    return pl.pallas_call(
        functools.partial(_attend_kernel, heads=heads, tq=tq, tk=tk, nwin=nwin),
        out_shape=jax.ShapeDtypeStruct((bsz, seq, qd), F32),
        grid=(bsz, seq // tq),
        in_specs=[
            pl.BlockSpec((1, heads, ka_w, tq), lambda b, i: (b, 0, 0, i)),
            pl.BlockSpec((1,) + ka.shape[1:], lambda b, i: (b, 0, 0, 0)),
            pl.BlockSpec((1,) + vst.shape[1:], lambda b, i: (b, 0, 0, 0)),
            pl.BlockSpec((1,) + kwa.shape[1:], lambda b, i: (b, 0, 0, 0)),
            pl.BlockSpec((1,) + vwt.shape[1:], lambda b, i: (b, 0, 0, 0)),
            pl.BlockSpec((1, tq, LANES), lambda b, i: (b, i, gn_block)),
            pl.BlockSpec((1, qd, tq), lambda b, i: (b, 0, i)),
        ],
        out_specs=pl.BlockSpec((1, tq, qd), lambda b, i: (b, i, 0)),
        scratch_shapes=[
            pltpu.VMEM((1, heads * tq), F32),
            pltpu.VMEM((1, heads * tq), F32),
            pltpu.VMEM((HEAD_DIM, heads * tq), F32),
        ],
        compiler_params=_cparams(("arbitrary", "arbitrary")),
        name="nsa_attend",
    )(qa, ka, vst, kwa, vwt, z_nsa, o_cmp_t)


def _merge_kernel(ys_ref, yb_ref, gm_ref, x_ref, wb_ref, wo_ref, gate_ref, o_ref):
    w = yb_ref.shape[2]
    d = x_ref.shape[2]
    ys = ys_ref[...]
    branches = (ys[:, 0:w], yb_ref[0], ys[:, w:2 * w], ys[:, 2 * w:3 * w])
    merged = None
    for mi, y in enumerate(branches):
        term = _sigmoid(gm_ref[0, :, mi * d:(mi + 1) * d].astype(F32)) * _dot(y.astype(BF16), wb_ref[mi])
        merged = term if merged is None else merged + term
    mix = _dot(merged.astype(BF16), wo_ref[...])
    o_ref[0] = x_ref[0] + gate_ref[0] * mix


def _merge(y_scan_tm, y_b, g_merge, x, w_branch, w_out, gate, *, tm):
    bsz, seq, d = x.shape
    w = y_b.shape[2]
    ys = y_scan_tm.reshape(seq, bsz * 3 * w)
    return pl.pallas_call(
        _merge_kernel,
        out_shape=jax.ShapeDtypeStruct((bsz, seq, d), F32),
        grid=(bsz, seq // tm),
        in_specs=[
            pl.BlockSpec((tm, 3 * w), lambda b, i: (i, b)),
            pl.BlockSpec((1, tm, w), lambda b, i: (b, i, 0)),
            pl.BlockSpec((1, tm, N_BRANCH * d), lambda b, i: (b, i, 0)),
            pl.BlockSpec((1, tm, d), lambda b, i: (b, i, 0)),
            pl.BlockSpec(w_branch.shape, lambda b, i: (0, 0, 0)),
            pl.BlockSpec(w_out.shape, lambda b, i: (0, 0)),
            pl.BlockSpec((1, 1, d), lambda b, i: (b, 0, 0)),
        ],
        out_specs=pl.BlockSpec((1, tm, d), lambda b, i: (b, i, 0)),
        compiler_params=_cparams(("arbitrary", "arbitrary")),
        name="merge_out",
    )(ys, y_b, g_merge, x, w_branch, w_out, gate)


def _final_norm(out, fg_ref):
    var = jnp.mean(out * out, axis=-1, keepdims=True)
    return out * lax.rsqrt(var + RMS_EPS) * fg_ref[...]


def _ffn_kernel(x_ref, g_ref, sc_ref, sh_ref, gate_ref, wg_ref, wu_ref, wd_ref, *rest, final):
    if final:
        fg_ref, o_ref, h_ref, acc_ref = rest
    else:
        o_ref, h_ref, acc_ref = rest
    f = pl.program_id(2)
    nf = pl.num_programs(2)

    @pl.when(f == 0)
    def _():
        h_ref[...] = _norm_mod(x_ref[0], g_ref[...], sc_ref[0], sh_ref[0]).astype(BF16)
        acc_ref[...] = jnp.zeros_like(acc_ref)

    hb = h_ref[...]
    a = (_silu(_dot(hb, wg_ref[...])) * _dot(hb, wu_ref[...])).astype(BF16)
    acc_ref[...] += _dot(a, wd_ref[...])

    @pl.when(f == nf - 1)
    def _():
        out = x_ref[0] + gate_ref[0] * acc_ref[...]
        o_ref[0] = _final_norm(out, fg_ref) if final else out


def _ffn(x, g, scale, shift, gate, weights, *, final_g, tm, tf):
    bsz, seq, d = x.shape
    row = lambda: pl.BlockSpec((1, 1, d), lambda b, i, f: (b, 0, 0))
    wg, wu, wd = weights
    in_specs = [
        pl.BlockSpec((1, tm, d), lambda b, i, f: (b, i, 0)),
        pl.BlockSpec((1, d), lambda b, i, f: (0, 0)),
        row(), row(), row(),
        pl.BlockSpec((d, tf), lambda b, i, f: (0, f)),
        pl.BlockSpec((d, tf), lambda b, i, f: (0, f)),
        pl.BlockSpec((tf, d), lambda b, i, f: (f, 0)),
    ]
    args = [x, g.reshape(1, d), scale, shift, gate, wg, wu, wd]
    final = final_g is not None
    if final:
        in_specs.append(pl.BlockSpec((1, d), lambda b, i, f: (0, 0)))
        args.append(final_g.reshape(1, d))
    return pl.pallas_call(
        functools.partial(_ffn_kernel, final=final),
        out_shape=jax.ShapeDtypeStruct((bsz, seq, d), F32),
        grid=(bsz, seq // tm, wg.shape[1] // tf),
        in_specs=in_specs,
        out_specs=pl.BlockSpec((1, tm, d), lambda b, i, f: (b, i, 0)),
        scratch_shapes=[pltpu.VMEM((tm, d), BF16), pltpu.VMEM((tm, d), F32)],
        compiler_params=_cparams(("arbitrary", "arbitrary", "arbitrary")),
        name="dense_ffn",
    )(*args)


def _route_kernel(x_ref, g_ref, sc_ref, sh_ref, rw_ref, rb_ref, h_ref, r_ref):
    h = _norm_mod(x_ref[0], g_ref[...], sc_ref[0], sh_ref[0])
    h_ref[0] = h.astype(BF16)
    logits = _dot(h, rw_ref[...], HI) + rb_ref[...]
    lane = lax.broadcasted_iota(jnp.int32, logits.shape, 1)
    big = logits.shape[1]
    v1 = jnp.max(logits, axis=-1, keepdims=True)
    i1 = jnp.min(jnp.where(logits == v1, lane, big), axis=-1, keepdims=True)
    rest_l = jnp.where(lane == i1, NEG_INF * 4.0, logits)
    v2 = jnp.max(rest_l, axis=-1, keepdims=True)
    i2 = jnp.min(jnp.where(rest_l == v2, lane, big), axis=-1, keepdims=True)
    e2 = jnp.exp(v2 - v1)
    w1 = 1.0 / (1.0 + e2)
    w2 = e2 / (1.0 + e2)
    r_ref[0] = jnp.where(lane == 0, i1.astype(F32), jnp.where(lane == 1, i2.astype(F32), jnp.where(lane == 2, w1, w2)))


def _route(x, g, scale, shift, rw, rb, *, tm):
    bsz, seq, d = x.shape
    row = lambda: pl.BlockSpec((1, 1, d), lambda b, i: (b, 0, 0))
    return pl.pallas_call(
        _route_kernel,
        out_shape=(jax.ShapeDtypeStruct((bsz, seq, d), BF16), jax.ShapeDtypeStruct((bsz, seq, LANES), F32)),
        grid=(bsz, seq // tm),
        in_specs=[
            pl.BlockSpec((1, tm, d), lambda b, i: (b, i, 0)),
            pl.BlockSpec((1, d), lambda b, i: (0, 0)),
            row(), row(),
            pl.BlockSpec(rw.shape, lambda b, i: (0, 0)),
            pl.BlockSpec(rb.shape, lambda b, i: (0, 0)),
        ],
        out_specs=(pl.BlockSpec((1, tm, d), lambda b, i: (b, i, 0)), pl.BlockSpec((1, tm, LANES), lambda b, i: (b, i, 0))),
        compiler_params=_cparams(("arbitrary", "arbitrary")),
        name="moe_route",
    )(x, g.reshape(1, d), scale, shift, rw, rb)


def _gffn_kernel(te_ref, nu_ref, x_ref, wg_ref, wu_ref, wd_ref, o_ref):
    i = pl.program_id(0)

    @pl.when(i < nu_ref[0])
    def _():
        hb = x_ref[...]
        a = (_silu(_dot(hb, wg_ref[0])) * _dot(hb, wu_ref[0])).astype(BF16)
        o_ref[...] = _dot(a, wd_ref[0])

    @pl.when(i >= nu_ref[0])
    def _():
        o_ref[...] = jnp.zeros_like(o_ref)


def _grouped_ffn(xs, tile_expert, n_used, wg, wu, wd, *, tmg):
    rows, d = xs.shape
    _, _, fe = wg.shape
    return pl.pallas_call(
        _gffn_kernel,
        out_shape=jax.ShapeDtypeStruct((rows, d), F32),
        grid_spec=pltpu.PrefetchScalarGridSpec(
            num_scalar_prefetch=2,
            grid=(rows // tmg,),
            in_specs=[
                pl.BlockSpec((tmg, d), lambda i, te, nu: (i, 0)),
                pl.BlockSpec((1, d, fe), lambda i, te, nu: (te[i], 0, 0)),
                pl.BlockSpec((1, d, fe), lambda i, te, nu: (te[i], 0, 0)),
                pl.BlockSpec((1, fe, d), lambda i, te, nu: (te[i], 0, 0)),
            ],
            out_specs=pl.BlockSpec((tmg, d), lambda i, te, nu: (i, 0)),
        ),
        compiler_params=_cparams(("arbitrary",)),
        name="moe_grouped_ffn",
    )(tile_expert, n_used, xs, wg, wu, wd)


def _combine_kernel(x_ref, gate_ref, y_ref, y2_ref, r_ref, *rest, final):
    if final:
        fg_ref, o_ref = rest
    else:
        (o_ref,) = rest
    r = r_ref[0]
    f = r[:, 2:3] * y_ref[0] + r[:, 3:4] * y2_ref[0]
    out = x_ref[0] + gate_ref[0] * f
    o_ref[0] = _final_norm(out, fg_ref) if final else out


def _combine(x, gate, y1, y2, route, *, final_g, tm):
    bsz, seq, d = x.shape
    tile = lambda: pl.BlockSpec((1, tm, d), lambda b, i: (b, i, 0))
    in_specs = [
        tile(),
        pl.BlockSpec((1, 1, d), lambda b, i: (b, 0, 0)),
        tile(), tile(),
        pl.BlockSpec((1, tm, LANES), lambda b, i: (b, i, 0)),
    ]
    args = [x, gate, y1, y2, route]
    final = final_g is not None
    if final:
        in_specs.append(pl.BlockSpec((1, d), lambda b, i: (0, 0)))
        args.append(final_g.reshape(1, d))
    return pl.pallas_call(
        functools.partial(_combine_kernel, final=final),
        out_shape=jax.ShapeDtypeStruct((bsz, seq, d), F32),
        grid=(bsz, seq // tm),
        in_specs=in_specs,
        out_specs=pl.BlockSpec((1, tm, d), lambda b, i: (b, i, 0)),
        compiler_params=_cparams(("arbitrary", "arbitrary")),
        name="moe_combine",
    )(*args)


def _moe(x, g, scale, shift, gate, rw, rb, wg, wu, wd, *, final_g, tm, tmg):
    bsz, seq, d = x.shape
    ne = wg.shape[0]
    tokens = bsz * seq
    h, route = _route(x, g, scale, shift, rw, rb, tm=tm)
    e_flat = route[:, :, 0:2].astype(jnp.int32).reshape(tokens * 2)
    onehot = (e_flat[:, None] == jnp.arange(ne, dtype=jnp.int32)[None, :]).astype(jnp.int32)
    csum = jnp.cumsum(onehot, axis=0)
    counts = csum[-1]
    padded = ((counts + tmg - 1) // tmg) * tmg
    ends = jnp.cumsum(padded)
    offs = ends - padded
    rank = jnp.take_along_axis(csum, e_flat[:, None], axis=1)[:, 0] - 1
    dest = offs[e_flat] + rank
    rows = tokens * 2 + ne * tmg
    src = jnp.zeros((rows,), jnp.int32).at[dest].set(jnp.arange(tokens * 2, dtype=jnp.int32) // 2)
    tile_start = jnp.arange(rows // tmg, dtype=jnp.int32) * tmg
    tile_expert = jnp.minimum(jnp.sum((tile_start[:, None] >= ends[None, :]).astype(jnp.int32), axis=1), ne - 1)
    n_used = (ends[-1] // tmg).astype(jnp.int32).reshape(1)
    xs = jnp.take(h.reshape(tokens, d), src, axis=0)
    ys = _grouped_ffn(xs, tile_expert, n_used, wg, wu, wd, tmg=tmg)
    dest2 = dest.reshape(tokens, 2)
    y1 = jnp.take(ys, dest2[:, 0], axis=0).reshape(bsz, seq, d)
    y2 = jnp.take(ys, dest2[:, 1], axis=0).reshape(bsz, seq, d)
    return _combine(x, gate, y1, y2, route, final_g=final_g, tm=tm)


def _block_diag(blocks):
    n, r, c = blocks.shape
    eye = jnp.eye(n, dtype=blocks.dtype)
    return (eye[:, None, :, None] * blocks[:, :, None, :]).reshape(n * r, n * c)


def _rope_tables(seq, heads_q):
    inv = ROPE_THETA ** (-jnp.arange(0, HEAD_DIM, 2, dtype=F32) / HEAD_DIM)
    ang = jnp.arange(seq, dtype=F32)[:, None] * inv[None, :]
    cos, sin = jnp.cos(ang), jnp.sin(ang)
    cos_h = jnp.concatenate([cos, cos], axis=1)
    sin_h = jnp.concatenate([-sin, sin], axis=1)
    one_h = jnp.ones_like(cos_h)
    zero_h = jnp.zeros_like(cos_h)
    cos_t = jnp.concatenate([cos_h] * heads_q + [one_h, one_h] + [cos_h, one_h] * 3, axis=1)
    sin_t = jnp.concatenate([sin_h] * heads_q + [zero_h, zero_h] + [sin_h, zero_h] * 3, axis=1)
    return cos_t, sin_t


def kernel(x, c, mod_w, mod_b, norm_mix_g, norm_ffn_g, w_in, s5_lambda_re, s5_lambda_im, s5_log_step, s5_b_re, s5_b_im, s5_c_re, s5_c_im, s5_d, s5_w_glu, nsa_pe_k, nsa_pe_v, nsa_cmp_k_w1, nsa_cmp_k_w2, nsa_cmp_v_w1, nsa_cmp_v_w2, lru_conv_w, lru_conv_b, lru_w_a, lru_b_a, lru_w_x, lru_b_x, lru_lambda, sc_conv_w, w_branch, w_out, ffn_w_gate, ffn_w_up, ffn_w_down, moe_router_w, moe_router_b, moe_w_gate, moe_w_up, moe_w_down, final_norm_g):
    bsz, seq, d = x.shape
    depth = mod_w.shape[0]
    w = d // N_BRANCH
    heads = w // HEAD_DIM
    groups = w // S5_GROUP
    assert bsz == SUBLANES, "time-major scan blocks hold one timestep per sublane tile"
    tm = min(512, seq)
    tq = 128
    steps = min(64, seq)
    nsel = seq // L_SEL
    assert nsel <= LANES and seq % tm == 0 and seq >= WINDOW + tq

    mod = _modulation(c, mod_w, mod_b)
    mod = mod.reshape(depth, bsz, 6, 1, d)

    sizes = (w, w, HEAD_DIM, HEAD_DIM, HEAD_DIM, HEAD_DIM, HEAD_DIM, HEAD_DIM, heads * 3, w, w, w, w, w, N_BRANCH * d)
    offs = np.concatenate([[0], np.cumsum(sizes)])
    col = lambda wmat, k: wmat[:, offs[k]:offs[k + 1]]

    cos_t, sin_t = _rope_tables(seq, heads)
    n_cmp = (seq - L_CMP) // CMP_STRIDE + 1
    nblk = seq // CMP_STRIDE
    cmp_start = np.arange(nblk) * CMP_STRIDE
    sel_start = np.arange(nsel) * L_SEL
    ovl = np.clip(np.minimum(cmp_start[:, None] + L_CMP, sel_start[None, :] + L_SEL)
                  - np.maximum(cmp_start[:, None], sel_start[None, :]), 0, None) / L_CMP
    ovl[n_cmp:] = 0.0
    overlap = jnp.asarray(np.pad(ovl, ((0, 0), (0, LANES - nsel))), F32)
    onehot = jnp.asarray((np.arange(seq)[:, None] // L_SEL) == np.arange(nsel)[None, :], BF16)

    for li in range(depth):
        sh1, sc1, gt1, sh2, sc2, gt2 = (mod[li, :, k] for k in range(6))
        wl = w_in[li]
        w_scan = jnp.concatenate([col(wl, 0), col(wl, 9), col(wl, 10), col(wl, 11), col(wl, 12), col(wl, 13)], axis=1).astype(BF16)
        gpad = jnp.zeros((d, LANES - heads * 3), wl.dtype)
        w_nsa = jnp.concatenate([col(wl, 1), col(wl, 8), gpad] + [col(wl, k) for k in range(2, 8)], axis=1).astype(BF16)
        w_gm = col(wl, 14).astype(BF16)
        g1 = norm_mix_g[li]
        z_scan = _project(x, g1, sc1, sh1, w_scan, time_major=True, out_dtype=F32, tm=tm)
        z_nsa = _project(x, g1, sc1, sh1, w_nsa, time_major=False, out_dtype=F32, tm=tm, rope_tabs=(cos_t, sin_t))
        g_merge = _project(x, g1, sc1, sh1, w_gm, time_major=False, out_dtype=BF16, tm=tm)

        a_re, a_im, bb_re, bb_im = _s5_params(s5_lambda_re[li], s5_lambda_im[li], s5_log_step[li], s5_b_re[li], s5_b_im[li])
        to_bd = lambda m: _block_diag(jnp.swapaxes(m.reshape(groups, S5_STATE, S5_GROUP), 1, 2))
        s5b = jnp.concatenate([to_bd(bb_re), to_bd(bb_im)], axis=1).astype(BF16)
        c_bd = lambda m: _block_diag(jnp.swapaxes(m, 1, 2))
        s5c = jnp.concatenate([c_bd(s5_c_re[li]), -c_bd(s5_c_im[li])], axis=0).astype(BF16)
        r1 = lambda a: a.reshape(1, -1)
        y_scan = _scan_mixers(
            z_scan.reshape(seq * bsz, 6 * w), bsz, s5b, s5c, r1(a_re), r1(a_im), r1(s5_d[li]), s5_w_glu[li].astype(BF16),
            lru_conv_w[li], r1(lru_conv_b[li]), _block_diag(lru_w_a[li]).astype(BF16), r1(lru_b_a[li]),
            _block_diag(lru_w_x[li]).astype(BF16), r1(lru_b_x[li]), r1(lru_lambda[li]), sc_conv_w[li], steps=steps)

        kv0 = heads * HEAD_DIM + LANES
        kvs = [z_nsa[:, :, kv0 + k * HEAD_DIM:kv0 + (k + 1) * HEAD_DIM] for k in range(6)]
        kc, vc, ks, vs, kw, vw = kvs
        grp = lambda a: a.reshape(bsz, nblk, CMP_STRIDE * HEAD_DIM)
        k_cmp, v_cmp = _compress(grp(kc), grp(vc), nsa_pe_k[li], nsa_pe_v[li], nsa_cmp_k_w1[li], nsa_cmp_k_w2[li], nsa_cmp_v_w1[li], nsa_cmp_v_w2[li])
        o_cmp, qa = _cmp_select(z_nsa, k_cmp, v_cmp, overlap, heads=heads, tq=tq, nsel=nsel)
        tk = min(256, seq)
        ka = jnp.concatenate([ks.astype(BF16), jnp.broadcast_to(onehot, (bsz, seq, nsel))], axis=2)
        ka = ka.reshape(bsz, seq // tk, tk, HEAD_DIM + nsel)
        kwa = jnp.concatenate([kw.astype(BF16), jnp.zeros((bsz, seq, nsel), BF16)], axis=2)
        kwa = kwa.reshape(bsz, seq // tq, tq, HEAD_DIM + nsel)
        tiles_t = lambda a, n: jnp.swapaxes(a.astype(BF16).reshape(bsz, seq // n, n, HEAD_DIM), 2, 3)
        y_b = _attend(qa, ka, tiles_t(vs, tk), kwa, tiles_t(vw, tq), z_nsa, (heads * HEAD_DIM) // LANES, o_cmp, tq=tq, tk=tk)

        x = _merge(y_scan, y_b, g_merge, x, w_branch[li].astype(BF16), w_out[li].astype(BF16), gt1, tm=tm)

        final_g = final_norm_g if li == depth - 1 else None
        j = li // 2
        if li % 2 == 0:
            weights = (ffn_w_gate[j].astype(BF16), ffn_w_up[j].astype(BF16), ffn_w_down[j].astype(BF16))
            x = _ffn(x, norm_ffn_g[li], sc2, sh2, gt2, weights, final_g=final_g, tm=tm, tf=ffn_w_gate.shape[2] // 2)
        else:
            rw = jnp.pad(moe_router_w[j], ((0, 0), (0, LANES - N_EXPERTS)))
            rb = jnp.pad(moe_router_b[j], (0, LANES - N_EXPERTS), constant_values=NEG_INF).reshape(1, LANES)
            x = _moe(x, norm_ffn_g[li], sc2, sh2, gt2, rw, rb, moe_w_gate[j].astype(BF16), moe_w_up[j].astype(BF16),
                     moe_w_down[j].astype(BF16), final_g=final_g, tm=tm, tmg=512)
    return x
```

```python
import functools
import math

import jax
import jax.numpy as jnp
import numpy as np
from jax import lax
from jax.experimental import pallas as pl
from jax.experimental.pallas import tpu as pltpu

F32 = jnp.float32
BF16 = jnp.bfloat16

N_BRANCH = 4
S5_GROUP = 16
S5_STATE = 64
HEAD_DIM = 64
L_CMP = 32
CMP_STRIDE = 16
L_SEL = 64
TOP_N = 16
WINDOW = 512
ROPE_THETA = 10000.0
FORCED_SCORE = 1.0e4
NEG_INF = -1.0e30
LRU_CONV = 4
LRU_C = 8.0
SC_CONV = 3
N_EXPERTS = 8
RMS_EPS = 1e-6

LANES = 128
SUBLANES = 8
VMEM_LIMIT = 52 * 1024 * 1024

HI = lax.Precision.HIGHEST


def _cparams(sem):
    return pltpu.CompilerParams(dimension_semantics=sem, vmem_limit_bytes=VMEM_LIMIT)


def _dot(a, b, precision=None):
    return jnp.dot(a, b, preferred_element_type=F32, precision=precision)


def _dot_t(a, b, precision=None):
    return lax.dot_general(a, b, (((1,), (1,)), ((), ())), preferred_element_type=F32, precision=precision)


def _gelu(x):
    return 0.5 * x * (1.0 + jnp.tanh(math.sqrt(2.0 / math.pi) * (x + 0.044715 * (x * x * x))))


def _sigmoid(x):
    return 1.0 / (1.0 + jnp.exp(-x))


def _silu(x):
    return x * _sigmoid(x)


def _norm_mod(x, g, scale, shift):
    var = jnp.mean(x * x, axis=-1, keepdims=True)
    return (x * lax.rsqrt(var + RMS_EPS) * g) * (1.0 + scale) + shift


def _rot_half(z):
    n = z.shape[-1]
    lane = lax.broadcasted_iota(jnp.int32, z.shape, z.ndim - 1)
    first = (lane % HEAD_DIM) < (HEAD_DIM // 2)
    return jnp.where(first, pltpu.roll(z, n - HEAD_DIM // 2, z.ndim - 1), pltpu.roll(z, HEAD_DIM // 2, z.ndim - 1))


def _mod_kernel(c_ref, w_ref, b_ref, o_ref):
    c = c_ref[...]
    o_ref[0] = _dot(_silu(c), w_ref[0], HI) + b_ref[0]


def _modulation(c, mod_w, mod_b):
    depth, d, n = mod_w.shape
    bsz = c.shape[0]
    tn = 1024
    return pl.pallas_call(
        _mod_kernel,
        out_shape=jax.ShapeDtypeStruct((depth, bsz, n), F32),
        grid=(depth, n // tn),
        in_specs=[
            pl.BlockSpec((bsz, d), lambda l, j: (0, 0)),
            pl.BlockSpec((1, d, tn), lambda l, j: (l, 0, j)),
            pl.BlockSpec((1, 1, tn), lambda l, j: (l, 0, j)),
        ],
        out_specs=pl.BlockSpec((1, bsz, tn), lambda l, j: (l, 0, j)),
        compiler_params=_cparams(("arbitrary", "arbitrary")),
        name="modulation",
    )(c, mod_w, mod_b.reshape(depth, 1, n))


def _proj_kernel(x_ref, g_ref, sc_ref, sh_ref, w_ref, *rest, rope, nchunk):
    if rope:
        cos_ref, sin_ref, o_ref = rest
    else:
        (o_ref,) = rest
    h = _norm_mod(x_ref[0], g_ref[...], sc_ref[0], sh_ref[0]).astype(BF16)
    n = w_ref.shape[0]
    for n0 in range(0, n, nchunk):
        z = _dot_t(h, w_ref[n0:n0 + nchunk, :])
        if rope:
            z = z * cos_ref[:, n0:n0 + nchunk] + _rot_half(z) * sin_ref[:, n0:n0 + nchunk]
        o_ref[0, :, n0:n0 + nchunk] = z.astype(o_ref.dtype)


def _proj_tm_kernel(x_ref, g_ref, sc_ref, sh_ref, w_ref, o_ref, h_ref, *, nchunk):
    nb, tt, _ = x_ref.shape
    nlc = h_ref.shape[0]
    for b in range(nb):
        hb = _norm_mod(x_ref[b], g_ref[...], sc_ref[b], sh_ref[b])
        for c in range(nlc):
            h_ref[c, pl.ds(b, tt, stride=nb), :] = hb[:, c * LANES:(c + 1) * LANES]
    h = jnp.concatenate([h_ref[c] for c in range(nlc)], axis=1).astype(BF16)
    for n0 in range(0, w_ref.shape[0], nchunk):
        o_ref[:, n0:n0 + nchunk] = _dot_t(h, w_ref[n0:n0 + nchunk, :])


def _project_tm(x, g, scale, shift, w, *, tt):
    bsz, seq, d = x.shape
    n = w.shape[0]
    return pl.pallas_call(
        functools.partial(_proj_tm_kernel, nchunk=512),
        out_shape=jax.ShapeDtypeStruct((seq * bsz, n), F32),
        grid=(seq // tt,),
        in_specs=[
            pl.BlockSpec((bsz, tt, d), lambda i: (0, i, 0)),
            pl.BlockSpec((1, d), lambda i: (0, 0)),
            pl.BlockSpec((bsz, 1, d), lambda i: (0, 0, 0)),
            pl.BlockSpec((bsz, 1, d), lambda i: (0, 0, 0)),
            pl.BlockSpec((n, d), lambda i: (0, 0)),
        ],
        out_specs=pl.BlockSpec((tt * bsz, n), lambda i: (i, 0)),
        scratch_shapes=[pltpu.VMEM((d // LANES, tt * bsz, LANES), F32)],
        compiler_params=_cparams(("arbitrary",)),
        name="proj_tm",
    )(x, g.reshape(1, d), scale, shift, w)


def _project(x, g, scale, shift, w, *, out_dtype, tm, rope_tabs=None):
    bsz, seq, d = x.shape
    n = w.shape[0]
    rope = rope_tabs is not None
    in_specs = [
        pl.BlockSpec((1, tm, d), lambda b, i: (b, i, 0)),
        pl.BlockSpec((1, d), lambda b, i: (0, 0)),
        pl.BlockSpec((1, 1, d), lambda b, i: (b, 0, 0)),
        pl.BlockSpec((1, 1, d), lambda b, i: (b, 0, 0)),
        pl.BlockSpec((n, d), lambda b, i: (0, 0)),
    ]
    args = [x, g.reshape(1, d), scale, shift, w]
    if rope:
        in_specs += [pl.BlockSpec((tm, n), lambda b, i: (i, 0))] * 2
        args += list(rope_tabs)
    nchunk = 256 if rope else 512
    return pl.pallas_call(
        functools.partial(_proj_kernel, rope=rope, nchunk=nchunk),
        out_shape=jax.ShapeDtypeStruct((bsz, seq, n), out_dtype),
        grid=(bsz, seq // tm),
        in_specs=in_specs,
        out_specs=pl.BlockSpec((1, tm, n), lambda b, i: (b, i, 0)),
        compiler_params=_cparams(("arbitrary", "arbitrary")),
        name="proj_rope" if rope else "proj_bm",
    )(*args)


def _s5_param_kernel(lre_ref, lim_ref, ls_ref, bre_ref, bim_ref, are_ref, aim_ref, bbre_ref, bbim_ref):
    lre = lre_ref[...]
    lim = lim_ref[...]
    step = jnp.exp(ls_ref[...])
    mag = jnp.exp(lre * step)
    ar = mag * jnp.cos(lim * step)
    ai = mag * jnp.sin(lim * step)
    xr = ar - 1.0
    den = lre * lre + lim * lim
    cr = (xr * lre + ai * lim) / den
    ci = (ai * lre - xr * lim) / den
    bre = bre_ref[...]
    bim = bim_ref[...]
    are_ref[...] = ar
    aim_ref[...] = ai
    bbre_ref[...] = cr * bre - ci * bim
    bbim_ref[...] = cr * bim + ci * bre


def _s5_params(lam_re, lam_im, log_step, b_re, b_im):
    g, n = lam_re.shape
    c = b_re.shape[-1]
    rows = g * n
    col = lambda a: a.reshape(rows, 1)
    ls = jnp.broadcast_to(log_step[:, None], (g, n))
    out = pl.pallas_call(
        _s5_param_kernel,
        out_shape=(
            jax.ShapeDtypeStruct((rows, 1), F32),
            jax.ShapeDtypeStruct((rows, 1), F32),
            jax.ShapeDtypeStruct((rows, c), F32),
            jax.ShapeDtypeStruct((rows, c), F32),
        ),
        name="s5_params",
    )(col(lam_re), col(lam_im), col(ls), b_re.reshape(rows, c), b_im.reshape(rows, c))
    return out


def _scan_kernel(z_ref, s5b_ref, s5c_ref, are_ref, aim_ref, d_ref, wglu_ref,
                 cw_ref, cb_ref, wa_ref, ba_ref, wx_ref, bx_ref, lam_ref, scw_ref,
                 o_ref,
                 bu_ref, hre_ref, him_ref, xe_ref, hl_ref, pe_ref, ab_ref, bb_ref, *, nb, steps):
    c = pl.program_id(0)
    rows = nb * steps
    w = d_ref.shape[1]
    ns = are_ref.shape[1]

    @pl.when(c == 0)
    def _():
        hre_ref[...] = jnp.zeros_like(hre_ref)
        him_ref[...] = jnp.zeros_like(him_ref)
        hl_ref[...] = jnp.zeros_like(hl_ref)
        xe_ref[0:(LRU_CONV - 1) * nb, :] = jnp.zeros(((LRU_CONV - 1) * nb, w), F32)
        pe_ref[0:(SC_CONV - 1) * nb, :] = jnp.zeros(((SC_CONV - 1) * nb, w), F32)

    u = z_ref[:, 0:w]
    bu_ref[...] = _dot(u.astype(BF16), s5b_ref[...])
    a_re = jnp.broadcast_to(are_ref[...], (nb, ns))
    a_im = jnp.broadcast_to(aim_ref[...], (nb, ns))

    def s5_step(t, carry):
        h_re, h_im = carry
        r0 = pl.multiple_of(t * nb, nb)
        b_re = bu_ref[pl.ds(r0, nb), 0:ns]
        b_im = bu_ref[pl.ds(r0, nb), ns:2 * ns]
        n_re = a_re * h_re - a_im * h_im + b_re
        n_im = a_re * h_im + a_im * h_re + b_im
        bu_ref[pl.ds(r0, nb), 0:ns] = n_re
        bu_ref[pl.ds(r0, nb), ns:2 * ns] = n_im
        return n_re, n_im

    h_re, h_im = lax.fori_loop(0, steps, s5_step, (hre_ref[...], him_ref[...]))
    hre_ref[...] = h_re
    him_ref[...] = h_im
    y = _dot(bu_ref[...].astype(BF16), s5c_ref[...]) + d_ref[...] * u
    zg = _gelu(y)
    def put(branch, val):
        for k in range(w // LANES):
            o_ref[branch * (w // LANES) + k] = val[:, k * LANES:(k + 1) * LANES]

    put(0, zg * _sigmoid(_dot(zg.astype(BF16), wglu_ref[...])))

    nt = (LRU_CONV - 1) * nb
    x = z_ref[:, w:2 * w]
    xe_ref[nt:nt + rows, :] = x
    xc = cb_ref[...] + cw_ref[LRU_CONV - 1:LRU_CONV, :] * x
    for k in range(LRU_CONV - 1):
        xc = xc + cw_ref[k:k + 1, :] * xe_ref[k * nb:k * nb + rows, :]
    xe_ref[0:nt, :] = xe_ref[rows:rows + nt, :]
    xcb = xc.astype(BF16)
    r = _sigmoid(_dot(xcb, wa_ref[...]) + ba_ref[...])
    gi = _sigmoid(_dot(xcb, wx_ref[...]) + bx_ref[...])
    nl = -lam_ref[...]
    softplus = jnp.maximum(nl, 0.0) + jnp.log(1.0 + jnp.exp(-jnp.abs(nl)))
    log_a = (-LRU_C) * r * softplus
    ab_ref[...] = jnp.exp(log_a)
    bb_ref[...] = jnp.sqrt(1.0 - jnp.exp(2.0 * log_a)) * (gi * xc)

    def lru_step(t, h):
        r0 = pl.multiple_of(t * nb, nb)
        h = ab_ref[pl.ds(r0, nb), :] * h + bb_ref[pl.ds(r0, nb), :]
        bb_ref[pl.ds(r0, nb), :] = h
        return h

    hl_ref[...] = lax.fori_loop(0, steps, lru_step, hl_ref[...])
    put(1, bb_ref[...] * _gelu(z_ref[:, 2 * w:3 * w]))

    npt = (SC_CONV - 1) * nb
    p = z_ref[:, 4 * w:5 * w] * z_ref[:, 5 * w:6 * w]
    pe_ref[npt:npt + rows, :] = p
    cv = scw_ref[SC_CONV - 1:SC_CONV, :] * p
    for k in range(SC_CONV - 1):
        cv = cv + scw_ref[k:k + 1, :] * pe_ref[k * nb:k * nb + rows, :]
    pe_ref[0:npt, :] = pe_ref[rows:rows + npt, :]
    put(2, z_ref[:, 3 * w:4 * w] * cv)


def _scan_mixers(z_tm, nb, s5b, s5c, a_re, a_im, d_skip, w_glu, conv_w, conv_b, wa, ba, wx, bx, lam, sc_w, *, steps):
    total, zc = z_tm.shape
    rows = nb * steps
    w = zc // 6
    ns = a_re.shape[1]
    full = lambda a: pl.BlockSpec(a.shape, lambda c: (0,) * a.ndim)
    params = [s5b, s5c, a_re, a_im, d_skip, w_glu, conv_w, conv_b, wa, ba, wx, bx, lam, sc_w]
    return pl.pallas_call(
        functools.partial(_scan_kernel, nb=nb, steps=steps),
        out_shape=jax.ShapeDtypeStruct((3 * w // LANES, total, LANES), F32),
        grid=(total // rows,),
        in_specs=[pl.BlockSpec((rows, zc), lambda c: (c, 0))] + [full(a) for a in params],
        out_specs=pl.BlockSpec((3 * w // LANES, rows, LANES), lambda c: (0, c, 0)),
        scratch_shapes=[
            pltpu.VMEM((rows, 2 * ns), F32),
            pltpu.VMEM((nb, ns), F32),
            pltpu.VMEM((nb, ns), F32),
            pltpu.VMEM((rows + (LRU_CONV - 1) * nb, w), F32),
            pltpu.VMEM((nb, w), F32),
            pltpu.VMEM((rows + (SC_CONV - 1) * nb, w), F32),
            pltpu.VMEM((rows, w), F32),
            pltpu.VMEM((rows, w), F32),
        ],
        compiler_params=_cparams(("arbitrary",)),
        name="scan_mixers",
    )(z_tm, *params)


def _compress_kernel(gk_ref, gv_ref, pek_ref, pev_ref, wk1_ref, wk2_ref, wv1_ref, wv2_ref, ko_ref, vo_ref):
    half = gk_ref.shape[2]
    nblk = gk_ref.shape[1]

    def phi(g, pe, w1_ref, w2_ref):
        top = _dot(g, w1_ref[0:half, :], HI)
        bot = _dot(g, w1_ref[half:2 * half, :], HI)
        hid = top + pltpu.roll(bot, nblk - 1, 0) + _dot(pe, w1_ref[...], HI)
        return _dot(_gelu(hid), w2_ref[...], HI)

    ko_ref[0] = phi(gk_ref[0], pek_ref[...], wk1_ref, wk2_ref)
    vo_ref[0] = phi(gv_ref[0], pev_ref[...], wv1_ref, wv2_ref)


def _compress(gk, gv, pe_k, pe_v, wk1, wk2, wv1, wv2):
    bsz, nblk, half = gk.shape
    hd = wk2.shape[1]
    full = lambda a: pl.BlockSpec(a.shape, lambda b: (0,) * a.ndim)
    pek = pe_k.reshape(1, -1)
    pev = pe_v.reshape(1, -1)
    params = [pek, pev, wk1, wk2, wv1, wv2]
    blk = pl.BlockSpec((1, nblk, half), lambda b: (b, 0, 0))
    oblk = pl.BlockSpec((1, nblk, hd), lambda b: (b, 0, 0))
    return pl.pallas_call(
        _compress_kernel,
        out_shape=(jax.ShapeDtypeStruct((bsz, nblk, hd), F32),) * 2,
        grid=(bsz,),
        in_specs=[blk, blk] + [full(a) for a in params],
        out_specs=(oblk, oblk),
        compiler_params=_cparams(("arbitrary",)),
        name="nsa_compress",
    )(gk, gv, *params)


def _split_bf16(a):
    hi = a.astype(BF16)
    return hi, (a - hi.astype(F32)).astype(BF16)


def _cmp_kernel(q_ref, kc_ref, vct_ref, ovt_ref, ocmp_ref, qa_ref, *, heads, tq):
    i = pl.program_id(1)
    q_t = (q_ref[0] * HEAD_DIM ** -0.5).T
    kc_hi, kc_lo = _split_bf16(kc_ref[0])
    vct = vct_ref[0]
    nblk = kc_hi.shape[0]
    t = i * tq + lax.broadcasted_iota(jnp.int32, (nblk, tq), 1)
    n_id = lax.broadcasted_iota(jnp.int32, (nblk, tq), 0)
    visible = (n_id * CMP_STRIDE + (L_CMP - 1)) <= t
    psum = jnp.zeros((nblk, tq), F32)
    for h in range(heads):
        q_hi, q_lo = _split_bf16(q_t[h * HEAD_DIM:(h + 1) * HEAD_DIM, :])
        s = _dot(kc_hi, q_hi) + (_dot(kc_hi, q_lo) + _dot(kc_lo, q_hi))
        s = jnp.where(visible, s, NEG_INF)
        m = jnp.max(s, axis=0, keepdims=True)
        e = jnp.where(visible, jnp.exp(s - m), 0.0)
        den = jnp.sum(e, axis=0, keepdims=True)
        p = e / jnp.where(den > 0.0, den, 1.0)
        psum = psum + p
        ocmp_ref[0, h * HEAD_DIM:(h + 1) * HEAD_DIM, :] = _dot(vct, p.astype(BF16))

    p_hi, p_lo = _split_bf16(psum)
    imp_t = _dot(ovt_ref[...], p_hi) + _dot(ovt_ref[...], p_lo)
    nslot = ovt_ref.shape[0]
    tt = i * tq + lax.broadcasted_iota(jnp.int32, (nslot, tq), 1)
    j = lax.broadcasted_iota(jnp.int32, (nslot, tq), 0)
    cur = tt // L_SEL
    forced = (j == 0) | (j == cur) | (j == cur - 1)
    future = (j * L_SEL) > tt
    imp_t = jnp.where(future, -1.0, jnp.where(forced, FORCED_SCORE, imp_t))
    nreal = qa_ref.shape[2] - HEAD_DIM
    cand = imp_t[0:nreal, :]
    jrow = lax.broadcasted_iota(jnp.int32, (nreal, tq), 0)
    rank = jnp.zeros((nreal, tq), jnp.int32)
    for k in range(nreal):
        rk = imp_t[k:k + 1, :]
        beats = (rk > cand) | ((rk == cand) & (jrow > k))
        rank = rank + beats.astype(jnp.int32)
    bias_t = jnp.where(rank < TOP_N, 0.0, NEG_INF)
    for h in range(heads):
        qa_ref[0, h] = jnp.concatenate([q_t[h * HEAD_DIM:(h + 1) * HEAD_DIM, :], bias_t], axis=0).astype(BF16)


def _cmp_select(q, k_cmp, v_cmp_t, overlap_t, *, heads, tq, nsel):
    bsz, seq, _ = q.shape
    qd = heads * HEAD_DIM
    nblk = k_cmp.shape[1]
    return pl.pallas_call(
        functools.partial(_cmp_kernel, heads=heads, tq=tq),
        out_shape=(
            jax.ShapeDtypeStruct((bsz, qd, seq), F32),
            jax.ShapeDtypeStruct((bsz, heads, HEAD_DIM + nsel, seq), BF16),
        ),
        grid=(bsz, seq // tq),
        in_specs=[
            pl.BlockSpec((1, tq, qd), lambda b, i: (b, i, 0)),
            pl.BlockSpec((1, nblk, HEAD_DIM), lambda b, i: (b, 0, 0)),
            pl.BlockSpec((1, HEAD_DIM, nblk), lambda b, i: (b, 0, 0)),
            pl.BlockSpec(overlap_t.shape, lambda b, i: (0, 0)),
        ],
        out_specs=(
            pl.BlockSpec((1, qd, tq), lambda b, i: (b, 0, i)),
            pl.BlockSpec((1, heads, HEAD_DIM + nsel, tq), lambda b, i: (b, 0, 0, i)),
        ),
        compiler_params=_cparams(("arbitrary", "arbitrary")),
        name="nsa_cmp_select",
    )(q, k_cmp, v_cmp_t, overlap_t)


def _attend_kernel(qa_ref, ka_ref, vst_ref, kwa_ref, vwt_ref, gn_ref, ocmp_ref, o_ref,
                   m_ref, l_ref, acc_ref, *, heads, tq, tk, nwin):
    i = pl.program_id(1)
    cols = heads * tq
    qat = jnp.concatenate([qa_ref[0, h] for h in range(heads)], axis=1)
    t_row = i * tq + (lax.broadcasted_iota(jnp.int32, (1, cols), 1) % tq)

    m_ref[...] = jnp.full(m_ref.shape, NEG_INF, F32)
    l_ref[...] = jnp.zeros(l_ref.shape, F32)
    acc_ref[...] = jnp.zeros(acc_ref.shape, F32)
    key_id = lax.broadcasted_iota(jnp.int32, (tk, cols), 0)

    def tile(kt, masked):
        s = _dot(ka_ref[0, kt], qat)
        if masked:
            s = jnp.where(kt * tk + key_id <= t_row, s, NEG_INF)
        m_old = m_ref[...]
        m_new = jnp.maximum(m_old, jnp.max(s, axis=0, keepdims=True))
        alpha = jnp.exp(m_old - m_new)
        p = jnp.exp(s - m_new)
        l_ref[...] = alpha * l_ref[...] + jnp.sum(p, axis=0, keepdims=True)
        acc_ref[...] = alpha * acc_ref[...] + _dot(vst_ref[0, kt], p.astype(BF16))
        m_ref[...] = m_new

    n_full = (i * tq + 1) // tk
    n_all = (i * tq + tq + tk - 1) // tk

    def full_body(kt, carry):
        tile(kt, False)
        return carry

    def diag_body(kt, carry):
        tile(kt, True)
        return carry

    lax.fori_loop(0, n_full, full_body, 0)
    lax.fori_loop(n_full, n_all, diag_body, 0)
    osel = acc_ref[...] / l_ref[...]

    st = jnp.maximum(i + 1 - nwin, 0)
    kw = kwa_ref[0, pl.ds(st, nwin)].reshape(nwin * tq, kwa_ref.shape[3])
    s = _dot(kw, qat)
    dist = t_row - (st * tq + lax.broadcasted_iota(jnp.int32, (nwin * tq, cols), 0))
    s = jnp.where((dist >= 0) & (dist < WINDOW), s, NEG_INF)
    m = jnp.max(s, axis=0, keepdims=True)
    p = jnp.exp(s - m)
    den = jnp.sum(p, axis=0, keepdims=True)
    pb = p.astype(BF16)
    ow = _dot(vwt_ref[0, st], pb[0:tq, :])
    for j in range(1, nwin):
        ow = ow + _dot(vwt_ref[0, st + j], pb[j * tq:(j + 1) * tq, :])
    owin = ow / den

    g_t = _sigmoid(gn_ref[0]).T
    outs = []
    for h in range(heads):
        cs = slice(h * tq, (h + 1) * tq)
        outs.append(g_t[3 * h:3 * h + 1, :] * ocmp_ref[0, h * HEAD_DIM:(h + 1) * HEAD_DIM, :]
                    + g_t[3 * h + 1:3 * h + 2, :] * osel[:, cs]
                    + g_t[3 * h + 2:3 * h + 3, :] * owin[:, cs])
    o_ref[0] = jnp.concatenate(outs, axis=0).T


def _attend(qa, ka, vst, kwa, vwt, z_nsa, gn_block, o_cmp_t, *, tq, tk):
    bsz, heads, ka_w, seq = qa.shape
    qd = heads * HEAD_DIM
    nwin = WINDOW // tq + 1
    return pl.pallas_call(
        functools.partial(_attend_kernel, heads=heads, tq=tq, tk=tk, nwin=nwin),
        out_shape=jax.ShapeDtypeStruct((bsz, seq, qd), F32),
        grid=(bsz, seq // tq),
        in_specs=[
            pl.BlockSpec((1, heads, ka_w, tq), lambda b, i: (b, 0, 0, i)),
            pl.BlockSpec((1,) + ka.shape[1:], lambda b, i: (b, 0, 0, 0)),
            pl.BlockSpec((1,) + vst.shape[1:], lambda b, i: (b, 0, 0, 0)),
            pl.BlockSpec((1,) + kwa.shape[1:], lambda b, i: (b, 0, 0, 0)),
            pl.BlockSpec((1,) + vwt.shape[1:], lambda b, i: (b, 0, 0, 0)),
            pl.BlockSpec((1, tq, LANES), lambda b, i: (b, i, gn_block)),
            pl.BlockSpec((1, qd, tq), lambda b, i: (b, 0, i)),
        ],
        out_specs=pl.BlockSpec((1, tq, qd), lambda b, i: (b, i, 0)),
        scratch_shapes=[
            pltpu.VMEM((1, heads * tq), F32),
            pltpu.VMEM((1, heads * tq), F32),
            pltpu.VMEM((HEAD_DIM, heads * tq), F32),
        ],
        compiler_params=_cparams(("arbitrary", "arbitrary")),
        name="nsa_attend",
    )(qa, ka, vst, kwa, vwt, z_nsa, o_cmp_t)


def _merge_kernel(ys_ref, yb_ref, gm_ref, x_ref, wb_ref, wo_ref, gate_ref, o_ref):
    nb, tt, w = yb_ref.shape
    d = x_ref.shape[2]
    rows = nb * tt
    def slab(c):
        return jnp.concatenate([ys_ref[c, pl.ds(b, tt, stride=nb), :] for b in range(nb)], axis=0)

    ys = jnp.concatenate([slab(c) for c in range(ys_ref.shape[0])], axis=1)
    branches = (ys[:, 0:w], yb_ref[...].reshape(rows, w), ys[:, w:2 * w], ys[:, 2 * w:3 * w])
    merged = None
    for mi, y in enumerate(branches):
        gm = gm_ref[:, :, mi * d:(mi + 1) * d].reshape(rows, d)
        term = _sigmoid(gm.astype(F32)) * _dot(y.astype(BF16), wb_ref[mi])
        merged = term if merged is None else merged + term
    mix = _dot(merged.astype(BF16), wo_ref[...]).reshape(nb, tt, d)
    o_ref[...] = x_ref[...] + gate_ref[...] * mix


def _merge(y_scan_tm, y_b, g_merge, x, w_branch, w_out, gate, *, tt):
    bsz, seq, d = x.shape
    w = y_b.shape[2]
    tile = lambda n: pl.BlockSpec((bsz, tt, n), lambda i: (0, i, 0))
    return pl.pallas_call(
        _merge_kernel,
        out_shape=jax.ShapeDtypeStruct((bsz, seq, d), F32),
        grid=(seq // tt,),
        in_specs=[
            pl.BlockSpec((3 * w // LANES, tt * bsz, LANES), lambda i: (0, i, 0)),
            tile(w),
            tile(N_BRANCH * d),
            tile(d),
            pl.BlockSpec(w_branch.shape, lambda i: (0, 0, 0)),
            pl.BlockSpec(w_out.shape, lambda i: (0, 0)),
            pl.BlockSpec((bsz, 1, d), lambda i: (0, 0, 0)),
        ],
        out_specs=tile(d),
        compiler_params=_cparams(("arbitrary",)),
        name="merge_out",
    )(y_scan_tm, y_b, g_merge, x, w_branch, w_out, gate)


def _final_norm(out, fg_ref):
    var = jnp.mean(out * out, axis=-1, keepdims=True)
    return out * lax.rsqrt(var + RMS_EPS) * fg_ref[...]


def _ffn_kernel(x_ref, g_ref, sc_ref, sh_ref, gate_ref, wg_ref, wu_ref, wd_ref, *rest, final):
    if final:
        fg_ref, o_ref, h_ref, acc_ref = rest
    else:
        o_ref, h_ref, acc_ref = rest
    f = pl.program_id(2)
    nf = pl.num_programs(2)

    @pl.when(f == 0)
    def _():
        h_ref[...] = _norm_mod(x_ref[0], g_ref[...], sc_ref[0], sh_ref[0]).astype(BF16)
        acc_ref[...] = jnp.zeros_like(acc_ref)

    hb = h_ref[...]
    a = (_silu(_dot(hb, wg_ref[...])) * _dot(hb, wu_ref[...])).astype(BF16)
    acc_ref[...] += _dot(a, wd_ref[...])

    @pl.when(f == nf - 1)
    def _():
        out = x_ref[0] + gate_ref[0] * acc_ref[...]
        o_ref[0] = _final_norm(out, fg_ref) if final else out


def _ffn(x, g, scale, shift, gate, weights, *, final_g, tm, tf):
    bsz, seq, d = x.shape
    row = lambda: pl.BlockSpec((1, 1, d), lambda b, i, f: (b, 0, 0))
    wg, wu, wd = weights
    in_specs = [
        pl.BlockSpec((1, tm, d), lambda b, i, f: (b, i, 0)),
        pl.BlockSpec((1, d), lambda b, i, f: (0, 0)),
        row(), row(), row(),
        pl.BlockSpec((d, tf), lambda b, i, f: (0, f)),
        pl.BlockSpec((d, tf), lambda b, i, f: (0, f)),
        pl.BlockSpec((tf, d), lambda b, i, f: (f, 0)),
    ]
    args = [x, g.reshape(1, d), scale, shift, gate, wg, wu, wd]
    final = final_g is not None
    if final:
        in_specs.append(pl.BlockSpec((1, d), lambda b, i, f: (0, 0)))
        args.append(final_g.reshape(1, d))
    return pl.pallas_call(
        functools.partial(_ffn_kernel, final=final),
        out_shape=jax.ShapeDtypeStruct((bsz, seq, d), F32),
        grid=(bsz, seq // tm, wg.shape[1] // tf),
        in_specs=in_specs,
        out_specs=pl.BlockSpec((1, tm, d), lambda b, i, f: (b, i, 0)),
        scratch_shapes=[pltpu.VMEM((tm, d), BF16), pltpu.VMEM((tm, d), F32)],
        compiler_params=_cparams(("arbitrary", "arbitrary", "arbitrary")),
        name="dense_ffn",
    )(*args)


def _route_kernel(x_ref, g_ref, sc_ref, sh_ref, rw_ref, rb_ref, h_ref, r_ref):
    h = _norm_mod(x_ref[0], g_ref[...], sc_ref[0], sh_ref[0])
    h_ref[0] = h.astype(BF16)
    logits = _dot(h, rw_ref[...], HI) + rb_ref[...]
    lane = lax.broadcasted_iota(jnp.int32, logits.shape, 1)
    big = logits.shape[1]
    v1 = jnp.max(logits, axis=-1, keepdims=True)
    i1 = jnp.min(jnp.where(logits == v1, lane, big), axis=-1, keepdims=True)
    rest_l = jnp.where(lane == i1, NEG_INF * 4.0, logits)
    v2 = jnp.max(rest_l, axis=-1, keepdims=True)
    i2 = jnp.min(jnp.where(rest_l == v2, lane, big), axis=-1, keepdims=True)
    e2 = jnp.exp(v2 - v1)
    w1 = 1.0 / (1.0 + e2)
    w2 = e2 / (1.0 + e2)
    r_ref[0] = jnp.where(lane == 0, i1.astype(F32), jnp.where(lane == 1, i2.astype(F32), jnp.where(lane == 2, w1, w2)))


def _route(x, g, scale, shift, rw, rb, *, tm):
    bsz, seq, d = x.shape
    row = lambda: pl.BlockSpec((1, 1, d), lambda b, i: (b, 0, 0))
    return pl.pallas_call(
        _route_kernel,
        out_shape=(jax.ShapeDtypeStruct((bsz, seq, d), BF16), jax.ShapeDtypeStruct((bsz, seq, LANES), F32)),
        grid=(bsz, seq // tm),
        in_specs=[
            pl.BlockSpec((1, tm, d), lambda b, i: (b, i, 0)),
            pl.BlockSpec((1, d), lambda b, i: (0, 0)),
            row(), row(),
            pl.BlockSpec(rw.shape, lambda b, i: (0, 0)),
            pl.BlockSpec(rb.shape, lambda b, i: (0, 0)),
        ],
        out_specs=(pl.BlockSpec((1, tm, d), lambda b, i: (b, i, 0)), pl.BlockSpec((1, tm, LANES), lambda b, i: (b, i, 0))),
        compiler_params=_cparams(("arbitrary", "arbitrary")),
        name="moe_route",
    )(x, g.reshape(1, d), scale, shift, rw, rb)


def _gffn_kernel(te_ref, nu_ref, x_ref, wg_ref, wu_ref, wd_ref, o_ref):
    i = pl.program_id(0)

    @pl.when(i < nu_ref[0])
    def _():
        hb = x_ref[...]
        a = (_silu(_dot(hb, wg_ref[0])) * _dot(hb, wu_ref[0])).astype(BF16)
        o_ref[...] = _dot(a, wd_ref[0])

    @pl.when(i >= nu_ref[0])
    def _():
        o_ref[...] = jnp.zeros_like(o_ref)


def _grouped_ffn(xs, tile_expert, n_used, wg, wu, wd, *, tmg):
    rows, d = xs.shape
    _, _, fe = wg.shape
    return pl.pallas_call(
        _gffn_kernel,
        out_shape=jax.ShapeDtypeStruct((rows, d), F32),
        grid_spec=pltpu.PrefetchScalarGridSpec(
            num_scalar_prefetch=2,
            grid=(rows // tmg,),
            in_specs=[
                pl.BlockSpec((tmg, d), lambda i, te, nu: (i, 0)),
                pl.BlockSpec((1, d, fe), lambda i, te, nu: (te[i], 0, 0)),
                pl.BlockSpec((1, d, fe), lambda i, te, nu: (te[i], 0, 0)),
                pl.BlockSpec((1, fe, d), lambda i, te, nu: (te[i], 0, 0)),
            ],
            out_specs=pl.BlockSpec((tmg, d), lambda i, te, nu: (i, 0)),
        ),
        compiler_params=_cparams(("arbitrary",)),
        name="moe_grouped_ffn",
    )(tile_expert, n_used, xs, wg, wu, wd)


def _combine_kernel(x_ref, gate_ref, y_ref, y2_ref, r_ref, *rest, final):
    if final:
        fg_ref, o_ref = rest
    else:
        (o_ref,) = rest
    r = r_ref[0]
    f = r[:, 2:3] * y_ref[0] + r[:, 3:4] * y2_ref[0]
    out = x_ref[0] + gate_ref[0] * f
    o_ref[0] = _final_norm(out, fg_ref) if final else out


def _combine(x, gate, y1, y2, route, *, final_g, tm):
    bsz, seq, d = x.shape
    tile = lambda: pl.BlockSpec((1, tm, d), lambda b, i: (b, i, 0))
    in_specs = [
        tile(),
        pl.BlockSpec((1, 1, d), lambda b, i: (b, 0, 0)),
        tile(), tile(),
        pl.BlockSpec((1, tm, LANES), lambda b, i: (b, i, 0)),
    ]
    args = [x, gate, y1, y2, route]
    final = final_g is not None
    if final:
        in_specs.append(pl.BlockSpec((1, d), lambda b, i: (0, 0)))
        args.append(final_g.reshape(1, d))
    return pl.pallas_call(
        functools.partial(_combine_kernel, final=final),
        out_shape=jax.ShapeDtypeStruct((bsz, seq, d), F32),
        grid=(bsz, seq // tm),
        in_specs=in_specs,
        out_specs=pl.BlockSpec((1, tm, d), lambda b, i: (b, i, 0)),
        compiler_params=_cparams(("arbitrary", "arbitrary")),
        name="moe_combine",
    )(*args)


def _moe(x, g, scale, shift, gate, rw, rb, wg, wu, wd, *, final_g, tm, tmg):
    bsz, seq, d = x.shape
    ne = wg.shape[0]
    tokens = bsz * seq
    h, route = _route(x, g, scale, shift, rw, rb, tm=tm)
    e_flat = route[:, :, 0:2].astype(jnp.int32).reshape(tokens * 2)
    onehot = (e_flat[:, None] == jnp.arange(ne, dtype=jnp.int32)[None, :]).astype(jnp.int32)
    csum = jnp.cumsum(onehot, axis=0)
    counts = csum[-1]
    padded = ((counts + tmg - 1) // tmg) * tmg
    ends = jnp.cumsum(padded)
    offs = ends - padded
    rank = jnp.take_along_axis(csum, e_flat[:, None], axis=1)[:, 0] - 1
    dest = offs[e_flat] + rank
    rows = tokens * 2 + ne * tmg
    src = jnp.zeros((rows,), jnp.int32).at[dest].set(jnp.arange(tokens * 2, dtype=jnp.int32) // 2,
                                                      unique_indices=True, mode="promise_in_bounds")
    tile_start = jnp.arange(rows // tmg, dtype=jnp.int32) * tmg
    tile_expert = jnp.minimum(jnp.sum((tile_start[:, None] >= ends[None, :]).astype(jnp.int32), axis=1), ne - 1)
    n_used = (ends[-1] // tmg).astype(jnp.int32).reshape(1)
    rows_of = lambda a, idx: a.at[idx].get(mode="promise_in_bounds")
    xs = rows_of(h.reshape(tokens, d), src)
    ys = _grouped_ffn(xs, tile_expert, n_used, wg, wu, wd, tmg=tmg)
    dest2 = dest.reshape(tokens, 2)
    y1 = rows_of(ys, dest2[:, 0]).reshape(bsz, seq, d)
    y2 = rows_of(ys, dest2[:, 1]).reshape(bsz, seq, d)
    return _combine(x, gate, y1, y2, route, final_g=final_g, tm=tm)


def _block_diag(blocks):
    n, r, c = blocks.shape
    eye = jnp.eye(n, dtype=blocks.dtype)
    return (eye[:, None, :, None] * blocks[:, :, None, :]).reshape(n * r, n * c)


def _rope_tables(seq, heads_q):
    inv = ROPE_THETA ** (-jnp.arange(0, HEAD_DIM, 2, dtype=F32) / HEAD_DIM)
    ang = jnp.arange(seq, dtype=F32)[:, None] * inv[None, :]
    cos, sin = jnp.cos(ang), jnp.sin(ang)
    cos_h = jnp.concatenate([cos, cos], axis=1)
    sin_h = jnp.concatenate([-sin, sin], axis=1)
    one_h = jnp.ones_like(cos_h)
    zero_h = jnp.zeros_like(cos_h)
    cos_t = jnp.concatenate([cos_h] * heads_q + [one_h, one_h] + [cos_h, one_h] * 3, axis=1)
    sin_t = jnp.concatenate([sin_h] * heads_q + [zero_h, zero_h] + [sin_h, zero_h] * 3, axis=1)
    return cos_t, sin_t


def kernel(x, c, mod_w, mod_b, norm_mix_g, norm_ffn_g, w_in, s5_lambda_re, s5_lambda_im, s5_log_step, s5_b_re, s5_b_im, s5_c_re, s5_c_im, s5_d, s5_w_glu, nsa_pe_k, nsa_pe_v, nsa_cmp_k_w1, nsa_cmp_k_w2, nsa_cmp_v_w1, nsa_cmp_v_w2, lru_conv_w, lru_conv_b, lru_w_a, lru_b_a, lru_w_x, lru_b_x, lru_lambda, sc_conv_w, w_branch, w_out, ffn_w_gate, ffn_w_up, ffn_w_down, moe_router_w, moe_router_b, moe_w_gate, moe_w_up, moe_w_down, final_norm_g):
    bsz, seq, d = x.shape
    depth = mod_w.shape[0]
    w = d // N_BRANCH
    heads = w // HEAD_DIM
    groups = w // S5_GROUP
    assert bsz == SUBLANES, "time-major scan blocks hold one timestep per sublane tile"
    tm = min(512, seq)
    tq = 128
    steps = min(64, seq)
    nsel = seq // L_SEL
    assert nsel <= LANES and seq % tm == 0 and seq >= WINDOW + tq

    mod = _modulation(c, mod_w, mod_b)
    mod = mod.reshape(depth, bsz, 6, 1, d)

    sizes = (w, w, HEAD_DIM, HEAD_DIM, HEAD_DIM, HEAD_DIM, HEAD_DIM, HEAD_DIM, heads * 3, w, w, w, w, w, N_BRANCH * d)
    offs = np.concatenate([[0], np.cumsum(sizes)])
    col = lambda wmat_t, k: wmat_t[offs[k]:offs[k + 1], :]

    cos_t, sin_t = _rope_tables(seq, heads)
    n_cmp = (seq - L_CMP) // CMP_STRIDE + 1
    nblk = seq // CMP_STRIDE
    cmp_start = np.arange(nblk) * CMP_STRIDE
    sel_start = np.arange(nsel) * L_SEL
    ovl = np.clip(np.minimum(cmp_start[:, None] + L_CMP, sel_start[None, :] + L_SEL)
                  - np.maximum(cmp_start[:, None], sel_start[None, :]), 0, None) / L_CMP
    ovl[n_cmp:] = 0.0
    overlap_t = jnp.asarray(np.pad(ovl, ((0, 0), (0, LANES - nsel))).T, BF16)
    onehot = jnp.asarray((np.arange(seq)[:, None] // L_SEL) == np.arange(nsel)[None, :], BF16)

    for li in range(depth):
        sh1, sc1, gt1, sh2, sc2, gt2 = (mod[li, :, k] for k in range(6))
        wl = w_in[li].T
        w_scan = jnp.concatenate([col(wl, 0), col(wl, 9), col(wl, 10), col(wl, 11), col(wl, 12), col(wl, 13)], axis=0).astype(BF16)
        gpad = jnp.zeros((LANES - heads * 3, d), wl.dtype)
        w_nsa = jnp.concatenate([col(wl, 1), col(wl, 8), gpad] + [col(wl, k) for k in range(2, 8)], axis=0).astype(BF16)
        w_gm = col(wl, 14).astype(BF16)
        g1 = norm_mix_g[li]
        z_scan = _project_tm(x, g1, sc1, sh1, w_scan, tt=steps)
        z_nsa = _project(x, g1, sc1, sh1, w_nsa, out_dtype=F32, tm=tm, rope_tabs=(cos_t, sin_t))
        g_merge = _project(x, g1, sc1, sh1, w_gm, out_dtype=BF16, tm=tm)

        a_re, a_im, bb_re, bb_im = _s5_params(s5_lambda_re[li], s5_lambda_im[li], s5_log_step[li], s5_b_re[li], s5_b_im[li])
        to_bd = lambda m: _block_diag(jnp.swapaxes(m.reshape(groups, S5_STATE, S5_GROUP), 1, 2))
        s5b = jnp.concatenate([to_bd(bb_re), to_bd(bb_im)], axis=1).astype(BF16)
        c_bd = lambda m: _block_diag(jnp.swapaxes(m, 1, 2))
        s5c = jnp.concatenate([c_bd(s5_c_re[li]), -c_bd(s5_c_im[li])], axis=0).astype(BF16)
        r1 = lambda a: a.reshape(1, -1)
        y_scan = _scan_mixers(
            z_scan, bsz, s5b, s5c, r1(a_re), r1(a_im), r1(s5_d[li]), s5_w_glu[li].astype(BF16),
            lru_conv_w[li], r1(lru_conv_b[li]), _block_diag(lru_w_a[li]).astype(BF16), r1(lru_b_a[li]),
            _block_diag(lru_w_x[li]).astype(BF16), r1(lru_b_x[li]), r1(lru_lambda[li]), sc_conv_w[li], steps=steps)

        kv0 = heads * HEAD_DIM + LANES
        kvs = [z_nsa[:, :, kv0 + k * HEAD_DIM:kv0 + (k + 1) * HEAD_DIM] for k in range(6)]
        kc, vc, ks, vs, kw, vw = kvs
        grp = lambda a: a.reshape(bsz, nblk, CMP_STRIDE * HEAD_DIM)
        k_cmp, v_cmp = _compress(grp(kc), grp(vc), nsa_pe_k[li], nsa_pe_v[li], nsa_cmp_k_w1[li], nsa_cmp_k_w2[li], nsa_cmp_v_w1[li], nsa_cmp_v_w2[li])
        o_cmp, qa = _cmp_select(z_nsa, k_cmp, jnp.swapaxes(v_cmp, 1, 2).astype(BF16), overlap_t, heads=heads, tq=tq, nsel=nsel)
        tk = min(512, seq)
        ka = jnp.concatenate([ks.astype(BF16), jnp.broadcast_to(onehot, (bsz, seq, nsel))], axis=2)
        ka = ka.reshape(bsz, seq // tk, tk, HEAD_DIM + nsel)
        kwa = jnp.concatenate([kw.astype(BF16), jnp.zeros((bsz, seq, nsel), BF16)], axis=2)
        kwa = kwa.reshape(bsz, seq // tq, tq, HEAD_DIM + nsel)
        tiles_t = lambda a, n: jnp.swapaxes(a.astype(BF16).reshape(bsz, seq // n, n, HEAD_DIM), 2, 3)
        y_b = _attend(qa, ka, tiles_t(vs, tk), kwa, tiles_t(vw, tq), z_nsa, (heads * HEAD_DIM) // LANES, o_cmp, tq=tq, tk=tk)

        x = _merge(y_scan, y_b, g_merge, x, w_branch[li].astype(BF16), w_out[li].astype(BF16), gt1, tt=steps)

        final_g = final_norm_g if li == depth - 1 else None
        j = li // 2
        if li % 2 == 0:
            weights = (ffn_w_gate[j].astype(BF16), ffn_w_up[j].astype(BF16), ffn_w_down[j].astype(BF16))
            x = _ffn(x, norm_ffn_g[li], sc2, sh2, gt2, weights, final_g=final_g, tm=tm, tf=ffn_w_gate.shape[2] // 2)
        else:
            rw = jnp.pad(moe_router_w[j], ((0, 0), (0, LANES - N_EXPERTS)))
            rb = jnp.pad(moe_router_b[j], (0, LANES - N_EXPERTS), constant_values=NEG_INF).reshape(1, LANES)
            x = _moe(x, norm_ffn_g[li], sc2, sh2, gt2, rw, rb, moe_w_gate[j].astype(BF16), moe_w_up[j].astype(BF16),
                     moe_w_down[j].astype(BF16), final_g=final_g, tm=tm, tmg=512)
    return x
```

```python
import functools
import math

import jax
import jax.numpy as jnp
import numpy as np
from jax import lax
from jax.experimental import pallas as pl
from jax.experimental.pallas import tpu as pltpu

F32 = jnp.float32
BF16 = jnp.bfloat16

N_BRANCH = 4
S5_GROUP = 16
S5_STATE = 64
HEAD_DIM = 64
L_CMP = 32
CMP_STRIDE = 16
L_SEL = 64
TOP_N = 16
WINDOW = 512
ROPE_THETA = 10000.0
FORCED_SCORE = 1.0e4
NEG_INF = -1.0e30
LRU_CONV = 4
LRU_C = 8.0
SC_CONV = 3
N_EXPERTS = 8
RMS_EPS = 1e-6

LANES = 128
SUBLANES = 8
MXU_WIDTH = 256
VMEM_LIMIT = 52 * 1024 * 1024

HI = lax.Precision.HIGHEST


def _cparams(sem):
    return pltpu.CompilerParams(dimension_semantics=sem, vmem_limit_bytes=VMEM_LIMIT)


def _dot(a, b, precision=None):
    return jnp.dot(a, b, preferred_element_type=F32, precision=precision)


def _dot_t(a, b, precision=None):
    return lax.dot_general(a, b, (((1,), (1,)), ((), ())), preferred_element_type=F32, precision=precision)


def _gelu(x):
    return 0.5 * x * (1.0 + jnp.tanh(math.sqrt(2.0 / math.pi) * (x + 0.044715 * (x * x * x))))


def _sigmoid(x):
    return 1.0 / (1.0 + jnp.exp(-x))


def _silu(x):
    return x * _sigmoid(x)


def _norm_mod(x, g, scale, shift):
    var = jnp.mean(x * x, axis=-1, keepdims=True)
    return (x * lax.rsqrt(var + RMS_EPS) * g) * (1.0 + scale) + shift


def _rot_half(z):
    n = z.shape[-1]
    lane = lax.broadcasted_iota(jnp.int32, z.shape, z.ndim - 1)
    first = (lane % HEAD_DIM) < (HEAD_DIM // 2)
    return jnp.where(first, pltpu.roll(z, n - HEAD_DIM // 2, z.ndim - 1), pltpu.roll(z, HEAD_DIM // 2, z.ndim - 1))


def _mod_kernel(c_ref, w_ref, b_ref, o_ref):
    c = c_ref[...]
    o_ref[0] = _dot(_silu(c), w_ref[0], HI) + b_ref[0]


def _modulation(c, mod_w, mod_b):
    depth, d, n = mod_w.shape
    bsz = c.shape[0]
    tn = 1024
    return pl.pallas_call(
        _mod_kernel,
        out_shape=jax.ShapeDtypeStruct((depth, bsz, n), F32),
        grid=(depth, n // tn),
        in_specs=[
            pl.BlockSpec((bsz, d), lambda l, j: (0, 0)),
            pl.BlockSpec((1, d, tn), lambda l, j: (l, 0, j)),
            pl.BlockSpec((1, 1, tn), lambda l, j: (l, 0, j)),
        ],
        out_specs=pl.BlockSpec((1, bsz, tn), lambda l, j: (l, 0, j)),
        compiler_params=_cparams(("arbitrary", "arbitrary")),
        name="modulation",
    )(c, mod_w, mod_b.reshape(depth, 1, n))


def _proj_kernel(x_ref, g_ref, sc_ref, sh_ref, w_ref, *rest, rope, nchunk):
    if rope:
        cos_ref, sin_ref, o_ref = rest
    else:
        (o_ref,) = rest
    h = _norm_mod(x_ref[0], g_ref[...], sc_ref[0], sh_ref[0]).astype(BF16)
    n = w_ref.shape[0]
    for n0 in range(0, n, nchunk):
        z = _dot_t(h, w_ref[n0:n0 + nchunk, :])
        if rope:
            z = z * cos_ref[:, n0:n0 + nchunk] + _rot_half(z) * sin_ref[:, n0:n0 + nchunk]
        o_ref[0, :, n0:n0 + nchunk] = z.astype(o_ref.dtype)


def _proj_tm_kernel(x_ref, g_ref, sc_ref, sh_ref, w_ref, o_ref, h_ref, *, nchunk):
    nb, tt, _ = x_ref.shape
    nlc = h_ref.shape[0]
    for b in range(nb):
        hb = _norm_mod(x_ref[b], g_ref[...], sc_ref[b], sh_ref[b])
        for c in range(nlc):
            h_ref[c, pl.ds(b, tt, stride=nb), :] = hb[:, c * LANES:(c + 1) * LANES]
    h = jnp.concatenate([h_ref[c] for c in range(nlc)], axis=1).astype(BF16)
    for n0 in range(0, w_ref.shape[0], nchunk):
        o_ref[:, n0:n0 + nchunk] = _dot_t(h, w_ref[n0:n0 + nchunk, :])


def _project_tm(x, g, scale, shift, w, *, tt):
    bsz, seq, d = x.shape
    n = w.shape[0]
    return pl.pallas_call(
        functools.partial(_proj_tm_kernel, nchunk=512),
        out_shape=jax.ShapeDtypeStruct((seq * bsz, n), F32),
        grid=(seq // tt,),
        in_specs=[
            pl.BlockSpec((bsz, tt, d), lambda i: (0, i, 0)),
            pl.BlockSpec((1, d), lambda i: (0, 0)),
            pl.BlockSpec((bsz, 1, d), lambda i: (0, 0, 0)),
            pl.BlockSpec((bsz, 1, d), lambda i: (0, 0, 0)),
            pl.BlockSpec((n, d), lambda i: (0, 0)),
        ],
        out_specs=pl.BlockSpec((tt * bsz, n), lambda i: (i, 0)),
        scratch_shapes=[pltpu.VMEM((d // LANES, tt * bsz, LANES), F32)],
        compiler_params=_cparams(("arbitrary",)),
        name="proj_tm",
    )(x, g.reshape(1, d), scale, shift, w)


def _project(x, g, scale, shift, w, *, out_dtype, tm, rope_tabs=None):
    bsz, seq, d = x.shape
    n = w.shape[0]
    rope = rope_tabs is not None
    in_specs = [
        pl.BlockSpec((1, tm, d), lambda b, i: (b, i, 0)),
        pl.BlockSpec((1, d), lambda b, i: (0, 0)),
        pl.BlockSpec((1, 1, d), lambda b, i: (b, 0, 0)),
        pl.BlockSpec((1, 1, d), lambda b, i: (b, 0, 0)),
        pl.BlockSpec((n, d), lambda b, i: (0, 0)),
    ]
    args = [x, g.reshape(1, d), scale, shift, w]
    if rope:
        in_specs += [pl.BlockSpec((tm, n), lambda b, i: (i, 0))] * 2
        args += list(rope_tabs)
    nchunk = 256 if rope else 512
    return pl.pallas_call(
        functools.partial(_proj_kernel, rope=rope, nchunk=nchunk),
        out_shape=jax.ShapeDtypeStruct((bsz, seq, n), out_dtype),
        grid=(bsz, seq // tm),
        in_specs=in_specs,
        out_specs=pl.BlockSpec((1, tm, n), lambda b, i: (b, i, 0)),
        compiler_params=_cparams(("arbitrary", "arbitrary")),
        name="proj_rope" if rope else "proj_bm",
    )(*args)


def _s5_param_kernel(lre_ref, lim_ref, ls_ref, bre_ref, bim_ref, are_ref, aim_ref, bbre_ref, bbim_ref):
    lre = lre_ref[...]
    lim = lim_ref[...]
    step = jnp.exp(ls_ref[...])
    mag = jnp.exp(lre * step)
    ar = mag * jnp.cos(lim * step)
    ai = mag * jnp.sin(lim * step)
    xr = ar - 1.0
    den = lre * lre + lim * lim
    cr = (xr * lre + ai * lim) / den
    ci = (ai * lre - xr * lim) / den
    bre = bre_ref[...]
    bim = bim_ref[...]
    are_ref[...] = ar
    aim_ref[...] = ai
    bbre_ref[...] = cr * bre - ci * bim
    bbim_ref[...] = cr * bim + ci * bre


def _s5_params(lam_re, lam_im, log_step, b_re, b_im):
    g, n = lam_re.shape
    c = b_re.shape[-1]
    rows = g * n
    col = lambda a: a.reshape(rows, 1)
    ls = jnp.broadcast_to(log_step[:, None], (g, n))
    out = pl.pallas_call(
        _s5_param_kernel,
        out_shape=(
            jax.ShapeDtypeStruct((rows, 1), F32),
            jax.ShapeDtypeStruct((rows, 1), F32),
            jax.ShapeDtypeStruct((rows, c), F32),
            jax.ShapeDtypeStruct((rows, c), F32),
        ),
        name="s5_params",
    )(col(lam_re), col(lam_im), col(ls), b_re.reshape(rows, c), b_im.reshape(rows, c))
    return out


def _scan_kernel(z_ref, s5b_ref, s5c_ref, are_ref, aim_ref, d_ref, wglu_ref,
                 cw_ref, cb_ref, wa_ref, ba_ref, wx_ref, bx_ref, lam_ref, scw_ref,
                 o_ref,
                 bu_ref, hre_ref, him_ref, xe_ref, hl_ref, pe_ref, ab_ref, bb_ref, *, nb, steps):
    c = pl.program_id(0)
    rows = nb * steps
    w = d_ref.shape[1]
    ns = are_ref.shape[1]

    @pl.when(c == 0)
    def _():
        hre_ref[...] = jnp.zeros_like(hre_ref)
        him_ref[...] = jnp.zeros_like(him_ref)
        hl_ref[...] = jnp.zeros_like(hl_ref)
        xe_ref[0:(LRU_CONV - 1) * nb, :] = jnp.zeros(((LRU_CONV - 1) * nb, w), F32)
        pe_ref[0:(SC_CONV - 1) * nb, :] = jnp.zeros(((SC_CONV - 1) * nb, w), F32)

    u = z_ref[:, 0:w]
    bu_ref[...] = _dot(u.astype(BF16), s5b_ref[...])
    a_re = jnp.broadcast_to(are_ref[...], (nb, ns))
    a_im = jnp.broadcast_to(aim_ref[...], (nb, ns))

    def s5_step(t, carry):
        h_re, h_im = carry
        r0 = pl.multiple_of(t * nb, nb)
        b_re = bu_ref[pl.ds(r0, nb), 0:ns]
        b_im = bu_ref[pl.ds(r0, nb), ns:2 * ns]
        n_re = a_re * h_re - a_im * h_im + b_re
        n_im = a_re * h_im + a_im * h_re + b_im
        bu_ref[pl.ds(r0, nb), 0:ns] = n_re
        bu_ref[pl.ds(r0, nb), ns:2 * ns] = n_im
        return n_re, n_im

    h_re, h_im = lax.fori_loop(0, steps, s5_step, (hre_ref[...], him_ref[...]))
    hre_ref[...] = h_re
    him_ref[...] = h_im
    y = _dot(bu_ref[...].astype(BF16), s5c_ref[...]) + d_ref[...] * u
    zg = _gelu(y)
    def put(branch, val):
        for k in range(w // LANES):
            o_ref[branch * (w // LANES) + k] = val[:, k * LANES:(k + 1) * LANES]

    put(0, zg * _sigmoid(_dot(zg.astype(BF16), wglu_ref[...])))

    nt = (LRU_CONV - 1) * nb
    x = z_ref[:, w:2 * w]
    xe_ref[nt:nt + rows, :] = x
    xc = cb_ref[...] + cw_ref[LRU_CONV - 1:LRU_CONV, :] * x
    for k in range(LRU_CONV - 1):
        xc = xc + cw_ref[k:k + 1, :] * xe_ref[k * nb:k * nb + rows, :]
    xe_ref[0:nt, :] = xe_ref[rows:rows + nt, :]
    xcb = xc.astype(BF16)
    r = _sigmoid(_dot(xcb, wa_ref[...]) + ba_ref[...])
    gi = _sigmoid(_dot(xcb, wx_ref[...]) + bx_ref[...])
    nl = -lam_ref[...]
    softplus = jnp.maximum(nl, 0.0) + jnp.log(1.0 + jnp.exp(-jnp.abs(nl)))
    log_a = (-LRU_C) * r * softplus
    ab_ref[...] = jnp.exp(log_a)
    bb_ref[...] = jnp.sqrt(1.0 - jnp.exp(2.0 * log_a)) * (gi * xc)

    def lru_step(t, h):
        r0 = pl.multiple_of(t * nb, nb)
        h = ab_ref[pl.ds(r0, nb), :] * h + bb_ref[pl.ds(r0, nb), :]
        bb_ref[pl.ds(r0, nb), :] = h
        return h

    hl_ref[...] = lax.fori_loop(0, steps, lru_step, hl_ref[...])
    put(1, bb_ref[...] * _gelu(z_ref[:, 2 * w:3 * w]))

    npt = (SC_CONV - 1) * nb
    p = z_ref[:, 4 * w:5 * w] * z_ref[:, 5 * w:6 * w]
    pe_ref[npt:npt + rows, :] = p
    cv = scw_ref[SC_CONV - 1:SC_CONV, :] * p
    for k in range(SC_CONV - 1):
        cv = cv + scw_ref[k:k + 1, :] * pe_ref[k * nb:k * nb + rows, :]
    pe_ref[0:npt, :] = pe_ref[rows:rows + npt, :]
    put(2, z_ref[:, 3 * w:4 * w] * cv)


def _scan_mixers(z_tm, nb, s5b, s5c, a_re, a_im, d_skip, w_glu, conv_w, conv_b, wa, ba, wx, bx, lam, sc_w, *, steps):
    total, zc = z_tm.shape
    rows = nb * steps
    w = zc // 6
    ns = a_re.shape[1]
    full = lambda a: pl.BlockSpec(a.shape, lambda c: (0,) * a.ndim)
    params = [s5b, s5c, a_re, a_im, d_skip, w_glu, conv_w, conv_b, wa, ba, wx, bx, lam, sc_w]
    return pl.pallas_call(
        functools.partial(_scan_kernel, nb=nb, steps=steps),
        out_shape=jax.ShapeDtypeStruct((3 * w // LANES, total, LANES), F32),
        grid=(total // rows,),
        in_specs=[pl.BlockSpec((rows, zc), lambda c: (c, 0))] + [full(a) for a in params],
        out_specs=pl.BlockSpec((3 * w // LANES, rows, LANES), lambda c: (0, c, 0)),
        scratch_shapes=[
            pltpu.VMEM((rows, 2 * ns), F32),
            pltpu.VMEM((nb, ns), F32),
            pltpu.VMEM((nb, ns), F32),
            pltpu.VMEM((rows + (LRU_CONV - 1) * nb, w), F32),
            pltpu.VMEM((nb, w), F32),
            pltpu.VMEM((rows + (SC_CONV - 1) * nb, w), F32),
            pltpu.VMEM((rows, w), F32),
            pltpu.VMEM((rows, w), F32),
        ],
        compiler_params=_cparams(("arbitrary",)),
        name="scan_mixers",
    )(z_tm, *params)


def _compress_kernel(gk_ref, gv_ref, pek_ref, pev_ref, wk1_ref, wk2_ref, wv1_ref, wv2_ref, ko_ref, vo_ref):
    half = gk_ref.shape[2]
    nblk = gk_ref.shape[1]

    def phi(g, pe, w1_ref, w2_ref):
        top = _dot(g, w1_ref[0:half, :], HI)
        bot = _dot(g, w1_ref[half:2 * half, :], HI)
        hid = top + pltpu.roll(bot, nblk - 1, 0) + _dot(pe, w1_ref[...], HI)
        return _dot(_gelu(hid), w2_ref[...], HI)

    ko_ref[0] = phi(gk_ref[0], pek_ref[...], wk1_ref, wk2_ref)
    vo_ref[0] = phi(gv_ref[0], pev_ref[...], wv1_ref, wv2_ref)


def _compress(gk, gv, pe_k, pe_v, wk1, wk2, wv1, wv2):
    bsz, nblk, half = gk.shape
    hd = wk2.shape[1]
    full = lambda a: pl.BlockSpec(a.shape, lambda b: (0,) * a.ndim)
    pek = pe_k.reshape(1, -1)
    pev = pe_v.reshape(1, -1)
    params = [pek, pev, wk1, wk2, wv1, wv2]
    blk = pl.BlockSpec((1, nblk, half), lambda b: (b, 0, 0))
    oblk = pl.BlockSpec((1, nblk, hd), lambda b: (b, 0, 0))
    return pl.pallas_call(
        _compress_kernel,
        out_shape=(jax.ShapeDtypeStruct((bsz, nblk, hd), F32),) * 2,
        grid=(bsz,),
        in_specs=[blk, blk] + [full(a) for a in params],
        out_specs=(oblk, oblk),
        compiler_params=_cparams(("arbitrary",)),
        name="nsa_compress",
    )(gk, gv, *params)


def _split_bf16(a):
    hi = a.astype(BF16)
    return hi, (a - hi.astype(F32)).astype(BF16)


def _cmp_kernel(q_ref, kc_ref, vct_ref, ovt_ref, ocmp_ref, qa_ref, *, heads, tq):
    i = pl.program_id(1)
    q_t = (q_ref[0] * HEAD_DIM ** -0.5).T
    kc_hi, kc_lo = _split_bf16(kc_ref[0])
    vct = vct_ref[0]
    nblk = kc_hi.shape[0]
    t = i * tq + lax.broadcasted_iota(jnp.int32, (nblk, tq), 1)
    n_id = lax.broadcasted_iota(jnp.int32, (nblk, tq), 0)
    visible = (n_id * CMP_STRIDE + (L_CMP - 1)) <= t
    psum = jnp.zeros((nblk, tq), F32)
    for h in range(heads):
        q_hi, q_lo = _split_bf16(q_t[h * HEAD_DIM:(h + 1) * HEAD_DIM, :])
        s = _dot(kc_hi, q_hi) + (_dot(kc_hi, q_lo) + _dot(kc_lo, q_hi))
        s = jnp.where(visible, s, NEG_INF)
        m = jnp.max(s, axis=0, keepdims=True)
        e = jnp.where(visible, jnp.exp(s - m), 0.0)
        den = jnp.sum(e, axis=0, keepdims=True)
        p = e / jnp.where(den > 0.0, den, 1.0)
        psum = psum + p
        ocmp_ref[0, h * HEAD_DIM:(h + 1) * HEAD_DIM, :] = _dot(vct, p.astype(BF16))

    p_hi, p_lo = _split_bf16(psum)
    imp_t = _dot(ovt_ref[...], p_hi) + _dot(ovt_ref[...], p_lo)
    nslot = ovt_ref.shape[0]
    tt = i * tq + lax.broadcasted_iota(jnp.int32, (nslot, tq), 1)
    j = lax.broadcasted_iota(jnp.int32, (nslot, tq), 0)
    cur = tt // L_SEL
    forced = (j == 0) | (j == cur) | (j == cur - 1)
    future = (j * L_SEL) > tt
    imp_t = jnp.where(future, -1.0, jnp.where(forced, FORCED_SCORE, imp_t))
    nreal = qa_ref.shape[2] - HEAD_DIM
    cand = imp_t[0:nreal, :]
    jrow = lax.broadcasted_iota(jnp.int32, (nreal, tq), 0)
    rank = jnp.zeros((nreal, tq), jnp.int32)
    for k in range(nreal):
        rk = imp_t[k:k + 1, :]
        beats = (rk > cand) | ((rk == cand) & (jrow > k))
        rank = rank + beats.astype(jnp.int32)
    bias_t = jnp.where(rank < TOP_N, 0.0, NEG_INF)
    for h in range(heads):
        qa_ref[0, h] = jnp.concatenate([q_t[h * HEAD_DIM:(h + 1) * HEAD_DIM, :], bias_t], axis=0).astype(BF16)


def _cmp_select(q, k_cmp, v_cmp_t, overlap_t, *, heads, tq, nsel):
    bsz, seq, _ = q.shape
    qd = heads * HEAD_DIM
    nblk = k_cmp.shape[1]
    return pl.pallas_call(
        functools.partial(_cmp_kernel, heads=heads, tq=tq),
        out_shape=(
            jax.ShapeDtypeStruct((bsz, qd, seq), F32),
            jax.ShapeDtypeStruct((bsz, heads, HEAD_DIM + nsel, seq), BF16),
        ),
        grid=(bsz, seq // tq),
        in_specs=[
            pl.BlockSpec((1, tq, qd), lambda b, i: (b, i, 0)),
            pl.BlockSpec((1, nblk, HEAD_DIM), lambda b, i: (b, 0, 0)),
            pl.BlockSpec((1, HEAD_DIM, nblk), lambda b, i: (b, 0, 0)),
            pl.BlockSpec(overlap_t.shape, lambda b, i: (0, 0)),
        ],
        out_specs=(
            pl.BlockSpec((1, qd, tq), lambda b, i: (b, 0, i)),
            pl.BlockSpec((1, heads, HEAD_DIM + nsel, tq), lambda b, i: (b, 0, 0, i)),
        ),
        compiler_params=_cparams(("arbitrary", "arbitrary")),
        name="nsa_cmp_select",
    )(q, k_cmp, v_cmp_t, overlap_t)


def _attend_kernel(qa_ref, ka_ref, vst_ref, kwa_ref, vwt_ref, gn_ref, ocmp_ref, o_ref,
                   m_ref, acc_ref, s0_ref, s1_ref, *, heads, tq, tk, nwin):
    i = pl.program_id(1)
    cols = heads * tq
    qat = jnp.concatenate([qa_ref[0, h] for h in range(heads)], axis=1)
    t_row = i * tq + (lax.broadcasted_iota(jnp.int32, (1, cols), 1) % tq)

    m_ref[...] = jnp.full(m_ref.shape, NEG_INF, F32)
    acc_ref[...] = jnp.zeros(acc_ref.shape, F32)
    key_id = lax.broadcasted_iota(jnp.int32, (tk, cols), 0)
    n_full = (i * tq + 1) // tk
    s0_ref[...] = _dot(ka_ref[0, 0], qat)

    def softmax_tile(kt, s, masked):
        if masked:
            s = jnp.where(kt * tk + key_id <= t_row, s, NEG_INF)
        m_old = m_ref[...]
        m_new = jnp.maximum(m_old, jnp.max(s, axis=0, keepdims=True))
        p = jnp.exp(s - m_new)
        acc_ref[...] = jnp.exp(m_old - m_new) * acc_ref[...] + _dot(vst_ref[0, kt], p.astype(BF16))
        m_ref[...] = m_new

    def pair_body(j, carry):
        s1_ref[...] = _dot(ka_ref[0, 2 * j + 1], qat)
        softmax_tile(2 * j, s0_ref[...], False)
        s0_ref[...] = _dot(ka_ref[0, 2 * j + 2], qat)
        softmax_tile(2 * j + 1, s1_ref[...], False)
        return carry

    n_pair = n_full // 2
    lax.fori_loop(0, n_pair, pair_body, 0)
    s1_ref[...] = _dot(ka_ref[0, 2 * n_pair + 1], qat)
    softmax_tile(2 * n_pair, s0_ref[...], True)
    softmax_tile(2 * n_pair + 1, s1_ref[...], True)

    osel = acc_ref[0:HEAD_DIM, :] / acc_ref[HEAD_DIM:HEAD_DIM + 1, :]

    st = jnp.maximum(i + 1 - nwin, 0)
    kw = kwa_ref[0, pl.ds(st, nwin)].reshape(nwin * tq, kwa_ref.shape[3])
    s = _dot(kw, qat)
    dist = t_row - (st * tq + lax.broadcasted_iota(jnp.int32, (nwin * tq, cols), 0))
    s = jnp.where((dist >= 0) & (dist < WINDOW), s, NEG_INF)
    m = jnp.max(s, axis=0, keepdims=True)
    pb = jnp.exp(s - m).astype(BF16)
    ow = _dot(vwt_ref[0, st], pb[0:tq, :])
    for j in range(1, nwin):
        ow = ow + _dot(vwt_ref[0, st + j], pb[j * tq:(j + 1) * tq, :])
    owin = ow[0:HEAD_DIM, :] / ow[HEAD_DIM:HEAD_DIM + 1, :]

    g_t = _sigmoid(gn_ref[0]).T
    outs = []
    for h in range(heads):
        cs = slice(h * tq, (h + 1) * tq)
        outs.append(g_t[3 * h:3 * h + 1, :] * ocmp_ref[0, h * HEAD_DIM:(h + 1) * HEAD_DIM, :]
                    + g_t[3 * h + 1:3 * h + 2, :] * osel[:, cs]
                    + g_t[3 * h + 2:3 * h + 3, :] * owin[:, cs])
    o_ref[0] = jnp.concatenate(outs, axis=0).T


def _attend(qa, ka, vst, kwa, vwt, z_nsa, gn_block, o_cmp_t, *, tq, tk):
    bsz, heads, ka_w, seq = qa.shape
    qd = heads * HEAD_DIM
    nwin = WINDOW // tq + 1
    return pl.pallas_call(
        functools.partial(_attend_kernel, heads=heads, tq=tq, tk=tk, nwin=nwin),
        out_shape=jax.ShapeDtypeStruct((bsz, seq, qd), F32),
        grid=(bsz, seq // tq),
        in_specs=[
            pl.BlockSpec((1, heads, ka_w, tq), lambda b, i: (b, 0, 0, i)),
            pl.BlockSpec((1,) + ka.shape[1:], lambda b, i: (b, 0, 0, 0)),
            pl.BlockSpec((1,) + vst.shape[1:], lambda b, i: (b, 0, 0, 0)),
            pl.BlockSpec((1,) + kwa.shape[1:], lambda b, i: (b, 0, 0, 0)),
            pl.BlockSpec((1,) + vwt.shape[1:], lambda b, i: (b, 0, 0, 0)),
            pl.BlockSpec((1, tq, LANES), lambda b, i: (b, i, gn_block)),
            pl.BlockSpec((1, qd, tq), lambda b, i: (b, 0, i)),
        ],
        out_specs=pl.BlockSpec((1, tq, qd), lambda b, i: (b, i, 0)),
        scratch_shapes=[
            pltpu.VMEM((1, heads * tq), F32),
            pltpu.VMEM((vst.shape[2], heads * tq), F32),
            pltpu.VMEM((tk, heads * tq), F32),
            pltpu.VMEM((tk, heads * tq), F32),
        ],
        compiler_params=_cparams(("arbitrary", "arbitrary")),
        name="nsa_attend",
    )(qa, ka, vst, kwa, vwt, z_nsa, o_cmp_t)


def _merge_kernel(ys_ref, yb_ref, gm_ref, x_ref, wb_ref, wo_ref, gate_ref, o_ref):
    nb, tt, w = yb_ref.shape
    d = x_ref.shape[2]
    rows = nb * tt
    def slab(c):
        return jnp.concatenate([ys_ref[c, pl.ds(b, tt, stride=nb), :] for b in range(nb)], axis=0)

    ys = jnp.concatenate([slab(c) for c in range(ys_ref.shape[0])], axis=1)
    branches = (ys[:, 0:w], yb_ref[...].reshape(rows, w), ys[:, w:2 * w], ys[:, 2 * w:3 * w])
    merged = None
    for mi, y in enumerate(branches):
        gm = gm_ref[:, :, mi * d:(mi + 1) * d].reshape(rows, d)
        term = _sigmoid(gm.astype(F32)) * _dot(y.astype(BF16), wb_ref[mi])
        merged = term if merged is None else merged + term
    mix = _dot(merged.astype(BF16), wo_ref[...]).reshape(nb, tt, d)
    o_ref[...] = x_ref[...] + gate_ref[...] * mix


def _merge(y_scan_tm, y_b, g_merge, x, w_branch, w_out, gate, *, tt):
    bsz, seq, d = x.shape
    w = y_b.shape[2]
    tile = lambda n: pl.BlockSpec((bsz, tt, n), lambda i: (0, i, 0))
    return pl.pallas_call(
        _merge_kernel,
        out_shape=jax.ShapeDtypeStruct((bsz, seq, d), F32),
        grid=(seq // tt,),
        in_specs=[
            pl.BlockSpec((3 * w // LANES, tt * bsz, LANES), lambda i: (0, i, 0)),
            tile(w),
            tile(N_BRANCH * d),
            tile(d),
            pl.BlockSpec(w_branch.shape, lambda i: (0, 0, 0)),
            pl.BlockSpec(w_out.shape, lambda i: (0, 0)),
            pl.BlockSpec((bsz, 1, d), lambda i: (0, 0, 0)),
        ],
        out_specs=tile(d),
        compiler_params=_cparams(("arbitrary",)),
        name="merge_out",
    )(y_scan_tm, y_b, g_merge, x, w_branch, w_out, gate)


def _final_norm(out, fg_ref):
    var = jnp.mean(out * out, axis=-1, keepdims=True)
    return out * lax.rsqrt(var + RMS_EPS) * fg_ref[...]


def _ffn_kernel(x_ref, g_ref, sc_ref, sh_ref, gate_ref, wg_ref, wu_ref, wd_ref, *rest, final, fc):
    if final:
        fg_ref, o_ref, a_ref = rest
    else:
        o_ref, a_ref = rest
    hb = _norm_mod(x_ref[0], g_ref[...], sc_ref[0], sh_ref[0]).astype(BF16)
    for f0 in range(0, a_ref.shape[1], fc):
        a_ref[:, f0:f0 + fc] = (_silu(_dot(hb, wg_ref[:, f0:f0 + fc])) * _dot(hb, wu_ref[:, f0:f0 + fc])).astype(BF16)
    out = x_ref[0] + gate_ref[0] * _dot(a_ref[...], wd_ref[...])
    o_ref[0] = _final_norm(out, fg_ref) if final else out


def _ffn(x, g, scale, shift, gate, weights, *, final_g, tm):
    bsz, seq, d = x.shape
    row = lambda: pl.BlockSpec((1, 1, d), lambda b, i: (b, 0, 0))
    wg, wu, wd = weights
    ff = wg.shape[1]
    in_specs = [
        pl.BlockSpec((1, tm, d), lambda b, i: (b, i, 0)),
        pl.BlockSpec((1, d), lambda b, i: (0, 0)),
        row(), row(), row(),
        pl.BlockSpec((d, ff), lambda b, i: (0, 0)),
        pl.BlockSpec((d, ff), lambda b, i: (0, 0)),
        pl.BlockSpec((ff, d), lambda b, i: (0, 0)),
    ]
    args = [x, g.reshape(1, d), scale, shift, gate, wg, wu, wd]
    final = final_g is not None
    if final:
        in_specs.append(pl.BlockSpec((1, d), lambda b, i: (0, 0)))
        args.append(final_g.reshape(1, d))
    return pl.pallas_call(
        functools.partial(_ffn_kernel, final=final, fc=MXU_WIDTH),
        out_shape=jax.ShapeDtypeStruct((bsz, seq, d), F32),
        grid=(bsz, seq // tm),
        in_specs=in_specs,
        out_specs=pl.BlockSpec((1, tm, d), lambda b, i: (b, i, 0)),
        scratch_shapes=[pltpu.VMEM((tm, ff), BF16)],
        compiler_params=_cparams(("arbitrary", "arbitrary")),
        name="dense_ffn",
    )(*args)


def _route_kernel(x_ref, g_ref, sc_ref, sh_ref, rw_ref, rb_ref, h_ref, r_ref):
    h = _norm_mod(x_ref[0], g_ref[...], sc_ref[0], sh_ref[0])
    h_ref[0] = h.astype(BF16)
    logits = _dot(h, rw_ref[...], HI) + rb_ref[...]
    lane = lax.broadcasted_iota(jnp.int32, logits.shape, 1)
    big = logits.shape[1]
    v1 = jnp.max(logits, axis=-1, keepdims=True)
    i1 = jnp.min(jnp.where(logits == v1, lane, big), axis=-1, keepdims=True)
    rest_l = jnp.where(lane == i1, NEG_INF * 4.0, logits)
    v2 = jnp.max(rest_l, axis=-1, keepdims=True)
    i2 = jnp.min(jnp.where(rest_l == v2, lane, big), axis=-1, keepdims=True)
    e2 = jnp.exp(v2 - v1)
    w1 = 1.0 / (1.0 + e2)
    w2 = e2 / (1.0 + e2)
    r_ref[0] = jnp.where(lane == 0, i1.astype(F32), jnp.where(lane == 1, i2.astype(F32), jnp.where(lane == 2, w1, w2)))


def _route(x, g, scale, shift, rw, rb, *, tm):
    bsz, seq, d = x.shape
    row = lambda: pl.BlockSpec((1, 1, d), lambda b, i: (b, 0, 0))
    return pl.pallas_call(
        _route_kernel,
        out_shape=(jax.ShapeDtypeStruct((bsz, seq, d), BF16), jax.ShapeDtypeStruct((bsz, seq, LANES), F32)),
        grid=(bsz, seq // tm),
        in_specs=[
            pl.BlockSpec((1, tm, d), lambda b, i: (b, i, 0)),
            pl.BlockSpec((1, d), lambda b, i: (0, 0)),
            row(), row(),
            pl.BlockSpec(rw.shape, lambda b, i: (0, 0)),
            pl.BlockSpec(rb.shape, lambda b, i: (0, 0)),
        ],
        out_specs=(pl.BlockSpec((1, tm, d), lambda b, i: (b, i, 0)), pl.BlockSpec((1, tm, LANES), lambda b, i: (b, i, 0))),
        compiler_params=_cparams(("arbitrary", "arbitrary")),
        name="moe_route",
    )(x, g.reshape(1, d), scale, shift, rw, rb)


def _gffn_kernel(te_ref, nu_ref, x_ref, wg_ref, wu_ref, wd_ref, o_ref):
    i = pl.program_id(0)

    @pl.when(i < nu_ref[0])
    def _():
        hb = x_ref[...]
        a = (_silu(_dot(hb, wg_ref[0])) * _dot(hb, wu_ref[0])).astype(BF16)
        o_ref[...] = _dot(a, wd_ref[0])

    @pl.when(i >= nu_ref[0])
    def _():
        o_ref[...] = jnp.zeros_like(o_ref)


def _grouped_ffn(xs, tile_expert, n_used, wg, wu, wd, *, tmg):
    rows, d = xs.shape
    _, _, fe = wg.shape
    return pl.pallas_call(
        _gffn_kernel,
        out_shape=jax.ShapeDtypeStruct((rows, d), F32),
        grid_spec=pltpu.PrefetchScalarGridSpec(
            num_scalar_prefetch=2,
            grid=(rows // tmg,),
            in_specs=[
                pl.BlockSpec((tmg, d), lambda i, te, nu: (i, 0)),
                pl.BlockSpec((1, d, fe), lambda i, te, nu: (te[i], 0, 0)),
                pl.BlockSpec((1, d, fe), lambda i, te, nu: (te[i], 0, 0)),
                pl.BlockSpec((1, fe, d), lambda i, te, nu: (te[i], 0, 0)),
            ],
            out_specs=pl.BlockSpec((tmg, d), lambda i, te, nu: (i, 0)),
        ),
        compiler_params=_cparams(("arbitrary",)),
        name="moe_grouped_ffn",
    )(tile_expert, n_used, xs, wg, wu, wd)


def _combine_kernel(x_ref, gate_ref, y_ref, y2_ref, r_ref, *rest, final):
    if final:
        fg_ref, o_ref = rest
    else:
        (o_ref,) = rest
    r = r_ref[0]
    f = r[:, 2:3] * y_ref[0] + r[:, 3:4] * y2_ref[0]
    out = x_ref[0] + gate_ref[0] * f
    o_ref[0] = _final_norm(out, fg_ref) if final else out


def _combine(x, gate, y1, y2, route, *, final_g, tm):
    bsz, seq, d = x.shape
    tile = lambda: pl.BlockSpec((1, tm, d), lambda b, i: (b, i, 0))
    in_specs = [
        tile(),
        pl.BlockSpec((1, 1, d), lambda b, i: (b, 0, 0)),
        tile(), tile(),
        pl.BlockSpec((1, tm, LANES), lambda b, i: (b, i, 0)),
    ]
    args = [x, gate, y1, y2, route]
    final = final_g is not None
    if final:
        in_specs.append(pl.BlockSpec((1, d), lambda b, i: (0, 0)))
        args.append(final_g.reshape(1, d))
    return pl.pallas_call(
        functools.partial(_combine_kernel, final=final),
        out_shape=jax.ShapeDtypeStruct((bsz, seq, d), F32),
        grid=(bsz, seq // tm),
        in_specs=in_specs,
        out_specs=pl.BlockSpec((1, tm, d), lambda b, i: (b, i, 0)),
        compiler_params=_cparams(("arbitrary", "arbitrary")),
        name="moe_combine",
    )(*args)


def _moe(x, g, scale, shift, gate, rw, rb, wg, wu, wd, *, final_g, tm, tmg):
    bsz, seq, d = x.shape
    ne = wg.shape[0]
    tokens = bsz * seq
    h, route = _route(x, g, scale, shift, rw, rb, tm=tm)
    e_flat = route[:, :, 0:2].astype(jnp.int32).reshape(tokens * 2)
    onehot = (e_flat[:, None] == jnp.arange(ne, dtype=jnp.int32)[None, :]).astype(jnp.int32)
    csum = jnp.cumsum(onehot, axis=0)
    counts = csum[-1]
    padded = ((counts + tmg - 1) // tmg) * tmg
    ends = jnp.cumsum(padded)
    offs = ends - padded
    rank = jnp.take_along_axis(csum, e_flat[:, None], axis=1)[:, 0] - 1
    dest = offs[e_flat] + rank
    rows = tokens * 2 + ne * tmg
    src = jnp.zeros((rows,), jnp.int32).at[dest].set(jnp.arange(tokens * 2, dtype=jnp.int32) // 2,
                                                      unique_indices=True, mode="promise_in_bounds")
    tile_start = jnp.arange(rows // tmg, dtype=jnp.int32) * tmg
    tile_expert = jnp.minimum(jnp.sum((tile_start[:, None] >= ends[None, :]).astype(jnp.int32), axis=1), ne - 1)
    n_used = (ends[-1] // tmg).astype(jnp.int32).reshape(1)
    rows_of = lambda a, idx: a.at[idx].get(mode="promise_in_bounds")
    xs = rows_of(h.reshape(tokens, d), src)
    ys = _grouped_ffn(xs, tile_expert, n_used, wg, wu, wd, tmg=tmg)
    dest2 = dest.reshape(tokens, 2)
    y1 = rows_of(ys, dest2[:, 0]).reshape(bsz, seq, d)
    y2 = rows_of(ys, dest2[:, 1]).reshape(bsz, seq, d)
    return _combine(x, gate, y1, y2, route, final_g=final_g, tm=tm)


def _block_diag(blocks):
    n, r, c = blocks.shape
    eye = jnp.eye(n, dtype=blocks.dtype)
    return (eye[:, None, :, None] * blocks[:, :, None, :]).reshape(n * r, n * c)


def _rope_tables(seq, heads_q):
    inv = ROPE_THETA ** (-jnp.arange(0, HEAD_DIM, 2, dtype=F32) / HEAD_DIM)
    ang = jnp.arange(seq, dtype=F32)[:, None] * inv[None, :]
    cos, sin = jnp.cos(ang), jnp.sin(ang)
    cos_h = jnp.concatenate([cos, cos], axis=1)
    sin_h = jnp.concatenate([-sin, sin], axis=1)
    one_h = jnp.ones_like(cos_h)
    zero_h = jnp.zeros_like(cos_h)
    cos_t = jnp.concatenate([cos_h] * heads_q + [one_h, one_h] + [cos_h, one_h] * 3, axis=1)
    sin_t = jnp.concatenate([sin_h] * heads_q + [zero_h, zero_h] + [sin_h, zero_h] * 3, axis=1)
    return cos_t, sin_t


def kernel(x, c, mod_w, mod_b, norm_mix_g, norm_ffn_g, w_in, s5_lambda_re, s5_lambda_im, s5_log_step, s5_b_re, s5_b_im, s5_c_re, s5_c_im, s5_d, s5_w_glu, nsa_pe_k, nsa_pe_v, nsa_cmp_k_w1, nsa_cmp_k_w2, nsa_cmp_v_w1, nsa_cmp_v_w2, lru_conv_w, lru_conv_b, lru_w_a, lru_b_a, lru_w_x, lru_b_x, lru_lambda, sc_conv_w, w_branch, w_out, ffn_w_gate, ffn_w_up, ffn_w_down, moe_router_w, moe_router_b, moe_w_gate, moe_w_up, moe_w_down, final_norm_g):
    bsz, seq, d = x.shape
    depth = mod_w.shape[0]
    w = d // N_BRANCH
    heads = w // HEAD_DIM
    groups = w // S5_GROUP
    assert bsz == SUBLANES, "time-major scan blocks hold one timestep per sublane tile"
    tm = min(512, seq)
    tq = 128
    steps = min(64, seq)
    nsel = seq // L_SEL
    assert nsel <= LANES and seq % tm == 0 and seq >= WINDOW + tq

    mod = _modulation(c, mod_w, mod_b)
    mod = mod.reshape(depth, bsz, 6, 1, d)

    sizes = (w, w, HEAD_DIM, HEAD_DIM, HEAD_DIM, HEAD_DIM, HEAD_DIM, HEAD_DIM, heads * 3, w, w, w, w, w, N_BRANCH * d)
    offs = np.concatenate([[0], np.cumsum(sizes)])
    col = lambda wmat_t, k: wmat_t[offs[k]:offs[k + 1], :]

    cos_t, sin_t = _rope_tables(seq, heads)
    n_cmp = (seq - L_CMP) // CMP_STRIDE + 1
    nblk = seq // CMP_STRIDE
    cmp_start = np.arange(nblk) * CMP_STRIDE
    sel_start = np.arange(nsel) * L_SEL
    ovl = np.clip(np.minimum(cmp_start[:, None] + L_CMP, sel_start[None, :] + L_SEL)
                  - np.maximum(cmp_start[:, None], sel_start[None, :]), 0, None) / L_CMP
    ovl[n_cmp:] = 0.0
    overlap_t = jnp.asarray(np.pad(ovl, ((0, 0), (0, LANES - nsel))).T, BF16)
    onehot = jnp.asarray((np.arange(seq)[:, None] // L_SEL) == np.arange(nsel)[None, :], BF16)

    for li in range(depth):
        sh1, sc1, gt1, sh2, sc2, gt2 = (mod[li, :, k] for k in range(6))
        wl = w_in[li].T
        w_scan = jnp.concatenate([col(wl, 0), col(wl, 9), col(wl, 10), col(wl, 11), col(wl, 12), col(wl, 13)], axis=0).astype(BF16)
        gpad = jnp.zeros((LANES - heads * 3, d), wl.dtype)
        w_nsa = jnp.concatenate([col(wl, 1), col(wl, 8), gpad] + [col(wl, k) for k in range(2, 8)], axis=0).astype(BF16)
        w_gm = col(wl, 14).astype(BF16)
        g1 = norm_mix_g[li]
        z_scan = _project_tm(x, g1, sc1, sh1, w_scan, tt=steps)
        z_nsa = _project(x, g1, sc1, sh1, w_nsa, out_dtype=F32, tm=tm, rope_tabs=(cos_t, sin_t))
        g_merge = _project(x, g1, sc1, sh1, w_gm, out_dtype=BF16, tm=tm)

        a_re, a_im, bb_re, bb_im = _s5_params(s5_lambda_re[li], s5_lambda_im[li], s5_log_step[li], s5_b_re[li], s5_b_im[li])
        to_bd = lambda m: _block_diag(jnp.swapaxes(m.reshape(groups, S5_STATE, S5_GROUP), 1, 2))
        s5b = jnp.concatenate([to_bd(bb_re), to_bd(bb_im)], axis=1).astype(BF16)
        c_bd = lambda m: _block_diag(jnp.swapaxes(m, 1, 2))
        s5c = jnp.concatenate([c_bd(s5_c_re[li]), -c_bd(s5_c_im[li])], axis=0).astype(BF16)
        r1 = lambda a: a.reshape(1, -1)
        y_scan = _scan_mixers(
            z_scan, bsz, s5b, s5c, r1(a_re), r1(a_im), r1(s5_d[li]), s5_w_glu[li].astype(BF16),
            lru_conv_w[li], r1(lru_conv_b[li]), _block_diag(lru_w_a[li]).astype(BF16), r1(lru_b_a[li]),
            _block_diag(lru_w_x[li]).astype(BF16), r1(lru_b_x[li]), r1(lru_lambda[li]), sc_conv_w[li], steps=steps)

        kv0 = heads * HEAD_DIM + LANES
        kvs = [z_nsa[:, :, kv0 + k * HEAD_DIM:kv0 + (k + 1) * HEAD_DIM] for k in range(6)]
        kc, vc, ks, vs, kw, vw = kvs
        grp = lambda a: a.reshape(bsz, nblk, CMP_STRIDE * HEAD_DIM)
        k_cmp, v_cmp = _compress(grp(kc), grp(vc), nsa_pe_k[li], nsa_pe_v[li], nsa_cmp_k_w1[li], nsa_cmp_k_w2[li], nsa_cmp_v_w1[li], nsa_cmp_v_w2[li])
        o_cmp, qa = _cmp_select(z_nsa, k_cmp, jnp.swapaxes(v_cmp, 1, 2).astype(BF16), overlap_t, heads=heads, tq=2 * tq, nsel=nsel)
        tk = min(512, seq)
        ka = jnp.concatenate([ks.astype(BF16), jnp.broadcast_to(onehot, (bsz, seq, nsel))], axis=2)
        ka = ka.reshape(bsz, seq // tk, tk, HEAD_DIM + nsel)
        kwa = jnp.concatenate([kw.astype(BF16), jnp.zeros((bsz, seq, nsel), BF16)], axis=2)
        kwa = kwa.reshape(bsz, seq // tq, tq, HEAD_DIM + nsel)
        ones_pad = jnp.concatenate([jnp.ones((bsz, seq, 1), BF16), jnp.zeros((bsz, seq, SUBLANES - 1), BF16)], axis=2)
        tiles_t = lambda a, n: jnp.swapaxes(
            jnp.concatenate([a.astype(BF16), ones_pad], axis=2).reshape(bsz, seq // n, n, HEAD_DIM + SUBLANES), 2, 3)
        y_b = _attend(qa, ka, tiles_t(vs, tk), kwa, tiles_t(vw, tq), z_nsa, (heads * HEAD_DIM) // LANES, o_cmp, tq=tq, tk=tk)

        x = _merge(y_scan, y_b, g_merge, x, w_branch[li].astype(BF16), w_out[li].astype(BF16), gt1, tt=steps)

        final_g = final_norm_g if li == depth - 1 else None
        j = li // 2
        if li % 2 == 0:
            weights = (ffn_w_gate[j].astype(BF16), ffn_w_up[j].astype(BF16), ffn_w_down[j].astype(BF16))
            x = _ffn(x, norm_ffn_g[li], sc2, sh2, gt2, weights, final_g=final_g, tm=tm)
        else:
            rw = jnp.pad(moe_router_w[j], ((0, 0), (0, LANES - N_EXPERTS)))
            rb = jnp.pad(moe_router_b[j], (0, LANES - N_EXPERTS), constant_values=NEG_INF).reshape(1, LANES)
            x = _moe(x, norm_ffn_g[li], sc2, sh2, gt2, rw, rb, moe_w_gate[j].astype(BF16), moe_w_up[j].astype(BF16),
                     moe_w_down[j].astype(BF16), final_g=final_g, tm=tm, tmg=512)
    return x
```

```python
import functools
import math

import jax
import jax.numpy as jnp
import numpy as np
from jax import lax
from jax.experimental import pallas as pl
from jax.experimental.pallas import tpu as pltpu

F32 = jnp.float32
BF16 = jnp.bfloat16

N_BRANCH = 4
S5_GROUP = 16
S5_STATE = 64
HEAD_DIM = 64
L_CMP = 32
CMP_STRIDE = 16
L_SEL = 64
TOP_N = 16
WINDOW = 512
ROPE_THETA = 10000.0
FORCED_SCORE = 1.0e4
NEG_INF = -1.0e30
LRU_CONV = 4
LRU_C = 8.0
SC_CONV = 3
N_EXPERTS = 8
RMS_EPS = 1e-6
LOG2E = math.log2(math.e)

LANES = 128
SUBLANES = 8
MXU_WIDTH = 256
VMEM_LIMIT = 52 * 1024 * 1024

HI = lax.Precision.HIGHEST


def _cparams(sem):
    return pltpu.CompilerParams(dimension_semantics=sem, vmem_limit_bytes=VMEM_LIMIT)


def _dot(a, b, precision=None):
    return jnp.dot(a, b, preferred_element_type=F32, precision=precision)


def _dot_t(a, b, precision=None):
    return lax.dot_general(a, b, (((1,), (1,)), ((), ())), preferred_element_type=F32, precision=precision)


def _gelu(x):
    return 0.5 * x * (1.0 + jnp.tanh(math.sqrt(2.0 / math.pi) * (x + 0.044715 * (x * x * x))))


def _sigmoid(x):
    return 0.5 * jnp.tanh(0.5 * x) + 0.5


def _silu(x):
    hx = 0.5 * x
    return hx * jnp.tanh(hx) + hx


def _norm_mod(x, g, scale, shift):
    var = jnp.mean(x * x, axis=-1, keepdims=True)
    return (x * lax.rsqrt(var + RMS_EPS) * g) * (1.0 + scale) + shift


def _rot_half(z):
    n = z.shape[-1]
    lane = lax.broadcasted_iota(jnp.int32, z.shape, z.ndim - 1)
    first = (lane % HEAD_DIM) < (HEAD_DIM // 2)
    return jnp.where(first, pltpu.roll(z, n - HEAD_DIM // 2, z.ndim - 1), pltpu.roll(z, HEAD_DIM // 2, z.ndim - 1))


def _mod_kernel(c_ref, w_ref, b_ref, o_ref):
    c = c_ref[...]
    o_ref[0] = _dot(_silu(c), w_ref[0], HI) + b_ref[0]


def _modulation(c, mod_w, mod_b):
    depth, d, n = mod_w.shape
    bsz = c.shape[0]
    tn = 1024
    return pl.pallas_call(
        _mod_kernel,
        out_shape=jax.ShapeDtypeStruct((depth, bsz, n), F32),
        grid=(depth, n // tn),
        in_specs=[
            pl.BlockSpec((bsz, d), lambda l, j: (0, 0)),
            pl.BlockSpec((1, d, tn), lambda l, j: (l, 0, j)),
            pl.BlockSpec((1, 1, tn), lambda l, j: (l, 0, j)),
        ],
        out_specs=pl.BlockSpec((1, bsz, tn), lambda l, j: (l, 0, j)),
        compiler_params=_cparams(("arbitrary", "arbitrary")),
        name="modulation",
    )(c, mod_w, mod_b.reshape(depth, 1, n))


def _proj_kernel(x_ref, g_ref, sc_ref, sh_ref, ws_ref, wn_ref, wg_ref, cos_ref, sin_ref,
                 zs_ref, zn_ref, gm_ref, h_ref):
    nb, tt, _ = x_ref.shape
    rows = nb * tt
    nlc = h_ref.shape[0]
    parts = []
    for b in range(nb):
        hb = _norm_mod(x_ref[b], g_ref[...], sc_ref[b], sh_ref[b])
        for c in range(nlc):
            h_ref[c, pl.ds(b, tt, stride=nb), :] = hb[:, c * LANES:(c + 1) * LANES]
        parts.append(hb.astype(BF16))
    h_bt = jnp.concatenate(parts, axis=0)
    h_tb = jnp.concatenate([h_ref[c] for c in range(nlc)], axis=1).astype(BF16)
    for n0 in range(0, ws_ref.shape[0], 2 * MXU_WIDTH):
        zs_ref[:, n0:n0 + 2 * MXU_WIDTH] = _dot_t(h_tb, ws_ref[n0:n0 + 2 * MXU_WIDTH, :])
    for n0 in range(0, wn_ref.shape[0], MXU_WIDTH):
        z = _dot_t(h_bt, wn_ref[n0:n0 + MXU_WIDTH, :])
        cos = jnp.concatenate([cos_ref[:, n0:n0 + MXU_WIDTH]] * nb, axis=0)
        sin = jnp.concatenate([sin_ref[:, n0:n0 + MXU_WIDTH]] * nb, axis=0)
        zn_ref[:, :, n0:n0 + MXU_WIDTH] = (z * cos + _rot_half(z) * sin).reshape(nb, tt, MXU_WIDTH)
    for n0 in range(0, wg_ref.shape[0], 2 * MXU_WIDTH):
        z = _dot_t(h_bt, wg_ref[n0:n0 + 2 * MXU_WIDTH, :])
        gm_ref[:, :, n0:n0 + 2 * MXU_WIDTH] = z.astype(gm_ref.dtype).reshape(nb, tt, 2 * MXU_WIDTH)


def _project(x, g, scale, shift, w_scan, w_nsa, w_gm, rope_tabs, *, tt):
    bsz, seq, d = x.shape
    ns, nn, ng = w_scan.shape[0], w_nsa.shape[0], w_gm.shape[0]
    full = lambda a: pl.BlockSpec(a.shape, lambda i: (0,) * a.ndim)
    return pl.pallas_call(
        _proj_kernel,
        out_shape=(
            jax.ShapeDtypeStruct((seq * bsz, ns), F32),
            jax.ShapeDtypeStruct((bsz, seq, nn), F32),
            jax.ShapeDtypeStruct((bsz, seq, ng), BF16),
        ),
        grid=(seq // tt,),
        in_specs=[
            pl.BlockSpec((bsz, tt, d), lambda i: (0, i, 0)),
            pl.BlockSpec((1, d), lambda i: (0, 0)),
            pl.BlockSpec((bsz, 1, d), lambda i: (0, 0, 0)),
            pl.BlockSpec((bsz, 1, d), lambda i: (0, 0, 0)),
            full(w_scan), full(w_nsa), full(w_gm),
            pl.BlockSpec((tt, nn), lambda i: (i, 0)),
            pl.BlockSpec((tt, nn), lambda i: (i, 0)),
        ],
        out_specs=(
            pl.BlockSpec((tt * bsz, ns), lambda i: (i, 0)),
            pl.BlockSpec((bsz, tt, nn), lambda i: (0, i, 0)),
            pl.BlockSpec((bsz, tt, ng), lambda i: (0, i, 0)),
        ),
        scratch_shapes=[pltpu.VMEM((d // LANES, tt * bsz, LANES), F32)],
        compiler_params=_cparams(("arbitrary",)),
        name="proj_in",
    )(x, g.reshape(1, d), scale, shift, w_scan, w_nsa, w_gm, *rope_tabs)


def _s5_param_kernel(lre_ref, lim_ref, ls_ref, bre_ref, bim_ref, are_ref, aim_ref, bbre_ref, bbim_ref):
    lre = lre_ref[...]
    lim = lim_ref[...]
    step = jnp.exp(ls_ref[...])
    mag = jnp.exp(lre * step)
    ar = mag * jnp.cos(lim * step)
    ai = mag * jnp.sin(lim * step)
    xr = ar - 1.0
    den = lre * lre + lim * lim
    cr = (xr * lre + ai * lim) / den
    ci = (ai * lre - xr * lim) / den
    bre = bre_ref[...]
    bim = bim_ref[...]
    are_ref[...] = ar
    aim_ref[...] = ai
    bbre_ref[...] = cr * bre - ci * bim
    bbim_ref[...] = cr * bim + ci * bre


def _s5_params(lam_re, lam_im, log_step, b_re, b_im):
    g, n = lam_re.shape
    c = b_re.shape[-1]
    rows = g * n
    col = lambda a: a.reshape(rows, 1)
    ls = jnp.broadcast_to(log_step[:, None], (g, n))
    out = pl.pallas_call(
        _s5_param_kernel,
        out_shape=(
            jax.ShapeDtypeStruct((rows, 1), F32),
            jax.ShapeDtypeStruct((rows, 1), F32),
            jax.ShapeDtypeStruct((rows, c), F32),
            jax.ShapeDtypeStruct((rows, c), F32),
        ),
        name="s5_params",
    )(col(lam_re), col(lam_im), col(ls), b_re.reshape(rows, c), b_im.reshape(rows, c))
    return out


def _scan_kernel(z_ref, s5b_ref, s5c_ref, are_ref, aim_ref, d_ref, wglu_ref,
                 cw_ref, cb_ref, wa_ref, ba_ref, wx_ref, bx_ref, lam_ref, scw_ref,
                 o_ref,
                 bu_ref, hre_ref, him_ref, xe_ref, hl_ref, pe_ref, ab_ref, bb_ref, *, nb, steps):
    c = pl.program_id(0)
    rows = nb * steps
    w = d_ref.shape[1]
    ns = are_ref.shape[1]

    @pl.when(c == 0)
    def _():
        hre_ref[...] = jnp.zeros_like(hre_ref)
        him_ref[...] = jnp.zeros_like(him_ref)
        hl_ref[...] = jnp.zeros_like(hl_ref)
        xe_ref[0:(LRU_CONV - 1) * nb, :] = jnp.zeros(((LRU_CONV - 1) * nb, w), F32)
        pe_ref[0:(SC_CONV - 1) * nb, :] = jnp.zeros(((SC_CONV - 1) * nb, w), F32)

    u = z_ref[:, 0:w]
    bu_ref[...] = _dot(u.astype(BF16), s5b_ref[...])
    a_re = jnp.broadcast_to(are_ref[...], (nb, ns))
    a_im = jnp.broadcast_to(aim_ref[...], (nb, ns))

    def s5_step(t, carry):
        h_re, h_im = carry
        r0 = pl.multiple_of(t * nb, nb)
        b_re = bu_ref[pl.ds(r0, nb), 0:ns]
        b_im = bu_ref[pl.ds(r0, nb), ns:2 * ns]
        n_re = a_re * h_re - a_im * h_im + b_re
        n_im = a_re * h_im + a_im * h_re + b_im
        bu_ref[pl.ds(r0, nb), 0:ns] = n_re
        bu_ref[pl.ds(r0, nb), ns:2 * ns] = n_im
        return n_re, n_im

    h_re, h_im = lax.fori_loop(0, steps, s5_step, (hre_ref[...], him_ref[...]))
    hre_ref[...] = h_re
    him_ref[...] = h_im
    y = _dot(bu_ref[...].astype(BF16), s5c_ref[...]) + d_ref[...] * u
    zg = _gelu(y)
    def put(branch, val):
        for k in range(w // LANES):
            o_ref[branch * (w // LANES) + k] = val[:, k * LANES:(k + 1) * LANES]

    put(0, zg * _sigmoid(_dot(zg.astype(BF16), wglu_ref[...])))

    nt = (LRU_CONV - 1) * nb
    x = z_ref[:, w:2 * w]
    xe_ref[nt:nt + rows, :] = x
    xc = cb_ref[...] + cw_ref[LRU_CONV - 1:LRU_CONV, :] * x
    for k in range(LRU_CONV - 1):
        xc = xc + cw_ref[k:k + 1, :] * xe_ref[k * nb:k * nb + rows, :]
    xe_ref[0:nt, :] = xe_ref[rows:rows + nt, :]
    xcb = xc.astype(BF16)
    r = _sigmoid(_dot(xcb, wa_ref[...]) + ba_ref[...])
    gi = _sigmoid(_dot(xcb, wx_ref[...]) + bx_ref[...])
    nl = -lam_ref[...]
    softplus = jnp.maximum(nl, 0.0) + jnp.log(1.0 + jnp.exp(-jnp.abs(nl)))
    log_a = (-LRU_C) * r * softplus
    a = jnp.exp(log_a)
    ab_ref[...] = a
    bb_ref[...] = jnp.sqrt(1.0 - a * a) * (gi * xc)

    def lru_step(t, h):
        r0 = pl.multiple_of(t * nb, nb)
        h = ab_ref[pl.ds(r0, nb), :] * h + bb_ref[pl.ds(r0, nb), :]
        bb_ref[pl.ds(r0, nb), :] = h
        return h

    hl_ref[...] = lax.fori_loop(0, steps, lru_step, hl_ref[...])
    put(1, bb_ref[...] * _gelu(z_ref[:, 2 * w:3 * w]))

    npt = (SC_CONV - 1) * nb
    p = z_ref[:, 4 * w:5 * w] * z_ref[:, 5 * w:6 * w]
    pe_ref[npt:npt + rows, :] = p
    cv = scw_ref[SC_CONV - 1:SC_CONV, :] * p
    for k in range(SC_CONV - 1):
        cv = cv + scw_ref[k:k + 1, :] * pe_ref[k * nb:k * nb + rows, :]
    pe_ref[0:npt, :] = pe_ref[rows:rows + npt, :]
    put(2, z_ref[:, 3 * w:4 * w] * cv)


def _scan_mixers(z_tm, nb, s5b, s5c, a_re, a_im, d_skip, w_glu, conv_w, conv_b, wa, ba, wx, bx, lam, sc_w, *, steps):
    total, zc = z_tm.shape
    rows = nb * steps
    w = zc // 6
    ns = a_re.shape[1]
    full = lambda a: pl.BlockSpec(a.shape, lambda c: (0,) * a.ndim)
    params = [s5b, s5c, a_re, a_im, d_skip, w_glu, conv_w, conv_b, wa, ba, wx, bx, lam, sc_w]
    return pl.pallas_call(
        functools.partial(_scan_kernel, nb=nb, steps=steps),
        out_shape=jax.ShapeDtypeStruct((3 * w // LANES, total, LANES), F32),
        grid=(total // rows,),
        in_specs=[pl.BlockSpec((rows, zc), lambda c: (c, 0))] + [full(a) for a in params],
        out_specs=pl.BlockSpec((3 * w // LANES, rows, LANES), lambda c: (0, c, 0)),
        scratch_shapes=[
            pltpu.VMEM((rows, 2 * ns), F32),
            pltpu.VMEM((nb, ns), F32),
            pltpu.VMEM((nb, ns), F32),
            pltpu.VMEM((rows + (LRU_CONV - 1) * nb, w), F32),
            pltpu.VMEM((nb, w), F32),
            pltpu.VMEM((rows + (SC_CONV - 1) * nb, w), F32),
            pltpu.VMEM((rows, w), F32),
            pltpu.VMEM((rows, w), F32),
        ],
        compiler_params=_cparams(("arbitrary",)),
        name="scan_mixers",
    )(z_tm, *params)


def _compress_kernel(gk_ref, gv_ref, pek_ref, pev_ref, wk1_ref, wk2_ref, wv1_ref, wv2_ref, ko_ref, vo_ref):
    half = gk_ref.shape[2]
    nblk = gk_ref.shape[1]

    def phi(g, pe, w1_ref, w2_ref):
        top = _dot(g, w1_ref[0:half, :], HI)
        bot = _dot(g, w1_ref[half:2 * half, :], HI)
        hid = top + pltpu.roll(bot, nblk - 1, 0) + _dot(pe, w1_ref[...], HI)
        return _dot(_gelu(hid), w2_ref[...], HI)

    ko_ref[0] = phi(gk_ref[0], pek_ref[...], wk1_ref, wk2_ref)
    vo_ref[0] = phi(gv_ref[0], pev_ref[...], wv1_ref, wv2_ref)


def _compress(gk, gv, pe_k, pe_v, wk1, wk2, wv1, wv2):
    bsz, nblk, half = gk.shape
    hd = wk2.shape[1]
    full = lambda a: pl.BlockSpec(a.shape, lambda b: (0,) * a.ndim)
    pek = pe_k.reshape(1, -1)
    pev = pe_v.reshape(1, -1)
    params = [pek, pev, wk1, wk2, wv1, wv2]
    blk = pl.BlockSpec((1, nblk, half), lambda b: (b, 0, 0))
    oblk = pl.BlockSpec((1, nblk, hd), lambda b: (b, 0, 0))
    return pl.pallas_call(
        _compress_kernel,
        out_shape=(jax.ShapeDtypeStruct((bsz, nblk, hd), F32),) * 2,
        grid=(bsz,),
        in_specs=[blk, blk] + [full(a) for a in params],
        out_specs=(oblk, oblk),
        compiler_params=_cparams(("arbitrary",)),
        name="nsa_compress",
    )(gk, gv, *params)


def _split_bf16(a):
    hi = a.astype(BF16)
    return hi, (a - hi.astype(F32)).astype(BF16)


def _cmp_kernel(q_ref, kc_ref, vct_ref, ovt_ref, ocmp_ref, qa_ref, *, heads, tq):
    i = pl.program_id(1)
    q_t = (q_ref[0] * HEAD_DIM ** -0.5).T
    kc_hi, kc_lo = _split_bf16(kc_ref[0])
    vct = vct_ref[0]
    nblk = kc_hi.shape[0]
    t = i * tq + lax.broadcasted_iota(jnp.int32, (nblk, tq), 1)
    n_id = lax.broadcasted_iota(jnp.int32, (nblk, tq), 0)
    visible = (n_id * CMP_STRIDE + (L_CMP - 1)) <= t
    psum = jnp.zeros((nblk, tq), F32)
    for h in range(heads):
        q_hi, q_lo = _split_bf16(q_t[h * HEAD_DIM:(h + 1) * HEAD_DIM, :])
        s = _dot(kc_hi, q_hi) + (_dot(kc_hi, q_lo) + _dot(kc_lo, q_hi))
        s = jnp.where(visible, s, NEG_INF)
        m = jnp.max(s, axis=0, keepdims=True)
        e = jnp.where(visible, jnp.exp(s - m), 0.0)
        den = jnp.sum(e, axis=0, keepdims=True)
        p = e / jnp.where(den > 0.0, den, 1.0)
        psum = psum + p
        ocmp_ref[0, h * HEAD_DIM:(h + 1) * HEAD_DIM, :] = _dot(vct, p.astype(BF16))

    p_hi, p_lo = _split_bf16(psum)
    imp_t = _dot(ovt_ref[...], p_hi) + _dot(ovt_ref[...], p_lo)
    nslot = ovt_ref.shape[0]
    tt = i * tq + lax.broadcasted_iota(jnp.int32, (nslot, tq), 1)
    j = lax.broadcasted_iota(jnp.int32, (nslot, tq), 0)
    cur = tt // L_SEL
    forced = (j == 0) | (j == cur) | (j == cur - 1)
    future = (j * L_SEL) > tt
    imp_t = jnp.where(future, -1.0, jnp.where(forced, FORCED_SCORE, imp_t))
    nreal = qa_ref.shape[2] - HEAD_DIM
    cand = imp_t[0:nreal, :]
    jrow = lax.broadcasted_iota(jnp.int32, (nreal, tq), 0)
    rank = jnp.zeros((nreal, tq), jnp.int32)
    for k in range(nreal):
        rk = imp_t[k:k + 1, :]
        beats = (rk > cand) | ((rk == cand) & (jrow > k))
        rank = rank + beats.astype(jnp.int32)
    bias_t = jnp.where(rank < TOP_N, 0.0, NEG_INF)
    for h in range(heads):
        qa_ref[0, h] = jnp.concatenate([q_t[h * HEAD_DIM:(h + 1) * HEAD_DIM, :] * LOG2E, bias_t], axis=0).astype(BF16)


def _cmp_select(q, k_cmp, v_cmp_t, overlap_t, *, heads, tq, nsel):
    bsz, seq, _ = q.shape
    qd = heads * HEAD_DIM
    nblk = k_cmp.shape[1]
    return pl.pallas_call(
        functools.partial(_cmp_kernel, heads=heads, tq=tq),
        out_shape=(
            jax.ShapeDtypeStruct((bsz, qd, seq), F32),
            jax.ShapeDtypeStruct((bsz, heads, HEAD_DIM + nsel, seq), BF16),
        ),
        grid=(bsz, seq // tq),
        in_specs=[
            pl.BlockSpec((1, tq, qd), lambda b, i: (b, i, 0)),
            pl.BlockSpec((1, nblk, HEAD_DIM), lambda b, i: (b, 0, 0)),
            pl.BlockSpec((1, HEAD_DIM, nblk), lambda b, i: (b, 0, 0)),
            pl.BlockSpec(overlap_t.shape, lambda b, i: (0, 0)),
        ],
        out_specs=(
            pl.BlockSpec((1, qd, tq), lambda b, i: (b, 0, i)),
            pl.BlockSpec((1, heads, HEAD_DIM + nsel, tq), lambda b, i: (b, 0, 0, i)),
        ),
        compiler_params=_cparams(("arbitrary", "arbitrary")),
        name="nsa_cmp_select",
    )(q, k_cmp, v_cmp_t, overlap_t)


def _attend_kernel(qa_ref, ka_ref, vst_ref, kwa_ref, vwt_ref, gn_ref, ocmp_ref, o_ref,
                   m_ref, acc_ref, s0_ref, s1_ref, *, heads, tq, tk, nwin):
    i = pl.program_id(1)
    cols = heads * tq
    qat = jnp.concatenate([qa_ref[0, h] for h in range(heads)], axis=1)
    t_row = i * tq + (lax.broadcasted_iota(jnp.int32, (1, cols), 1) % tq)

    m_ref[...] = jnp.full(m_ref.shape, NEG_INF, F32)
    acc_ref[...] = jnp.zeros(acc_ref.shape, F32)
    key_id = lax.broadcasted_iota(jnp.int32, (tk, cols), 0)
    n_full = (i * tq + 1) // tk
    s0_ref[...] = _dot(ka_ref[0, 0], qat)

    st = jnp.maximum(i + 1 - nwin, 0)
    kw = kwa_ref[0, pl.ds(st, nwin)].reshape(nwin * tq, kwa_ref.shape[3])
    s = _dot(kw, qat)
    dist = t_row - (st * tq + lax.broadcasted_iota(jnp.int32, (nwin * tq, cols), 0))
    s = jnp.where((dist >= 0) & (dist < WINDOW), s, NEG_INF)
    m = jnp.max(s, axis=0, keepdims=True)
    pb = jnp.exp2(s - m).astype(BF16)
    ow = _dot(vwt_ref[0, st], pb[0:tq, :])
    for j in range(1, nwin):
        ow = ow + _dot(vwt_ref[0, st + j], pb[j * tq:(j + 1) * tq, :])
    owin = ow[0:HEAD_DIM, :] / ow[HEAD_DIM:HEAD_DIM + 1, :]

    def softmax_tile(kt, s, masked):
        if masked:
            s = jnp.where(kt * tk + key_id <= t_row, s, NEG_INF)
        m_old = m_ref[...]
        m_new = jnp.maximum(m_old, jnp.max(s, axis=0, keepdims=True))
        p = jnp.exp2(s - m_new).astype(BF16)
        acc_ref[...] = jnp.exp2(m_old - m_new) * acc_ref[...] + _dot(vst_ref[0, kt], p)
        m_ref[...] = m_new

    def pair_body(j, carry):
        s1_ref[...] = _dot(ka_ref[0, 2 * j + 1], qat)
        softmax_tile(2 * j, s0_ref[...], False)
        s0_ref[...] = _dot(ka_ref[0, 2 * j + 2], qat)
        softmax_tile(2 * j + 1, s1_ref[...], False)
        return carry

    n_pair = n_full // 2
    lax.fori_loop(0, n_pair, pair_body, 0)
    s1_ref[...] = _dot(ka_ref[0, 2 * n_pair + 1], qat)
    softmax_tile(2 * n_pair, s0_ref[...], True)
    softmax_tile(2 * n_pair + 1, s1_ref[...], True)

    osel = acc_ref[0:HEAD_DIM, :] / acc_ref[HEAD_DIM:HEAD_DIM + 1, :]

    g_t = _sigmoid(gn_ref[0]).T
    outs = []
    for h in range(heads):
        cs = slice(h * tq, (h + 1) * tq)
        outs.append(g_t[3 * h:3 * h + 1, :] * ocmp_ref[0, h * HEAD_DIM:(h + 1) * HEAD_DIM, :]
                    + g_t[3 * h + 1:3 * h + 2, :] * osel[:, cs]
                    + g_t[3 * h + 2:3 * h + 3, :] * owin[:, cs])
    o_ref[0] = jnp.concatenate(outs, axis=0).T


def _attend(qa, ka, vst, kwa, vwt, z_nsa, gn_block, o_cmp_t, *, tq, tk):
    bsz, heads, ka_w, seq = qa.shape
    qd = heads * HEAD_DIM
    nwin = WINDOW // tq + 1
    return pl.pallas_call(
        functools.partial(_attend_kernel, heads=heads, tq=tq, tk=tk, nwin=nwin),
        out_shape=jax.ShapeDtypeStruct((bsz, seq, qd), F32),
        grid=(bsz, seq // tq),
        in_specs=[
            pl.BlockSpec((1, heads, ka_w, tq), lambda b, i: (b, 0, 0, i)),
            pl.BlockSpec((1,) + ka.shape[1:], lambda b, i: (b, 0, 0, 0)),
            pl.BlockSpec((1,) + vst.shape[1:], lambda b, i: (b, 0, 0, 0)),
            pl.BlockSpec((1,) + kwa.shape[1:], lambda b, i: (b, 0, 0, 0)),
            pl.BlockSpec((1,) + vwt.shape[1:], lambda b, i: (b, 0, 0, 0)),
            pl.BlockSpec((1, tq, LANES), lambda b, i: (b, i, gn_block)),
            pl.BlockSpec((1, qd, tq), lambda b, i: (b, 0, i)),
        ],
        out_specs=pl.BlockSpec((1, tq, qd), lambda b, i: (b, i, 0)),
        scratch_shapes=[
            pltpu.VMEM((1, heads * tq), F32),
            pltpu.VMEM((vst.shape[2], heads * tq), F32),
            pltpu.VMEM((tk, heads * tq), F32),
            pltpu.VMEM((tk, heads * tq), F32),
        ],
        compiler_params=_cparams(("arbitrary", "arbitrary")),
        name="nsa_attend",
    )(qa, ka, vst, kwa, vwt, z_nsa, o_cmp_t)


def _merge_kernel(ys_ref, yb_ref, gm_ref, x_ref, wb_ref, wo_ref, gate_ref, o_ref):
    nb, tt, w = yb_ref.shape
    d = x_ref.shape[2]
    rows = nb * tt
    def slab(c):
        return jnp.concatenate([ys_ref[c, pl.ds(b, tt, stride=nb), :] for b in range(nb)], axis=0)

    ys = jnp.concatenate([slab(c) for c in range(ys_ref.shape[0])], axis=1)
    branches = (ys[:, 0:w], yb_ref[...].reshape(rows, w), ys[:, w:2 * w], ys[:, 2 * w:3 * w])
    plain = None
    gated = None
    for mi, y in enumerate(branches):
        p = _dot(y.astype(BF16), wb_ref[mi])
        t = jnp.tanh(gm_ref[:, :, mi * d:(mi + 1) * d].reshape(rows, d).astype(F32)) * p
        plain = p if plain is None else plain + p
        gated = t if gated is None else gated + t
    merged = 0.5 * (plain + gated)
    mix = _dot(merged.astype(BF16), wo_ref[...]).reshape(nb, tt, d)
    o_ref[...] = x_ref[...] + gate_ref[...] * mix


def _merge(y_scan_tm, y_b, g_merge, x, w_branch, w_out, gate, *, tt):
    bsz, seq, d = x.shape
    w = y_b.shape[2]
    tile = lambda n: pl.BlockSpec((bsz, tt, n), lambda i: (0, i, 0))
    return pl.pallas_call(
        _merge_kernel,
        out_shape=jax.ShapeDtypeStruct((bsz, seq, d), F32),
        grid=(seq // tt,),
        in_specs=[
            pl.BlockSpec((3 * w // LANES, tt * bsz, LANES), lambda i: (0, i, 0)),
            tile(w),
            tile(N_BRANCH * d),
            tile(d),
            pl.BlockSpec(w_branch.shape, lambda i: (0, 0, 0)),
            pl.BlockSpec(w_out.shape, lambda i: (0, 0)),
            pl.BlockSpec((bsz, 1, d), lambda i: (0, 0, 0)),
        ],
        out_specs=tile(d),
        compiler_params=_cparams(("arbitrary",)),
        name="merge_out",
    )(y_scan_tm, y_b, g_merge, x, w_branch, w_out, gate)


def _final_norm(out, fg_ref):
    var = jnp.mean(out * out, axis=-1, keepdims=True)
    return out * lax.rsqrt(var + RMS_EPS) * fg_ref[...]


def _ffn_kernel(x_ref, g_ref, sc_ref, sh_ref, gate_ref, wg_ref, wu_ref, wd_ref, *rest, final, fc):
    if final:
        fg_ref, o_ref, a_ref = rest
    else:
        o_ref, a_ref = rest
    hb = _norm_mod(x_ref[0], g_ref[...], sc_ref[0], sh_ref[0]).astype(BF16)
    for f0 in range(0, a_ref.shape[1], fc):
        a_ref[:, f0:f0 + fc] = (_silu(_dot(hb, wg_ref[:, f0:f0 + fc])) * _dot(hb, wu_ref[:, f0:f0 + fc])).astype(BF16)
    out = x_ref[0] + gate_ref[0] * _dot(a_ref[...], wd_ref[...])
    o_ref[0] = _final_norm(out, fg_ref) if final else out


def _ffn(x, g, scale, shift, gate, weights, *, final_g, tm):
    bsz, seq, d = x.shape
    row = lambda: pl.BlockSpec((1, 1, d), lambda b, i: (b, 0, 0))
    wg, wu, wd = weights
    ff = wg.shape[1]
    in_specs = [
        pl.BlockSpec((1, tm, d), lambda b, i: (b, i, 0)),
        pl.BlockSpec((1, d), lambda b, i: (0, 0)),
        row(), row(), row(),
        pl.BlockSpec((d, ff), lambda b, i: (0, 0)),
        pl.BlockSpec((d, ff), lambda b, i: (0, 0)),
        pl.BlockSpec((ff, d), lambda b, i: (0, 0)),
    ]
    args = [x, g.reshape(1, d), scale, shift, gate, wg, wu, wd]
    final = final_g is not None
    if final:
        in_specs.append(pl.BlockSpec((1, d), lambda b, i: (0, 0)))
        args.append(final_g.reshape(1, d))
    return pl.pallas_call(
        functools.partial(_ffn_kernel, final=final, fc=MXU_WIDTH),
        out_shape=jax.ShapeDtypeStruct((bsz, seq, d), F32),
        grid=(bsz, seq // tm),
        in_specs=in_specs,
        out_specs=pl.BlockSpec((1, tm, d), lambda b, i: (b, i, 0)),
        scratch_shapes=[pltpu.VMEM((tm, ff), BF16)],
        compiler_params=_cparams(("arbitrary", "arbitrary")),
        name="dense_ffn",
    )(*args)


def _route_kernel(x_ref, g_ref, sc_ref, sh_ref, rw_ref, rb_ref, h_ref, r_ref):
    h = _norm_mod(x_ref[0], g_ref[...], sc_ref[0], sh_ref[0])
    h_hi, h_lo = _split_bf16(h)
    h_ref[0] = h_hi
    w_hi, w_lo = _split_bf16(rw_ref[...])
    logits = _dot(h_hi, w_hi) + (_dot(h_hi, w_lo) + _dot(h_lo, w_hi)) + rb_ref[...]
    lane = lax.broadcasted_iota(jnp.int32, logits.shape, 1)
    big = logits.shape[1]
    v1 = jnp.max(logits, axis=-1, keepdims=True)
    i1 = jnp.min(jnp.where(logits == v1, lane, big), axis=-1, keepdims=True)
    rest_l = jnp.where(lane == i1, NEG_INF * 4.0, logits)
    v2 = jnp.max(rest_l, axis=-1, keepdims=True)
    i2 = jnp.min(jnp.where(rest_l == v2, lane, big), axis=-1, keepdims=True)
    e2 = jnp.exp(v2 - v1)
    w1 = 1.0 / (1.0 + e2)
    w2 = e2 / (1.0 + e2)
    r_ref[0] = jnp.where(lane == 0, i1.astype(F32), jnp.where(lane == 1, i2.astype(F32), jnp.where(lane == 2, w1, w2)))


def _route(x, g, scale, shift, rw, rb, *, tm):
    bsz, seq, d = x.shape
    row = lambda: pl.BlockSpec((1, 1, d), lambda b, i: (b, 0, 0))
    return pl.pallas_call(
        _route_kernel,
        out_shape=(jax.ShapeDtypeStruct((bsz, seq, d), BF16), jax.ShapeDtypeStruct((bsz, seq, LANES), F32)),
        grid=(bsz, seq // tm),
        in_specs=[
            pl.BlockSpec((1, tm, d), lambda b, i: (b, i, 0)),
            pl.BlockSpec((1, d), lambda b, i: (0, 0)),
            row(), row(),
            pl.BlockSpec(rw.shape, lambda b, i: (0, 0)),
            pl.BlockSpec(rb.shape, lambda b, i: (0, 0)),
        ],
        out_specs=(pl.BlockSpec((1, tm, d), lambda b, i: (b, i, 0)), pl.BlockSpec((1, tm, LANES), lambda b, i: (b, i, 0))),
        compiler_params=_cparams(("arbitrary", "arbitrary")),
        name="moe_route",
    )(x, g.reshape(1, d), scale, shift, rw, rb)


def _gffn_kernel(te_ref, nu_ref, x_ref, wg_ref, wu_ref, wd_ref, o_ref):
    i = pl.program_id(0)

    @pl.when(i < nu_ref[0])
    def _():
        hb = x_ref[...]
        a = (_silu(_dot(hb, wg_ref[0])) * _dot(hb, wu_ref[0])).astype(BF16)
        o_ref[...] = _dot(a, wd_ref[0]).astype(o_ref.dtype)

    @pl.when(i >= nu_ref[0])
    def _():
        o_ref[...] = jnp.zeros_like(o_ref)


def _grouped_ffn(xs, tile_expert, n_used, wg, wu, wd, *, tmg):
    rows, d = xs.shape
    _, _, fe = wg.shape
    return pl.pallas_call(
        _gffn_kernel,
        out_shape=jax.ShapeDtypeStruct((rows, d), BF16),
        grid_spec=pltpu.PrefetchScalarGridSpec(
            num_scalar_prefetch=2,
            grid=(rows // tmg,),
            in_specs=[
                pl.BlockSpec((tmg, d), lambda i, te, nu: (i, 0)),
                pl.BlockSpec((1, d, fe), lambda i, te, nu: (te[i], 0, 0)),
                pl.BlockSpec((1, d, fe), lambda i, te, nu: (te[i], 0, 0)),
                pl.BlockSpec((1, fe, d), lambda i, te, nu: (te[i], 0, 0)),
            ],
            out_specs=pl.BlockSpec((tmg, d), lambda i, te, nu: (i, 0)),
        ),
        compiler_params=_cparams(("arbitrary",)),
        name="moe_grouped_ffn",
    )(tile_expert, n_used, xs, wg, wu, wd)


def _combine_kernel(x_ref, gate_ref, y_ref, y2_ref, r_ref, *rest, final):
    if final:
        fg_ref, o_ref = rest
    else:
        (o_ref,) = rest
    r = r_ref[0]
    f = r[:, 2:3] * y_ref[0].astype(F32) + r[:, 3:4] * y2_ref[0].astype(F32)
    out = x_ref[0] + gate_ref[0] * f
    o_ref[0] = _final_norm(out, fg_ref) if final else out


def _combine(x, gate, y1, y2, route, *, final_g, tm):
    bsz, seq, d = x.shape
    tile = lambda: pl.BlockSpec((1, tm, d), lambda b, i: (b, i, 0))
    in_specs = [
        tile(),
        pl.BlockSpec((1, 1, d), lambda b, i: (b, 0, 0)),
        tile(), tile(),
        pl.BlockSpec((1, tm, LANES), lambda b, i: (b, i, 0)),
    ]
    args = [x, gate, y1, y2, route]
    final = final_g is not None
    if final:
        in_specs.append(pl.BlockSpec((1, d), lambda b, i: (0, 0)))
        args.append(final_g.reshape(1, d))
    return pl.pallas_call(
        functools.partial(_combine_kernel, final=final),
        out_shape=jax.ShapeDtypeStruct((bsz, seq, d), F32),
        grid=(bsz, seq // tm),
        in_specs=in_specs,
        out_specs=pl.BlockSpec((1, tm, d), lambda b, i: (b, i, 0)),
        compiler_params=_cparams(("arbitrary", "arbitrary")),
        name="moe_combine",
    )(*args)


def _moe(x, g, scale, shift, gate, rw, rb, wg, wu, wd, *, final_g, tm, tmg):
    bsz, seq, d = x.shape
    ne = wg.shape[0]
    tokens = bsz * seq
    h, route = _route(x, g, scale, shift, rw, rb, tm=tm)
    e_flat = route[:, :, 0:2].astype(jnp.int32).reshape(tokens * 2)
    onehot = (e_flat[:, None] == jnp.arange(ne, dtype=jnp.int32)[None, :]).astype(jnp.int32)
    csum = jnp.cumsum(onehot, axis=0)
    counts = csum[-1]
    padded = ((counts + tmg - 1) // tmg) * tmg
    ends = jnp.cumsum(padded)
    offs = ends - padded
    rank = jnp.take_along_axis(csum, e_flat[:, None], axis=1)[:, 0] - 1
    dest = offs[e_flat] + rank
    rows = tokens * 2 + ne * tmg
    src = jnp.zeros((rows,), jnp.int32).at[dest].set(jnp.arange(tokens * 2, dtype=jnp.int32) // 2,
                                                      unique_indices=True, mode="promise_in_bounds")
    tile_start = jnp.arange(rows // tmg, dtype=jnp.int32) * tmg
    tile_expert = jnp.minimum(jnp.sum((tile_start[:, None] >= ends[None, :]).astype(jnp.int32), axis=1), ne - 1)
    n_used = (ends[-1] // tmg).astype(jnp.int32).reshape(1)
    rows_of = lambda a, idx: a.at[idx].get(mode="promise_in_bounds")
    xs = rows_of(h.reshape(tokens, d), src)
    ys = _grouped_ffn(xs, tile_expert, n_used, wg, wu, wd, tmg=tmg)
    dest2 = dest.reshape(tokens, 2)
    y1 = rows_of(ys, dest2[:, 0]).reshape(bsz, seq, d)
    y2 = rows_of(ys, dest2[:, 1]).reshape(bsz, seq, d)
    return _combine(x, gate, y1, y2, route, final_g=final_g, tm=tm)


def _block_diag(blocks):
    n, r, c = blocks.shape
    eye = jnp.eye(n, dtype=blocks.dtype)
    return (eye[:, None, :, None] * blocks[:, :, None, :]).reshape(n * r, n * c)


def _rope_tables(seq, heads_q):
    inv = ROPE_THETA ** (-jnp.arange(0, HEAD_DIM, 2, dtype=F32) / HEAD_DIM)
    ang = jnp.arange(seq, dtype=F32)[:, None] * inv[None, :]
    cos, sin = jnp.cos(ang), jnp.sin(ang)
    cos_h = jnp.concatenate([cos, cos], axis=1)
    sin_h = jnp.concatenate([-sin, sin], axis=1)
    one_h = jnp.ones_like(cos_h)
    zero_h = jnp.zeros_like(cos_h)
    cos_t = jnp.concatenate([cos_h] * heads_q + [one_h, one_h] + [cos_h, one_h] * 3, axis=1)
    sin_t = jnp.concatenate([sin_h] * heads_q + [zero_h, zero_h] + [sin_h, zero_h] * 3, axis=1)
    return cos_t, sin_t


def kernel(x, c, mod_w, mod_b, norm_mix_g, norm_ffn_g, w_in, s5_lambda_re, s5_lambda_im, s5_log_step, s5_b_re, s5_b_im, s5_c_re, s5_c_im, s5_d, s5_w_glu, nsa_pe_k, nsa_pe_v, nsa_cmp_k_w1, nsa_cmp_k_w2, nsa_cmp_v_w1, nsa_cmp_v_w2, lru_conv_w, lru_conv_b, lru_w_a, lru_b_a, lru_w_x, lru_b_x, lru_lambda, sc_conv_w, w_branch, w_out, ffn_w_gate, ffn_w_up, ffn_w_down, moe_router_w, moe_router_b, moe_w_gate, moe_w_up, moe_w_down, final_norm_g):
    bsz, seq, d = x.shape
    depth = mod_w.shape[0]
    w = d // N_BRANCH
    heads = w // HEAD_DIM
    groups = w // S5_GROUP
    assert bsz == SUBLANES, "time-major scan blocks hold one timestep per sublane tile"
    tm = min(512, seq)
    tq = 128
    steps = min(64, seq)
    nsel = seq // L_SEL
    assert nsel <= LANES and seq % tm == 0 and seq >= WINDOW + tq

    mod = _modulation(c, mod_w, mod_b)
    mod = mod.reshape(depth, bsz, 6, 1, d)

    sizes = (w, w, HEAD_DIM, HEAD_DIM, HEAD_DIM, HEAD_DIM, HEAD_DIM, HEAD_DIM, heads * 3, w, w, w, w, w, N_BRANCH * d)
    offs = np.concatenate([[0], np.cumsum(sizes)])
    col = lambda wmat_t, k: wmat_t[offs[k]:offs[k + 1], :]

    cos_t, sin_t = _rope_tables(seq, heads)
    n_cmp = (seq - L_CMP) // CMP_STRIDE + 1
    nblk = seq // CMP_STRIDE
    cmp_start = np.arange(nblk) * CMP_STRIDE
    sel_start = np.arange(nsel) * L_SEL
    ovl = np.clip(np.minimum(cmp_start[:, None] + L_CMP, sel_start[None, :] + L_SEL)
                  - np.maximum(cmp_start[:, None], sel_start[None, :]), 0, None) / L_CMP
    ovl[n_cmp:] = 0.0
    overlap_t = jnp.asarray(np.pad(ovl, ((0, 0), (0, LANES - nsel))).T, BF16)
    onehot = jnp.asarray((np.arange(seq)[:, None] // L_SEL) == np.arange(nsel)[None, :], BF16)

    for li in range(depth):
        sh1, sc1, gt1, sh2, sc2, gt2 = (mod[li, :, k] for k in range(6))
        wl = w_in[li].T
        w_scan = jnp.concatenate([col(wl, 0), col(wl, 9), col(wl, 10), col(wl, 11), col(wl, 12), col(wl, 13)], axis=0).astype(BF16)
        gpad = jnp.zeros((LANES - heads * 3, d), wl.dtype)
        w_nsa = jnp.concatenate([col(wl, 1), col(wl, 8), gpad] + [col(wl, k) for k in range(2, 8)], axis=0).astype(BF16)
        w_gm = (0.5 * col(wl, 14)).astype(BF16)
        g1 = norm_mix_g[li]
        z_scan, z_nsa, g_merge = _project(x, g1, sc1, sh1, w_scan, w_nsa, w_gm, (cos_t, sin_t), tt=steps)

        a_re, a_im, bb_re, bb_im = _s5_params(s5_lambda_re[li], s5_lambda_im[li], s5_log_step[li], s5_b_re[li], s5_b_im[li])
        to_bd = lambda m: _block_diag(jnp.swapaxes(m.reshape(groups, S5_STATE, S5_GROUP), 1, 2))
        s5b = jnp.concatenate([to_bd(bb_re), to_bd(bb_im)], axis=1).astype(BF16)
        c_bd = lambda m: _block_diag(jnp.swapaxes(m, 1, 2))
        s5c = jnp.concatenate([c_bd(s5_c_re[li]), -c_bd(s5_c_im[li])], axis=0).astype(BF16)
        r1 = lambda a: a.reshape(1, -1)
        y_scan = _scan_mixers(
            z_scan, bsz, s5b, s5c, r1(a_re), r1(a_im), r1(s5_d[li]), s5_w_glu[li].astype(BF16),
            lru_conv_w[li], r1(lru_conv_b[li]), _block_diag(lru_w_a[li]).astype(BF16), r1(lru_b_a[li]),
            _block_diag(lru_w_x[li]).astype(BF16), r1(lru_b_x[li]), r1(lru_lambda[li]), sc_conv_w[li], steps=steps)

        kv0 = heads * HEAD_DIM + LANES
        kvs = [z_nsa[:, :, kv0 + k * HEAD_DIM:kv0 + (k + 1) * HEAD_DIM] for k in range(6)]
        kc, vc, ks, vs, kw, vw = kvs
        grp = lambda a: a.reshape(bsz, nblk, CMP_STRIDE * HEAD_DIM)
        k_cmp, v_cmp = _compress(grp(kc), grp(vc), nsa_pe_k[li], nsa_pe_v[li], nsa_cmp_k_w1[li], nsa_cmp_k_w2[li], nsa_cmp_v_w1[li], nsa_cmp_v_w2[li])
        o_cmp, qa = _cmp_select(z_nsa, k_cmp, jnp.swapaxes(v_cmp, 1, 2).astype(BF16), overlap_t, heads=heads, tq=2 * tq, nsel=nsel)
        tk = min(512, seq)
        ka = jnp.concatenate([ks.astype(BF16), jnp.broadcast_to(onehot, (bsz, seq, nsel))], axis=2)
        ka = ka.reshape(bsz, seq // tk, tk, HEAD_DIM + nsel)
        kwa = jnp.concatenate([kw.astype(BF16), jnp.zeros((bsz, seq, nsel), BF16)], axis=2)
        kwa = kwa.reshape(bsz, seq // tq, tq, HEAD_DIM + nsel)
        ones_pad = jnp.concatenate([jnp.ones((bsz, seq, 1), BF16), jnp.zeros((bsz, seq, SUBLANES - 1), BF16)], axis=2)
        tiles_t = lambda a, n: jnp.swapaxes(
            jnp.concatenate([a.astype(BF16), ones_pad], axis=2).reshape(bsz, seq // n, n, HEAD_DIM + SUBLANES), 2, 3)
        y_b = _attend(qa, ka, tiles_t(vs, tk), kwa, tiles_t(vw, tq), z_nsa, (heads * HEAD_DIM) // LANES, o_cmp, tq=tq, tk=tk)

        x = _merge(y_scan, y_b, g_merge, x, w_branch[li].astype(BF16), w_out[li].astype(BF16), gt1, tt=steps)

        final_g = final_norm_g if li == depth - 1 else None
        j = li // 2
        if li % 2 == 0:
            weights = (ffn_w_gate[j].astype(BF16), ffn_w_up[j].astype(BF16), ffn_w_down[j].astype(BF16))
            x = _ffn(x, norm_ffn_g[li], sc2, sh2, gt2, weights, final_g=final_g, tm=tm)
        else:
            rw = jnp.pad(moe_router_w[j], ((0, 0), (0, LANES - N_EXPERTS)))
            rb = jnp.pad(moe_router_b[j], (0, LANES - N_EXPERTS), constant_values=NEG_INF).reshape(1, LANES)
            x = _moe(x, norm_ffn_g[li], sc2, sh2, gt2, rw, rb, moe_w_gate[j].astype(BF16), moe_w_up[j].astype(BF16),
                     moe_w_down[j].astype(BF16), final_g=final_g, tm=tm, tmg=512)
    return x
```

```python
import functools
import math

import jax
import jax.numpy as jnp
import numpy as np
from jax import lax
from jax.experimental import pallas as pl
from jax.experimental.pallas import tpu as pltpu

F32 = jnp.float32
BF16 = jnp.bfloat16

N_BRANCH = 4
S5_GROUP = 16
S5_STATE = 64
HEAD_DIM = 64
L_CMP = 32
CMP_STRIDE = 16
L_SEL = 64
TOP_N = 16
WINDOW = 512
ROPE_THETA = 10000.0
FORCED_SCORE = 1.0e4
NEG_INF = -1.0e30
LRU_CONV = 4
LRU_C = 8.0
SC_CONV = 3
N_EXPERTS = 8
RMS_EPS = 1e-6
LOG2E = math.log2(math.e)

LANES = 128
SUBLANES = 8
MXU_WIDTH = 256
VMEM_LIMIT = 52 * 1024 * 1024

HI = lax.Precision.HIGHEST


def _cparams(sem):
    return pltpu.CompilerParams(dimension_semantics=sem, vmem_limit_bytes=VMEM_LIMIT)


def _dot(a, b, precision=None):
    return jnp.dot(a, b, preferred_element_type=F32, precision=precision)


def _dot_t(a, b, precision=None):
    return lax.dot_general(a, b, (((1,), (1,)), ((), ())), preferred_element_type=F32, precision=precision)


def _gelu(x):
    return 0.5 * x * (1.0 + jnp.tanh(math.sqrt(2.0 / math.pi) * (x + 0.044715 * (x * x * x))))


def _sigmoid(x):
    return 0.5 * jnp.tanh(0.5 * x) + 0.5


def _silu(x):
    hx = 0.5 * x
    return hx * jnp.tanh(hx) + hx


def _norm_mod(x, g, scale, shift):
    var = jnp.mean(x * x, axis=-1, keepdims=True)
    return (x * lax.rsqrt(var + RMS_EPS) * g) * (1.0 + scale) + shift


def _rot_half(z):
    n = z.shape[-1]
    lane = lax.broadcasted_iota(jnp.int32, z.shape, z.ndim - 1)
    first = (lane % HEAD_DIM) < (HEAD_DIM // 2)
    return jnp.where(first, pltpu.roll(z, n - HEAD_DIM // 2, z.ndim - 1), pltpu.roll(z, HEAD_DIM // 2, z.ndim - 1))


def _cast_kernel(x_ref, o_ref):
    o_ref[...] = x_ref[...].astype(o_ref.dtype)


def _to_bf16(a):
    n, r, c = a.shape
    return pl.pallas_call(
        _cast_kernel,
        out_shape=jax.ShapeDtypeStruct(a.shape, BF16),
        grid=(n,),
        in_specs=[pl.BlockSpec((1, r, c), lambda i: (i, 0, 0))],
        out_specs=pl.BlockSpec((1, r, c), lambda i: (i, 0, 0)),
        compiler_params=_cparams(("arbitrary",)),
        name="cast_bf16",
    )(a)


def _mod_kernel(c_ref, w_ref, b_ref, o_ref):
    c = c_ref[...]
    o_ref[0] = _dot(_silu(c), w_ref[0], HI) + b_ref[0]


def _modulation(c, mod_w, mod_b):
    depth, d, n = mod_w.shape
    bsz = c.shape[0]
    tn = 1024
    return pl.pallas_call(
        _mod_kernel,
        out_shape=jax.ShapeDtypeStruct((depth, bsz, n), F32),
        grid=(depth, n // tn),
        in_specs=[
            pl.BlockSpec((bsz, d), lambda l, j: (0, 0)),
            pl.BlockSpec((1, d, tn), lambda l, j: (l, 0, j)),
            pl.BlockSpec((1, 1, tn), lambda l, j: (l, 0, j)),
        ],
        out_specs=pl.BlockSpec((1, bsz, tn), lambda l, j: (l, 0, j)),
        compiler_params=_cparams(("arbitrary", "arbitrary")),
        name="modulation",
    )(c, mod_w, mod_b.reshape(depth, 1, n))


def _proj_kernel(x_ref, g_ref, sc_ref, sh_ref, ws_ref, wn_ref, wg_ref, cos_ref, sin_ref,
                 zs_ref, zn_ref, gm_ref, h_ref):
    nb, tt, _ = x_ref.shape
    rows = nb * tt
    nlc = h_ref.shape[0]
    parts = []
    for b in range(nb):
        hb = _norm_mod(x_ref[b], g_ref[...], sc_ref[b], sh_ref[b])
        for c in range(nlc):
            h_ref[c, pl.ds(b, tt, stride=nb), :] = hb[:, c * LANES:(c + 1) * LANES]
        parts.append(hb.astype(BF16))
    h_bt = jnp.concatenate(parts, axis=0)
    h_tb = jnp.concatenate([h_ref[c] for c in range(nlc)], axis=1).astype(BF16)
    for n0 in range(0, ws_ref.shape[0], 2 * MXU_WIDTH):
        zs_ref[:, n0:n0 + 2 * MXU_WIDTH] = _dot_t(h_tb, ws_ref[n0:n0 + 2 * MXU_WIDTH, :])
    for n0 in range(0, wn_ref.shape[0], MXU_WIDTH):
        z = _dot_t(h_bt, wn_ref[n0:n0 + MXU_WIDTH, :])
        cos = jnp.concatenate([cos_ref[:, n0:n0 + MXU_WIDTH]] * nb, axis=0)
        sin = jnp.concatenate([sin_ref[:, n0:n0 + MXU_WIDTH]] * nb, axis=0)
        zn_ref[:, :, n0:n0 + MXU_WIDTH] = (z * cos + _rot_half(z) * sin).reshape(nb, tt, MXU_WIDTH)
    for n0 in range(0, wg_ref.shape[0], 2 * MXU_WIDTH):
        z = _dot_t(h_bt, wg_ref[n0:n0 + 2 * MXU_WIDTH, :])
        gm_ref[:, :, n0:n0 + 2 * MXU_WIDTH] = z.astype(gm_ref.dtype).reshape(nb, tt, 2 * MXU_WIDTH)


def _project(x, g, scale, shift, w_scan, w_nsa, w_gm, rope_tabs, *, tt):
    bsz, seq, d = x.shape
    ns, nn, ng = w_scan.shape[0], w_nsa.shape[0], w_gm.shape[0]
    full = lambda a: pl.BlockSpec(a.shape, lambda i: (0,) * a.ndim)
    return pl.pallas_call(
        _proj_kernel,
        out_shape=(
            jax.ShapeDtypeStruct((seq * bsz, ns), F32),
            jax.ShapeDtypeStruct((bsz, seq, nn), F32),
            jax.ShapeDtypeStruct((bsz, seq, ng), BF16),
        ),
        grid=(seq // tt,),
        in_specs=[
            pl.BlockSpec((bsz, tt, d), lambda i: (0, i, 0)),
            pl.BlockSpec((1, d), lambda i: (0, 0)),
            pl.BlockSpec((bsz, 1, d), lambda i: (0, 0, 0)),
            pl.BlockSpec((bsz, 1, d), lambda i: (0, 0, 0)),
            full(w_scan), full(w_nsa), full(w_gm),
            pl.BlockSpec((tt, nn), lambda i: (i, 0)),
            pl.BlockSpec((tt, nn), lambda i: (i, 0)),
        ],
        out_specs=(
            pl.BlockSpec((tt * bsz, ns), lambda i: (i, 0)),
            pl.BlockSpec((bsz, tt, nn), lambda i: (0, i, 0)),
            pl.BlockSpec((bsz, tt, ng), lambda i: (0, i, 0)),
        ),
        scratch_shapes=[pltpu.VMEM((d // LANES, tt * bsz, LANES), F32)],
        compiler_params=_cparams(("arbitrary",)),
        name="proj_in",
    )(x, g.reshape(1, d), scale, shift, w_scan, w_nsa, w_gm, *rope_tabs)


def _s5_param_kernel(lre_ref, lim_ref, ls_ref, bre_ref, bim_ref, are_ref, aim_ref, bbre_ref, bbim_ref):
    lre = lre_ref[...]
    lim = lim_ref[...]
    step = jnp.exp(ls_ref[...])
    mag = jnp.exp(lre * step)
    ar = mag * jnp.cos(lim * step)
    ai = mag * jnp.sin(lim * step)
    xr = ar - 1.0
    den = lre * lre + lim * lim
    cr = (xr * lre + ai * lim) / den
    ci = (ai * lre - xr * lim) / den
    bre = bre_ref[...]
    bim = bim_ref[...]
    are_ref[...] = ar
    aim_ref[...] = ai
    bbre_ref[...] = cr * bre - ci * bim
    bbim_ref[...] = cr * bim + ci * bre


def _s5_params(lam_re, lam_im, log_step, b_re, b_im):
    g, n = lam_re.shape
    c = b_re.shape[-1]
    rows = g * n
    col = lambda a: a.reshape(rows, 1)
    ls = jnp.broadcast_to(log_step[:, None], (g, n))
    out = pl.pallas_call(
        _s5_param_kernel,
        out_shape=(
            jax.ShapeDtypeStruct((rows, 1), F32),
            jax.ShapeDtypeStruct((rows, 1), F32),
            jax.ShapeDtypeStruct((rows, c), F32),
            jax.ShapeDtypeStruct((rows, c), F32),
        ),
        name="s5_params",
    )(col(lam_re), col(lam_im), col(ls), b_re.reshape(rows, c), b_im.reshape(rows, c))
    return out


def _scan_kernel(z_ref, s5b_ref, s5c_ref, are_ref, aim_ref, d_ref, wglu_ref,
                 cw_ref, cb_ref, wa_ref, ba_ref, wx_ref, bx_ref, lam_ref, scw_ref,
                 o_ref,
                 bu_ref, hre_ref, him_ref, xe_ref, hl_ref, pe_ref, ab_ref, bb_ref, *, nb, steps):
    c = pl.program_id(0)
    rows = nb * steps
    w = d_ref.shape[1]
    ns = are_ref.shape[1]

    @pl.when(c == 0)
    def _():
        hre_ref[...] = jnp.zeros_like(hre_ref)
        him_ref[...] = jnp.zeros_like(him_ref)
        hl_ref[...] = jnp.zeros_like(hl_ref)
        xe_ref[0:(LRU_CONV - 1) * nb, :] = jnp.zeros(((LRU_CONV - 1) * nb, w), F32)
        pe_ref[0:(SC_CONV - 1) * nb, :] = jnp.zeros(((SC_CONV - 1) * nb, w), F32)

    u = z_ref[:, 0:w]
    bu_ref[...] = _dot(u.astype(BF16), s5b_ref[...])
    a_re = jnp.broadcast_to(are_ref[...], (nb, ns))
    a_im = jnp.broadcast_to(aim_ref[...], (nb, ns))

    def s5_step(t, carry):
        h_re, h_im = carry
        r0 = pl.multiple_of(t * nb, nb)
        b_re = bu_ref[pl.ds(r0, nb), 0:ns]
        b_im = bu_ref[pl.ds(r0, nb), ns:2 * ns]
        n_re = a_re * h_re - a_im * h_im + b_re
        n_im = a_re * h_im + a_im * h_re + b_im
        bu_ref[pl.ds(r0, nb), 0:ns] = n_re
        bu_ref[pl.ds(r0, nb), ns:2 * ns] = n_im
        return n_re, n_im

    h_re, h_im = lax.fori_loop(0, steps, s5_step, (hre_ref[...], him_ref[...]))
    hre_ref[...] = h_re
    him_ref[...] = h_im
    y = _dot(bu_ref[...].astype(BF16), s5c_ref[...]) + d_ref[...] * u
    zg = _gelu(y)
    def put(branch, val):
        for k in range(w // LANES):
            o_ref[branch * (w // LANES) + k] = val[:, k * LANES:(k + 1) * LANES]

    put(0, zg * _sigmoid(_dot(zg.astype(BF16), wglu_ref[...])))

    nt = (LRU_CONV - 1) * nb
    x = z_ref[:, w:2 * w]
    xe_ref[nt:nt + rows, :] = x
    xc = cb_ref[...] + cw_ref[LRU_CONV - 1:LRU_CONV, :] * x
    for k in range(LRU_CONV - 1):
        xc = xc + cw_ref[k:k + 1, :] * xe_ref[k * nb:k * nb + rows, :]
    xe_ref[0:nt, :] = xe_ref[rows:rows + nt, :]
    xcb = xc.astype(BF16)
    r = _sigmoid(_dot(xcb, wa_ref[...]) + ba_ref[...])
    gi = _sigmoid(_dot(xcb, wx_ref[...]) + bx_ref[...])
    nl = -lam_ref[...]
    softplus = jnp.maximum(nl, 0.0) + jnp.log(1.0 + jnp.exp(-jnp.abs(nl)))
    log_a = (-LRU_C) * r * softplus
    a = jnp.exp(log_a)
    ab_ref[...] = a
    bb_ref[...] = jnp.sqrt(1.0 - a * a) * (gi * xc)

    def lru_step(t, h):
        r0 = pl.multiple_of(t * nb, nb)
        h = ab_ref[pl.ds(r0, nb), :] * h + bb_ref[pl.ds(r0, nb), :]
        bb_ref[pl.ds(r0, nb), :] = h
        return h

    hl_ref[...] = lax.fori_loop(0, steps, lru_step, hl_ref[...])
    put(1, bb_ref[...] * _gelu(z_ref[:, 2 * w:3 * w]))

    npt = (SC_CONV - 1) * nb
    p = z_ref[:, 4 * w:5 * w] * z_ref[:, 5 * w:6 * w]
    pe_ref[npt:npt + rows, :] = p
    cv = scw_ref[SC_CONV - 1:SC_CONV, :] * p
    for k in range(SC_CONV - 1):
        cv = cv + scw_ref[k:k + 1, :] * pe_ref[k * nb:k * nb + rows, :]
    pe_ref[0:npt, :] = pe_ref[rows:rows + npt, :]
    put(2, z_ref[:, 3 * w:4 * w] * cv)


def _scan_mixers(z_tm, nb, s5b, s5c, a_re, a_im, d_skip, w_glu, conv_w, conv_b, wa, ba, wx, bx, lam, sc_w, *, steps):
    total, zc = z_tm.shape
    rows = nb * steps
    w = zc // 6
    ns = a_re.shape[1]
    full = lambda a: pl.BlockSpec(a.shape, lambda c: (0,) * a.ndim)
    params = [s5b, s5c, a_re, a_im, d_skip, w_glu, conv_w, conv_b, wa, ba, wx, bx, lam, sc_w]
    return pl.pallas_call(
        functools.partial(_scan_kernel, nb=nb, steps=steps),
        out_shape=jax.ShapeDtypeStruct((3 * w // LANES, total, LANES), F32),
        grid=(total // rows,),
        in_specs=[pl.BlockSpec((rows, zc), lambda c: (c, 0))] + [full(a) for a in params],
        out_specs=pl.BlockSpec((3 * w // LANES, rows, LANES), lambda c: (0, c, 0)),
        scratch_shapes=[
            pltpu.VMEM((rows, 2 * ns), F32),
            pltpu.VMEM((nb, ns), F32),
            pltpu.VMEM((nb, ns), F32),
            pltpu.VMEM((rows + (LRU_CONV - 1) * nb, w), F32),
            pltpu.VMEM((nb, w), F32),
            pltpu.VMEM((rows + (SC_CONV - 1) * nb, w), F32),
            pltpu.VMEM((rows, w), F32),
            pltpu.VMEM((rows, w), F32),
        ],
        compiler_params=_cparams(("arbitrary",)),
        name="scan_mixers",
    )(z_tm, *params)


def _compress_kernel(gk_ref, gv_ref, pek_ref, pev_ref, wk1_ref, wk2_ref, wv1_ref, wv2_ref, ko_ref, vo_ref):
    half = gk_ref.shape[2]
    nblk = gk_ref.shape[1]

    def phi(g, pe, w1_ref, w2_ref):
        top = _dot(g, w1_ref[0:half, :], HI)
        bot = _dot(g, w1_ref[half:2 * half, :], HI)
        hid = top + pltpu.roll(bot, nblk - 1, 0) + _dot(pe, w1_ref[...], HI)
        return _dot(_gelu(hid), w2_ref[...], HI)

    ko_ref[0] = phi(gk_ref[0], pek_ref[...], wk1_ref, wk2_ref)
    vo_ref[0] = phi(gv_ref[0], pev_ref[...], wv1_ref, wv2_ref)


def _compress(gk, gv, pe_k, pe_v, wk1, wk2, wv1, wv2):
    bsz, nblk, half = gk.shape
    hd = wk2.shape[1]
    full = lambda a: pl.BlockSpec(a.shape, lambda b: (0,) * a.ndim)
    pek = pe_k.reshape(1, -1)
    pev = pe_v.reshape(1, -1)
    params = [pek, pev, wk1, wk2, wv1, wv2]
    blk = pl.BlockSpec((1, nblk, half), lambda b: (b, 0, 0))
    oblk = pl.BlockSpec((1, nblk, hd), lambda b: (b, 0, 0))
    return pl.pallas_call(
        _compress_kernel,
        out_shape=(jax.ShapeDtypeStruct((bsz, nblk, hd), F32),) * 2,
        grid=(bsz,),
        in_specs=[blk, blk] + [full(a) for a in params],
        out_specs=(oblk, oblk),
        compiler_params=_cparams(("arbitrary",)),
        name="nsa_compress",
    )(gk, gv, *params)


def _split_bf16(a):
    hi = a.astype(BF16)
    return hi, (a - hi.astype(F32)).astype(BF16)


def _cmp_kernel(q_ref, kc_ref, vct_ref, ovt_ref, ocmp_ref, qa_ref, *, heads, tq):
    i = pl.program_id(1)
    q_t = (q_ref[0] * HEAD_DIM ** -0.5).T
    kc_hi, kc_lo = _split_bf16(kc_ref[0])
    vct = vct_ref[0]
    nblk = kc_hi.shape[0]
    t = i * tq + lax.broadcasted_iota(jnp.int32, (nblk, tq), 1)
    n_id = lax.broadcasted_iota(jnp.int32, (nblk, tq), 0)
    visible = (n_id * CMP_STRIDE + (L_CMP - 1)) <= t
    psum = jnp.zeros((nblk, tq), F32)
    for h in range(heads):
        q_hi, q_lo = _split_bf16(q_t[h * HEAD_DIM:(h + 1) * HEAD_DIM, :])
        s = _dot(kc_hi, q_hi) + (_dot(kc_hi, q_lo) + _dot(kc_lo, q_hi))
        s = jnp.where(visible, s, NEG_INF)
        m = jnp.max(s, axis=0, keepdims=True)
        e = jnp.where(visible, jnp.exp(s - m), 0.0)
        den = jnp.sum(e, axis=0, keepdims=True)
        p = e / jnp.where(den > 0.0, den, 1.0)
        psum = psum + p
        ocmp_ref[0, h * HEAD_DIM:(h + 1) * HEAD_DIM, :] = _dot(vct, p.astype(BF16))

    p_hi, p_lo = _split_bf16(psum)
    imp_t = _dot(ovt_ref[...], p_hi) + _dot(ovt_ref[...], p_lo)
    nslot = ovt_ref.shape[0]
    tt = i * tq + lax.broadcasted_iota(jnp.int32, (nslot, tq), 1)
    j = lax.broadcasted_iota(jnp.int32, (nslot, tq), 0)
    cur = tt // L_SEL
    forced = (j == 0) | (j == cur) | (j == cur - 1)
    future = (j * L_SEL) > tt
    imp_t = jnp.where(future, -1.0, jnp.where(forced, FORCED_SCORE, imp_t))
    nreal = qa_ref.shape[2] - HEAD_DIM
    cand = imp_t[0:nreal, :]
    jrow = lax.broadcasted_iota(jnp.int32, (nreal, tq), 0)
    bias_t = jnp.full((nreal, tq), NEG_INF, F32)
    for _ in range(TOP_N):
        best = jnp.max(cand, axis=0, keepdims=True)
        first = jnp.min(jnp.where(cand == best, jrow, nreal), axis=0, keepdims=True)
        hit = jrow == first
        bias_t = jnp.where(hit, 0.0, bias_t)
        cand = jnp.where(hit, -2.0, cand)
    for h in range(heads):
        qa_ref[0, h] = jnp.concatenate([q_t[h * HEAD_DIM:(h + 1) * HEAD_DIM, :] * LOG2E, bias_t], axis=0).astype(BF16)


def _cmp_select(q, k_cmp, v_cmp_t, overlap_t, *, heads, tq, nsel):
    bsz, seq, _ = q.shape
    qd = heads * HEAD_DIM
    nblk = k_cmp.shape[1]
    return pl.pallas_call(
        functools.partial(_cmp_kernel, heads=heads, tq=tq),
        out_shape=(
            jax.ShapeDtypeStruct((bsz, qd, seq), F32),
            jax.ShapeDtypeStruct((bsz, heads, HEAD_DIM + nsel, seq), BF16),
        ),
        grid=(bsz, seq // tq),
        in_specs=[
            pl.BlockSpec((1, tq, qd), lambda b, i: (b, i, 0)),
            pl.BlockSpec((1, nblk, HEAD_DIM), lambda b, i: (b, 0, 0)),
            pl.BlockSpec((1, HEAD_DIM, nblk), lambda b, i: (b, 0, 0)),
            pl.BlockSpec(overlap_t.shape, lambda b, i: (0, 0)),
        ],
        out_specs=(
            pl.BlockSpec((1, qd, tq), lambda b, i: (b, 0, i)),
            pl.BlockSpec((1, heads, HEAD_DIM + nsel, tq), lambda b, i: (b, 0, 0, i)),
        ),
        compiler_params=_cparams(("arbitrary", "arbitrary")),
        name="nsa_cmp_select",
    )(q, k_cmp, v_cmp_t, overlap_t)


def _attend_kernel(qa_ref, ka_ref, vst_ref, kwa_ref, vwt_ref, gn_ref, ocmp_ref, o_ref,
                   m_ref, acc_ref, s0_ref, s1_ref, *, heads, tq, tk, nwin):
    i = pl.program_id(1)
    cols = heads * tq
    qat = jnp.concatenate([qa_ref[0, h] for h in range(heads)], axis=1)
    t_row = i * tq + (lax.broadcasted_iota(jnp.int32, (1, cols), 1) % tq)

    m_ref[...] = jnp.full(m_ref.shape, NEG_INF, F32)
    acc_ref[...] = jnp.zeros(acc_ref.shape, F32)
    key_id = lax.broadcasted_iota(jnp.int32, (tk, cols), 0)
    n_full = (i * tq + 1) // tk
    s0_ref[...] = _dot(ka_ref[0, 0], qat)

    st = jnp.maximum(i + 1 - nwin, 0)
    kw = kwa_ref[0, pl.ds(st, nwin)].reshape(nwin * tq, kwa_ref.shape[3])
    s = _dot(kw, qat)
    dist = t_row - (st * tq + lax.broadcasted_iota(jnp.int32, (nwin * tq, cols), 0))
    s = jnp.where((dist >= 0) & (dist < WINDOW), s, NEG_INF)
    m = jnp.max(s, axis=0, keepdims=True)
    pb = jnp.exp2(s - m).astype(BF16)
    ow = _dot(vwt_ref[0, st], pb[0:tq, :])
    for j in range(1, nwin):
        ow = ow + _dot(vwt_ref[0, st + j], pb[j * tq:(j + 1) * tq, :])
    owin = ow[0:HEAD_DIM, :] / ow[HEAD_DIM:HEAD_DIM + 1, :]

    def softmax_tile(kt, s, masked):
        if masked:
            s = jnp.where(kt * tk + key_id <= t_row, s, NEG_INF)
        m_old = m_ref[...]
        m_new = jnp.maximum(m_old, jnp.max(s, axis=0, keepdims=True))
        p = jnp.exp2(s - m_new).astype(BF16)
        acc_ref[...] = jnp.exp2(m_old - m_new) * acc_ref[...] + _dot(vst_ref[0, kt], p)
        m_ref[...] = m_new

    def pair_body(j, carry):
        s1_ref[...] = _dot(ka_ref[0, 2 * j + 1], qat)
        softmax_tile(2 * j, s0_ref[...], False)
        s0_ref[...] = _dot(ka_ref[0, 2 * j + 2], qat)
        softmax_tile(2 * j + 1, s1_ref[...], False)
        return carry

    n_pair = n_full // 2
    lax.fori_loop(0, n_pair, pair_body, 0)
    s1_ref[...] = _dot(ka_ref[0, 2 * n_pair + 1], qat)
    softmax_tile(2 * n_pair, s0_ref[...], True)
    softmax_tile(2 * n_pair + 1, s1_ref[...], True)

    osel = acc_ref[0:HEAD_DIM, :] / acc_ref[HEAD_DIM:HEAD_DIM + 1, :]

    g_t = _sigmoid(gn_ref[0]).T
    outs = []
    for h in range(heads):
        cs = slice(h * tq, (h + 1) * tq)
        outs.append(g_t[3 * h:3 * h + 1, :] * ocmp_ref[0, h * HEAD_DIM:(h + 1) * HEAD_DIM, :]
                    + g_t[3 * h + 1:3 * h + 2, :] * osel[:, cs]
                    + g_t[3 * h + 2:3 * h + 3, :] * owin[:, cs])
    o_ref[0] = jnp.concatenate(outs, axis=0).T


def _attend(qa, ka, vst, kwa, vwt, z_nsa, gn_block, o_cmp_t, *, tq, tk):
    bsz, heads, ka_w, seq = qa.shape
    qd = heads * HEAD_DIM
    nwin = WINDOW // tq + 1
    return pl.pallas_call(
        functools.partial(_attend_kernel, heads=heads, tq=tq, tk=tk, nwin=nwin),
        out_shape=jax.ShapeDtypeStruct((bsz, seq, qd), F32),
        grid=(bsz, seq // tq),
        in_specs=[
            pl.BlockSpec((1, heads, ka_w, tq), lambda b, i: (b, 0, 0, i)),
            pl.BlockSpec((1,) + ka.shape[1:], lambda b, i: (b, 0, 0, 0)),
            pl.BlockSpec((1,) + vst.shape[1:], lambda b, i: (b, 0, 0, 0)),
            pl.BlockSpec((1,) + kwa.shape[1:], lambda b, i: (b, 0, 0, 0)),
            pl.BlockSpec((1,) + vwt.shape[1:], lambda b, i: (b, 0, 0, 0)),
            pl.BlockSpec((1, tq, LANES), lambda b, i: (b, i, gn_block)),
            pl.BlockSpec((1, qd, tq), lambda b, i: (b, 0, i)),
        ],
        out_specs=pl.BlockSpec((1, tq, qd), lambda b, i: (b, i, 0)),
        scratch_shapes=[
            pltpu.VMEM((1, heads * tq), F32),
            pltpu.VMEM((vst.shape[2], heads * tq), F32),
            pltpu.VMEM((tk, heads * tq), F32),
            pltpu.VMEM((tk, heads * tq), F32),
        ],
        compiler_params=_cparams(("arbitrary", "arbitrary")),
        name="nsa_attend",
    )(qa, ka, vst, kwa, vwt, z_nsa, o_cmp_t)


def _merge_kernel(ys_ref, yb_ref, gm_ref, x_ref, wb_ref, wo_ref, gate_ref, o_ref):
    nb, tt, w = yb_ref.shape
    d = x_ref.shape[2]
    rows = nb * tt
    def slab(c):
        return jnp.concatenate([ys_ref[c, pl.ds(b, tt, stride=nb), :] for b in range(nb)], axis=0)

    ys = jnp.concatenate([slab(c) for c in range(ys_ref.shape[0])], axis=1)
    branches = (ys[:, 0:w], yb_ref[...].reshape(rows, w), ys[:, w:2 * w], ys[:, 2 * w:3 * w])
    plain = None
    gated = None
    for mi, y in enumerate(branches):
        p = _dot(y.astype(BF16), wb_ref[mi])
        t = jnp.tanh(gm_ref[:, :, mi * d:(mi + 1) * d].reshape(rows, d).astype(F32)) * p
        plain = p if plain is None else plain + p
        gated = t if gated is None else gated + t
    merged = 0.5 * (plain + gated)
    mix = _dot(merged.astype(BF16), wo_ref[...]).reshape(nb, tt, d)
    o_ref[...] = x_ref[...] + gate_ref[...] * mix


def _merge(y_scan_tm, y_b, g_merge, x, w_branch, w_out, gate, *, tt):
    bsz, seq, d = x.shape
    w = y_b.shape[2]
    tile = lambda n: pl.BlockSpec((bsz, tt, n), lambda i: (0, i, 0))
    return pl.pallas_call(
        _merge_kernel,
        out_shape=jax.ShapeDtypeStruct((bsz, seq, d), F32),
        grid=(seq // tt,),
        in_specs=[
            pl.BlockSpec((3 * w // LANES, tt * bsz, LANES), lambda i: (0, i, 0)),
            tile(w),
            tile(N_BRANCH * d),
            tile(d),
            pl.BlockSpec(w_branch.shape, lambda i: (0, 0, 0)),
            pl.BlockSpec(w_out.shape, lambda i: (0, 0)),
            pl.BlockSpec((bsz, 1, d), lambda i: (0, 0, 0)),
        ],
        out_specs=tile(d),
        compiler_params=_cparams(("arbitrary",)),
        name="merge_out",
    )(y_scan_tm, y_b, g_merge, x, w_branch, w_out, gate)


def _final_norm(out, fg_ref):
    var = jnp.mean(out * out, axis=-1, keepdims=True)
    return out * lax.rsqrt(var + RMS_EPS) * fg_ref[...]


def _ffn_kernel(x_ref, g_ref, sc_ref, sh_ref, gate_ref, wg_ref, wu_ref, wd_ref, *rest, final, fc):
    if final:
        fg_ref, o_ref, a_ref = rest
    else:
        o_ref, a_ref = rest
    hb = _norm_mod(x_ref[0], g_ref[...], sc_ref[0], sh_ref[0]).astype(BF16)
    for f0 in range(0, a_ref.shape[1], fc):
        a_ref[:, f0:f0 + fc] = (_silu(_dot(hb, wg_ref[:, f0:f0 + fc])) * _dot(hb, wu_ref[:, f0:f0 + fc])).astype(BF16)
    out = x_ref[0] + gate_ref[0] * _dot(a_ref[...], wd_ref[...])
    o_ref[0] = _final_norm(out, fg_ref) if final else out


def _ffn(x, g, scale, shift, gate, weights, *, final_g, tm):
    bsz, seq, d = x.shape
    row = lambda: pl.BlockSpec((1, 1, d), lambda b, i: (b, 0, 0))
    wg, wu, wd = weights
    ff = wg.shape[1]
    in_specs = [
        pl.BlockSpec((1, tm, d), lambda b, i: (b, i, 0)),
        pl.BlockSpec((1, d), lambda b, i: (0, 0)),
        row(), row(), row(),
        pl.BlockSpec((d, ff), lambda b, i: (0, 0)),
        pl.BlockSpec((d, ff), lambda b, i: (0, 0)),
        pl.BlockSpec((ff, d), lambda b, i: (0, 0)),
    ]
    args = [x, g.reshape(1, d), scale, shift, gate, wg, wu, wd]
    final = final_g is not None
    if final:
        in_specs.append(pl.BlockSpec((1, d), lambda b, i: (0, 0)))
        args.append(final_g.reshape(1, d))
    return pl.pallas_call(
        functools.partial(_ffn_kernel, final=final, fc=MXU_WIDTH),
        out_shape=jax.ShapeDtypeStruct((bsz, seq, d), F32),
        grid=(bsz, seq // tm),
        in_specs=in_specs,
        out_specs=pl.BlockSpec((1, tm, d), lambda b, i: (b, i, 0)),
        scratch_shapes=[pltpu.VMEM((tm, ff), BF16)],
        compiler_params=_cparams(("arbitrary", "arbitrary")),
        name="dense_ffn",
    )(*args)


def _route_kernel(x_ref, g_ref, sc_ref, sh_ref, rw_ref, rb_ref, h_ref, r_ref):
    h = _norm_mod(x_ref[0], g_ref[...], sc_ref[0], sh_ref[0])
    h_hi, h_lo = _split_bf16(h)
    h_ref[0] = h_hi
    w_hi, w_lo = _split_bf16(rw_ref[...])
    logits = _dot(h_hi, w_hi) + (_dot(h_hi, w_lo) + _dot(h_lo, w_hi)) + rb_ref[...]
    lane = lax.broadcasted_iota(jnp.int32, logits.shape, 1)
    big = logits.shape[1]
    v1 = jnp.max(logits, axis=-1, keepdims=True)
    i1 = jnp.min(jnp.where(logits == v1, lane, big), axis=-1, keepdims=True)
    rest_l = jnp.where(lane == i1, NEG_INF * 4.0, logits)
    v2 = jnp.max(rest_l, axis=-1, keepdims=True)
    i2 = jnp.min(jnp.where(rest_l == v2, lane, big), axis=-1, keepdims=True)
    e2 = jnp.exp(v2 - v1)
    w1 = 1.0 / (1.0 + e2)
    w2 = e2 / (1.0 + e2)
    r_ref[0] = jnp.where(lane == 0, i1.astype(F32), jnp.where(lane == 1, i2.astype(F32), jnp.where(lane == 2, w1, w2)))


def _route(x, g, scale, shift, rw, rb, *, tm):
    bsz, seq, d = x.shape
    row = lambda: pl.BlockSpec((1, 1, d), lambda b, i: (b, 0, 0))
    return pl.pallas_call(
        _route_kernel,
        out_shape=(jax.ShapeDtypeStruct((bsz, seq, d), BF16), jax.ShapeDtypeStruct((bsz, seq, LANES), F32)),
        grid=(bsz, seq // tm),
        in_specs=[
            pl.BlockSpec((1, tm, d), lambda b, i: (b, i, 0)),
            pl.BlockSpec((1, d), lambda b, i: (0, 0)),
            row(), row(),
            pl.BlockSpec(rw.shape, lambda b, i: (0, 0)),
            pl.BlockSpec(rb.shape, lambda b, i: (0, 0)),
        ],
        out_specs=(pl.BlockSpec((1, tm, d), lambda b, i: (b, i, 0)), pl.BlockSpec((1, tm, LANES), lambda b, i: (b, i, 0))),
        compiler_params=_cparams(("arbitrary", "arbitrary")),
        name="moe_route",
    )(x, g.reshape(1, d), scale, shift, rw, rb)


def _gffn_kernel(te_ref, nu_ref, x_ref, wg_ref, wu_ref, wd_ref, o_ref):
    i = pl.program_id(0)

    @pl.when(i < nu_ref[0])
    def _():
        hb = x_ref[...]
        a = (_silu(_dot(hb, wg_ref[0])) * _dot(hb, wu_ref[0])).astype(BF16)
        o_ref[...] = _dot(a, wd_ref[0]).astype(o_ref.dtype)

    @pl.when(i >= nu_ref[0])
    def _():
        o_ref[...] = jnp.zeros_like(o_ref)


def _grouped_ffn(xs, tile_expert, n_used, wg, wu, wd, *, tmg):
    rows, d = xs.shape
    _, _, fe = wg.shape
    return pl.pallas_call(
        _gffn_kernel,
        out_shape=jax.ShapeDtypeStruct((rows, d), BF16),
        grid_spec=pltpu.PrefetchScalarGridSpec(
            num_scalar_prefetch=2,
            grid=(rows // tmg,),
            in_specs=[
                pl.BlockSpec((tmg, d), lambda i, te, nu: (i, 0)),
                pl.BlockSpec((1, d, fe), lambda i, te, nu: (te[i], 0, 0)),
                pl.BlockSpec((1, d, fe), lambda i, te, nu: (te[i], 0, 0)),
                pl.BlockSpec((1, fe, d), lambda i, te, nu: (te[i], 0, 0)),
            ],
            out_specs=pl.BlockSpec((tmg, d), lambda i, te, nu: (i, 0)),
        ),
        compiler_params=_cparams(("arbitrary",)),
        name="moe_grouped_ffn",
    )(tile_expert, n_used, xs, wg, wu, wd)


def _combine_kernel(x_ref, gate_ref, y_ref, y2_ref, r_ref, *rest, final):
    if final:
        fg_ref, o_ref = rest
    else:
        (o_ref,) = rest
    r = r_ref[0]
    f = r[:, 2:3] * y_ref[0].astype(F32) + r[:, 3:4] * y2_ref[0].astype(F32)
    out = x_ref[0] + gate_ref[0] * f
    o_ref[0] = _final_norm(out, fg_ref) if final else out


def _combine(x, gate, y1, y2, route, *, final_g, tm):
    bsz, seq, d = x.shape
    tile = lambda: pl.BlockSpec((1, tm, d), lambda b, i: (b, i, 0))
    in_specs = [
        tile(),
        pl.BlockSpec((1, 1, d), lambda b, i: (b, 0, 0)),
        tile(), tile(),
        pl.BlockSpec((1, tm, LANES), lambda b, i: (b, i, 0)),
    ]
    args = [x, gate, y1, y2, route]
    final = final_g is not None
    if final:
        in_specs.append(pl.BlockSpec((1, d), lambda b, i: (0, 0)))
        args.append(final_g.reshape(1, d))
    return pl.pallas_call(
        functools.partial(_combine_kernel, final=final),
        out_shape=jax.ShapeDtypeStruct((bsz, seq, d), F32),
        grid=(bsz, seq // tm),
        in_specs=in_specs,
        out_specs=pl.BlockSpec((1, tm, d), lambda b, i: (b, i, 0)),
        compiler_params=_cparams(("arbitrary", "arbitrary")),
        name="moe_combine",
    )(*args)


def _src_kernel(te_ref, k0_ref, jlo_ref, jhi_ref, c_ref, o_ref, cnt_ref, *, nblk, tb, last):
    i = pl.program_id(0)
    tmg = o_ref.shape[0]
    base = te_ref[i] * nblk
    kcol = k0_ref[i] + lax.broadcasted_iota(jnp.int32, (tmg, 1), 0)
    cnt_ref[...] = jnp.zeros_like(cnt_ref)

    def blk(j, carry):
        hit = (c_ref[base + j] <= kcol).astype(jnp.int32)
        part = hit[:, 0:LANES]
        for q in range(1, tb // LANES):
            part = part + hit[:, q * LANES:(q + 1) * LANES]
        cnt_ref[...] += part
        return carry

    lax.fori_loop(jlo_ref[i], jhi_ref[i], blk, 0)
    total = jnp.sum(cnt_ref[...], axis=1, keepdims=True) + jlo_ref[i] * tb
    o_ref[...] = jnp.minimum(total, last)


def _moe(x, g, scale, shift, gate, rw, rb, wg, wu, wd, *, final_g, tm, tmg):
    bsz, seq, d = x.shape
    ne = wg.shape[0]
    tokens = bsz * seq
    tb = 512
    nblk = tokens // tb
    h, route = _route(x, g, scale, shift, rw, rb, tm=tm)
    ar = jnp.arange(ne, dtype=jnp.int32)[:, None]
    m1 = route[:, :, 0].astype(jnp.int32).reshape(1, tokens) == ar
    m2 = route[:, :, 1].astype(jnp.int32).reshape(1, tokens) == ar
    count = jnp.cumsum((m1 | m2).astype(jnp.int32), axis=1)
    padded = ((count[:, -1] + tmg - 1) // tmg) * tmg
    ends = jnp.cumsum(padded)
    offs = ends - padded
    row_of = count - 1 + offs[:, None]
    dest1 = jnp.sum(jnp.where(m1, row_of, 0), axis=0)
    dest2 = jnp.sum(jnp.where(m2, row_of, 0), axis=0)
    rows = tokens * 2 + ne * tmg
    ntiles = rows // tmg
    tile_start = jnp.arange(ntiles, dtype=jnp.int32) * tmg
    tile_expert = jnp.minimum(jnp.sum((tile_start[:, None] >= ends[None, :]).astype(jnp.int32), axis=1), ne - 1)
    n_used = (ends[-1] // tmg).astype(jnp.int32).reshape(1)
    k0 = tile_start - offs[tile_expert]
    cend = count[:, tb - 1::tb]
    cstart = jnp.concatenate([jnp.zeros((ne, 1), jnp.int32), cend[:, :-1]], axis=1)
    jlo = jnp.sum((cend[tile_expert] <= k0[:, None]).astype(jnp.int32), axis=1)
    jhi = jnp.sum((cstart[tile_expert] <= (k0 + tmg - 1)[:, None]).astype(jnp.int32), axis=1)
    src = pl.pallas_call(
        functools.partial(_src_kernel, nblk=nblk, tb=tb, last=tokens - 1),
        out_shape=jax.ShapeDtypeStruct((rows, 1), jnp.int32),
        grid_spec=pltpu.PrefetchScalarGridSpec(
            num_scalar_prefetch=4,
            grid=(ntiles,),
            in_specs=[pl.BlockSpec((ne * nblk, 1, tb), lambda i, *_: (0, 0, 0))],
            out_specs=pl.BlockSpec((tmg, 1), lambda i, *_: (i, 0)),
            scratch_shapes=[pltpu.VMEM((tmg, LANES), jnp.int32)],
        ),
        compiler_params=_cparams(("arbitrary",)),
        name="moe_sources",
    )(tile_expert, k0, jlo, jhi, count.reshape(ne * nblk, 1, tb))
    rows_of = lambda a, idx: a.at[idx].get(mode="promise_in_bounds")
    xs = rows_of(h.reshape(tokens, d), src[:, 0])
    ys = _grouped_ffn(xs, tile_expert, n_used, wg, wu, wd, tmg=tmg)
    y1 = rows_of(ys, dest1).reshape(bsz, seq, d)
    y2 = rows_of(ys, dest2).reshape(bsz, seq, d)
    return _combine(x, gate, y1, y2, route, final_g=final_g, tm=tm)


def _block_diag(blocks):
    n, r, c = blocks.shape
    eye = jnp.eye(n, dtype=blocks.dtype)
    return (eye[:, None, :, None] * blocks[:, :, None, :]).reshape(n * r, n * c)


def _rope_tables(seq, heads_q):
    inv = ROPE_THETA ** (-jnp.arange(0, HEAD_DIM, 2, dtype=F32) / HEAD_DIM)
    ang = jnp.arange(seq, dtype=F32)[:, None] * inv[None, :]
    cos, sin = jnp.cos(ang), jnp.sin(ang)
    cos_h = jnp.concatenate([cos, cos], axis=1)
    sin_h = jnp.concatenate([-sin, sin], axis=1)
    one_h = jnp.ones_like(cos_h)
    zero_h = jnp.zeros_like(cos_h)
    cos_t = jnp.concatenate([cos_h] * heads_q + [one_h, one_h] + [cos_h, one_h] * 3, axis=1)
    sin_t = jnp.concatenate([sin_h] * heads_q + [zero_h, zero_h] + [sin_h, zero_h] * 3, axis=1)
    return cos_t, sin_t


def kernel(x, c, mod_w, mod_b, norm_mix_g, norm_ffn_g, w_in, s5_lambda_re, s5_lambda_im, s5_log_step, s5_b_re, s5_b_im, s5_c_re, s5_c_im, s5_d, s5_w_glu, nsa_pe_k, nsa_pe_v, nsa_cmp_k_w1, nsa_cmp_k_w2, nsa_cmp_v_w1, nsa_cmp_v_w2, lru_conv_w, lru_conv_b, lru_w_a, lru_b_a, lru_w_x, lru_b_x, lru_lambda, sc_conv_w, w_branch, w_out, ffn_w_gate, ffn_w_up, ffn_w_down, moe_router_w, moe_router_b, moe_w_gate, moe_w_up, moe_w_down, final_norm_g):
    bsz, seq, d = x.shape
    depth = mod_w.shape[0]
    w = d // N_BRANCH
    heads = w // HEAD_DIM
    groups = w // S5_GROUP
    assert bsz == SUBLANES, "time-major scan blocks hold one timestep per sublane tile"
    tm = min(512, seq)
    tq = 128
    steps = min(64, seq)
    nsel = seq // L_SEL
    assert TOP_N <= nsel <= LANES and seq % tm == 0 and seq >= WINDOW + tq

    mod = _modulation(c, mod_w, mod_b)
    mod = mod.reshape(depth, bsz, 6, 1, d)

    sizes = (w, w, HEAD_DIM, HEAD_DIM, HEAD_DIM, HEAD_DIM, HEAD_DIM, HEAD_DIM, heads * 3, w, w, w, w, w, N_BRANCH * d)
    offs = np.concatenate([[0], np.cumsum(sizes)])
    col = lambda wmat_t, k: wmat_t[offs[k]:offs[k + 1], :]

    cos_t, sin_t = _rope_tables(seq, heads)
    n_cmp = (seq - L_CMP) // CMP_STRIDE + 1
    nblk = seq // CMP_STRIDE
    cmp_start = np.arange(nblk) * CMP_STRIDE
    sel_start = np.arange(nsel) * L_SEL
    ovl = np.clip(np.minimum(cmp_start[:, None] + L_CMP, sel_start[None, :] + L_SEL)
                  - np.maximum(cmp_start[:, None], sel_start[None, :]), 0, None) / L_CMP
    ovl[n_cmp:] = 0.0
    overlap_t = jnp.asarray(np.pad(ovl, ((0, 0), (0, LANES - nsel))).T, BF16)
    onehot = jnp.asarray((np.arange(seq)[:, None] // L_SEL) == np.arange(nsel)[None, :], BF16)

    for li in range(depth):
        sh1, sc1, gt1, sh2, sc2, gt2 = (mod[li, :, k] for k in range(6))
        wl = w_in[li].T
        w_scan = jnp.concatenate([col(wl, 0), col(wl, 9), col(wl, 10), col(wl, 11), col(wl, 12), col(wl, 13)], axis=0).astype(BF16)
        gpad = jnp.zeros((LANES - heads * 3, d), wl.dtype)
        w_nsa = jnp.concatenate([col(wl, 1), col(wl, 8), gpad] + [col(wl, k) for k in range(2, 8)], axis=0).astype(BF16)
        w_gm = (0.5 * col(wl, 14)).astype(BF16)
        g1 = norm_mix_g[li]
        z_scan, z_nsa, g_merge = _project(x, g1, sc1, sh1, w_scan, w_nsa, w_gm, (cos_t, sin_t), tt=steps)

        a_re, a_im, bb_re, bb_im = _s5_params(s5_lambda_re[li], s5_lambda_im[li], s5_log_step[li], s5_b_re[li], s5_b_im[li])
        to_bd = lambda m: _block_diag(jnp.swapaxes(m.reshape(groups, S5_STATE, S5_GROUP), 1, 2))
        s5b = jnp.concatenate([to_bd(bb_re), to_bd(bb_im)], axis=1).astype(BF16)
        c_bd = lambda m: _block_diag(jnp.swapaxes(m, 1, 2))
        s5c = jnp.concatenate([c_bd(s5_c_re[li]), -c_bd(s5_c_im[li])], axis=0).astype(BF16)
        r1 = lambda a: a.reshape(1, -1)
        y_scan = _scan_mixers(
            z_scan, bsz, s5b, s5c, r1(a_re), r1(a_im), r1(s5_d[li]), s5_w_glu[li].astype(BF16),
            lru_conv_w[li], r1(lru_conv_b[li]), _block_diag(lru_w_a[li]).astype(BF16), r1(lru_b_a[li]),
            _block_diag(lru_w_x[li]).astype(BF16), r1(lru_b_x[li]), r1(lru_lambda[li]), sc_conv_w[li], steps=2 * steps)

        kv0 = heads * HEAD_DIM + LANES
        kvs = [z_nsa[:, :, kv0 + k * HEAD_DIM:kv0 + (k + 1) * HEAD_DIM] for k in range(6)]
        kc, vc, ks, vs, kw, vw = kvs
        grp = lambda a: a.reshape(bsz, nblk, CMP_STRIDE * HEAD_DIM)
        k_cmp, v_cmp = _compress(grp(kc), grp(vc), nsa_pe_k[li], nsa_pe_v[li], nsa_cmp_k_w1[li], nsa_cmp_k_w2[li], nsa_cmp_v_w1[li], nsa_cmp_v_w2[li])
        o_cmp, qa = _cmp_select(z_nsa, k_cmp, jnp.swapaxes(v_cmp, 1, 2).astype(BF16), overlap_t, heads=heads, tq=2 * tq, nsel=nsel)
        tk = min(512, seq)
        ka = jnp.concatenate([ks.astype(BF16), jnp.broadcast_to(onehot, (bsz, seq, nsel))], axis=2)
        ka = ka.reshape(bsz, seq // tk, tk, HEAD_DIM + nsel)
        kwa = jnp.concatenate([kw.astype(BF16), jnp.zeros((bsz, seq, nsel), BF16)], axis=2)
        kwa = kwa.reshape(bsz, seq // tq, tq, HEAD_DIM + nsel)
        ones_pad = jnp.concatenate([jnp.ones((bsz, seq, 1), BF16), jnp.zeros((bsz, seq, SUBLANES - 1), BF16)], axis=2)
        tiles_t = lambda a, n: jnp.swapaxes(
            jnp.concatenate([a.astype(BF16), ones_pad], axis=2).reshape(bsz, seq // n, n, HEAD_DIM + SUBLANES), 2, 3)
        y_b = _attend(qa, ka, tiles_t(vs, tk), kwa, tiles_t(vw, tq), z_nsa, (heads * HEAD_DIM) // LANES, o_cmp, tq=tq, tk=tk)

        x = _merge(y_scan, y_b, g_merge, x, w_branch[li].astype(BF16), w_out[li].astype(BF16), gt1, tt=steps)

        final_g = final_norm_g if li == depth - 1 else None
        j = li // 2
        if li % 2 == 0:
            weights = tuple(_to_bf16(wt[j:j + 1])[0] for wt in (ffn_w_gate, ffn_w_up, ffn_w_down))
            x = _ffn(x, norm_ffn_g[li], sc2, sh2, gt2, weights, final_g=final_g, tm=tm)
        else:
            rw = jnp.pad(moe_router_w[j], ((0, 0), (0, LANES - N_EXPERTS)))
            rb = jnp.pad(moe_router_b[j], (0, LANES - N_EXPERTS), constant_values=NEG_INF).reshape(1, LANES)
            x = _moe(x, norm_ffn_g[li], sc2, sh2, gt2, rw, rb, _to_bf16(moe_w_gate[j]), _to_bf16(moe_w_up[j]),
                     _to_bf16(moe_w_down[j]), final_g=final_g, tm=tm, tmg=512)
    return x
```

```python
import functools
import math

import jax
import jax.numpy as jnp
import numpy as np
from jax import lax
from jax.experimental import pallas as pl
from jax.experimental.pallas import tpu as pltpu

F32 = jnp.float32
BF16 = jnp.bfloat16

N_BRANCH = 4
S5_GROUP = 16
S5_STATE = 64
HEAD_DIM = 64
L_CMP = 32
CMP_STRIDE = 16
L_SEL = 64
TOP_N = 16
WINDOW = 512
ROPE_THETA = 10000.0
FORCED_SCORE = 1.0e4
NEG_INF = -1.0e30
LRU_CONV = 4
LRU_C = 8.0
SC_CONV = 3
N_EXPERTS = 8
RMS_EPS = 1e-6
LOG2E = math.log2(math.e)

LANES = 128
SUBLANES = 8
MXU_WIDTH = 256
VMEM_LIMIT = 52 * 1024 * 1024

HI = lax.Precision.HIGHEST


def _cparams(sem):
    return pltpu.CompilerParams(dimension_semantics=sem, vmem_limit_bytes=VMEM_LIMIT)


def _dot(a, b, precision=None):
    return jnp.dot(a, b, preferred_element_type=F32, precision=precision)


def _dot_t(a, b, precision=None):
    return lax.dot_general(a, b, (((1,), (1,)), ((), ())), preferred_element_type=F32, precision=precision)


def _gelu(x):
    return 0.5 * x * (1.0 + jnp.tanh(math.sqrt(2.0 / math.pi) * (x + 0.044715 * (x * x * x))))


def _sigmoid(x):
    return 0.5 * jnp.tanh(0.5 * x) + 0.5


def _silu(x):
    hx = 0.5 * x
    return hx * jnp.tanh(hx) + hx


def _norm_mod(x, g, scale, shift):
    var = jnp.mean(x * x, axis=-1, keepdims=True)
    return (x * lax.rsqrt(var + RMS_EPS) * g) * (1.0 + scale) + shift


def _rot_half(z):
    n = z.shape[-1]
    lane = lax.broadcasted_iota(jnp.int32, z.shape, z.ndim - 1)
    first = (lane % HEAD_DIM) < (HEAD_DIM // 2)
    return jnp.where(first, pltpu.roll(z, n - HEAD_DIM // 2, z.ndim - 1), pltpu.roll(z, HEAD_DIM // 2, z.ndim - 1))


def _cast_kernel(x_ref, o_ref):
    o_ref[...] = x_ref[...].astype(o_ref.dtype)


def _to_bf16(a):
    n, r, c = a.shape
    return pl.pallas_call(
        _cast_kernel,
        out_shape=jax.ShapeDtypeStruct(a.shape, BF16),
        grid=(n,),
        in_specs=[pl.BlockSpec((1, r, c), lambda i: (i, 0, 0))],
        out_specs=pl.BlockSpec((1, r, c), lambda i: (i, 0, 0)),
        compiler_params=_cparams(("arbitrary",)),
        name="cast_bf16",
    )(a)


def _mod_kernel(c_ref, w_ref, b_ref, o_ref):
    c = c_ref[...]
    o_ref[0] = _dot(_silu(c), w_ref[0], HI) + b_ref[0]


def _modulation(c, mod_w, mod_b):
    depth, d, n = mod_w.shape
    bsz = c.shape[0]
    tn = 1024
    return pl.pallas_call(
        _mod_kernel,
        out_shape=jax.ShapeDtypeStruct((depth, bsz, n), F32),
        grid=(depth, n // tn),
        in_specs=[
            pl.BlockSpec((bsz, d), lambda l, j: (0, 0)),
            pl.BlockSpec((1, d, tn), lambda l, j: (l, 0, j)),
            pl.BlockSpec((1, 1, tn), lambda l, j: (l, 0, j)),
        ],
        out_specs=pl.BlockSpec((1, bsz, tn), lambda l, j: (l, 0, j)),
        compiler_params=_cparams(("arbitrary", "arbitrary")),
        name="modulation",
    )(c, mod_w, mod_b.reshape(depth, 1, n))


def _proj_kernel(x_ref, g_ref, sc_ref, sh_ref, ws_ref, wn_ref, wg_ref, cos_ref, sin_ref,
                 zs_ref, zn_ref, gm_ref, h_ref):
    nb, tt, _ = x_ref.shape
    rows = nb * tt
    nlc = h_ref.shape[0]
    parts = []
    for b in range(nb):
        hb = _norm_mod(x_ref[b], g_ref[...], sc_ref[b], sh_ref[b])
        for c in range(nlc):
            h_ref[c, pl.ds(b, tt, stride=nb), :] = hb[:, c * LANES:(c + 1) * LANES]
        parts.append(hb.astype(BF16))
    h_bt = jnp.concatenate(parts, axis=0)
    h_tb = jnp.concatenate([h_ref[c] for c in range(nlc)], axis=1).astype(BF16)
    for n0 in range(0, ws_ref.shape[0], 2 * MXU_WIDTH):
        zs_ref[:, n0:n0 + 2 * MXU_WIDTH] = _dot_t(h_tb, ws_ref[n0:n0 + 2 * MXU_WIDTH, :])
    for n0 in range(0, wn_ref.shape[0], MXU_WIDTH):
        z = _dot_t(h_bt, wn_ref[n0:n0 + MXU_WIDTH, :])
        cos = jnp.concatenate([cos_ref[:, n0:n0 + MXU_WIDTH]] * nb, axis=0)
        sin = jnp.concatenate([sin_ref[:, n0:n0 + MXU_WIDTH]] * nb, axis=0)
        zn_ref[:, :, n0:n0 + MXU_WIDTH] = (z * cos + _rot_half(z) * sin).reshape(nb, tt, MXU_WIDTH)
    for n0 in range(0, wg_ref.shape[0], 2 * MXU_WIDTH):
        z = _dot_t(h_bt, wg_ref[n0:n0 + 2 * MXU_WIDTH, :])
        gm_ref[:, :, n0:n0 + 2 * MXU_WIDTH] = z.astype(gm_ref.dtype).reshape(nb, tt, 2 * MXU_WIDTH)


def _project(x, g, scale, shift, w_scan, w_nsa, w_gm, rope_tabs, *, tt):
    bsz, seq, d = x.shape
    ns, nn, ng = w_scan.shape[0], w_nsa.shape[0], w_gm.shape[0]
    full = lambda a: pl.BlockSpec(a.shape, lambda i: (0,) * a.ndim)
    return pl.pallas_call(
        _proj_kernel,
        out_shape=(
            jax.ShapeDtypeStruct((seq * bsz, ns), F32),
            jax.ShapeDtypeStruct((bsz, seq, nn), F32),
            jax.ShapeDtypeStruct((bsz, seq, ng), BF16),
        ),
        grid=(seq // tt,),
        in_specs=[
            pl.BlockSpec((bsz, tt, d), lambda i: (0, i, 0)),
            pl.BlockSpec((1, d), lambda i: (0, 0)),
            pl.BlockSpec((bsz, 1, d), lambda i: (0, 0, 0)),
            pl.BlockSpec((bsz, 1, d), lambda i: (0, 0, 0)),
            full(w_scan), full(w_nsa), full(w_gm),
            pl.BlockSpec((tt, nn), lambda i: (i, 0)),
            pl.BlockSpec((tt, nn), lambda i: (i, 0)),
        ],
        out_specs=(
            pl.BlockSpec((tt * bsz, ns), lambda i: (i, 0)),
            pl.BlockSpec((bsz, tt, nn), lambda i: (0, i, 0)),
            pl.BlockSpec((bsz, tt, ng), lambda i: (0, i, 0)),
        ),
        scratch_shapes=[pltpu.VMEM((d // LANES, tt * bsz, LANES), F32)],
        compiler_params=_cparams(("arbitrary",)),
        name="proj_in",
    )(x, g.reshape(1, d), scale, shift, w_scan, w_nsa, w_gm, *rope_tabs)


def _s5_param_kernel(lre_ref, lim_ref, ls_ref, bre_ref, bim_ref, are_ref, aim_ref, bbre_ref, bbim_ref):
    lre = lre_ref[...]
    lim = lim_ref[...]
    step = jnp.exp(ls_ref[...])
    mag = jnp.exp(lre * step)
    ar = mag * jnp.cos(lim * step)
    ai = mag * jnp.sin(lim * step)
    xr = ar - 1.0
    den = lre * lre + lim * lim
    cr = (xr * lre + ai * lim) / den
    ci = (ai * lre - xr * lim) / den
    bre = bre_ref[...]
    bim = bim_ref[...]
    are_ref[...] = ar
    aim_ref[...] = ai
    bbre_ref[...] = cr * bre - ci * bim
    bbim_ref[...] = cr * bim + ci * bre


def _s5_params(lam_re, lam_im, log_step, b_re, b_im):
    g, n = lam_re.shape
    c = b_re.shape[-1]
    rows = g * n
    col = lambda a: a.reshape(rows, 1)
    ls = jnp.broadcast_to(log_step[:, None], (g, n))
    out = pl.pallas_call(
        _s5_param_kernel,
        out_shape=(
            jax.ShapeDtypeStruct((rows, 1), F32),
            jax.ShapeDtypeStruct((rows, 1), F32),
            jax.ShapeDtypeStruct((rows, c), F32),
            jax.ShapeDtypeStruct((rows, c), F32),
        ),
        name="s5_params",
    )(col(lam_re), col(lam_im), col(ls), b_re.reshape(rows, c), b_im.reshape(rows, c))
    return out


def _scan_kernel(z_ref, s5b_ref, s5c_ref, are_ref, aim_ref, d_ref, wglu_ref,
                 cw_ref, cb_ref, wa_ref, ba_ref, wx_ref, bx_ref, lam_ref, scw_ref,
                 o_ref,
                 bu_ref, hre_ref, him_ref, xe_ref, hl_ref, pe_ref, ab_ref, bb_ref, *, nb, steps):
    c = pl.program_id(0)
    rows = nb * steps
    w = d_ref.shape[1]
    ns = are_ref.shape[1]

    @pl.when(c == 0)
    def _():
        hre_ref[...] = jnp.zeros_like(hre_ref)
        him_ref[...] = jnp.zeros_like(him_ref)
        hl_ref[...] = jnp.zeros_like(hl_ref)
        xe_ref[0:(LRU_CONV - 1) * nb, :] = jnp.zeros(((LRU_CONV - 1) * nb, w), F32)
        pe_ref[0:(SC_CONV - 1) * nb, :] = jnp.zeros(((SC_CONV - 1) * nb, w), F32)

    u = z_ref[:, 0:w]
    bu_ref[...] = _dot(u.astype(BF16), s5b_ref[...])
    a_re = jnp.broadcast_to(are_ref[...], (nb, ns))
    a_im = jnp.broadcast_to(aim_ref[...], (nb, ns))

    def s5_step(t, carry):
        h_re, h_im = carry
        r0 = pl.multiple_of(t * nb, nb)
        b_re = bu_ref[pl.ds(r0, nb), 0:ns]
        b_im = bu_ref[pl.ds(r0, nb), ns:2 * ns]
        n_re = a_re * h_re - a_im * h_im + b_re
        n_im = a_re * h_im + a_im * h_re + b_im
        bu_ref[pl.ds(r0, nb), 0:ns] = n_re
        bu_ref[pl.ds(r0, nb), ns:2 * ns] = n_im
        return n_re, n_im

    h_re, h_im = lax.fori_loop(0, steps, s5_step, (hre_ref[...], him_ref[...]))
    hre_ref[...] = h_re
    him_ref[...] = h_im
    y = _dot(bu_ref[...].astype(BF16), s5c_ref[...]) + d_ref[...] * u
    zg = _gelu(y)
    def put(branch, val):
        for k in range(w // LANES):
            o_ref[branch * (w // LANES) + k] = val[:, k * LANES:(k + 1) * LANES]

    put(0, zg * _sigmoid(_dot(zg.astype(BF16), wglu_ref[...])))

    nt = (LRU_CONV - 1) * nb
    x = z_ref[:, w:2 * w]
    xe_ref[nt:nt + rows, :] = x
    xc = cb_ref[...] + cw_ref[LRU_CONV - 1:LRU_CONV, :] * x
    for k in range(LRU_CONV - 1):
        xc = xc + cw_ref[k:k + 1, :] * xe_ref[k * nb:k * nb + rows, :]
    xe_ref[0:nt, :] = xe_ref[rows:rows + nt, :]
    xcb = xc.astype(BF16)
    r = _sigmoid(_dot(xcb, wa_ref[...]) + ba_ref[...])
    gi = _sigmoid(_dot(xcb, wx_ref[...]) + bx_ref[...])
    nl = -lam_ref[...]
    softplus = jnp.maximum(nl, 0.0) + jnp.log(1.0 + jnp.exp(-jnp.abs(nl)))
    log_a = (-LRU_C) * r * softplus
    a = jnp.exp(log_a)
    ab_ref[...] = a
    bb_ref[...] = jnp.sqrt(1.0 - a * a) * (gi * xc)

    def lru_step(t, h):
        r0 = pl.multiple_of(t * nb, nb)
        h = ab_ref[pl.ds(r0, nb), :] * h + bb_ref[pl.ds(r0, nb), :]
        bb_ref[pl.ds(r0, nb), :] = h
        return h

    hl_ref[...] = lax.fori_loop(0, steps, lru_step, hl_ref[...])
    put(1, bb_ref[...] * _gelu(z_ref[:, 2 * w:3 * w]))

    npt = (SC_CONV - 1) * nb
    p = z_ref[:, 4 * w:5 * w] * z_ref[:, 5 * w:6 * w]
    pe_ref[npt:npt + rows, :] = p
    cv = scw_ref[SC_CONV - 1:SC_CONV, :] * p
    for k in range(SC_CONV - 1):
        cv = cv + scw_ref[k:k + 1, :] * pe_ref[k * nb:k * nb + rows, :]
    pe_ref[0:npt, :] = pe_ref[rows:rows + npt, :]
    put(2, z_ref[:, 3 * w:4 * w] * cv)


def _scan_mixers(z_tm, nb, s5b, s5c, a_re, a_im, d_skip, w_glu, conv_w, conv_b, wa, ba, wx, bx, lam, sc_w, *, steps):
    total, zc = z_tm.shape
    rows = nb * steps
    w = zc // 6
    ns = a_re.shape[1]
    full = lambda a: pl.BlockSpec(a.shape, lambda c: (0,) * a.ndim)
    params = [s5b, s5c, a_re, a_im, d_skip, w_glu, conv_w, conv_b, wa, ba, wx, bx, lam, sc_w]
    return pl.pallas_call(
        functools.partial(_scan_kernel, nb=nb, steps=steps),
        out_shape=jax.ShapeDtypeStruct((3 * w // LANES, total, LANES), F32),
        grid=(total // rows,),
        in_specs=[pl.BlockSpec((rows, zc), lambda c: (c, 0))] + [full(a) for a in params],
        out_specs=pl.BlockSpec((3 * w // LANES, rows, LANES), lambda c: (0, c, 0)),
        scratch_shapes=[
            pltpu.VMEM((rows, 2 * ns), F32),
            pltpu.VMEM((nb, ns), F32),
            pltpu.VMEM((nb, ns), F32),
            pltpu.VMEM((rows + (LRU_CONV - 1) * nb, w), F32),
            pltpu.VMEM((nb, w), F32),
            pltpu.VMEM((rows + (SC_CONV - 1) * nb, w), F32),
            pltpu.VMEM((rows, w), F32),
            pltpu.VMEM((rows, w), F32),
        ],
        compiler_params=_cparams(("arbitrary",)),
        name="scan_mixers",
    )(z_tm, *params)


def _compress_kernel(gk_ref, gv_ref, pek_ref, pev_ref, wk1_ref, wk2_ref, wv1_ref, wv2_ref, ko_ref, vo_ref):
    half = gk_ref.shape[2]
    nblk = gk_ref.shape[1]

    def phi(g, pe, w1_ref, w2_ref):
        top = _dot(g, w1_ref[0:half, :], HI)
        bot = _dot(g, w1_ref[half:2 * half, :], HI)
        hid = top + pltpu.roll(bot, nblk - 1, 0) + _dot(pe, w1_ref[...], HI)
        return _dot(_gelu(hid), w2_ref[...], HI)

    ko_ref[0] = phi(gk_ref[0], pek_ref[...], wk1_ref, wk2_ref)
    vo_ref[0] = phi(gv_ref[0], pev_ref[...], wv1_ref, wv2_ref)


def _compress(gk, gv, pe_k, pe_v, wk1, wk2, wv1, wv2):
    bsz, nblk, half = gk.shape
    hd = wk2.shape[1]
    full = lambda a: pl.BlockSpec(a.shape, lambda b: (0,) * a.ndim)
    pek = pe_k.reshape(1, -1)
    pev = pe_v.reshape(1, -1)
    params = [pek, pev, wk1, wk2, wv1, wv2]
    blk = pl.BlockSpec((1, nblk, half), lambda b: (b, 0, 0))
    oblk = pl.BlockSpec((1, nblk, hd), lambda b: (b, 0, 0))
    return pl.pallas_call(
        _compress_kernel,
        out_shape=(jax.ShapeDtypeStruct((bsz, nblk, hd), F32),) * 2,
        grid=(bsz,),
        in_specs=[blk, blk] + [full(a) for a in params],
        out_specs=(oblk, oblk),
        compiler_params=_cparams(("arbitrary",)),
        name="nsa_compress",
    )(gk, gv, *params)


def _split_bf16(a):
    hi = a.astype(BF16)
    return hi, (a - hi.astype(F32)).astype(BF16)


def _cmp_kernel(q_ref, kc_ref, vct_ref, ovt_ref, ocmp_ref, qa_ref, *, heads, tq):
    i = pl.program_id(1)
    q_t = (q_ref[0] * HEAD_DIM ** -0.5).T
    kc_hi, kc_lo = _split_bf16(kc_ref[0])
    vct = vct_ref[0]
    nblk = kc_hi.shape[0]
    t = i * tq + lax.broadcasted_iota(jnp.int32, (nblk, tq), 1)
    n_id = lax.broadcasted_iota(jnp.int32, (nblk, tq), 0)
    visible = (n_id * CMP_STRIDE + (L_CMP - 1)) <= t
    psum = jnp.zeros((nblk, tq), F32)
    for h in range(heads):
        q_hi, q_lo = _split_bf16(q_t[h * HEAD_DIM:(h + 1) * HEAD_DIM, :])
        s = _dot(kc_hi, q_hi) + (_dot(kc_hi, q_lo) + _dot(kc_lo, q_hi))
        s = jnp.where(visible, s, NEG_INF)
        m = jnp.max(s, axis=0, keepdims=True)
        e = jnp.where(visible, jnp.exp(s - m), 0.0)
        den = jnp.sum(e, axis=0, keepdims=True)
        p = e / jnp.where(den > 0.0, den, 1.0)
        psum = psum + p
        ocmp_ref[0, h * HEAD_DIM:(h + 1) * HEAD_DIM, :] = _dot(vct, p.astype(BF16))

    p_hi, p_lo = _split_bf16(psum)
    imp_t = _dot(ovt_ref[...], p_hi) + _dot(ovt_ref[...], p_lo)
    nslot = ovt_ref.shape[0]
    tt = i * tq + lax.broadcasted_iota(jnp.int32, (nslot, tq), 1)
    j = lax.broadcasted_iota(jnp.int32, (nslot, tq), 0)
    cur = tt // L_SEL
    forced = (j == 0) | (j == cur) | (j == cur - 1)
    future = (j * L_SEL) > tt
    imp_t = jnp.where(future, -1.0, jnp.where(forced, FORCED_SCORE, imp_t))
    nreal = qa_ref.shape[2] - HEAD_DIM
    cand = imp_t[0:nreal, :]
    jrow = lax.broadcasted_iota(jnp.int32, (nreal, tq), 0)
    bias_t = jnp.full((nreal, tq), NEG_INF, F32)
    for _ in range(TOP_N):
        best = jnp.max(cand, axis=0, keepdims=True)
        first = jnp.min(jnp.where(cand == best, jrow, nreal), axis=0, keepdims=True)
        hit = jrow == first
        bias_t = jnp.where(hit, 0.0, bias_t)
        cand = jnp.where(hit, -2.0, cand)
    for h in range(heads):
        qa_ref[0, h] = jnp.concatenate([q_t[h * HEAD_DIM:(h + 1) * HEAD_DIM, :] * LOG2E, bias_t], axis=0).astype(BF16)


def _cmp_select(q, k_cmp, v_cmp_t, overlap_t, *, heads, tq, nsel):
    bsz, seq, _ = q.shape
    qd = heads * HEAD_DIM
    nblk = k_cmp.shape[1]
    return pl.pallas_call(
        functools.partial(_cmp_kernel, heads=heads, tq=tq),
        out_shape=(
            jax.ShapeDtypeStruct((bsz, qd, seq), F32),
            jax.ShapeDtypeStruct((bsz, heads, HEAD_DIM + nsel, seq), BF16),
        ),
        grid=(bsz, seq // tq),
        in_specs=[
            pl.BlockSpec((1, tq, qd), lambda b, i: (b, i, 0)),
            pl.BlockSpec((1, nblk, HEAD_DIM), lambda b, i: (b, 0, 0)),
            pl.BlockSpec((1, HEAD_DIM, nblk), lambda b, i: (b, 0, 0)),
            pl.BlockSpec(overlap_t.shape, lambda b, i: (0, 0)),
        ],
        out_specs=(
            pl.BlockSpec((1, qd, tq), lambda b, i: (b, 0, i)),
            pl.BlockSpec((1, heads, HEAD_DIM + nsel, tq), lambda b, i: (b, 0, 0, i)),
        ),
        compiler_params=_cparams(("arbitrary", "arbitrary")),
        name="nsa_cmp_select",
    )(q, k_cmp, v_cmp_t, overlap_t)


def _attend_kernel(qa_ref, ka_ref, vst_ref, kwa_ref, vwt_ref, gn_ref, ocmp_ref, o_ref,
                   m_ref, acc_ref, s0_ref, s1_ref, *, heads, tq, tk, nwin):
    i = pl.program_id(1)
    cols = heads * tq
    qat = jnp.concatenate([qa_ref[0, h] for h in range(heads)], axis=1)
    t_row = i * tq + (lax.broadcasted_iota(jnp.int32, (1, cols), 1) % tq)

    m_ref[...] = jnp.full(m_ref.shape, NEG_INF, F32)
    acc_ref[...] = jnp.zeros(acc_ref.shape, F32)
    key_id = lax.broadcasted_iota(jnp.int32, (tk, cols), 0)
    n_full = (i * tq + 1) // tk
    s0_ref[...] = _dot(ka_ref[0, 0], qat)

    st = jnp.maximum(i + 1 - nwin, 0)
    kw = kwa_ref[0, pl.ds(st, nwin)].reshape(nwin * tq, kwa_ref.shape[3])
    s = _dot(kw, qat)
    dist = t_row - (st * tq + lax.broadcasted_iota(jnp.int32, (nwin * tq, cols), 0))
    s = jnp.where((dist >= 0) & (dist < WINDOW), s, NEG_INF)
    m = jnp.max(s, axis=0, keepdims=True)
    pb = jnp.exp2(s - m).astype(BF16)
    ow = _dot(vwt_ref[0, st], pb[0:tq, :])
    for j in range(1, nwin):
        ow = ow + _dot(vwt_ref[0, st + j], pb[j * tq:(j + 1) * tq, :])
    owin = ow[0:HEAD_DIM, :] / ow[HEAD_DIM:HEAD_DIM + 1, :]

    def softmax_tile(kt, s, masked):
        if masked:
            s = jnp.where(kt * tk + key_id <= t_row, s, NEG_INF)
        m_old = m_ref[...]
        m_new = jnp.maximum(m_old, jnp.max(s, axis=0, keepdims=True))
        p = jnp.exp2(s - m_new).astype(BF16)
        acc_ref[...] = jnp.exp2(m_old - m_new) * acc_ref[...] + _dot(vst_ref[0, kt], p)
        m_ref[...] = m_new

    def pair_body(j, carry):
        s1_ref[...] = _dot(ka_ref[0, 2 * j + 1], qat)
        softmax_tile(2 * j, s0_ref[...], False)
        s0_ref[...] = _dot(ka_ref[0, 2 * j + 2], qat)
        softmax_tile(2 * j + 1, s1_ref[...], False)
        return carry

    n_pair = n_full // 2
    lax.fori_loop(0, n_pair, pair_body, 0)
    s1_ref[...] = _dot(ka_ref[0, 2 * n_pair + 1], qat)
    softmax_tile(2 * n_pair, s0_ref[...], True)
    softmax_tile(2 * n_pair + 1, s1_ref[...], True)

    osel = acc_ref[0:HEAD_DIM, :] / acc_ref[HEAD_DIM:HEAD_DIM + 1, :]

    g_t = _sigmoid(gn_ref[0]).T
    outs = []
    for h in range(heads):
        cs = slice(h * tq, (h + 1) * tq)
        outs.append(g_t[3 * h:3 * h + 1, :] * ocmp_ref[0, h * HEAD_DIM:(h + 1) * HEAD_DIM, :]
                    + g_t[3 * h + 1:3 * h + 2, :] * osel[:, cs]
                    + g_t[3 * h + 2:3 * h + 3, :] * owin[:, cs])
    o_ref[0] = jnp.concatenate(outs, axis=0).T


def _attend(qa, ka, vst, kwa, vwt, z_nsa, gn_block, o_cmp_t, *, tq, tk):
    bsz, heads, ka_w, seq = qa.shape
    qd = heads * HEAD_DIM
    nwin = WINDOW // tq + 1
    return pl.pallas_call(
        functools.partial(_attend_kernel, heads=heads, tq=tq, tk=tk, nwin=nwin),
        out_shape=jax.ShapeDtypeStruct((bsz, seq, qd), F32),
        grid=(bsz, seq // tq),
        in_specs=[
            pl.BlockSpec((1, heads, ka_w, tq), lambda b, i: (b, 0, 0, i)),
            pl.BlockSpec((1,) + ka.shape[1:], lambda b, i: (b, 0, 0, 0)),
            pl.BlockSpec((1,) + vst.shape[1:], lambda b, i: (b, 0, 0, 0)),
            pl.BlockSpec((1,) + kwa.shape[1:], lambda b, i: (b, 0, 0, 0)),
            pl.BlockSpec((1,) + vwt.shape[1:], lambda b, i: (b, 0, 0, 0)),
            pl.BlockSpec((1, tq, LANES), lambda b, i: (b, i, gn_block)),
            pl.BlockSpec((1, qd, tq), lambda b, i: (b, 0, i)),
        ],
        out_specs=pl.BlockSpec((1, tq, qd), lambda b, i: (b, i, 0)),
        scratch_shapes=[
            pltpu.VMEM((1, heads * tq), F32),
            pltpu.VMEM((vst.shape[2], heads * tq), F32),
            pltpu.VMEM((tk, heads * tq), F32),
            pltpu.VMEM((tk, heads * tq), F32),
        ],
        compiler_params=_cparams(("arbitrary", "arbitrary")),
        name="nsa_attend",
    )(qa, ka, vst, kwa, vwt, z_nsa, o_cmp_t)


def _merge_kernel(ys_ref, yb_ref, gm_ref, x_ref, wb_ref, wo_ref, gate_ref, o_ref, m_ref):
    nb, tt, w = yb_ref.shape
    d = x_ref.shape[2]
    rows = nb * tt
    def slab(c):
        return jnp.concatenate([ys_ref[c, pl.ds(b, tt, stride=nb), :] for b in range(nb)], axis=0)

    ys = jnp.concatenate([slab(c) for c in range(ys_ref.shape[0])], axis=1)
    branches = (ys[:, 0:w], yb_ref[...].reshape(rows, w), ys[:, w:2 * w], ys[:, 2 * w:3 * w])
    ybs = [y.astype(BF16) for y in branches]
    for n0 in range(0, d, MXU_WIDTH):
        plain = None
        gated = None
        for mi, y in enumerate(ybs):
            p = _dot(y, wb_ref[mi, :, n0:n0 + MXU_WIDTH])
            gm = gm_ref[:, :, mi * d + n0:mi * d + n0 + MXU_WIDTH].reshape(rows, MXU_WIDTH)
            t = jnp.tanh(gm.astype(F32)) * p
            plain = p if plain is None else plain + p
            gated = t if gated is None else gated + t
        m_ref[:, n0:n0 + MXU_WIDTH] = (0.5 * (plain + gated)).astype(BF16)
    mix = _dot(m_ref[...], wo_ref[...]).reshape(nb, tt, d)
    o_ref[...] = x_ref[...] + gate_ref[...] * mix


def _merge(y_scan_tm, y_b, g_merge, x, w_branch, w_out, gate, *, tt):
    bsz, seq, d = x.shape
    w = y_b.shape[2]
    tile = lambda n: pl.BlockSpec((bsz, tt, n), lambda i: (0, i, 0))
    return pl.pallas_call(
        _merge_kernel,
        out_shape=jax.ShapeDtypeStruct((bsz, seq, d), F32),
        grid=(seq // tt,),
        in_specs=[
            pl.BlockSpec((3 * w // LANES, tt * bsz, LANES), lambda i: (0, i, 0)),
            tile(w),
            tile(N_BRANCH * d),
            tile(d),
            pl.BlockSpec(w_branch.shape, lambda i: (0, 0, 0)),
            pl.BlockSpec(w_out.shape, lambda i: (0, 0)),
            pl.BlockSpec((bsz, 1, d), lambda i: (0, 0, 0)),
        ],
        out_specs=tile(d),
        scratch_shapes=[pltpu.VMEM((tt * bsz, d), BF16)],
        compiler_params=_cparams(("arbitrary",)),
        name="merge_out",
    )(y_scan_tm, y_b, g_merge, x, w_branch, w_out, gate)


def _final_norm(out, fg_ref):
    var = jnp.mean(out * out, axis=-1, keepdims=True)
    return out * lax.rsqrt(var + RMS_EPS) * fg_ref[...]


def _ffn_kernel(x_ref, g_ref, sc_ref, sh_ref, gate_ref, wg_ref, wu_ref, wd_ref, *rest, final, fc):
    if final:
        fg_ref, o_ref, a_ref = rest
    else:
        o_ref, a_ref = rest
    hb = _norm_mod(x_ref[0], g_ref[...], sc_ref[0], sh_ref[0]).astype(BF16)
    for f0 in range(0, a_ref.shape[1], fc):
        a_ref[:, f0:f0 + fc] = (_silu(_dot(hb, wg_ref[:, f0:f0 + fc])) * _dot(hb, wu_ref[:, f0:f0 + fc])).astype(BF16)
    out = x_ref[0] + gate_ref[0] * _dot(a_ref[...], wd_ref[...])
    o_ref[0] = _final_norm(out, fg_ref) if final else out


def _ffn(x, g, scale, shift, gate, weights, *, final_g, tm):
    bsz, seq, d = x.shape
    row = lambda: pl.BlockSpec((1, 1, d), lambda b, i: (b, 0, 0))
    wg, wu, wd = weights
    ff = wg.shape[1]
    in_specs = [
        pl.BlockSpec((1, tm, d), lambda b, i: (b, i, 0)),
        pl.BlockSpec((1, d), lambda b, i: (0, 0)),
        row(), row(), row(),
        pl.BlockSpec((d, ff), lambda b, i: (0, 0)),
        pl.BlockSpec((d, ff), lambda b, i: (0, 0)),
        pl.BlockSpec((ff, d), lambda b, i: (0, 0)),
    ]
    args = [x, g.reshape(1, d), scale, shift, gate, wg, wu, wd]
    final = final_g is not None
    if final:
        in_specs.append(pl.BlockSpec((1, d), lambda b, i: (0, 0)))
        args.append(final_g.reshape(1, d))
    return pl.pallas_call(
        functools.partial(_ffn_kernel, final=final, fc=MXU_WIDTH),
        out_shape=jax.ShapeDtypeStruct((bsz, seq, d), F32),
        grid=(bsz, seq // tm),
        in_specs=in_specs,
        out_specs=pl.BlockSpec((1, tm, d), lambda b, i: (b, i, 0)),
        scratch_shapes=[pltpu.VMEM((tm, ff), BF16)],
        compiler_params=_cparams(("arbitrary", "arbitrary")),
        name="dense_ffn",
    )(*args)


def _route_kernel(x_ref, g_ref, sc_ref, sh_ref, rw_ref, rb_ref, h_ref, r_ref):
    h = _norm_mod(x_ref[0], g_ref[...], sc_ref[0], sh_ref[0])
    h_hi, h_lo = _split_bf16(h)
    h_ref[0] = h_hi
    w_hi, w_lo = _split_bf16(rw_ref[...])
    logits = _dot(h_hi, w_hi) + (_dot(h_hi, w_lo) + _dot(h_lo, w_hi)) + rb_ref[...]
    lane = lax.broadcasted_iota(jnp.int32, logits.shape, 1)
    big = logits.shape[1]
    v1 = jnp.max(logits, axis=-1, keepdims=True)
    i1 = jnp.min(jnp.where(logits == v1, lane, big), axis=-1, keepdims=True)
    rest_l = jnp.where(lane == i1, NEG_INF * 4.0, logits)
    v2 = jnp.max(rest_l, axis=-1, keepdims=True)
    i2 = jnp.min(jnp.where(rest_l == v2, lane, big), axis=-1, keepdims=True)
    e2 = jnp.exp(v2 - v1)
    w1 = 1.0 / (1.0 + e2)
    w2 = e2 / (1.0 + e2)
    r_ref[0] = jnp.where(lane == 0, i1.astype(F32), jnp.where(lane == 1, i2.astype(F32), jnp.where(lane == 2, w1, w2)))


def _route(x, g, scale, shift, rw, rb, *, tm):
    bsz, seq, d = x.shape
    row = lambda: pl.BlockSpec((1, 1, d), lambda b, i: (b, 0, 0))
    return pl.pallas_call(
        _route_kernel,
        out_shape=(jax.ShapeDtypeStruct((bsz, seq, d), BF16), jax.ShapeDtypeStruct((bsz, seq, LANES), F32)),
        grid=(bsz, seq // tm),
        in_specs=[
            pl.BlockSpec((1, tm, d), lambda b, i: (b, i, 0)),
            pl.BlockSpec((1, d), lambda b, i: (0, 0)),
            row(), row(),
            pl.BlockSpec(rw.shape, lambda b, i: (0, 0)),
            pl.BlockSpec(rb.shape, lambda b, i: (0, 0)),
        ],
        out_specs=(pl.BlockSpec((1, tm, d), lambda b, i: (b, i, 0)), pl.BlockSpec((1, tm, LANES), lambda b, i: (b, i, 0))),
        compiler_params=_cparams(("arbitrary", "arbitrary")),
        name="moe_route",
    )(x, g.reshape(1, d), scale, shift, rw, rb)


def _gffn_kernel(te_ref, nu_ref, x_ref, wg_ref, wu_ref, wd_ref, o_ref):
    i = pl.program_id(0)

    @pl.when(i < nu_ref[0])
    def _():
        hb = x_ref[...]
        a = (_silu(_dot(hb, wg_ref[0])) * _dot(hb, wu_ref[0])).astype(BF16)
        o_ref[...] = _dot(a, wd_ref[0]).astype(o_ref.dtype)

    @pl.when(i >= nu_ref[0])
    def _():
        o_ref[...] = jnp.zeros_like(o_ref)


def _grouped_ffn(xs, tile_expert, n_used, wg, wu, wd, *, tmg):
    rows, d = xs.shape
    _, _, fe = wg.shape
    return pl.pallas_call(
        _gffn_kernel,
        out_shape=jax.ShapeDtypeStruct((rows, d), BF16),
        grid_spec=pltpu.PrefetchScalarGridSpec(
            num_scalar_prefetch=2,
            grid=(rows // tmg,),
            in_specs=[
                pl.BlockSpec((tmg, d), lambda i, te, nu: (i, 0)),
                pl.BlockSpec((1, d, fe), lambda i, te, nu: (te[i], 0, 0)),
                pl.BlockSpec((1, d, fe), lambda i, te, nu: (te[i], 0, 0)),
                pl.BlockSpec((1, fe, d), lambda i, te, nu: (te[i], 0, 0)),
            ],
            out_specs=pl.BlockSpec((tmg, d), lambda i, te, nu: (i, 0)),
        ),
        compiler_params=_cparams(("arbitrary",)),
        name="moe_grouped_ffn",
    )(tile_expert, n_used, xs, wg, wu, wd)


def _combine_kernel(x_ref, gate_ref, y_ref, y2_ref, r_ref, *rest, final):
    if final:
        fg_ref, o_ref = rest
    else:
        (o_ref,) = rest
    r = r_ref[0]
    f = r[:, 2:3] * y_ref[0].astype(F32) + r[:, 3:4] * y2_ref[0].astype(F32)
    out = x_ref[0] + gate_ref[0] * f
    o_ref[0] = _final_norm(out, fg_ref) if final else out


def _combine(x, gate, y1, y2, route, *, final_g, tm):
    bsz, seq, d = x.shape
    tile = lambda: pl.BlockSpec((1, tm, d), lambda b, i: (b, i, 0))
    in_specs = [
        tile(),
        pl.BlockSpec((1, 1, d), lambda b, i: (b, 0, 0)),
        tile(), tile(),
        pl.BlockSpec((1, tm, LANES), lambda b, i: (b, i, 0)),
    ]
    args = [x, gate, y1, y2, route]
    final = final_g is not None
    if final:
        in_specs.append(pl.BlockSpec((1, d), lambda b, i: (0, 0)))
        args.append(final_g.reshape(1, d))
    return pl.pallas_call(
        functools.partial(_combine_kernel, final=final),
        out_shape=jax.ShapeDtypeStruct((bsz, seq, d), F32),
        grid=(bsz, seq // tm),
        in_specs=in_specs,
        out_specs=pl.BlockSpec((1, tm, d), lambda b, i: (b, i, 0)),
        compiler_params=_cparams(("arbitrary", "arbitrary")),
        name="moe_combine",
    )(*args)


def _src_kernel(te_ref, k0_ref, jlo_ref, jhi_ref, c_ref, o_ref, cnt_ref, *, nblk, tb, last):
    i = pl.program_id(0)
    tmg = o_ref.shape[0]
    base = te_ref[i] * nblk
    kcol = k0_ref[i] + lax.broadcasted_iota(jnp.int32, (tmg, 1), 0)
    cnt_ref[...] = jnp.zeros_like(cnt_ref)

    def blk(j, carry):
        hit = (c_ref[base + j] <= kcol).astype(jnp.int32)
        part = hit[:, 0:LANES]
        for q in range(1, tb // LANES):
            part = part + hit[:, q * LANES:(q + 1) * LANES]
        cnt_ref[...] += part
        return carry

    lax.fori_loop(jlo_ref[i], jhi_ref[i], blk, 0)
    total = jnp.sum(cnt_ref[...], axis=1, keepdims=True) + jlo_ref[i] * tb
    o_ref[...] = jnp.minimum(total, last)


def _moe(x, g, scale, shift, gate, rw, rb, wg, wu, wd, *, final_g, tm, tmg):
    bsz, seq, d = x.shape
    ne = wg.shape[0]
    tokens = bsz * seq
    tb = 512
    nblk = tokens // tb
    h, route = _route(x, g, scale, shift, rw, rb, tm=tm)
    ar = jnp.arange(ne, dtype=jnp.int32)[:, None]
    m1 = route[:, :, 0].astype(jnp.int32).reshape(1, tokens) == ar
    m2 = route[:, :, 1].astype(jnp.int32).reshape(1, tokens) == ar
    count = jnp.cumsum((m1 | m2).astype(jnp.int32), axis=1)
    padded = ((count[:, -1] + tmg - 1) // tmg) * tmg
    ends = jnp.cumsum(padded)
    offs = ends - padded
    row_of = count - 1 + offs[:, None]
    dest1 = jnp.sum(jnp.where(m1, row_of, 0), axis=0)
    dest2 = jnp.sum(jnp.where(m2, row_of, 0), axis=0)
    rows = tokens * 2 + ne * tmg
    ntiles = rows // tmg
    tile_start = jnp.arange(ntiles, dtype=jnp.int32) * tmg
    tile_expert = jnp.minimum(jnp.sum((tile_start[:, None] >= ends[None, :]).astype(jnp.int32), axis=1), ne - 1)
    n_used = (ends[-1] // tmg).astype(jnp.int32).reshape(1)
    k0 = tile_start - offs[tile_expert]
    cend = count[:, tb - 1::tb]
    cstart = jnp.concatenate([jnp.zeros((ne, 1), jnp.int32), cend[:, :-1]], axis=1)
    jlo = jnp.sum((cend[tile_expert] <= k0[:, None]).astype(jnp.int32), axis=1)
    jhi = jnp.sum((cstart[tile_expert] <= (k0 + tmg - 1)[:, None]).astype(jnp.int32), axis=1)
    src = pl.pallas_call(
        functools.partial(_src_kernel, nblk=nblk, tb=tb, last=tokens - 1),
        out_shape=jax.ShapeDtypeStruct((rows, 1), jnp.int32),
        grid_spec=pltpu.PrefetchScalarGridSpec(
            num_scalar_prefetch=4,
            grid=(ntiles,),
            in_specs=[pl.BlockSpec((ne * nblk, 1, tb), lambda i, *_: (0, 0, 0))],
            out_specs=pl.BlockSpec((tmg, 1), lambda i, *_: (i, 0)),
            scratch_shapes=[pltpu.VMEM((tmg, LANES), jnp.int32)],
        ),
        compiler_params=_cparams(("arbitrary",)),
        name="moe_sources",
    )(tile_expert, k0, jlo, jhi, count.reshape(ne * nblk, 1, tb))
    rows_of = lambda a, idx: a.at[idx].get(mode="promise_in_bounds")
    xs = rows_of(h.reshape(tokens, d), src[:, 0])
    ys = _grouped_ffn(xs, tile_expert, n_used, wg, wu, wd, tmg=tmg)
    y1 = rows_of(ys, dest1).reshape(bsz, seq, d)
    y2 = rows_of(ys, dest2).reshape(bsz, seq, d)
    return _combine(x, gate, y1, y2, route, final_g=final_g, tm=tm)


def _block_diag(blocks):
    n, r, c = blocks.shape
    eye = jnp.eye(n, dtype=blocks.dtype)
    return (eye[:, None, :, None] * blocks[:, :, None, :]).reshape(n * r, n * c)


def _rope_tables(seq, heads_q):
    inv = ROPE_THETA ** (-jnp.arange(0, HEAD_DIM, 2, dtype=F32) / HEAD_DIM)
    ang = jnp.arange(seq, dtype=F32)[:, None] * inv[None, :]
    cos, sin = jnp.cos(ang), jnp.sin(ang)
    cos_h = jnp.concatenate([cos, cos], axis=1)
    sin_h = jnp.concatenate([-sin, sin], axis=1)
    one_h = jnp.ones_like(cos_h)
    zero_h = jnp.zeros_like(cos_h)
    cos_t = jnp.concatenate([cos_h] * heads_q + [one_h, one_h] + [cos_h, one_h] * 3, axis=1)
    sin_t = jnp.concatenate([sin_h] * heads_q + [zero_h, zero_h] + [sin_h, zero_h] * 3, axis=1)
    return cos_t, sin_t


def kernel(x, c, mod_w, mod_b, norm_mix_g, norm_ffn_g, w_in, s5_lambda_re, s5_lambda_im, s5_log_step, s5_b_re, s5_b_im, s5_c_re, s5_c_im, s5_d, s5_w_glu, nsa_pe_k, nsa_pe_v, nsa_cmp_k_w1, nsa_cmp_k_w2, nsa_cmp_v_w1, nsa_cmp_v_w2, lru_conv_w, lru_conv_b, lru_w_a, lru_b_a, lru_w_x, lru_b_x, lru_lambda, sc_conv_w, w_branch, w_out, ffn_w_gate, ffn_w_up, ffn_w_down, moe_router_w, moe_router_b, moe_w_gate, moe_w_up, moe_w_down, final_norm_g):
    bsz, seq, d = x.shape
    depth = mod_w.shape[0]
    w = d // N_BRANCH
    heads = w // HEAD_DIM
    groups = w // S5_GROUP
    assert bsz == SUBLANES, "time-major scan blocks hold one timestep per sublane tile"
    tm = min(512, seq)
    tq = 128
    steps = min(64, seq)
    nsel = seq // L_SEL
    assert TOP_N <= nsel <= LANES and seq % tm == 0 and seq >= WINDOW + tq

    mod = _modulation(c, mod_w, mod_b)
    mod = mod.reshape(depth, bsz, 6, 1, d)

    sizes = (w, w, HEAD_DIM, HEAD_DIM, HEAD_DIM, HEAD_DIM, HEAD_DIM, HEAD_DIM, heads * 3, w, w, w, w, w, N_BRANCH * d)
    offs = np.concatenate([[0], np.cumsum(sizes)])
    col = lambda wmat_t, k: wmat_t[offs[k]:offs[k + 1], :]

    cos_t, sin_t = _rope_tables(seq, heads)
    n_cmp = (seq - L_CMP) // CMP_STRIDE + 1
    nblk = seq // CMP_STRIDE
    cmp_start = np.arange(nblk) * CMP_STRIDE
    sel_start = np.arange(nsel) * L_SEL
    ovl = np.clip(np.minimum(cmp_start[:, None] + L_CMP, sel_start[None, :] + L_SEL)
                  - np.maximum(cmp_start[:, None], sel_start[None, :]), 0, None) / L_CMP
    ovl[n_cmp:] = 0.0
    overlap_t = jnp.asarray(np.pad(ovl, ((0, 0), (0, LANES - nsel))).T, BF16)
    onehot = jnp.asarray((np.arange(seq)[:, None] // L_SEL) == np.arange(nsel)[None, :], BF16)

    for li in range(depth):
        sh1, sc1, gt1, sh2, sc2, gt2 = (mod[li, :, k] for k in range(6))
        wl = w_in[li].T
        w_scan = jnp.concatenate([col(wl, 0), col(wl, 9), col(wl, 10), col(wl, 11), col(wl, 12), col(wl, 13)], axis=0).astype(BF16)
        gpad = jnp.zeros((LANES - heads * 3, d), wl.dtype)
        w_nsa = jnp.concatenate([col(wl, 1), col(wl, 8), gpad] + [col(wl, k) for k in range(2, 8)], axis=0).astype(BF16)
        w_gm = (0.5 * col(wl, 14)).astype(BF16)
        g1 = norm_mix_g[li]
        z_scan, z_nsa, g_merge = _project(x, g1, sc1, sh1, w_scan, w_nsa, w_gm, (cos_t, sin_t), tt=steps)

        a_re, a_im, bb_re, bb_im = _s5_params(s5_lambda_re[li], s5_lambda_im[li], s5_log_step[li], s5_b_re[li], s5_b_im[li])
        to_bd = lambda m: _block_diag(jnp.swapaxes(m.reshape(groups, S5_STATE, S5_GROUP), 1, 2))
        s5b = jnp.concatenate([to_bd(bb_re), to_bd(bb_im)], axis=1).astype(BF16)
        c_bd = lambda m: _block_diag(jnp.swapaxes(m, 1, 2))
        s5c = jnp.concatenate([c_bd(s5_c_re[li]), -c_bd(s5_c_im[li])], axis=0).astype(BF16)
        r1 = lambda a: a.reshape(1, -1)
        y_scan = _scan_mixers(
            z_scan, bsz, s5b, s5c, r1(a_re), r1(a_im), r1(s5_d[li]), s5_w_glu[li].astype(BF16),
            lru_conv_w[li], r1(lru_conv_b[li]), _block_diag(lru_w_a[li]).astype(BF16), r1(lru_b_a[li]),
            _block_diag(lru_w_x[li]).astype(BF16), r1(lru_b_x[li]), r1(lru_lambda[li]), sc_conv_w[li], steps=2 * steps)

        kv0 = heads * HEAD_DIM + LANES
        kvs = [z_nsa[:, :, kv0 + k * HEAD_DIM:kv0 + (k + 1) * HEAD_DIM] for k in range(6)]
        kc, vc, ks, vs, kw, vw = kvs
        grp = lambda a: a.reshape(bsz, nblk, CMP_STRIDE * HEAD_DIM)
        k_cmp, v_cmp = _compress(grp(kc), grp(vc), nsa_pe_k[li], nsa_pe_v[li], nsa_cmp_k_w1[li], nsa_cmp_k_w2[li], nsa_cmp_v_w1[li], nsa_cmp_v_w2[li])
        o_cmp, qa = _cmp_select(z_nsa, k_cmp, jnp.swapaxes(v_cmp, 1, 2).astype(BF16), overlap_t, heads=heads, tq=4 * tq, nsel=nsel)
        tk = min(512, seq)
        ka = jnp.concatenate([ks.astype(BF16), jnp.broadcast_to(onehot, (bsz, seq, nsel))], axis=2)
        ka = ka.reshape(bsz, seq // tk, tk, HEAD_DIM + nsel)
        kwa = jnp.concatenate([kw.astype(BF16), jnp.zeros((bsz, seq, nsel), BF16)], axis=2)
        kwa = kwa.reshape(bsz, seq // tq, tq, HEAD_DIM + nsel)
        ones_pad = jnp.concatenate([jnp.ones((bsz, seq, 1), BF16), jnp.zeros((bsz, seq, SUBLANES - 1), BF16)], axis=2)
        tiles_t = lambda a, n: jnp.swapaxes(
            jnp.concatenate([a.astype(BF16), ones_pad], axis=2).reshape(bsz, seq // n, n, HEAD_DIM + SUBLANES), 2, 3)
        y_b = _attend(qa, ka, tiles_t(vs, tk), kwa, tiles_t(vw, tq), z_nsa, (heads * HEAD_DIM) // LANES, o_cmp, tq=tq, tk=tk)

        x = _merge(y_scan, y_b, g_merge, x, w_branch[li].astype(BF16), w_out[li].astype(BF16), gt1, tt=steps)

        final_g = final_norm_g if li == depth - 1 else None
        j = li // 2
        if li % 2 == 0:
            weights = tuple(_to_bf16(wt[j:j + 1])[0] for wt in (ffn_w_gate, ffn_w_up, ffn_w_down))
            x = _ffn(x, norm_ffn_g[li], sc2, sh2, gt2, weights, final_g=final_g, tm=tm)
        else:
            rw = jnp.pad(moe_router_w[j], ((0, 0), (0, LANES - N_EXPERTS)))
            rb = jnp.pad(moe_router_b[j], (0, LANES - N_EXPERTS), constant_values=NEG_INF).reshape(1, LANES)
            x = _moe(x, norm_ffn_g[li], sc2, sh2, gt2, rw, rb, _to_bf16(moe_w_gate[j]), _to_bf16(moe_w_up[j]),
                     _to_bf16(moe_w_down[j]), final_g=final_g, tm=tm, tmg=512)
    return x
```

```python
import functools
import math

import jax
import jax.numpy as jnp
import numpy as np
from jax import lax
from jax.experimental import pallas as pl
from jax.experimental.pallas import tpu as pltpu

F32 = jnp.float32
BF16 = jnp.bfloat16

N_BRANCH = 4
S5_GROUP = 16
S5_STATE = 64
HEAD_DIM = 64
L_CMP = 32
CMP_STRIDE = 16
L_SEL = 64
TOP_N = 16
WINDOW = 512
ROPE_THETA = 10000.0
FORCED_SCORE = 1.0e4
NEG_INF = -1.0e30
LRU_CONV = 4
LRU_C = 8.0
SC_CONV = 3
N_EXPERTS = 8
RMS_EPS = 1e-6
LOG2E = math.log2(math.e)

LANES = 128
SUBLANES = 8
MXU_WIDTH = 256
VMEM_LIMIT = 52 * 1024 * 1024

HI = lax.Precision.HIGHEST


def _cparams(sem):
    return pltpu.CompilerParams(dimension_semantics=sem, vmem_limit_bytes=VMEM_LIMIT)


def _dot(a, b, precision=None):
    return jnp.dot(a, b, preferred_element_type=F32, precision=precision)


def _dot_t(a, b, precision=None):
    return lax.dot_general(a, b, (((1,), (1,)), ((), ())), preferred_element_type=F32, precision=precision)


def _gelu(x):
    return 0.5 * x * (1.0 + jnp.tanh(math.sqrt(2.0 / math.pi) * (x + 0.044715 * (x * x * x))))


def _sigmoid(x):
    return 0.5 * jnp.tanh(0.5 * x) + 0.5


def _silu(x):
    hx = 0.5 * x
    return hx * jnp.tanh(hx) + hx


def _norm_mod(x, g, scale, shift):
    var = jnp.mean(x * x, axis=-1, keepdims=True)
    return (x * lax.rsqrt(var + RMS_EPS) * g) * (1.0 + scale) + shift


def _rot_half(z):
    n = z.shape[-1]
    lane = lax.broadcasted_iota(jnp.int32, z.shape, z.ndim - 1)
    first = (lane % HEAD_DIM) < (HEAD_DIM // 2)
    return jnp.where(first, pltpu.roll(z, n - HEAD_DIM // 2, z.ndim - 1), pltpu.roll(z, HEAD_DIM // 2, z.ndim - 1))


def _cast_kernel(x_ref, o_ref):
    o_ref[...] = x_ref[...].astype(o_ref.dtype)


def _to_bf16(a):
    n, r, c = a.shape
    return pl.pallas_call(
        _cast_kernel,
        out_shape=jax.ShapeDtypeStruct(a.shape, BF16),
        grid=(n,),
        in_specs=[pl.BlockSpec((1, r, c), lambda i: (i, 0, 0))],
        out_specs=pl.BlockSpec((1, r, c), lambda i: (i, 0, 0)),
        compiler_params=_cparams(("arbitrary",)),
        name="cast_bf16",
    )(a)


def _transpose_cast_kernel(x_ref, o_ref):
    o_ref[0] = x_ref[0].T.astype(o_ref.dtype)


def _transposed_bf16(w, tn):
    depth, d, n = w.shape
    nt = n // tn
    return pl.pallas_call(
        _transpose_cast_kernel,
        out_shape=jax.ShapeDtypeStruct((depth, nt * tn, d), BF16),
        grid=(depth, nt),
        in_specs=[pl.BlockSpec((1, d, tn), lambda l, j: (l, 0, j))],
        out_specs=pl.BlockSpec((1, tn, d), lambda l, j: (l, j, 0)),
        compiler_params=_cparams(("arbitrary", "arbitrary")),
        name="transpose_cast",
    )(w)


def _mod_kernel(c_ref, w_ref, b_ref, o_ref):
    c = c_ref[...]
    o_ref[0] = _dot(_silu(c), w_ref[0], HI) + b_ref[0]


def _modulation(c, mod_w, mod_b):
    depth, d, n = mod_w.shape
    bsz = c.shape[0]
    tn = 1024
    return pl.pallas_call(
        _mod_kernel,
        out_shape=jax.ShapeDtypeStruct((depth, bsz, n), F32),
        grid=(depth, n // tn),
        in_specs=[
            pl.BlockSpec((bsz, d), lambda l, j: (0, 0)),
            pl.BlockSpec((1, d, tn), lambda l, j: (l, 0, j)),
            pl.BlockSpec((1, 1, tn), lambda l, j: (l, 0, j)),
        ],
        out_specs=pl.BlockSpec((1, bsz, tn), lambda l, j: (l, 0, j)),
        compiler_params=_cparams(("arbitrary", "arbitrary")),
        name="modulation",
    )(c, mod_w, mod_b.reshape(depth, 1, n))


def _proj_kernel(x_ref, g_ref, sc_ref, sh_ref, ws_ref, wn_ref, wg_ref, cos_ref, sin_ref,
                 zs_ref, qg_ref, kcvc_ref, ka_ref, kwa_ref, vsw_ref, gm_ref, h_ref):
    nb, tt, _ = x_ref.shape
    rows = nb * tt
    nlc = h_ref.shape[0]
    parts = []
    for b in range(nb):
        hb = _norm_mod(x_ref[b], g_ref[...], sc_ref[b], sh_ref[b])
        for c in range(nlc):
            h_ref[c, pl.ds(b, tt, stride=nb), :] = hb[:, c * LANES:(c + 1) * LANES]
        parts.append(hb.astype(BF16))
    h_bt = jnp.concatenate(parts, axis=0)
    h_tb = jnp.concatenate([h_ref[c] for c in range(nlc)], axis=1).astype(BF16)
    for n0 in range(0, ws_ref.shape[0], 2 * MXU_WIDTH):
        zs_ref[:, n0:n0 + 2 * MXU_WIDTH] = _dot_t(h_tb, ws_ref[n0:n0 + 2 * MXU_WIDTH, :])
    def nsa_chunk(c):
        n0 = c * MXU_WIDTH
        z = _dot_t(h_bt, wn_ref[n0:n0 + MXU_WIDTH, :])
        cos = jnp.concatenate([cos_ref[:, n0:n0 + MXU_WIDTH]] * nb, axis=0)
        sin = jnp.concatenate([sin_ref[:, n0:n0 + MXU_WIDTH]] * nb, axis=0)
        return z * cos + _rot_half(z) * sin

    def put(ref, val):
        ref[...] = val.astype(ref.dtype).reshape(ref.shape)

    qg_ref[:, :, 0:MXU_WIDTH] = nsa_chunk(0).reshape(nb, tt, MXU_WIDTH)
    z1 = nsa_chunk(1)
    qg_ref[:, :, MXU_WIDTH:MXU_WIDTH + LANES] = z1[:, 0:LANES].reshape(nb, tt, LANES)
    put(kcvc_ref, z1[:, LANES:2 * LANES])
    z2 = nsa_chunk(2)
    ks_vw, kw_vs = z2[:, 0:LANES], z2[:, LANES:2 * LANES]
    lane = lax.broadcasted_iota(jnp.int32, (rows, LANES), 1)
    key_lanes = lane < HEAD_DIM
    t = pl.program_id(0) * tt + lax.broadcasted_iota(jnp.int32, (rows, LANES), 0) % tt
    block_onehot = (lane == HEAD_DIM + t // L_SEL).astype(F32)
    put(ka_ref, jnp.where(key_lanes, ks_vw, block_onehot))
    put(kwa_ref, jnp.where(key_lanes, kw_vs, 0.0))
    put(vsw_ref, jnp.where(key_lanes, pltpu.roll(kw_vs, HEAD_DIM, 1), ks_vw))
    for n0 in range(0, wg_ref.shape[0], 2 * MXU_WIDTH):
        z = _dot_t(h_bt, wg_ref[n0:n0 + 2 * MXU_WIDTH, :])
        gm_ref[:, :, n0:n0 + 2 * MXU_WIDTH] = z.astype(gm_ref.dtype).reshape(nb, tt, 2 * MXU_WIDTH)


def _project(x, g, scale, shift, w_scan, w_nsa, w_gm, rope_tabs, *, tt):
    bsz, seq, d = x.shape
    ns, nn, ng = w_scan.shape[0], w_nsa.shape[0], w_gm.shape[0]
    full = lambda a: pl.BlockSpec(a.shape, lambda i: (0,) * a.ndim)
    tok = lambda n, dt: jax.ShapeDtypeStruct((bsz, seq, n), dt)
    tok_spec = lambda n: pl.BlockSpec((bsz, tt, n), lambda i: (0, i, 0))
    return pl.pallas_call(
        _proj_kernel,
        out_shape=(
            jax.ShapeDtypeStruct((seq * bsz, ns), F32),
            tok(MXU_WIDTH + LANES, F32), tok(LANES, F32), tok(LANES, BF16), tok(LANES, BF16), tok(LANES, BF16),
            tok(ng, BF16),
        ),
        grid=(seq // tt,),
        in_specs=[
            pl.BlockSpec((bsz, tt, d), lambda i: (0, i, 0)),
            pl.BlockSpec((1, d), lambda i: (0, 0)),
            pl.BlockSpec((bsz, 1, d), lambda i: (0, 0, 0)),
            pl.BlockSpec((bsz, 1, d), lambda i: (0, 0, 0)),
            full(w_scan), full(w_nsa), full(w_gm),
            pl.BlockSpec((tt, nn), lambda i: (i, 0)),
            pl.BlockSpec((tt, nn), lambda i: (i, 0)),
        ],
        out_specs=(
            pl.BlockSpec((tt * bsz, ns), lambda i: (i, 0)),
            tok_spec(MXU_WIDTH + LANES), tok_spec(LANES), tok_spec(LANES), tok_spec(LANES), tok_spec(LANES),
            tok_spec(ng),
        ),
        scratch_shapes=[pltpu.VMEM((d // LANES, tt * bsz, LANES), F32)],
        compiler_params=_cparams(("arbitrary",)),
        name="proj_in",
    )(x, g.reshape(1, d), scale, shift, w_scan, w_nsa, w_gm, *rope_tabs)


def _s5_param_kernel(lre_ref, lim_ref, ls_ref, bre_ref, bim_ref, are_ref, aim_ref, bbre_ref, bbim_ref):
    lre = lre_ref[...]
    lim = lim_ref[...]
    step = jnp.exp(ls_ref[...])
    mag = jnp.exp(lre * step)
    ar = mag * jnp.cos(lim * step)
    ai = mag * jnp.sin(lim * step)
    xr = ar - 1.0
    den = lre * lre + lim * lim
    cr = (xr * lre + ai * lim) / den
    ci = (ai * lre - xr * lim) / den
    bre = bre_ref[...]
    bim = bim_ref[...]
    are_ref[...] = ar
    aim_ref[...] = ai
    bbre_ref[...] = cr * bre - ci * bim
    bbim_ref[...] = cr * bim + ci * bre


def _s5_params(lam_re, lam_im, log_step, b_re, b_im):
    g, n = lam_re.shape
    c = b_re.shape[-1]
    rows = g * n
    col = lambda a: a.reshape(rows, 1)
    ls = jnp.broadcast_to(log_step[:, None], (g, n))
    out = pl.pallas_call(
        _s5_param_kernel,
        out_shape=(
            jax.ShapeDtypeStruct((rows, 1), F32),
            jax.ShapeDtypeStruct((rows, 1), F32),
            jax.ShapeDtypeStruct((rows, c), F32),
            jax.ShapeDtypeStruct((rows, c), F32),
        ),
        name="s5_params",
    )(col(lam_re), col(lam_im), col(ls), b_re.reshape(rows, c), b_im.reshape(rows, c))
    return out


def _scan_kernel(z_ref, s5b_ref, s5c_ref, are_ref, aim_ref, d_ref, wglu_ref,
                 cw_ref, cb_ref, wa_ref, ba_ref, wx_ref, bx_ref, lam_ref, scw_ref,
                 o_ref,
                 bu_ref, hre_ref, him_ref, xe_ref, hl_ref, pe_ref, ab_ref, bb_ref, *, nb, steps):
    c = pl.program_id(0)
    rows = nb * steps
    w = d_ref.shape[1]
    ns = are_ref.shape[1]

    @pl.when(c == 0)
    def _():
        hre_ref[...] = jnp.zeros_like(hre_ref)
        him_ref[...] = jnp.zeros_like(him_ref)
        hl_ref[...] = jnp.zeros_like(hl_ref)
        xe_ref[0:(LRU_CONV - 1) * nb, :] = jnp.zeros(((LRU_CONV - 1) * nb, w), F32)
        pe_ref[0:(SC_CONV - 1) * nb, :] = jnp.zeros(((SC_CONV - 1) * nb, w), F32)

    u = z_ref[:, 0:w]
    bu_ref[...] = _dot(u.astype(BF16), s5b_ref[...])
    a_re = jnp.broadcast_to(are_ref[...], (nb, ns))
    a_im = jnp.broadcast_to(aim_ref[...], (nb, ns))

    def s5_step(t, carry):
        h_re, h_im = carry
        r0 = pl.multiple_of(t * nb, nb)
        b_re = bu_ref[pl.ds(r0, nb), 0:ns]
        b_im = bu_ref[pl.ds(r0, nb), ns:2 * ns]
        n_re = a_re * h_re - a_im * h_im + b_re
        n_im = a_re * h_im + a_im * h_re + b_im
        bu_ref[pl.ds(r0, nb), 0:ns] = n_re
        bu_ref[pl.ds(r0, nb), ns:2 * ns] = n_im
        return n_re, n_im

    h_re, h_im = lax.fori_loop(0, steps, s5_step, (hre_ref[...], him_ref[...]))
    hre_ref[...] = h_re
    him_ref[...] = h_im
    y = _dot(bu_ref[...].astype(BF16), s5c_ref[...]) + d_ref[...] * u
    zg = _gelu(y)
    def put(branch, val):
        for k in range(w // LANES):
            o_ref[branch * (w // LANES) + k] = val[:, k * LANES:(k + 1) * LANES]

    put(0, zg * _sigmoid(_dot(zg.astype(BF16), wglu_ref[...])))

    nt = (LRU_CONV - 1) * nb
    x = z_ref[:, w:2 * w]
    xe_ref[nt:nt + rows, :] = x
    xc = cb_ref[...] + cw_ref[LRU_CONV - 1:LRU_CONV, :] * x
    for k in range(LRU_CONV - 1):
        xc = xc + cw_ref[k:k + 1, :] * xe_ref[k * nb:k * nb + rows, :]
    xe_ref[0:nt, :] = xe_ref[rows:rows + nt, :]
    xcb = xc.astype(BF16)
    r = _sigmoid(_dot(xcb, wa_ref[...]) + ba_ref[...])
    gi = _sigmoid(_dot(xcb, wx_ref[...]) + bx_ref[...])
    nl = -lam_ref[...]
    softplus = jnp.maximum(nl, 0.0) + jnp.log(1.0 + jnp.exp(-jnp.abs(nl)))
    log_a = (-LRU_C) * r * softplus
    a = jnp.exp(log_a)
    ab_ref[...] = a
    bb_ref[...] = jnp.sqrt(1.0 - a * a) * (gi * xc)

    def lru_step(t, h):
        r0 = pl.multiple_of(t * nb, nb)
        h = ab_ref[pl.ds(r0, nb), :] * h + bb_ref[pl.ds(r0, nb), :]
        bb_ref[pl.ds(r0, nb), :] = h
        return h

    hl_ref[...] = lax.fori_loop(0, steps, lru_step, hl_ref[...])
    put(1, bb_ref[...] * _gelu(z_ref[:, 2 * w:3 * w]))

    npt = (SC_CONV - 1) * nb
    p = z_ref[:, 4 * w:5 * w] * z_ref[:, 5 * w:6 * w]
    pe_ref[npt:npt + rows, :] = p
    cv = scw_ref[SC_CONV - 1:SC_CONV, :] * p
    for k in range(SC_CONV - 1):
        cv = cv + scw_ref[k:k + 1, :] * pe_ref[k * nb:k * nb + rows, :]
    pe_ref[0:npt, :] = pe_ref[rows:rows + npt, :]
    put(2, z_ref[:, 3 * w:4 * w] * cv)


def _scan_mixers(z_tm, nb, s5b, s5c, a_re, a_im, d_skip, w_glu, conv_w, conv_b, wa, ba, wx, bx, lam, sc_w, *, steps):
    total, zc = z_tm.shape
    rows = nb * steps
    w = zc // 6
    ns = a_re.shape[1]
    full = lambda a: pl.BlockSpec(a.shape, lambda c: (0,) * a.ndim)
    params = [s5b, s5c, a_re, a_im, d_skip, w_glu, conv_w, conv_b, wa, ba, wx, bx, lam, sc_w]
    return pl.pallas_call(
        functools.partial(_scan_kernel, nb=nb, steps=steps),
        out_shape=jax.ShapeDtypeStruct((3 * w // LANES, total, LANES), F32),
        grid=(total // rows,),
        in_specs=[pl.BlockSpec((rows, zc), lambda c: (c, 0))] + [full(a) for a in params],
        out_specs=pl.BlockSpec((3 * w // LANES, rows, LANES), lambda c: (0, c, 0)),
        scratch_shapes=[
            pltpu.VMEM((rows, 2 * ns), F32),
            pltpu.VMEM((nb, ns), F32),
            pltpu.VMEM((nb, ns), F32),
            pltpu.VMEM((rows + (LRU_CONV - 1) * nb, w), F32),
            pltpu.VMEM((nb, w), F32),
            pltpu.VMEM((rows + (SC_CONV - 1) * nb, w), F32),
            pltpu.VMEM((rows, w), F32),
            pltpu.VMEM((rows, w), F32),
        ],
        compiler_params=_cparams(("arbitrary",)),
        name="scan_mixers",
    )(z_tm, *params)


def _compress_kernel(gk_ref, gv_ref, pek_ref, pev_ref, wk1_ref, wk2_ref, wv1_ref, wv2_ref, ko_ref, vo_ref):
    half = gk_ref.shape[2]
    nblk = gk_ref.shape[1]

    def phi(g, pe, w1_ref, w2_ref):
        top = _dot(g, w1_ref[0:half, :], HI)
        bot = _dot(g, w1_ref[half:2 * half, :], HI)
        hid = top + pltpu.roll(bot, nblk - 1, 0) + _dot(pe, w1_ref[...], HI)
        return _dot(_gelu(hid), w2_ref[...], HI)

    ko_ref[0] = phi(gk_ref[0], pek_ref[...], wk1_ref, wk2_ref)
    vo_ref[0] = phi(gv_ref[0], pev_ref[...], wv1_ref, wv2_ref)


def _compress(gk, gv, pe_k, pe_v, wk1, wk2, wv1, wv2):
    bsz, nblk, half = gk.shape
    hd = wk2.shape[1]
    full = lambda a: pl.BlockSpec(a.shape, lambda b: (0,) * a.ndim)
    pek = pe_k.reshape(1, -1)
    pev = pe_v.reshape(1, -1)
    params = [pek, pev, wk1, wk2, wv1, wv2]
    blk = pl.BlockSpec((1, nblk, half), lambda b: (b, 0, 0))
    oblk = pl.BlockSpec((1, nblk, hd), lambda b: (b, 0, 0))
    return pl.pallas_call(
        _compress_kernel,
        out_shape=(jax.ShapeDtypeStruct((bsz, nblk, hd), F32),) * 2,
        grid=(bsz,),
        in_specs=[blk, blk] + [full(a) for a in params],
        out_specs=(oblk, oblk),
        compiler_params=_cparams(("arbitrary",)),
        name="nsa_compress",
    )(gk, gv, *params)


def _split_bf16(a):
    hi = a.astype(BF16)
    return hi, (a - hi.astype(F32)).astype(BF16)


def _cmp_kernel(q_ref, kc_ref, vct_ref, ovt_ref, ocmp_ref, qa_ref, *, heads, tq):
    i = pl.program_id(1)
    q_t = (q_ref[0] * HEAD_DIM ** -0.5).T
    kc_hi, kc_lo = _split_bf16(kc_ref[0])
    vct = vct_ref[0]
    nblk = kc_hi.shape[0]
    t = i * tq + lax.broadcasted_iota(jnp.int32, (nblk, tq), 1)
    n_id = lax.broadcasted_iota(jnp.int32, (nblk, tq), 0)
    visible = (n_id * CMP_STRIDE + (L_CMP - 1)) <= t
    psum = jnp.zeros((nblk, tq), F32)
    for h in range(heads):
        q_hi, q_lo = _split_bf16(q_t[h * HEAD_DIM:(h + 1) * HEAD_DIM, :])
        s = _dot(kc_hi, q_hi) + (_dot(kc_hi, q_lo) + _dot(kc_lo, q_hi))
        s = jnp.where(visible, s, NEG_INF)
        m = jnp.max(s, axis=0, keepdims=True)
        e = jnp.where(visible, jnp.exp(s - m), 0.0)
        den = jnp.sum(e, axis=0, keepdims=True)
        p = e / jnp.where(den > 0.0, den, 1.0)
        psum = psum + p
        ocmp_ref[0, h * HEAD_DIM:(h + 1) * HEAD_DIM, :] = _dot(vct, p.astype(BF16))

    p_hi, p_lo = _split_bf16(psum)
    imp_t = _dot(ovt_ref[...], p_hi) + _dot(ovt_ref[...], p_lo)
    nslot = ovt_ref.shape[0]
    tt = i * tq + lax.broadcasted_iota(jnp.int32, (nslot, tq), 1)
    j = lax.broadcasted_iota(jnp.int32, (nslot, tq), 0)
    cur = tt // L_SEL
    forced = (j == 0) | (j == cur) | (j == cur - 1)
    future = (j * L_SEL) > tt
    imp_t = jnp.where(future, -1.0, jnp.where(forced, FORCED_SCORE, imp_t))
    nreal = qa_ref.shape[2] - HEAD_DIM
    cand = imp_t[0:nreal, :]
    jrow = lax.broadcasted_iota(jnp.int32, (nreal, tq), 0)
    bias_t = jnp.full((nreal, tq), NEG_INF, F32)
    for _ in range(TOP_N):
        best = jnp.max(cand, axis=0, keepdims=True)
        first = jnp.min(jnp.where(cand == best, jrow, nreal), axis=0, keepdims=True)
        hit = jrow == first
        bias_t = jnp.where(hit, 0.0, bias_t)
        cand = jnp.where(hit, -2.0, cand)
    for h in range(heads):
        qa_ref[0, h] = jnp.concatenate([q_t[h * HEAD_DIM:(h + 1) * HEAD_DIM, :] * LOG2E, bias_t], axis=0).astype(BF16)


def _cmp_select(q, k_cmp, v_cmp_t, overlap_t, *, heads, tq):
    bsz, seq, _ = q.shape
    qd = heads * HEAD_DIM
    nblk = k_cmp.shape[1]
    nsel = LANES - HEAD_DIM
    return pl.pallas_call(
        functools.partial(_cmp_kernel, heads=heads, tq=tq),
        out_shape=(
            jax.ShapeDtypeStruct((bsz, qd, seq), F32),
            jax.ShapeDtypeStruct((bsz, heads, HEAD_DIM + nsel, seq), BF16),
        ),
        grid=(bsz, seq // tq),
        in_specs=[
            pl.BlockSpec((1, tq, qd), lambda b, i: (b, i, 0)),
            pl.BlockSpec((1, nblk, HEAD_DIM), lambda b, i: (b, 0, 0)),
            pl.BlockSpec((1, HEAD_DIM, nblk), lambda b, i: (b, 0, 0)),
            pl.BlockSpec(overlap_t.shape, lambda b, i: (0, 0)),
        ],
        out_specs=(
            pl.BlockSpec((1, qd, tq), lambda b, i: (b, 0, i)),
            pl.BlockSpec((1, heads, HEAD_DIM + nsel, tq), lambda b, i: (b, 0, 0, i)),
        ),
        compiler_params=_cparams(("arbitrary", "arbitrary")),
        name="nsa_cmp_select",
    )(q, k_cmp, v_cmp_t, overlap_t)


def _attend_kernel(qa_ref, ka_ref, vst_ref, kwa_ref, vwt_ref, gn_ref, ocmp_ref, o_ref,
                   m_ref, acc_ref, s0_ref, s1_ref, *, heads, tq, tk, nwin):
    i = pl.program_id(1)
    cols = heads * tq
    qat = jnp.concatenate([qa_ref[0, h] for h in range(heads)], axis=1)
    t_row = i * tq + (lax.broadcasted_iota(jnp.int32, (1, cols), 1) % tq)

    m_ref[...] = jnp.full(m_ref.shape, NEG_INF, F32)
    acc_ref[...] = jnp.zeros(acc_ref.shape, F32)
    key_id = lax.broadcasted_iota(jnp.int32, (tk, cols), 0)
    n_full = (i * tq + 1) // tk
    s0_ref[...] = _dot(ka_ref[0, 0], qat)

    st = jnp.maximum(i + 1 - nwin, 0)
    kw = kwa_ref[0, pl.ds(st, nwin)].reshape(nwin * tq, kwa_ref.shape[3])
    s = _dot(kw, qat)
    dist = t_row - (st * tq + lax.broadcasted_iota(jnp.int32, (nwin * tq, cols), 0))
    s = jnp.where((dist >= 0) & (dist < WINDOW), s, NEG_INF)
    m = jnp.max(s, axis=0, keepdims=True)
    pb = jnp.exp2(s - m).astype(BF16)
    ow = _dot(vwt_ref[0, st], pb[0:tq, :])
    for j in range(1, nwin):
        ow = ow + _dot(vwt_ref[0, st + j], pb[j * tq:(j + 1) * tq, :])
    owin = ow[0:HEAD_DIM, :] / ow[HEAD_DIM:HEAD_DIM + 1, :]

    def softmax_tile(kt, s, masked):
        if masked:
            s = jnp.where(kt * tk + key_id <= t_row, s, NEG_INF)
        m_old = m_ref[...]
        m_new = jnp.maximum(m_old, jnp.max(s, axis=0, keepdims=True))
        p = jnp.exp2(s - m_new).astype(BF16)
        acc_ref[...] = jnp.exp2(m_old - m_new) * acc_ref[...] + _dot(vst_ref[0, kt], p)
        m_ref[...] = m_new

    def pair_body(j, carry):
        s1_ref[...] = _dot(ka_ref[0, 2 * j + 1], qat)
        softmax_tile(2 * j, s0_ref[...], False)
        s0_ref[...] = _dot(ka_ref[0, 2 * j + 2], qat)
        softmax_tile(2 * j + 1, s1_ref[...], False)
        return carry

    n_pair = n_full // 2
    lax.fori_loop(0, n_pair, pair_body, 0)
    s1_ref[...] = _dot(ka_ref[0, 2 * n_pair + 1], qat)
    softmax_tile(2 * n_pair, s0_ref[...], True)
    softmax_tile(2 * n_pair + 1, s1_ref[...], True)

    osel = acc_ref[0:HEAD_DIM, :] / acc_ref[HEAD_DIM:HEAD_DIM + 1, :]

    g_t = _sigmoid(gn_ref[0]).T
    outs = []
    for h in range(heads):
        cs = slice(h * tq, (h + 1) * tq)
        outs.append(g_t[3 * h:3 * h + 1, :] * ocmp_ref[0, h * HEAD_DIM:(h + 1) * HEAD_DIM, :]
                    + g_t[3 * h + 1:3 * h + 2, :] * osel[:, cs]
                    + g_t[3 * h + 2:3 * h + 3, :] * owin[:, cs])
    o_ref[0] = jnp.concatenate(outs, axis=0).T


def _attend(qa, ka, vst, kwa, vwt, z_nsa, gn_block, o_cmp_t, *, tq, tk):
    bsz, heads, ka_w, seq = qa.shape
    qd = heads * HEAD_DIM
    nwin = WINDOW // tq + 1
    return pl.pallas_call(
        functools.partial(_attend_kernel, heads=heads, tq=tq, tk=tk, nwin=nwin),
        out_shape=jax.ShapeDtypeStruct((bsz, seq, qd), F32),
        grid=(bsz, seq // tq),
        in_specs=[
            pl.BlockSpec((1, heads, ka_w, tq), lambda b, i: (b, 0, 0, i)),
            pl.BlockSpec((1,) + ka.shape[1:], lambda b, i: (b, 0, 0, 0)),
            pl.BlockSpec((1,) + vst.shape[1:], lambda b, i: (b, 0, 0, 0)),
            pl.BlockSpec((1,) + kwa.shape[1:], lambda b, i: (b, 0, 0, 0)),
            pl.BlockSpec((1,) + vwt.shape[1:], lambda b, i: (b, 0, 0, 0)),
            pl.BlockSpec((1, tq, LANES), lambda b, i: (b, i, gn_block)),
            pl.BlockSpec((1, qd, tq), lambda b, i: (b, 0, i)),
        ],
        out_specs=pl.BlockSpec((1, tq, qd), lambda b, i: (b, i, 0)),
        scratch_shapes=[
            pltpu.VMEM((1, heads * tq), F32),
            pltpu.VMEM((vst.shape[2], heads * tq), F32),
            pltpu.VMEM((tk, heads * tq), F32),
            pltpu.VMEM((tk, heads * tq), F32),
        ],
        compiler_params=_cparams(("arbitrary", "arbitrary")),
        name="nsa_attend",
    )(qa, ka, vst, kwa, vwt, z_nsa, o_cmp_t)


def _merge_kernel(ys_ref, yb_ref, gm_ref, x_ref, wb_ref, wo_ref, gate_ref, o_ref, m_ref):
    nb, tt, w = yb_ref.shape
    d = x_ref.shape[2]
    rows = nb * tt
    def slab(c):
        return jnp.concatenate([ys_ref[c, pl.ds(b, tt, stride=nb), :] for b in range(nb)], axis=0)

    ys = jnp.concatenate([slab(c) for c in range(ys_ref.shape[0])], axis=1)
    branches = (ys[:, 0:w], yb_ref[...].reshape(rows, w), ys[:, w:2 * w], ys[:, 2 * w:3 * w])
    ybs = [y.astype(BF16) for y in branches]
    for n0 in range(0, d, MXU_WIDTH):
        plain = None
        gated = None
        for mi, y in enumerate(ybs):
            p = _dot(y, wb_ref[mi, :, n0:n0 + MXU_WIDTH])
            gm = gm_ref[:, :, mi * d + n0:mi * d + n0 + MXU_WIDTH].reshape(rows, MXU_WIDTH)
            t = jnp.tanh(gm.astype(F32)) * p
            plain = p if plain is None else plain + p
            gated = t if gated is None else gated + t
        m_ref[:, n0:n0 + MXU_WIDTH] = (0.5 * (plain + gated)).astype(BF16)
    mix = _dot(m_ref[...], wo_ref[...]).reshape(nb, tt, d)
    o_ref[...] = x_ref[...] + gate_ref[...] * mix


def _merge(y_scan_tm, y_b, g_merge, x, w_branch, w_out, gate, *, tt):
    bsz, seq, d = x.shape
    w = y_b.shape[2]
    tile = lambda n: pl.BlockSpec((bsz, tt, n), lambda i: (0, i, 0))
    return pl.pallas_call(
        _merge_kernel,
        out_shape=jax.ShapeDtypeStruct((bsz, seq, d), F32),
        grid=(seq // tt,),
        in_specs=[
            pl.BlockSpec((3 * w // LANES, tt * bsz, LANES), lambda i: (0, i, 0)),
            tile(w),
            tile(N_BRANCH * d),
            tile(d),
            pl.BlockSpec(w_branch.shape, lambda i: (0, 0, 0)),
            pl.BlockSpec(w_out.shape, lambda i: (0, 0)),
            pl.BlockSpec((bsz, 1, d), lambda i: (0, 0, 0)),
        ],
        out_specs=tile(d),
        scratch_shapes=[pltpu.VMEM((tt * bsz, d), BF16)],
        compiler_params=_cparams(("arbitrary",)),
        name="merge_out",
    )(y_scan_tm, y_b, g_merge, x, w_branch, w_out, gate)


def _final_norm(out, fg_ref):
    var = jnp.mean(out * out, axis=-1, keepdims=True)
    return out * lax.rsqrt(var + RMS_EPS) * fg_ref[...]


def _ffn_kernel(x_ref, g_ref, sc_ref, sh_ref, gate_ref, wg_ref, wu_ref, wd_ref, *rest, final, fc):
    if final:
        fg_ref, o_ref, a_ref = rest
    else:
        o_ref, a_ref = rest
    hb = _norm_mod(x_ref[0], g_ref[...], sc_ref[0], sh_ref[0]).astype(BF16)
    for f0 in range(0, a_ref.shape[1], fc):
        a_ref[:, f0:f0 + fc] = (_silu(_dot(hb, wg_ref[:, f0:f0 + fc])) * _dot(hb, wu_ref[:, f0:f0 + fc])).astype(BF16)
    out = x_ref[0] + gate_ref[0] * _dot(a_ref[...], wd_ref[...])
    o_ref[0] = _final_norm(out, fg_ref) if final else out


def _ffn(x, g, scale, shift, gate, weights, *, final_g, tm):
    bsz, seq, d = x.shape
    row = lambda: pl.BlockSpec((1, 1, d), lambda b, i: (b, 0, 0))
    wg, wu, wd = weights
    ff = wg.shape[1]
    in_specs = [
        pl.BlockSpec((1, tm, d), lambda b, i: (b, i, 0)),
        pl.BlockSpec((1, d), lambda b, i: (0, 0)),
        row(), row(), row(),
        pl.BlockSpec((d, ff), lambda b, i: (0, 0)),
        pl.BlockSpec((d, ff), lambda b, i: (0, 0)),
        pl.BlockSpec((ff, d), lambda b, i: (0, 0)),
    ]
    args = [x, g.reshape(1, d), scale, shift, gate, wg, wu, wd]
    final = final_g is not None
    if final:
        in_specs.append(pl.BlockSpec((1, d), lambda b, i: (0, 0)))
        args.append(final_g.reshape(1, d))
    return pl.pallas_call(
        functools.partial(_ffn_kernel, final=final, fc=MXU_WIDTH),
        out_shape=jax.ShapeDtypeStruct((bsz, seq, d), F32),
        grid=(bsz, seq // tm),
        in_specs=in_specs,
        out_specs=pl.BlockSpec((1, tm, d), lambda b, i: (b, i, 0)),
        scratch_shapes=[pltpu.VMEM((tm, ff), BF16)],
        compiler_params=_cparams(("arbitrary", "arbitrary")),
        name="dense_ffn",
    )(*args)


def _route_kernel(x_ref, g_ref, sc_ref, sh_ref, rw_ref, rb_ref, h_ref, r_ref):
    h = _norm_mod(x_ref[0], g_ref[...], sc_ref[0], sh_ref[0])
    h_hi, h_lo = _split_bf16(h)
    h_ref[0] = h_hi
    w_hi, w_lo = _split_bf16(rw_ref[...])
    logits = _dot(h_hi, w_hi) + (_dot(h_hi, w_lo) + _dot(h_lo, w_hi)) + rb_ref[...]
    lane = lax.broadcasted_iota(jnp.int32, logits.shape, 1)
    big = logits.shape[1]
    v1 = jnp.max(logits, axis=-1, keepdims=True)
    i1 = jnp.min(jnp.where(logits == v1, lane, big), axis=-1, keepdims=True)
    rest_l = jnp.where(lane == i1, NEG_INF * 4.0, logits)
    v2 = jnp.max(rest_l, axis=-1, keepdims=True)
    i2 = jnp.min(jnp.where(rest_l == v2, lane, big), axis=-1, keepdims=True)
    e2 = jnp.exp(v2 - v1)
    w1 = 1.0 / (1.0 + e2)
    w2 = e2 / (1.0 + e2)
    r_ref[0] = jnp.where(lane == 0, i1.astype(F32), jnp.where(lane == 1, i2.astype(F32), jnp.where(lane == 2, w1, w2)))


def _route(x, g, scale, shift, rw, rb, *, tm):
    bsz, seq, d = x.shape
    row = lambda: pl.BlockSpec((1, 1, d), lambda b, i: (b, 0, 0))
    return pl.pallas_call(
        _route_kernel,
        out_shape=(jax.ShapeDtypeStruct((bsz, seq, d), BF16), jax.ShapeDtypeStruct((bsz, seq, LANES), F32)),
        grid=(bsz, seq // tm),
        in_specs=[
            pl.BlockSpec((1, tm, d), lambda b, i: (b, i, 0)),
            pl.BlockSpec((1, d), lambda b, i: (0, 0)),
            row(), row(),
            pl.BlockSpec(rw.shape, lambda b, i: (0, 0)),
            pl.BlockSpec(rb.shape, lambda b, i: (0, 0)),
        ],
        out_specs=(pl.BlockSpec((1, tm, d), lambda b, i: (b, i, 0)), pl.BlockSpec((1, tm, LANES), lambda b, i: (b, i, 0))),
        compiler_params=_cparams(("arbitrary", "arbitrary")),
        name="moe_route",
    )(x, g.reshape(1, d), scale, shift, rw, rb)


def _gffn_kernel(te_ref, nu_ref, x_ref, wg_ref, wu_ref, wd_ref, o_ref):
    i = pl.program_id(0)

    @pl.when(i < nu_ref[0])
    def _():
        hb = x_ref[...]
        a = (_silu(_dot(hb, wg_ref[0])) * _dot(hb, wu_ref[0])).astype(BF16)
        o_ref[...] = _dot(a, wd_ref[0]).astype(o_ref.dtype)

    @pl.when(i >= nu_ref[0])
    def _():
        o_ref[...] = jnp.zeros_like(o_ref)


def _grouped_ffn(xs, tile_expert, n_used, wg, wu, wd, *, tmg):
    rows, d = xs.shape
    _, _, fe = wg.shape
    return pl.pallas_call(
        _gffn_kernel,
        out_shape=jax.ShapeDtypeStruct((rows, d), BF16),
        grid_spec=pltpu.PrefetchScalarGridSpec(
            num_scalar_prefetch=2,
            grid=(rows // tmg,),
            in_specs=[
                pl.BlockSpec((tmg, d), lambda i, te, nu: (i, 0)),
                pl.BlockSpec((1, d, fe), lambda i, te, nu: (te[i], 0, 0)),
                pl.BlockSpec((1, d, fe), lambda i, te, nu: (te[i], 0, 0)),
                pl.BlockSpec((1, fe, d), lambda i, te, nu: (te[i], 0, 0)),
            ],
            out_specs=pl.BlockSpec((tmg, d), lambda i, te, nu: (i, 0)),
        ),
        compiler_params=_cparams(("arbitrary",)),
        name="moe_grouped_ffn",
    )(tile_expert, n_used, xs, wg, wu, wd)


def _combine_kernel(x_ref, gate_ref, y_ref, y2_ref, r_ref, *rest, final):
    if final:
        fg_ref, o_ref = rest
    else:
        (o_ref,) = rest
    r = r_ref[0]
    f = r[:, 2:3] * y_ref[0].astype(F32) + r[:, 3:4] * y2_ref[0].astype(F32)
    out = x_ref[0] + gate_ref[0] * f
    o_ref[0] = _final_norm(out, fg_ref) if final else out


def _combine(x, gate, y1, y2, route, *, final_g, tm):
    bsz, seq, d = x.shape
    tile = lambda: pl.BlockSpec((1, tm, d), lambda b, i: (b, i, 0))
    in_specs = [
        tile(),
        pl.BlockSpec((1, 1, d), lambda b, i: (b, 0, 0)),
        tile(), tile(),
        pl.BlockSpec((1, tm, LANES), lambda b, i: (b, i, 0)),
    ]
    args = [x, gate, y1, y2, route]
    final = final_g is not None
    if final:
        in_specs.append(pl.BlockSpec((1, d), lambda b, i: (0, 0)))
        args.append(final_g.reshape(1, d))
    return pl.pallas_call(
        functools.partial(_combine_kernel, final=final),
        out_shape=jax.ShapeDtypeStruct((bsz, seq, d), F32),
        grid=(bsz, seq // tm),
        in_specs=in_specs,
        out_specs=pl.BlockSpec((1, tm, d), lambda b, i: (b, i, 0)),
        compiler_params=_cparams(("arbitrary", "arbitrary")),
        name="moe_combine",
    )(*args)


def _src_kernel(te_ref, k0_ref, jlo_ref, jhi_ref, c_ref, o_ref, cnt_ref, *, nblk, tb, last):
    i = pl.program_id(0)
    tmg = o_ref.shape[0]
    base = te_ref[i] * nblk
    kcol = k0_ref[i] + lax.broadcasted_iota(jnp.int32, (tmg, 1), 0)
    cnt_ref[...] = jnp.zeros_like(cnt_ref)

    def blk(j, carry):
        hit = (c_ref[base + j] <= kcol).astype(jnp.int32)
        part = hit[:, 0:LANES]
        for q in range(1, tb // LANES):
            part = part + hit[:, q * LANES:(q + 1) * LANES]
        cnt_ref[...] += part
        return carry

    lax.fori_loop(jlo_ref[i], jhi_ref[i], blk, 0)
    total = jnp.sum(cnt_ref[...], axis=1, keepdims=True) + jlo_ref[i] * tb
    o_ref[...] = jnp.minimum(total, last)


def _moe(x, g, scale, shift, gate, rw, rb, wg, wu, wd, *, final_g, tm, tmg):
    bsz, seq, d = x.shape
    ne = wg.shape[0]
    tokens = bsz * seq
    tb = 512
    nblk = tokens // tb
    h, route = _route(x, g, scale, shift, rw, rb, tm=tm)
    ar = jnp.arange(ne, dtype=jnp.int32)[:, None]
    m1 = route[:, :, 0].astype(jnp.int32).reshape(1, tokens) == ar
    m2 = route[:, :, 1].astype(jnp.int32).reshape(1, tokens) == ar
    count = jnp.cumsum((m1 | m2).astype(jnp.int32), axis=1)
    padded = ((count[:, -1] + tmg - 1) // tmg) * tmg
    ends = jnp.cumsum(padded)
    offs = ends - padded
    row_of = count - 1 + offs[:, None]
    dest1 = jnp.sum(jnp.where(m1, row_of, 0), axis=0)
    dest2 = jnp.sum(jnp.where(m2, row_of, 0), axis=0)
    rows = tokens * 2 + ne * tmg
    ntiles = rows // tmg
    tile_start = jnp.arange(ntiles, dtype=jnp.int32) * tmg
    tile_expert = jnp.minimum(jnp.sum((tile_start[:, None] >= ends[None, :]).astype(jnp.int32), axis=1), ne - 1)
    n_used = (ends[-1] // tmg).astype(jnp.int32).reshape(1)
    k0 = tile_start - offs[tile_expert]
    cend = count[:, tb - 1::tb]
    cstart = jnp.concatenate([jnp.zeros((ne, 1), jnp.int32), cend[:, :-1]], axis=1)
    jlo = jnp.sum((cend[tile_expert] <= k0[:, None]).astype(jnp.int32), axis=1)
    jhi = jnp.sum((cstart[tile_expert] <= (k0 + tmg - 1)[:, None]).astype(jnp.int32), axis=1)
    src = pl.pallas_call(
        functools.partial(_src_kernel, nblk=nblk, tb=tb, last=tokens - 1),
        out_shape=jax.ShapeDtypeStruct((rows, 1), jnp.int32),
        grid_spec=pltpu.PrefetchScalarGridSpec(
            num_scalar_prefetch=4,
            grid=(ntiles,),
            in_specs=[pl.BlockSpec((ne * nblk, 1, tb), lambda i, *_: (0, 0, 0))],
            out_specs=pl.BlockSpec((tmg, 1), lambda i, *_: (i, 0)),
            scratch_shapes=[pltpu.VMEM((tmg, LANES), jnp.int32)],
        ),
        compiler_params=_cparams(("arbitrary",)),
        name="moe_sources",
    )(tile_expert, k0, jlo, jhi, count.reshape(ne * nblk, 1, tb))
    rows_of = lambda a, idx: a.at[idx].get(mode="promise_in_bounds")
    xs = rows_of(h.reshape(tokens, d), src[:, 0])
    ys = _grouped_ffn(xs, tile_expert, n_used, wg, wu, wd, tmg=tmg)
    y1 = rows_of(ys, dest1).reshape(bsz, seq, d)
    y2 = rows_of(ys, dest2).reshape(bsz, seq, d)
    return _combine(x, gate, y1, y2, route, final_g=final_g, tm=tm)


def _block_diag(blocks):
    n, r, c = blocks.shape
    eye = jnp.eye(n, dtype=blocks.dtype)
    return (eye[:, None, :, None] * blocks[:, :, None, :]).reshape(n * r, n * c)


def _rope_tables(seq, heads_q):
    inv = ROPE_THETA ** (-jnp.arange(0, HEAD_DIM, 2, dtype=F32) / HEAD_DIM)
    ang = jnp.arange(seq, dtype=F32)[:, None] * inv[None, :]
    cos, sin = jnp.cos(ang), jnp.sin(ang)
    cos_h = jnp.concatenate([cos, cos], axis=1)
    sin_h = jnp.concatenate([-sin, sin], axis=1)
    one_h = jnp.ones_like(cos_h)
    zero_h = jnp.zeros_like(cos_h)
    cos_t = jnp.concatenate([cos_h] * heads_q + [one_h, one_h] + [cos_h, one_h] * 3, axis=1)
    sin_t = jnp.concatenate([sin_h] * heads_q + [zero_h, zero_h] + [sin_h, zero_h] * 3, axis=1)
    return cos_t, sin_t


def kernel(x, c, mod_w, mod_b, norm_mix_g, norm_ffn_g, w_in, s5_lambda_re, s5_lambda_im, s5_log_step, s5_b_re, s5_b_im, s5_c_re, s5_c_im, s5_d, s5_w_glu, nsa_pe_k, nsa_pe_v, nsa_cmp_k_w1, nsa_cmp_k_w2, nsa_cmp_v_w1, nsa_cmp_v_w2, lru_conv_w, lru_conv_b, lru_w_a, lru_b_a, lru_w_x, lru_b_x, lru_lambda, sc_conv_w, w_branch, w_out, ffn_w_gate, ffn_w_up, ffn_w_down, moe_router_w, moe_router_b, moe_w_gate, moe_w_up, moe_w_down, final_norm_g):
    bsz, seq, d = x.shape
    depth = mod_w.shape[0]
    w = d // N_BRANCH
    heads = w // HEAD_DIM
    groups = w // S5_GROUP
    assert bsz == SUBLANES, "time-major scan blocks hold one timestep per sublane tile"
    tm = min(512, seq)
    tq = 128
    steps = min(64, seq)
    nsel = seq // L_SEL
    assert TOP_N <= nsel <= LANES - HEAD_DIM and seq % tm == 0 and seq >= WINDOW + tq

    mod = _modulation(c, mod_w, mod_b)
    mod = mod.reshape(depth, bsz, 6, 1, d)

    sizes = (w, w, HEAD_DIM, HEAD_DIM, HEAD_DIM, HEAD_DIM, HEAD_DIM, HEAD_DIM, heads * 3, w, w, w, w, w, N_BRANCH * d)
    offs = np.concatenate([[0], np.cumsum(sizes)])
    col = lambda wmat_t, k: wmat_t[offs[k]:offs[k + 1], :]

    cos_t, sin_t = _rope_tables(seq, heads)
    n_cmp = (seq - L_CMP) // CMP_STRIDE + 1
    nblk = seq // CMP_STRIDE
    cmp_start = np.arange(nblk) * CMP_STRIDE
    sel_start = np.arange(nsel) * L_SEL
    ovl = np.clip(np.minimum(cmp_start[:, None] + L_CMP, sel_start[None, :] + L_SEL)
                  - np.maximum(cmp_start[:, None], sel_start[None, :]), 0, None) / L_CMP
    ovl[n_cmp:] = 0.0
    overlap_t = jnp.asarray(np.pad(ovl, ((0, 0), (0, LANES - nsel))).T, BF16)

    w_in_t = _transposed_bf16(w_in, MXU_WIDTH)

    for li in range(depth):
        sh1, sc1, gt1, sh2, sc2, gt2 = (mod[li, :, k] for k in range(6))
        n_main = w_in_t.shape[1]
        assert offs[14] <= n_main
        wl = w_in_t[li]
        w_scan = jnp.concatenate([col(wl, 0), col(wl, 9), col(wl, 10), col(wl, 11), col(wl, 12), col(wl, 13)], axis=0)
        gpad = jnp.zeros((LANES - heads * 3, d), wl.dtype)
        w_nsa = jnp.concatenate([col(wl, 1), col(wl, 8), gpad] + [col(wl, k) for k in (2, 3, 4, 7, 6, 5)], axis=0)
        w_gm = 0.5 * jnp.concatenate([wl[offs[14]:], w_in[li][:, n_main:].T.astype(BF16)], axis=0)
        g1 = norm_mix_g[li]
        z_scan, q_gn, kcvc, ka, kwa, vsw, g_merge = _project(x, g1, sc1, sh1, w_scan, w_nsa, w_gm, (cos_t, sin_t), tt=steps)

        a_re, a_im, bb_re, bb_im = _s5_params(s5_lambda_re[li], s5_lambda_im[li], s5_log_step[li], s5_b_re[li], s5_b_im[li])
        to_bd = lambda m: _block_diag(jnp.swapaxes(m.reshape(groups, S5_STATE, S5_GROUP), 1, 2))
        s5b = jnp.concatenate([to_bd(bb_re), to_bd(bb_im)], axis=1).astype(BF16)
        c_bd = lambda m: _block_diag(jnp.swapaxes(m, 1, 2))
        s5c = jnp.concatenate([c_bd(s5_c_re[li]), -c_bd(s5_c_im[li])], axis=0).astype(BF16)
        r1 = lambda a: a.reshape(1, -1)
        y_scan = _scan_mixers(
            z_scan, bsz, s5b, s5c, r1(a_re), r1(a_im), r1(s5_d[li]), s5_w_glu[li].astype(BF16),
            lru_conv_w[li], r1(lru_conv_b[li]), _block_diag(lru_w_a[li]).astype(BF16), r1(lru_b_a[li]),
            _block_diag(lru_w_x[li]).astype(BF16), r1(lru_b_x[li]), r1(lru_lambda[li]), sc_conv_w[li], steps=2 * steps)

        grp = lambda a: a.reshape(bsz, nblk, CMP_STRIDE * HEAD_DIM)
        k_cmp, v_cmp = _compress(grp(kcvc[:, :, :HEAD_DIM]), grp(kcvc[:, :, HEAD_DIM:]), nsa_pe_k[li], nsa_pe_v[li],
                                 nsa_cmp_k_w1[li], nsa_cmp_k_w2[li], nsa_cmp_v_w1[li], nsa_cmp_v_w2[li])
        o_cmp, qa = _cmp_select(q_gn, k_cmp, jnp.swapaxes(v_cmp, 1, 2).astype(BF16), overlap_t, heads=heads, tq=4 * tq)
        tk = min(512, seq)
        ones_pad = jnp.concatenate([jnp.ones((bsz, seq, 1), BF16), jnp.zeros((bsz, seq, SUBLANES - 1), BF16)], axis=2)
        tiles_t = lambda a, n: jnp.swapaxes(
            jnp.concatenate([a, ones_pad], axis=2).reshape(bsz, seq // n, n, HEAD_DIM + SUBLANES), 2, 3)
        y_b = _attend(qa, ka.reshape(bsz, seq // tk, tk, LANES), tiles_t(vsw[:, :, :HEAD_DIM], tk),
                      kwa.reshape(bsz, seq // tq, tq, LANES), tiles_t(vsw[:, :, HEAD_DIM:], tq),
                      q_gn, (heads * HEAD_DIM) // LANES, o_cmp, tq=tq, tk=tk)

        x = _merge(y_scan, y_b, g_merge, x, w_branch[li].astype(BF16), w_out[li].astype(BF16), gt1, tt=steps)

        final_g = final_norm_g if li == depth - 1 else None
        j = li // 2
        if li % 2 == 0:
            weights = tuple(_to_bf16(wt[j:j + 1])[0] for wt in (ffn_w_gate, ffn_w_up, ffn_w_down))
            x = _ffn(x, norm_ffn_g[li], sc2, sh2, gt2, weights, final_g=final_g, tm=tm)
        else:
            rw = jnp.pad(moe_router_w[j], ((0, 0), (0, LANES - N_EXPERTS)))
            rb = jnp.pad(moe_router_b[j], (0, LANES - N_EXPERTS), constant_values=NEG_INF).reshape(1, LANES)
            x = _moe(x, norm_ffn_g[li], sc2, sh2, gt2, rw, rb, _to_bf16(moe_w_gate[j]), _to_bf16(moe_w_up[j]),
                     _to_bf16(moe_w_down[j]), final_g=final_g, tm=tm, tmg=512)
    return x
```

```python
import functools
import math

import jax
import jax.numpy as jnp
import numpy as np
from jax import lax
from jax.experimental import pallas as pl
from jax.experimental.pallas import tpu as pltpu

F32 = jnp.float32
BF16 = jnp.bfloat16

N_BRANCH = 4
S5_GROUP = 16
S5_STATE = 64
HEAD_DIM = 64
L_CMP = 32
CMP_STRIDE = 16
L_SEL = 64
TOP_N = 16
WINDOW = 512
ROPE_THETA = 10000.0
FORCED_SCORE = 1.0e4
NEG_INF = -1.0e30
LRU_CONV = 4
LRU_C = 8.0
SC_CONV = 3
N_EXPERTS = 8
RMS_EPS = 1e-6
LOG2E = math.log2(math.e)

LANES = 128
SUBLANES = 8
MXU_WIDTH = 256
VMEM_LIMIT = 52 * 1024 * 1024

HI = lax.Precision.HIGHEST


def _cparams(sem):
    return pltpu.CompilerParams(dimension_semantics=sem, vmem_limit_bytes=VMEM_LIMIT)


def _dot(a, b, precision=None):
    return jnp.dot(a, b, preferred_element_type=F32, precision=precision)


def _dot_t(a, b, precision=None):
    return lax.dot_general(a, b, (((1,), (1,)), ((), ())), preferred_element_type=F32, precision=precision)


def _gelu(x):
    return 0.5 * x * (1.0 + jnp.tanh(math.sqrt(2.0 / math.pi) * (x + 0.044715 * (x * x * x))))


def _sigmoid(x):
    return 0.5 * jnp.tanh(0.5 * x) + 0.5


def _silu(x):
    hx = 0.5 * x
    return hx * jnp.tanh(hx) + hx


def _norm_mod(x, g, scale, shift):
    var = jnp.mean(x * x, axis=-1, keepdims=True)
    return (x * lax.rsqrt(var + RMS_EPS) * g) * (1.0 + scale) + shift


def _rot_half(z):
    n = z.shape[-1]
    lane = lax.broadcasted_iota(jnp.int32, z.shape, z.ndim - 1)
    first = (lane % HEAD_DIM) < (HEAD_DIM // 2)
    return jnp.where(first, pltpu.roll(z, n - HEAD_DIM // 2, z.ndim - 1), pltpu.roll(z, HEAD_DIM // 2, z.ndim - 1))


def _cast_kernel(x_ref, o_ref):
    o_ref[...] = x_ref[...].astype(o_ref.dtype)


def _to_bf16(a):
    n, r, c = a.shape
    return pl.pallas_call(
        _cast_kernel,
        out_shape=jax.ShapeDtypeStruct(a.shape, BF16),
        grid=(n,),
        in_specs=[pl.BlockSpec((1, r, c), lambda i: (i, 0, 0))],
        out_specs=pl.BlockSpec((1, r, c), lambda i: (i, 0, 0)),
        compiler_params=_cparams(("arbitrary",)),
        name="cast_bf16",
    )(a)


def _mod_kernel(c_ref, w_ref, b_ref, o_ref):
    c = c_ref[...]
    o_ref[0] = _dot(_silu(c), w_ref[0], HI) + b_ref[0]


def _modulation(c, mod_w, mod_b):
    depth, d, n = mod_w.shape
    bsz = c.shape[0]
    tn = 1024
    return pl.pallas_call(
        _mod_kernel,
        out_shape=jax.ShapeDtypeStruct((depth, bsz, n), F32),
        grid=(depth, n // tn),
        in_specs=[
            pl.BlockSpec((bsz, d), lambda l, j: (0, 0)),
            pl.BlockSpec((1, d, tn), lambda l, j: (l, 0, j)),
            pl.BlockSpec((1, 1, tn), lambda l, j: (l, 0, j)),
        ],
        out_specs=pl.BlockSpec((1, bsz, tn), lambda l, j: (l, 0, j)),
        compiler_params=_cparams(("arbitrary", "arbitrary")),
        name="modulation",
    )(c, mod_w, mod_b.reshape(depth, 1, n))


def _proj_kernel(x_ref, g_ref, sc_ref, sh_ref, ws_ref, wn_ref, wg_ref, cos_ref, sin_ref,
                 zs_ref, qg_ref, kcvc_ref, ka_ref, kwa_ref, vsw_ref, gm_ref, h_ref):
    nb, tt, _ = x_ref.shape
    rows = nb * tt
    nlc = h_ref.shape[0]
    parts = []
    for b in range(nb):
        hb = _norm_mod(x_ref[b], g_ref[...], sc_ref[b], sh_ref[b])
        for c in range(nlc):
            h_ref[c, pl.ds(b, tt, stride=nb), :] = hb[:, c * LANES:(c + 1) * LANES]
        parts.append(hb.astype(BF16))
    h_bt = jnp.concatenate(parts, axis=0)
    h_tb = jnp.concatenate([h_ref[c] for c in range(nlc)], axis=1).astype(BF16)
    for n0 in range(0, ws_ref.shape[0], 2 * MXU_WIDTH):
        zs_ref[:, n0:n0 + 2 * MXU_WIDTH] = _dot_t(h_tb, ws_ref[n0:n0 + 2 * MXU_WIDTH, :])
    cos = jnp.concatenate([cos_ref[...]] * nb, axis=0)
    sin = jnp.concatenate([sin_ref[...]] * nb, axis=0)
    group = lax.broadcasted_iota(jnp.int32, (rows, MXU_WIDTH), 1) // HEAD_DIM

    def nsa_chunk(c, roped_groups):
        z = _dot_t(h_bt, wn_ref[c * MXU_WIDTH:(c + 1) * MXU_WIDTH, :])
        if len(roped_groups) == MXU_WIDTH // HEAD_DIM:
            return z * cos + _rot_half(z) * sin
        roped = group == roped_groups[0]
        for gi in roped_groups[1:]:
            roped = roped | (group == gi)
        return jnp.where(roped, z * cos + _rot_half(z) * sin, z)

    def put(ref, val):
        ref[...] = val.astype(ref.dtype).reshape(ref.shape)

    qg_ref[:, :, 0:MXU_WIDTH] = nsa_chunk(0, (0, 1, 2, 3)).reshape(nb, tt, MXU_WIDTH)
    z1 = nsa_chunk(1, (2,))
    qg_ref[:, :, MXU_WIDTH:MXU_WIDTH + LANES] = z1[:, 0:LANES].reshape(nb, tt, LANES)
    put(kcvc_ref, z1[:, LANES:2 * LANES])
    z2 = nsa_chunk(2, (0, 2))
    ks_vw, kw_vs = z2[:, 0:LANES], z2[:, LANES:2 * LANES]
    lane = lax.broadcasted_iota(jnp.int32, (rows, LANES), 1)
    key_lanes = lane < HEAD_DIM
    t = pl.program_id(0) * tt + lax.broadcasted_iota(jnp.int32, (rows, LANES), 0) % tt
    block_onehot = (lane == HEAD_DIM + t // L_SEL).astype(F32)
    put(ka_ref, jnp.where(key_lanes, ks_vw, block_onehot))
    put(kwa_ref, jnp.where(key_lanes, kw_vs, 0.0))
    put(vsw_ref, jnp.where(key_lanes, pltpu.roll(kw_vs, HEAD_DIM, 1), ks_vw))
    for n0 in range(0, wg_ref.shape[0], 2 * MXU_WIDTH):
        z = _dot_t(h_bt, wg_ref[n0:n0 + 2 * MXU_WIDTH, :])
        gm_ref[:, :, n0:n0 + 2 * MXU_WIDTH] = z.astype(gm_ref.dtype).reshape(nb, tt, 2 * MXU_WIDTH)


def _project(x, g, scale, shift, w_scan, w_nsa, w_gm, rope_tabs, *, tt):
    bsz, seq, d = x.shape
    ns, nn, ng = w_scan.shape[0], w_nsa.shape[0], w_gm.shape[0]
    full = lambda a: pl.BlockSpec(a.shape, lambda i: (0,) * a.ndim)
    tok = lambda n, dt: jax.ShapeDtypeStruct((bsz, seq, n), dt)
    tok_spec = lambda n: pl.BlockSpec((bsz, tt, n), lambda i: (0, i, 0))
    return pl.pallas_call(
        _proj_kernel,
        out_shape=(
            jax.ShapeDtypeStruct((seq * bsz, ns), F32),
            tok(MXU_WIDTH + LANES, F32), tok(LANES, F32), tok(LANES, BF16), tok(LANES, BF16), tok(LANES, BF16),
            tok(ng, BF16),
        ),
        grid=(seq // tt,),
        in_specs=[
            pl.BlockSpec((bsz, tt, d), lambda i: (0, i, 0)),
            pl.BlockSpec((1, d), lambda i: (0, 0)),
            pl.BlockSpec((bsz, 1, d), lambda i: (0, 0, 0)),
            pl.BlockSpec((bsz, 1, d), lambda i: (0, 0, 0)),
            full(w_scan), full(w_nsa), full(w_gm),
            pl.BlockSpec((tt, MXU_WIDTH), lambda i: (i, 0)),
            pl.BlockSpec((tt, MXU_WIDTH), lambda i: (i, 0)),
        ],
        out_specs=(
            pl.BlockSpec((tt * bsz, ns), lambda i: (i, 0)),
            tok_spec(MXU_WIDTH + LANES), tok_spec(LANES), tok_spec(LANES), tok_spec(LANES), tok_spec(LANES),
            tok_spec(ng),
        ),
        scratch_shapes=[pltpu.VMEM((d // LANES, tt * bsz, LANES), F32)],
        compiler_params=_cparams(("arbitrary",)),
        name="proj_in",
    )(x, g.reshape(1, d), scale, shift, w_scan, w_nsa, w_gm, *rope_tabs)


def _s5_param_kernel(lre_ref, lim_ref, ls_ref, bre_ref, bim_ref, are_ref, aim_ref, bbre_ref, bbim_ref):
    lre = lre_ref[...]
    lim = lim_ref[...]
    step = jnp.exp(ls_ref[...])
    mag = jnp.exp(lre * step)
    ar = mag * jnp.cos(lim * step)
    ai = mag * jnp.sin(lim * step)
    xr = ar - 1.0
    den = lre * lre + lim * lim
    cr = (xr * lre + ai * lim) / den
    ci = (ai * lre - xr * lim) / den
    bre = bre_ref[...]
    bim = bim_ref[...]
    are_ref[...] = ar
    aim_ref[...] = ai
    bbre_ref[...] = cr * bre - ci * bim
    bbim_ref[...] = cr * bim + ci * bre


def _s5_params(lam_re, lam_im, log_step, b_re, b_im):
    g, n = lam_re.shape
    c = b_re.shape[-1]
    rows = g * n
    col = lambda a: a.reshape(rows, 1)
    ls = jnp.broadcast_to(log_step[:, None], (g, n))
    out = pl.pallas_call(
        _s5_param_kernel,
        out_shape=(
            jax.ShapeDtypeStruct((rows, 1), F32),
            jax.ShapeDtypeStruct((rows, 1), F32),
            jax.ShapeDtypeStruct((rows, c), F32),
            jax.ShapeDtypeStruct((rows, c), F32),
        ),
        name="s5_params",
    )(col(lam_re), col(lam_im), col(ls), b_re.reshape(rows, c), b_im.reshape(rows, c))
    return out


def _scan_kernel(z_ref, s5b_ref, s5c_ref, are_ref, aim_ref, d_ref, wglu_ref,
                 cw_ref, cb_ref, wa_ref, ba_ref, wx_ref, bx_ref, lam_ref, scw_ref,
                 o_ref,
                 bu_ref, hre_ref, him_ref, xe_ref, hl_ref, pe_ref, ab_ref, bb_ref, *, nb, steps):
    c = pl.program_id(0)
    rows = nb * steps
    w = d_ref.shape[1]
    ns = are_ref.shape[1]

    @pl.when(c == 0)
    def _():
        hre_ref[...] = jnp.zeros_like(hre_ref)
        him_ref[...] = jnp.zeros_like(him_ref)
        hl_ref[...] = jnp.zeros_like(hl_ref)
        xe_ref[0:(LRU_CONV - 1) * nb, :] = jnp.zeros(((LRU_CONV - 1) * nb, w), F32)
        pe_ref[0:(SC_CONV - 1) * nb, :] = jnp.zeros(((SC_CONV - 1) * nb, w), F32)

    u = z_ref[:, 0:w]
    bu_ref[...] = _dot(u.astype(BF16), s5b_ref[...])
    a_re = jnp.broadcast_to(are_ref[...], (nb, ns))
    a_im = jnp.broadcast_to(aim_ref[...], (nb, ns))

    def s5_step(t, carry):
        h_re, h_im = carry
        r0 = pl.multiple_of(t * nb, nb)
        b_re = bu_ref[pl.ds(r0, nb), 0:ns]
        b_im = bu_ref[pl.ds(r0, nb), ns:2 * ns]
        n_re = a_re * h_re - a_im * h_im + b_re
        n_im = a_re * h_im + a_im * h_re + b_im
        bu_ref[pl.ds(r0, nb), 0:ns] = n_re
        bu_ref[pl.ds(r0, nb), ns:2 * ns] = n_im
        return n_re, n_im

    h_re, h_im = lax.fori_loop(0, steps, s5_step, (hre_ref[...], him_ref[...]))
    hre_ref[...] = h_re
    him_ref[...] = h_im
    y = _dot(bu_ref[...].astype(BF16), s5c_ref[...]) + d_ref[...] * u
    zg = _gelu(y)
    def put(branch, val):
        for k in range(w // LANES):
            o_ref[branch * (w // LANES) + k] = val[:, k * LANES:(k + 1) * LANES]

    put(0, zg * _sigmoid(_dot(zg.astype(BF16), wglu_ref[...])))

    nt = (LRU_CONV - 1) * nb
    x = z_ref[:, w:2 * w]
    xe_ref[nt:nt + rows, :] = x
    xc = cb_ref[...] + cw_ref[LRU_CONV - 1:LRU_CONV, :] * x
    for k in range(LRU_CONV - 1):
        xc = xc + cw_ref[k:k + 1, :] * xe_ref[k * nb:k * nb + rows, :]
    xe_ref[0:nt, :] = xe_ref[rows:rows + nt, :]
    xcb = xc.astype(BF16)
    r = _sigmoid(_dot(xcb, wa_ref[...]) + ba_ref[...])
    gi = _sigmoid(_dot(xcb, wx_ref[...]) + bx_ref[...])
    nl = -lam_ref[...]
    softplus = jnp.maximum(nl, 0.0) + jnp.log(1.0 + jnp.exp(-jnp.abs(nl)))
    log_a = (-LRU_C) * r * softplus
    a = jnp.exp(log_a)
    ab_ref[...] = a
    bb_ref[...] = jnp.sqrt(1.0 - a * a) * (gi * xc)

    def lru_step(t, h):
        r0 = pl.multiple_of(t * nb, nb)
        h = ab_ref[pl.ds(r0, nb), :] * h + bb_ref[pl.ds(r0, nb), :]
        bb_ref[pl.ds(r0, nb), :] = h
        return h

    hl_ref[...] = lax.fori_loop(0, steps, lru_step, hl_ref[...])
    put(1, bb_ref[...] * _gelu(z_ref[:, 2 * w:3 * w]))

    npt = (SC_CONV - 1) * nb
    p = z_ref[:, 4 * w:5 * w] * z_ref[:, 5 * w:6 * w]
    pe_ref[npt:npt + rows, :] = p
    cv = scw_ref[SC_CONV - 1:SC_CONV, :] * p
    for k in range(SC_CONV - 1):
        cv = cv + scw_ref[k:k + 1, :] * pe_ref[k * nb:k * nb + rows, :]
    pe_ref[0:npt, :] = pe_ref[rows:rows + npt, :]
    put(2, z_ref[:, 3 * w:4 * w] * cv)


def _scan_mixers(z_tm, nb, s5b, s5c, a_re, a_im, d_skip, w_glu, conv_w, conv_b, wa, ba, wx, bx, lam, sc_w, *, steps):
    total, zc = z_tm.shape
    rows = nb * steps
    w = zc // 6
    ns = a_re.shape[1]
    full = lambda a: pl.BlockSpec(a.shape, lambda c: (0,) * a.ndim)
    params = [s5b, s5c, a_re, a_im, d_skip, w_glu, conv_w, conv_b, wa, ba, wx, bx, lam, sc_w]
    return pl.pallas_call(
        functools.partial(_scan_kernel, nb=nb, steps=steps),
        out_shape=jax.ShapeDtypeStruct((3 * w // LANES, total, LANES), F32),
        grid=(total // rows,),
        in_specs=[pl.BlockSpec((rows, zc), lambda c: (c, 0))] + [full(a) for a in params],
        out_specs=pl.BlockSpec((3 * w // LANES, rows, LANES), lambda c: (0, c, 0)),
        scratch_shapes=[
            pltpu.VMEM((rows, 2 * ns), F32),
            pltpu.VMEM((nb, ns), F32),
            pltpu.VMEM((nb, ns), F32),
            pltpu.VMEM((rows + (LRU_CONV - 1) * nb, w), F32),
            pltpu.VMEM((nb, w), F32),
            pltpu.VMEM((rows + (SC_CONV - 1) * nb, w), F32),
            pltpu.VMEM((rows, w), F32),
            pltpu.VMEM((rows, w), F32),
        ],
        compiler_params=_cparams(("arbitrary",)),
        name="scan_mixers",
    )(z_tm, *params)


def _compress_kernel(kv_ref, pek_ref, pev_ref, wk1_ref, wk1p_ref, wk2_ref, wv1_ref, wv1p_ref, wv2_ref, ko_ref, vo_ref):
    nblk = ko_ref.shape[1]
    acc = [None] * 4
    for l in range(CMP_STRIDE):
        xl = kv_ref[0, pl.ds(l, nblk, stride=CMP_STRIDE), :]
        terms = (_dot(xl, wk1p_ref[l], HI), _dot(xl, wk1p_ref[CMP_STRIDE + l], HI),
                 _dot(xl, wv1p_ref[l], HI), _dot(xl, wv1p_ref[CMP_STRIDE + l], HI))
        acc = [t if a is None else a + t for a, t in zip(acc, terms)]

    def phi(first, second, pe, w1_ref, w2_ref):
        hid = first + pltpu.roll(second, nblk - 1, 0) + _dot(pe, w1_ref[...], HI)
        return _dot(_gelu(hid), w2_ref[...], HI)

    ko_ref[0] = phi(acc[0], acc[1], pek_ref[...], wk1_ref, wk2_ref)
    vo_ref[0] = phi(acc[2], acc[3], pev_ref[...], wv1_ref, wv2_ref)


def _compress(kcvc, pe_k, pe_v, wk1, wk2, wv1, wv2):
    bsz, seq, kvw = kcvc.shape
    nblk = seq // CMP_STRIDE
    hd = wk2.shape[1]
    hidden = wk1.shape[1]
    full = lambda a: pl.BlockSpec(a.shape, lambda b: (0,) * a.ndim)
    zeros = jnp.zeros((L_CMP, HEAD_DIM, hidden), wk1.dtype)
    wk1p = jnp.concatenate([wk1.reshape(L_CMP, HEAD_DIM, hidden), zeros], axis=1)
    wv1p = jnp.concatenate([zeros, wv1.reshape(L_CMP, HEAD_DIM, hidden)], axis=1)
    params = [pe_k.reshape(1, -1), pe_v.reshape(1, -1), wk1, wk1p, wk2, wv1, wv1p, wv2]
    oblk = pl.BlockSpec((1, nblk, hd), lambda b: (b, 0, 0))
    return pl.pallas_call(
        _compress_kernel,
        out_shape=(jax.ShapeDtypeStruct((bsz, nblk, hd), F32),) * 2,
        grid=(bsz,),
        in_specs=[pl.BlockSpec((1, seq, kvw), lambda b: (b, 0, 0))] + [full(a) for a in params],
        out_specs=(oblk, oblk),
        compiler_params=_cparams(("arbitrary",)),
        name="nsa_compress",
    )(kcvc, *params)


def _split_bf16(a):
    hi = a.astype(BF16)
    return hi, (a - hi.astype(F32)).astype(BF16)


def _cmp_kernel(q_ref, kc_ref, vct_ref, ovt_ref, ocmp_ref, qa_ref, *, heads, tq):
    i = pl.program_id(1)
    q_t = (q_ref[0] * HEAD_DIM ** -0.5).T
    kc_hi, kc_lo = _split_bf16(kc_ref[0])
    vct = vct_ref[0]
    nblk = kc_hi.shape[0]
    t = i * tq + lax.broadcasted_iota(jnp.int32, (nblk, tq), 1)
    n_id = lax.broadcasted_iota(jnp.int32, (nblk, tq), 0)
    visible = (n_id * CMP_STRIDE + (L_CMP - 1)) <= t
    psum = jnp.zeros((nblk, tq), F32)
    for h in range(heads):
        q_hi, q_lo = _split_bf16(q_t[h * HEAD_DIM:(h + 1) * HEAD_DIM, :])
        s = _dot(kc_hi, q_hi) + (_dot(kc_hi, q_lo) + _dot(kc_lo, q_hi))
        s = jnp.where(visible, s, NEG_INF)
        m = jnp.max(s, axis=0, keepdims=True)
        e = jnp.where(visible, jnp.exp(s - m), 0.0)
        den = jnp.sum(e, axis=0, keepdims=True)
        p = e / jnp.where(den > 0.0, den, 1.0)
        psum = psum + p
        ocmp_ref[0, h * HEAD_DIM:(h + 1) * HEAD_DIM, :] = _dot(vct, p.astype(BF16))

    p_hi, p_lo = _split_bf16(psum)
    imp_t = _dot(ovt_ref[...], p_hi) + _dot(ovt_ref[...], p_lo)
    nslot = ovt_ref.shape[0]
    tt = i * tq + lax.broadcasted_iota(jnp.int32, (nslot, tq), 1)
    j = lax.broadcasted_iota(jnp.int32, (nslot, tq), 0)
    cur = tt // L_SEL
    forced = (j == 0) | (j == cur) | (j == cur - 1)
    future = (j * L_SEL) > tt
    imp_t = jnp.where(future, -1.0, jnp.where(forced, FORCED_SCORE, imp_t))
    nreal = qa_ref.shape[2] - HEAD_DIM
    cand = imp_t[0:nreal, :]
    jrow = lax.broadcasted_iota(jnp.int32, (nreal, tq), 0)
    bias_t = jnp.full((nreal, tq), NEG_INF, F32)
    for _ in range(TOP_N):
        best = jnp.max(cand, axis=0, keepdims=True)
        first = jnp.min(jnp.where(cand == best, jrow, nreal), axis=0, keepdims=True)
        hit = jrow == first
        bias_t = jnp.where(hit, 0.0, bias_t)
        cand = jnp.where(hit, -2.0, cand)
    for h in range(heads):
        qa_ref[0, h] = jnp.concatenate([q_t[h * HEAD_DIM:(h + 1) * HEAD_DIM, :] * LOG2E, bias_t], axis=0).astype(BF16)


def _cmp_select(q, k_cmp, v_cmp_t, overlap_t, *, heads, tq):
    bsz, seq, _ = q.shape
    qd = heads * HEAD_DIM
    nblk = k_cmp.shape[1]
    nsel = LANES - HEAD_DIM
    return pl.pallas_call(
        functools.partial(_cmp_kernel, heads=heads, tq=tq),
        out_shape=(
            jax.ShapeDtypeStruct((bsz, qd, seq), F32),
            jax.ShapeDtypeStruct((bsz, heads, HEAD_DIM + nsel, seq), BF16),
        ),
        grid=(bsz, seq // tq),
        in_specs=[
            pl.BlockSpec((1, tq, qd), lambda b, i: (b, i, 0)),
            pl.BlockSpec((1, nblk, HEAD_DIM), lambda b, i: (b, 0, 0)),
            pl.BlockSpec((1, HEAD_DIM, nblk), lambda b, i: (b, 0, 0)),
            pl.BlockSpec(overlap_t.shape, lambda b, i: (0, 0)),
        ],
        out_specs=(
            pl.BlockSpec((1, qd, tq), lambda b, i: (b, 0, i)),
            pl.BlockSpec((1, heads, HEAD_DIM + nsel, tq), lambda b, i: (b, 0, 0, i)),
        ),
        compiler_params=_cparams(("arbitrary", "arbitrary")),
        name="nsa_cmp_select",
    )(q, k_cmp, v_cmp_t, overlap_t)


def _attend_kernel(qa_ref, ka_ref, vst_ref, kwa_ref, vwt_ref, gn_ref, ocmp_ref, o_ref,
                   m_ref, acc_ref, s0_ref, s1_ref, *, heads, tq, tk, nwin):
    i = pl.program_id(1)
    cols = heads * tq
    qat = jnp.concatenate([qa_ref[0, h] for h in range(heads)], axis=1)
    t_row = i * tq + (lax.broadcasted_iota(jnp.int32, (1, cols), 1) % tq)

    m_ref[...] = jnp.full(m_ref.shape, NEG_INF, F32)
    acc_ref[...] = jnp.zeros(acc_ref.shape, F32)
    key_id = lax.broadcasted_iota(jnp.int32, (tk, cols), 0)
    n_full = (i * tq + 1) // tk
    s0_ref[...] = _dot(ka_ref[0, 0], qat)

    st = jnp.maximum(i + 1 - nwin, 0)
    kw = kwa_ref[0, pl.ds(st, nwin)].reshape(nwin * tq, kwa_ref.shape[3])
    s = _dot(kw, qat)
    dist = t_row - (st * tq + lax.broadcasted_iota(jnp.int32, (nwin * tq, cols), 0))
    s = jnp.where((dist >= 0) & (dist < WINDOW), s, NEG_INF)
    m = jnp.max(s, axis=0, keepdims=True)
    pb = jnp.exp2(s - m).astype(BF16)
    ow = _dot(vwt_ref[0, st], pb[0:tq, :])
    for j in range(1, nwin):
        ow = ow + _dot(vwt_ref[0, st + j], pb[j * tq:(j + 1) * tq, :])
    owin = ow[0:HEAD_DIM, :] / ow[HEAD_DIM:HEAD_DIM + 1, :]

    def softmax_tile(kt, s, masked):
        if masked:
            s = jnp.where(kt * tk + key_id <= t_row, s, NEG_INF)
        m_old = m_ref[...]
        m_new = jnp.maximum(m_old, jnp.max(s, axis=0, keepdims=True))
        p = jnp.exp2(s - m_new).astype(BF16)
        acc_ref[...] = jnp.exp2(m_old - m_new) * acc_ref[...] + _dot(vst_ref[0, kt], p)
        m_ref[...] = m_new

    def pair_body(j, carry):
        s1_ref[...] = _dot(ka_ref[0, 2 * j + 1], qat)
        softmax_tile(2 * j, s0_ref[...], False)
        s0_ref[...] = _dot(ka_ref[0, 2 * j + 2], qat)
        softmax_tile(2 * j + 1, s1_ref[...], False)
        return carry

    n_pair = n_full // 2
    lax.fori_loop(0, n_pair, pair_body, 0)
    s1_ref[...] = _dot(ka_ref[0, 2 * n_pair + 1], qat)
    softmax_tile(2 * n_pair, s0_ref[...], True)
    softmax_tile(2 * n_pair + 1, s1_ref[...], True)

    osel = acc_ref[0:HEAD_DIM, :] / acc_ref[HEAD_DIM:HEAD_DIM + 1, :]

    g_t = _sigmoid(gn_ref[0]).T
    outs = []
    for h in range(heads):
        cs = slice(h * tq, (h + 1) * tq)
        outs.append(g_t[3 * h:3 * h + 1, :] * ocmp_ref[0, h * HEAD_DIM:(h + 1) * HEAD_DIM, :]
                    + g_t[3 * h + 1:3 * h + 2, :] * osel[:, cs]
                    + g_t[3 * h + 2:3 * h + 3, :] * owin[:, cs])
    o_ref[0] = jnp.concatenate(outs, axis=0).T


def _attend(qa, ka, vst, kwa, vwt, z_nsa, gn_block, o_cmp_t, *, tq, tk):
    bsz, heads, ka_w, seq = qa.shape
    qd = heads * HEAD_DIM
    nwin = WINDOW // tq + 1
    return pl.pallas_call(
        functools.partial(_attend_kernel, heads=heads, tq=tq, tk=tk, nwin=nwin),
        out_shape=jax.ShapeDtypeStruct((bsz, seq, qd), F32),
        grid=(bsz, seq // tq),
        in_specs=[
            pl.BlockSpec((1, heads, ka_w, tq), lambda b, i: (b, 0, 0, i)),
            pl.BlockSpec((1,) + ka.shape[1:], lambda b, i: (b, 0, 0, 0)),
            pl.BlockSpec((1,) + vst.shape[1:], lambda b, i: (b, 0, 0, 0)),
            pl.BlockSpec((1,) + kwa.shape[1:], lambda b, i: (b, 0, 0, 0)),
            pl.BlockSpec((1,) + vwt.shape[1:], lambda b, i: (b, 0, 0, 0)),
            pl.BlockSpec((1, tq, LANES), lambda b, i: (b, i, gn_block)),
            pl.BlockSpec((1, qd, tq), lambda b, i: (b, 0, i)),
        ],
        out_specs=pl.BlockSpec((1, tq, qd), lambda b, i: (b, i, 0)),
        scratch_shapes=[
            pltpu.VMEM((1, heads * tq), F32),
            pltpu.VMEM((vst.shape[2], heads * tq), F32),
            pltpu.VMEM((tk, heads * tq), F32),
            pltpu.VMEM((tk, heads * tq), F32),
        ],
        compiler_params=_cparams(("arbitrary", "arbitrary")),
        name="nsa_attend",
    )(qa, ka, vst, kwa, vwt, z_nsa, o_cmp_t)


def _merge_kernel(ys_ref, yb_ref, gm_ref, x_ref, wb_ref, wo_ref, gate_ref, o_ref, m_ref):
    nb, tt, w = yb_ref.shape
    d = x_ref.shape[2]
    rows = nb * tt
    def slab(c):
        return jnp.concatenate([ys_ref[c, pl.ds(b, tt, stride=nb), :] for b in range(nb)], axis=0)

    ys = jnp.concatenate([slab(c) for c in range(ys_ref.shape[0])], axis=1)
    branches = (ys[:, 0:w], yb_ref[...].reshape(rows, w), ys[:, w:2 * w], ys[:, 2 * w:3 * w])
    ybs = [y.astype(BF16) for y in branches]
    for n0 in range(0, d, MXU_WIDTH):
        plain = None
        gated = None
        for mi, y in enumerate(ybs):
            p = _dot(y, wb_ref[mi, :, n0:n0 + MXU_WIDTH])
            gm = gm_ref[:, :, mi * d + n0:mi * d + n0 + MXU_WIDTH].reshape(rows, MXU_WIDTH)
            t = jnp.tanh(gm.astype(F32)) * p
            plain = p if plain is None else plain + p
            gated = t if gated is None else gated + t
        m_ref[:, n0:n0 + MXU_WIDTH] = (0.5 * (plain + gated)).astype(BF16)
    mix = _dot(m_ref[...], wo_ref[...]).reshape(nb, tt, d)
    o_ref[...] = x_ref[...] + gate_ref[...] * mix


def _merge(y_scan_tm, y_b, g_merge, x, w_branch, w_out, gate, *, tt):
    bsz, seq, d = x.shape
    w = y_b.shape[2]
    tile = lambda n: pl.BlockSpec((bsz, tt, n), lambda i: (0, i, 0))
    return pl.pallas_call(
        _merge_kernel,
        out_shape=jax.ShapeDtypeStruct((bsz, seq, d), F32),
        grid=(seq // tt,),
        in_specs=[
            pl.BlockSpec((3 * w // LANES, tt * bsz, LANES), lambda i: (0, i, 0)),
            tile(w),
            tile(N_BRANCH * d),
            tile(d),
            pl.BlockSpec(w_branch.shape, lambda i: (0, 0, 0)),
            pl.BlockSpec(w_out.shape, lambda i: (0, 0)),
            pl.BlockSpec((bsz, 1, d), lambda i: (0, 0, 0)),
        ],
        out_specs=tile(d),
        scratch_shapes=[pltpu.VMEM((tt * bsz, d), BF16)],
        compiler_params=_cparams(("arbitrary",)),
        name="merge_out",
    )(y_scan_tm, y_b, g_merge, x, w_branch, w_out, gate)


def _final_norm(out, fg_ref):
    var = jnp.mean(out * out, axis=-1, keepdims=True)
    return out * lax.rsqrt(var + RMS_EPS) * fg_ref[...]


def _ffn_kernel(x_ref, g_ref, sc_ref, sh_ref, gate_ref, wg_ref, wu_ref, wd_ref, *rest, final, fc):
    if final:
        fg_ref, o_ref, a_ref = rest
    else:
        o_ref, a_ref = rest
    hb = _norm_mod(x_ref[0], g_ref[...], sc_ref[0], sh_ref[0]).astype(BF16)
    for f0 in range(0, a_ref.shape[1], fc):
        a_ref[:, f0:f0 + fc] = (_silu(_dot(hb, wg_ref[:, f0:f0 + fc])) * _dot(hb, wu_ref[:, f0:f0 + fc])).astype(BF16)
    out = x_ref[0] + gate_ref[0] * _dot(a_ref[...], wd_ref[...])
    o_ref[0] = _final_norm(out, fg_ref) if final else out


def _ffn(x, g, scale, shift, gate, weights, *, final_g, tm):
    bsz, seq, d = x.shape
    row = lambda: pl.BlockSpec((1, 1, d), lambda b, i: (b, 0, 0))
    wg, wu, wd = weights
    ff = wg.shape[1]
    in_specs = [
        pl.BlockSpec((1, tm, d), lambda b, i: (b, i, 0)),
        pl.BlockSpec((1, d), lambda b, i: (0, 0)),
        row(), row(), row(),
        pl.BlockSpec((d, ff), lambda b, i: (0, 0)),
        pl.BlockSpec((d, ff), lambda b, i: (0, 0)),
        pl.BlockSpec((ff, d), lambda b, i: (0, 0)),
    ]
    args = [x, g.reshape(1, d), scale, shift, gate, wg, wu, wd]
    final = final_g is not None
    if final:
        in_specs.append(pl.BlockSpec((1, d), lambda b, i: (0, 0)))
        args.append(final_g.reshape(1, d))
    return pl.pallas_call(
        functools.partial(_ffn_kernel, final=final, fc=MXU_WIDTH),
        out_shape=jax.ShapeDtypeStruct((bsz, seq, d), F32),
        grid=(bsz, seq // tm),
        in_specs=in_specs,
        out_specs=pl.BlockSpec((1, tm, d), lambda b, i: (b, i, 0)),
        scratch_shapes=[pltpu.VMEM((tm, ff), BF16)],
        compiler_params=_cparams(("arbitrary", "arbitrary")),
        name="dense_ffn",
    )(*args)


def _route_kernel(x_ref, g_ref, sc_ref, sh_ref, rw_ref, rb_ref, h_ref, r_ref):
    h = _norm_mod(x_ref[0], g_ref[...], sc_ref[0], sh_ref[0])
    h_hi, h_lo = _split_bf16(h)
    h_ref[0] = h_hi
    w_hi, w_lo = _split_bf16(rw_ref[...])
    logits = _dot(h_hi, w_hi) + (_dot(h_hi, w_lo) + _dot(h_lo, w_hi)) + rb_ref[...]
    lane = lax.broadcasted_iota(jnp.int32, logits.shape, 1)
    big = logits.shape[1]
    v1 = jnp.max(logits, axis=-1, keepdims=True)
    i1 = jnp.min(jnp.where(logits == v1, lane, big), axis=-1, keepdims=True)
    rest_l = jnp.where(lane == i1, NEG_INF * 4.0, logits)
    v2 = jnp.max(rest_l, axis=-1, keepdims=True)
    i2 = jnp.min(jnp.where(rest_l == v2, lane, big), axis=-1, keepdims=True)
    e2 = jnp.exp(v2 - v1)
    w1 = 1.0 / (1.0 + e2)
    w2 = e2 / (1.0 + e2)
    r_ref[0] = jnp.where(lane == 0, i1.astype(F32), jnp.where(lane == 1, i2.astype(F32), jnp.where(lane == 2, w1, w2)))


def _route(x, g, scale, shift, rw, rb, *, tm):
    bsz, seq, d = x.shape
    row = lambda: pl.BlockSpec((1, 1, d), lambda b, i: (b, 0, 0))
    return pl.pallas_call(
        _route_kernel,
        out_shape=(jax.ShapeDtypeStruct((bsz, seq, d), BF16), jax.ShapeDtypeStruct((bsz, seq, LANES), F32)),
        grid=(bsz, seq // tm),
        in_specs=[
            pl.BlockSpec((1, tm, d), lambda b, i: (b, i, 0)),
            pl.BlockSpec((1, d), lambda b, i: (0, 0)),
            row(), row(),
            pl.BlockSpec(rw.shape, lambda b, i: (0, 0)),
            pl.BlockSpec(rb.shape, lambda b, i: (0, 0)),
        ],
        out_specs=(pl.BlockSpec((1, tm, d), lambda b, i: (b, i, 0)), pl.BlockSpec((1, tm, LANES), lambda b, i: (b, i, 0))),
        compiler_params=_cparams(("arbitrary", "arbitrary")),
        name="moe_route",
    )(x, g.reshape(1, d), scale, shift, rw, rb)


def _gffn_kernel(te_ref, nu_ref, x_ref, wg_ref, wu_ref, wd_ref, o_ref):
    i = pl.program_id(0)

    @pl.when(i < nu_ref[0])
    def _():
        hb = x_ref[...]
        a = (_silu(_dot(hb, wg_ref[0])) * _dot(hb, wu_ref[0])).astype(BF16)
        o_ref[...] = _dot(a, wd_ref[0]).astype(o_ref.dtype)

    @pl.when(i >= nu_ref[0])
    def _():
        o_ref[...] = jnp.zeros_like(o_ref)


def _grouped_ffn(xs, tile_expert, n_used, wg, wu, wd, *, tmg):
    rows, d = xs.shape
    _, _, fe = wg.shape
    return pl.pallas_call(
        _gffn_kernel,
        out_shape=jax.ShapeDtypeStruct((rows, d), BF16),
        grid_spec=pltpu.PrefetchScalarGridSpec(
            num_scalar_prefetch=2,
            grid=(rows // tmg,),
            in_specs=[
                pl.BlockSpec((tmg, d), lambda i, te, nu: (i, 0)),
                pl.BlockSpec((1, d, fe), lambda i, te, nu: (te[i], 0, 0)),
                pl.BlockSpec((1, d, fe), lambda i, te, nu: (te[i], 0, 0)),
                pl.BlockSpec((1, fe, d), lambda i, te, nu: (te[i], 0, 0)),
            ],
            out_specs=pl.BlockSpec((tmg, d), lambda i, te, nu: (i, 0)),
        ),
        compiler_params=_cparams(("arbitrary",)),
        name="moe_grouped_ffn",
    )(tile_expert, n_used, xs, wg, wu, wd)


def _combine_kernel(x_ref, gate_ref, y_ref, y2_ref, r_ref, *rest, final):
    if final:
        fg_ref, o_ref = rest
    else:
        (o_ref,) = rest
    r = r_ref[0]
    f = r[:, 2:3] * y_ref[0].astype(F32) + r[:, 3:4] * y2_ref[0].astype(F32)
    out = x_ref[0] + gate_ref[0] * f
    o_ref[0] = _final_norm(out, fg_ref) if final else out


def _combine(x, gate, y1, y2, route, *, final_g, tm):
    bsz, seq, d = x.shape
    tile = lambda: pl.BlockSpec((1, tm, d), lambda b, i: (b, i, 0))
    in_specs = [
        tile(),
        pl.BlockSpec((1, 1, d), lambda b, i: (b, 0, 0)),
        tile(), tile(),
        pl.BlockSpec((1, tm, LANES), lambda b, i: (b, i, 0)),
    ]
    args = [x, gate, y1, y2, route]
    final = final_g is not None
    if final:
        in_specs.append(pl.BlockSpec((1, d), lambda b, i: (0, 0)))
        args.append(final_g.reshape(1, d))
    return pl.pallas_call(
        functools.partial(_combine_kernel, final=final),
        out_shape=jax.ShapeDtypeStruct((bsz, seq, d), F32),
        grid=(bsz, seq // tm),
        in_specs=in_specs,
        out_specs=pl.BlockSpec((1, tm, d), lambda b, i: (b, i, 0)),
        compiler_params=_cparams(("arbitrary", "arbitrary")),
        name="moe_combine",
    )(*args)


def _src_kernel(te_ref, k0_ref, jlo_ref, jhi_ref, c_ref, o_ref, cnt_ref, *, nblk, tb, last):
    i = pl.program_id(0)
    tmg = o_ref.shape[0]
    base = te_ref[i] * nblk
    kcol = k0_ref[i] + lax.broadcasted_iota(jnp.int32, (tmg, 1), 0)
    cnt_ref[...] = jnp.zeros_like(cnt_ref)

    def blk(j, carry):
        hit = (c_ref[base + j] <= kcol).astype(jnp.int32)
        part = hit[:, 0:LANES]
        for q in range(1, tb // LANES):
            part = part + hit[:, q * LANES:(q + 1) * LANES]
        cnt_ref[...] += part
        return carry

    lax.fori_loop(jlo_ref[i], jhi_ref[i], blk, 0)
    total = jnp.sum(cnt_ref[...], axis=1, keepdims=True) + jlo_ref[i] * tb
    o_ref[...] = jnp.minimum(total, last)


def _moe(x, g, scale, shift, gate, rw, rb, wg, wu, wd, *, final_g, tm, tmg):
    bsz, seq, d = x.shape
    ne = wg.shape[0]
    tokens = bsz * seq
    tb = 512
    nblk = tokens // tb
    h, route = _route(x, g, scale, shift, rw, rb, tm=tm)
    ar = jnp.arange(ne, dtype=jnp.int32)[:, None]
    m1 = route[:, :, 0].astype(jnp.int32).reshape(1, tokens) == ar
    m2 = route[:, :, 1].astype(jnp.int32).reshape(1, tokens) == ar
    count = jnp.cumsum((m1 | m2).astype(jnp.int32), axis=1)
    padded = ((count[:, -1] + tmg - 1) // tmg) * tmg
    ends = jnp.cumsum(padded)
    offs = ends - padded
    row_of = count - 1 + offs[:, None]
    dest1 = jnp.sum(jnp.where(m1, row_of, 0), axis=0)
    dest2 = jnp.sum(jnp.where(m2, row_of, 0), axis=0)
    rows = tokens * 2 + ne * tmg
    ntiles = rows // tmg
    tile_start = jnp.arange(ntiles, dtype=jnp.int32) * tmg
    tile_expert = jnp.minimum(jnp.sum((tile_start[:, None] >= ends[None, :]).astype(jnp.int32), axis=1), ne - 1)
    n_used = (ends[-1] // tmg).astype(jnp.int32).reshape(1)
    k0 = tile_start - offs[tile_expert]
    cend = count[:, tb - 1::tb]
    cstart = jnp.concatenate([jnp.zeros((ne, 1), jnp.int32), cend[:, :-1]], axis=1)
    jlo = jnp.sum((cend[tile_expert] <= k0[:, None]).astype(jnp.int32), axis=1)
    jhi = jnp.sum((cstart[tile_expert] <= (k0 + tmg - 1)[:, None]).astype(jnp.int32), axis=1)
    src = pl.pallas_call(
        functools.partial(_src_kernel, nblk=nblk, tb=tb, last=tokens - 1),
        out_shape=jax.ShapeDtypeStruct((rows, 1), jnp.int32),
        grid_spec=pltpu.PrefetchScalarGridSpec(
            num_scalar_prefetch=4,
            grid=(ntiles,),
            in_specs=[pl.BlockSpec((ne * nblk, 1, tb), lambda i, *_: (0, 0, 0))],
            out_specs=pl.BlockSpec((tmg, 1), lambda i, *_: (i, 0)),
            scratch_shapes=[pltpu.VMEM((tmg, LANES), jnp.int32)],
        ),
        compiler_params=_cparams(("arbitrary",)),
        name="moe_sources",
    )(tile_expert, k0, jlo, jhi, count.reshape(ne * nblk, 1, tb))
    rows_of = lambda a, idx: a.at[idx].get(mode="promise_in_bounds")
    xs = rows_of(h.reshape(tokens, d), src[:, 0])
    ys = _grouped_ffn(xs, tile_expert, n_used, wg, wu, wd, tmg=tmg)
    y1 = rows_of(ys, dest1).reshape(bsz, seq, d)
    y2 = rows_of(ys, dest2).reshape(bsz, seq, d)
    return _combine(x, gate, y1, y2, route, final_g=final_g, tm=tm)


def _block_diag(blocks):
    n, r, c = blocks.shape
    eye = jnp.eye(n, dtype=blocks.dtype)
    return (eye[:, None, :, None] * blocks[:, :, None, :]).reshape(n * r, n * c)


def _rope_tables(seq):
    inv = ROPE_THETA ** (-jnp.arange(0, HEAD_DIM, 2, dtype=F32) / HEAD_DIM)
    ang = jnp.arange(seq, dtype=F32)[:, None] * inv[None, :]
    cos, sin = jnp.cos(ang), jnp.sin(ang)
    reps = MXU_WIDTH // HEAD_DIM
    return jnp.concatenate([cos, cos] * reps, axis=1), jnp.concatenate([-sin, sin] * reps, axis=1)


def kernel(x, c, mod_w, mod_b, norm_mix_g, norm_ffn_g, w_in, s5_lambda_re, s5_lambda_im, s5_log_step, s5_b_re, s5_b_im, s5_c_re, s5_c_im, s5_d, s5_w_glu, nsa_pe_k, nsa_pe_v, nsa_cmp_k_w1, nsa_cmp_k_w2, nsa_cmp_v_w1, nsa_cmp_v_w2, lru_conv_w, lru_conv_b, lru_w_a, lru_b_a, lru_w_x, lru_b_x, lru_lambda, sc_conv_w, w_branch, w_out, ffn_w_gate, ffn_w_up, ffn_w_down, moe_router_w, moe_router_b, moe_w_gate, moe_w_up, moe_w_down, final_norm_g):
    bsz, seq, d = x.shape
    depth = mod_w.shape[0]
    w = d // N_BRANCH
    heads = w // HEAD_DIM
    groups = w // S5_GROUP
    assert bsz == SUBLANES, "time-major scan blocks hold one timestep per sublane tile"
    tm = min(512, seq)
    tq = 128
    steps = min(64, seq)
    nsel = seq // L_SEL
    assert TOP_N <= nsel <= LANES - HEAD_DIM and seq % tm == 0 and seq >= WINDOW + tq

    mod = _modulation(c, mod_w, mod_b)
    mod = mod.reshape(depth, bsz, 6, 1, d)

    sizes = (w, w, HEAD_DIM, HEAD_DIM, HEAD_DIM, HEAD_DIM, HEAD_DIM, HEAD_DIM, heads * 3, w, w, w, w, w, N_BRANCH * d)
    offs = np.concatenate([[0], np.cumsum(sizes)])
    col = lambda wmat_t, k: wmat_t[offs[k]:offs[k + 1], :]

    cos_t, sin_t = _rope_tables(seq)
    n_cmp = (seq - L_CMP) // CMP_STRIDE + 1
    nblk = seq // CMP_STRIDE
    cmp_start = np.arange(nblk) * CMP_STRIDE
    sel_start = np.arange(nsel) * L_SEL
    ovl = np.clip(np.minimum(cmp_start[:, None] + L_CMP, sel_start[None, :] + L_SEL)
                  - np.maximum(cmp_start[:, None], sel_start[None, :]), 0, None) / L_CMP
    ovl[n_cmp:] = 0.0
    overlap_t = jnp.asarray(np.pad(ovl, ((0, 0), (0, LANES - nsel))).T, BF16)

    w_in_t = jnp.swapaxes(w_in, 1, 2)

    for li in range(depth):
        sh1, sc1, gt1, sh2, sc2, gt2 = (mod[li, :, k] for k in range(6))
        wl = w_in_t[li]
        w_scan = jnp.concatenate([col(wl, 0), col(wl, 9), col(wl, 10), col(wl, 11), col(wl, 12), col(wl, 13)], axis=0).astype(BF16)
        gpad = jnp.zeros((LANES - heads * 3, d), wl.dtype)
        w_nsa = jnp.concatenate([col(wl, 1), col(wl, 8), gpad] + [col(wl, k) for k in (2, 3, 4, 7, 6, 5)], axis=0).astype(BF16)
        w_gm = (0.5 * col(wl, 14)).astype(BF16)
        g1 = norm_mix_g[li]
        z_scan, q_gn, kcvc, ka, kwa, vsw, g_merge = _project(x, g1, sc1, sh1, w_scan, w_nsa, w_gm, (cos_t, sin_t), tt=steps)

        a_re, a_im, bb_re, bb_im = _s5_params(s5_lambda_re[li], s5_lambda_im[li], s5_log_step[li], s5_b_re[li], s5_b_im[li])
        to_bd = lambda m: _block_diag(jnp.swapaxes(m.reshape(groups, S5_STATE, S5_GROUP), 1, 2))
        s5b = jnp.concatenate([to_bd(bb_re), to_bd(bb_im)], axis=1).astype(BF16)
        c_bd = lambda m: _block_diag(jnp.swapaxes(m, 1, 2))
        s5c = jnp.concatenate([c_bd(s5_c_re[li]), -c_bd(s5_c_im[li])], axis=0).astype(BF16)
        r1 = lambda a: a.reshape(1, -1)
        y_scan = _scan_mixers(
            z_scan, bsz, s5b, s5c, r1(a_re), r1(a_im), r1(s5_d[li]), s5_w_glu[li].astype(BF16),
            lru_conv_w[li], r1(lru_conv_b[li]), _block_diag(lru_w_a[li]).astype(BF16), r1(lru_b_a[li]),
            _block_diag(lru_w_x[li]).astype(BF16), r1(lru_b_x[li]), r1(lru_lambda[li]), sc_conv_w[li], steps=2 * steps)

        k_cmp, v_cmp = _compress(kcvc, nsa_pe_k[li], nsa_pe_v[li],
                                 nsa_cmp_k_w1[li], nsa_cmp_k_w2[li], nsa_cmp_v_w1[li], nsa_cmp_v_w2[li])
        o_cmp, qa = _cmp_select(q_gn, k_cmp, jnp.swapaxes(v_cmp, 1, 2).astype(BF16), overlap_t, heads=heads, tq=4 * tq)
        tk = min(512, seq)
        ones_pad = jnp.concatenate([jnp.ones((bsz, seq, 1), BF16), jnp.zeros((bsz, seq, SUBLANES - 1), BF16)], axis=2)
        tiles_t = lambda a, n: jnp.swapaxes(
            jnp.concatenate([a, ones_pad], axis=2).reshape(bsz, seq // n, n, HEAD_DIM + SUBLANES), 2, 3)
        y_b = _attend(qa, ka.reshape(bsz, seq // tk, tk, LANES), tiles_t(vsw[:, :, :HEAD_DIM], tk),
                      kwa.reshape(bsz, seq // tq, tq, LANES), tiles_t(vsw[:, :, HEAD_DIM:], tq),
                      q_gn, (heads * HEAD_DIM) // LANES, o_cmp, tq=tq, tk=tk)

        x = _merge(y_scan, y_b, g_merge, x, w_branch[li].astype(BF16), w_out[li].astype(BF16), gt1, tt=steps)

        final_g = final_norm_g if li == depth - 1 else None
        j = li // 2
        if li % 2 == 0:
            weights = tuple(_to_bf16(wt[j:j + 1])[0] for wt in (ffn_w_gate, ffn_w_up, ffn_w_down))
            x = _ffn(x, norm_ffn_g[li], sc2, sh2, gt2, weights, final_g=final_g, tm=tm)
        else:
            rw = jnp.pad(moe_router_w[j], ((0, 0), (0, LANES - N_EXPERTS)))
            rb = jnp.pad(moe_router_b[j], (0, LANES - N_EXPERTS), constant_values=NEG_INF).reshape(1, LANES)
            x = _moe(x, norm_ffn_g[li], sc2, sh2, gt2, rw, rb, _to_bf16(moe_w_gate[j]), _to_bf16(moe_w_up[j]),
                     _to_bf16(moe_w_down[j]), final_g=final_g, tm=tm, tmg=512)
    return x
```

```python
import functools
import math

import jax
import jax.numpy as jnp
import numpy as np
from jax import lax
from jax.experimental import pallas as pl
from jax.experimental.pallas import tpu as pltpu

F32 = jnp.float32
BF16 = jnp.bfloat16

N_BRANCH = 4
S5_GROUP = 16
S5_STATE = 64
HEAD_DIM = 64
L_CMP = 32
CMP_STRIDE = 16
L_SEL = 64
TOP_N = 16
WINDOW = 512
ROPE_THETA = 10000.0
FORCED_SCORE = 1.0e4
NEG_INF = -1.0e30
LRU_CONV = 4
LRU_C = 8.0
SC_CONV = 3
N_EXPERTS = 8
RMS_EPS = 1e-6
LOG2E = math.log2(math.e)

LANES = 128
SUBLANES = 8
MXU_WIDTH = 256
VMEM_LIMIT = 52 * 1024 * 1024

HI = lax.Precision.HIGHEST


def _cparams(sem):
    return pltpu.CompilerParams(dimension_semantics=sem, vmem_limit_bytes=VMEM_LIMIT)


def _dot(a, b, precision=None):
    return jnp.dot(a, b, preferred_element_type=F32, precision=precision)


def _dot_t(a, b, precision=None):
    return lax.dot_general(a, b, (((1,), (1,)), ((), ())), preferred_element_type=F32, precision=precision)


def _gelu(x):
    return 0.5 * x * (1.0 + jnp.tanh(math.sqrt(2.0 / math.pi) * (x + 0.044715 * (x * x * x))))


def _sigmoid(x):
    return 0.5 * jnp.tanh(0.5 * x) + 0.5


def _silu(x):
    hx = 0.5 * x
    return hx * jnp.tanh(hx) + hx


def _norm_mod(x, g, scale, shift):
    var = jnp.mean(x * x, axis=-1, keepdims=True)
    return (x * lax.rsqrt(var + RMS_EPS) * g) * (1.0 + scale) + shift


def _rot_half(z):
    n = z.shape[-1]
    lane = lax.broadcasted_iota(jnp.int32, z.shape, z.ndim - 1)
    first = (lane % HEAD_DIM) < (HEAD_DIM // 2)
    return jnp.where(first, pltpu.roll(z, n - HEAD_DIM // 2, z.ndim - 1), pltpu.roll(z, HEAD_DIM // 2, z.ndim - 1))


def _cast_kernel(x_ref, o_ref):
    o_ref[...] = x_ref[...].astype(o_ref.dtype)


def _to_bf16(a):
    n, r, c = a.shape
    return pl.pallas_call(
        _cast_kernel,
        out_shape=jax.ShapeDtypeStruct(a.shape, BF16),
        grid=(n,),
        in_specs=[pl.BlockSpec((1, r, c), lambda i: (i, 0, 0))],
        out_specs=pl.BlockSpec((1, r, c), lambda i: (i, 0, 0)),
        compiler_params=_cparams(("arbitrary",)),
        name="cast_bf16",
    )(a)


def _mod_kernel(c_ref, w_ref, b_ref, o_ref):
    c = c_ref[...]
    o_ref[0] = _dot(_silu(c), w_ref[0], HI) + b_ref[0]


def _modulation(c, mod_w, mod_b):
    depth, d, n = mod_w.shape
    bsz = c.shape[0]
    tn = 1024
    return pl.pallas_call(
        _mod_kernel,
        out_shape=jax.ShapeDtypeStruct((depth, bsz, n), F32),
        grid=(depth, n // tn),
        in_specs=[
            pl.BlockSpec((bsz, d), lambda l, j: (0, 0)),
            pl.BlockSpec((1, d, tn), lambda l, j: (l, 0, j)),
            pl.BlockSpec((1, 1, tn), lambda l, j: (l, 0, j)),
        ],
        out_specs=pl.BlockSpec((1, bsz, tn), lambda l, j: (l, 0, j)),
        compiler_params=_cparams(("arbitrary", "arbitrary")),
        name="modulation",
    )(c, mod_w, mod_b.reshape(depth, 1, n))


def _proj_kernel(x_ref, g_ref, sc_ref, sh_ref, ws_ref, wn_ref, wg_ref, cos_ref, sin_ref,
                 zs_ref, qg_ref, kcvc_ref, ka_ref, kwa_ref, vsw_ref, gm_ref, h_ref):
    nb, tt, _ = x_ref.shape
    rows = nb * tt
    nlc = h_ref.shape[0]
    parts = []
    for b in range(nb):
        hb = _norm_mod(x_ref[b], g_ref[...], sc_ref[b], sh_ref[b])
        for c in range(nlc):
            h_ref[c, pl.ds(b, tt, stride=nb), :] = hb[:, c * LANES:(c + 1) * LANES]
        parts.append(hb.astype(BF16))
    h_bt = jnp.concatenate(parts, axis=0)
    h_tb = jnp.concatenate([h_ref[c] for c in range(nlc)], axis=1).astype(BF16)
    for n0 in range(0, ws_ref.shape[0], 2 * MXU_WIDTH):
        zs_ref[:, n0:n0 + 2 * MXU_WIDTH] = _dot_t(h_tb, ws_ref[n0:n0 + 2 * MXU_WIDTH, :])
    cos = jnp.concatenate([cos_ref[...]] * nb, axis=0)
    sin = jnp.concatenate([sin_ref[...]] * nb, axis=0)
    group = lax.broadcasted_iota(jnp.int32, (rows, MXU_WIDTH), 1) // HEAD_DIM

    def nsa_chunk(c, roped_groups):
        z = _dot_t(h_bt, wn_ref[c * MXU_WIDTH:(c + 1) * MXU_WIDTH, :])
        if len(roped_groups) == MXU_WIDTH // HEAD_DIM:
            return z * cos + _rot_half(z) * sin
        roped = group == roped_groups[0]
        for gi in roped_groups[1:]:
            roped = roped | (group == gi)
        return jnp.where(roped, z * cos + _rot_half(z) * sin, z)

    def put(ref, val):
        ref[...] = val.astype(ref.dtype).reshape(ref.shape)

    qg_ref[:, :, 0:MXU_WIDTH] = nsa_chunk(0, (0, 1, 2, 3)).reshape(nb, tt, MXU_WIDTH)
    z1 = nsa_chunk(1, (2,))
    qg_ref[:, :, MXU_WIDTH:MXU_WIDTH + LANES] = z1[:, 0:LANES].reshape(nb, tt, LANES)
    put(kcvc_ref, z1[:, LANES:2 * LANES])
    z2 = nsa_chunk(2, (0, 2))
    ks_vw, kw_vs = z2[:, 0:LANES], z2[:, LANES:2 * LANES]
    lane = lax.broadcasted_iota(jnp.int32, (rows, LANES), 1)
    key_lanes = lane < HEAD_DIM
    t = pl.program_id(0) * tt + lax.broadcasted_iota(jnp.int32, (rows, LANES), 0) % tt
    block_onehot = (lane == HEAD_DIM + t // L_SEL).astype(F32)
    put(ka_ref, jnp.where(key_lanes, ks_vw, block_onehot))
    put(kwa_ref, jnp.where(key_lanes, kw_vs, 0.0))
    put(vsw_ref, jnp.where(key_lanes, pltpu.roll(kw_vs, HEAD_DIM, 1), ks_vw))
    for n0 in range(0, wg_ref.shape[0], 2 * MXU_WIDTH):
        z = _dot_t(h_bt, wg_ref[n0:n0 + 2 * MXU_WIDTH, :])
        gm_ref[:, :, n0:n0 + 2 * MXU_WIDTH] = z.astype(gm_ref.dtype).reshape(nb, tt, 2 * MXU_WIDTH)


def _project(x, g, scale, shift, w_scan, w_nsa, w_gm, rope_tabs, *, tt):
    bsz, seq, d = x.shape
    ns, nn, ng = w_scan.shape[0], w_nsa.shape[0], w_gm.shape[0]
    full = lambda a: pl.BlockSpec(a.shape, lambda i: (0,) * a.ndim)
    tok = lambda n, dt: jax.ShapeDtypeStruct((bsz, seq, n), dt)
    tok_spec = lambda n: pl.BlockSpec((bsz, tt, n), lambda i: (0, i, 0))
    return pl.pallas_call(
        _proj_kernel,
        out_shape=(
            jax.ShapeDtypeStruct((seq * bsz, ns), F32),
            tok(MXU_WIDTH + LANES, F32), tok(LANES, F32), tok(LANES, BF16), tok(LANES, BF16), tok(LANES, BF16),
            tok(ng, BF16),
        ),
        grid=(seq // tt,),
        in_specs=[
            pl.BlockSpec((bsz, tt, d), lambda i: (0, i, 0)),
            pl.BlockSpec((1, d), lambda i: (0, 0)),
            pl.BlockSpec((bsz, 1, d), lambda i: (0, 0, 0)),
            pl.BlockSpec((bsz, 1, d), lambda i: (0, 0, 0)),
            full(w_scan), full(w_nsa), full(w_gm),
            pl.BlockSpec((tt, MXU_WIDTH), lambda i: (i, 0)),
            pl.BlockSpec((tt, MXU_WIDTH), lambda i: (i, 0)),
        ],
        out_specs=(
            pl.BlockSpec((tt * bsz, ns), lambda i: (i, 0)),
            tok_spec(MXU_WIDTH + LANES), tok_spec(LANES), tok_spec(LANES), tok_spec(LANES), tok_spec(LANES),
            tok_spec(ng),
        ),
        scratch_shapes=[pltpu.VMEM((d // LANES, tt * bsz, LANES), F32)],
        compiler_params=_cparams(("arbitrary",)),
        name="proj_in",
    )(x, g.reshape(1, d), scale, shift, w_scan, w_nsa, w_gm, *rope_tabs)


def _s5_param_kernel(lre_ref, lim_ref, ls_ref, bre_ref, bim_ref, are_ref, aim_ref, bbre_ref, bbim_ref):
    lre = lre_ref[...]
    lim = lim_ref[...]
    step = jnp.exp(ls_ref[...])
    mag = jnp.exp(lre * step)
    ar = mag * jnp.cos(lim * step)
    ai = mag * jnp.sin(lim * step)
    xr = ar - 1.0
    den = lre * lre + lim * lim
    cr = (xr * lre + ai * lim) / den
    ci = (ai * lre - xr * lim) / den
    bre = bre_ref[...]
    bim = bim_ref[...]
    are_ref[...] = ar
    aim_ref[...] = ai
    bbre_ref[...] = cr * bre - ci * bim
    bbim_ref[...] = cr * bim + ci * bre


def _s5_params(lam_re, lam_im, log_step, b_re, b_im):
    g, n = lam_re.shape
    c = b_re.shape[-1]
    rows = g * n
    col = lambda a: a.reshape(rows, 1)
    ls = jnp.broadcast_to(log_step[:, None], (g, n))
    out = pl.pallas_call(
        _s5_param_kernel,
        out_shape=(
            jax.ShapeDtypeStruct((rows, 1), F32),
            jax.ShapeDtypeStruct((rows, 1), F32),
            jax.ShapeDtypeStruct((rows, c), F32),
            jax.ShapeDtypeStruct((rows, c), F32),
        ),
        name="s5_params",
    )(col(lam_re), col(lam_im), col(ls), b_re.reshape(rows, c), b_im.reshape(rows, c))
    return out


def _scan_kernel(z_ref, s5b_ref, s5c_ref, are_ref, aim_ref, d_ref, wglu_ref,
                 cw_ref, cb_ref, wa_ref, ba_ref, wx_ref, bx_ref, lam_ref, scw_ref,
                 o_ref,
                 bu_ref, hre_ref, him_ref, xe_ref, hl_ref, pe_ref, ab_ref, bb_ref, *, nb, steps):
    c = pl.program_id(0)
    rows = nb * steps
    w = d_ref.shape[1]
    ns = are_ref.shape[1]

    @pl.when(c == 0)
    def _():
        hre_ref[...] = jnp.zeros_like(hre_ref)
        him_ref[...] = jnp.zeros_like(him_ref)
        hl_ref[...] = jnp.zeros_like(hl_ref)
        xe_ref[0:(LRU_CONV - 1) * nb, :] = jnp.zeros(((LRU_CONV - 1) * nb, w), F32)
        pe_ref[0:(SC_CONV - 1) * nb, :] = jnp.zeros(((SC_CONV - 1) * nb, w), F32)

    u = z_ref[:, 0:w]
    bu_ref[...] = _dot(u.astype(BF16), s5b_ref[...])
    a_re = jnp.broadcast_to(are_ref[...], (nb, ns))
    a_im = jnp.broadcast_to(aim_ref[...], (nb, ns))

    def s5_step(t, carry):
        h_re, h_im = carry
        r0 = pl.multiple_of(t * nb, nb)
        b_re = bu_ref[pl.ds(r0, nb), 0:ns]
        b_im = bu_ref[pl.ds(r0, nb), ns:2 * ns]
        n_re = a_re * h_re - a_im * h_im + b_re
        n_im = a_re * h_im + a_im * h_re + b_im
        bu_ref[pl.ds(r0, nb), 0:ns] = n_re
        bu_ref[pl.ds(r0, nb), ns:2 * ns] = n_im
        return n_re, n_im

    h_re, h_im = lax.fori_loop(0, steps, s5_step, (hre_ref[...], him_ref[...]))
    hre_ref[...] = h_re
    him_ref[...] = h_im
    y = _dot(bu_ref[...].astype(BF16), s5c_ref[...]) + d_ref[...] * u
    zg = _gelu(y)
    def put(branch, val):
        for k in range(w // LANES):
            o_ref[branch * (w // LANES) + k] = val[:, k * LANES:(k + 1) * LANES]

    put(0, zg * _sigmoid(_dot(zg.astype(BF16), wglu_ref[...])))

    nt = (LRU_CONV - 1) * nb
    x = z_ref[:, w:2 * w]
    xe_ref[nt:nt + rows, :] = x
    xc = cb_ref[...] + cw_ref[LRU_CONV - 1:LRU_CONV, :] * x
    for k in range(LRU_CONV - 1):
        xc = xc + cw_ref[k:k + 1, :] * xe_ref[k * nb:k * nb + rows, :]
    xe_ref[0:nt, :] = xe_ref[rows:rows + nt, :]
    xcb = xc.astype(BF16)
    r = _sigmoid(_dot(xcb, wa_ref[...]) + ba_ref[...])
    gi = _sigmoid(_dot(xcb, wx_ref[...]) + bx_ref[...])
    nl = -lam_ref[...]
    softplus = jnp.maximum(nl, 0.0) + jnp.log(1.0 + jnp.exp(-jnp.abs(nl)))
    log_a = (-LRU_C) * r * softplus
    a = jnp.exp(log_a)
    ab_ref[...] = a
    bb_ref[...] = jnp.sqrt(1.0 - a * a) * (gi * xc)

    def lru_step(t, h):
        r0 = pl.multiple_of(t * nb, nb)
        h = ab_ref[pl.ds(r0, nb), :] * h + bb_ref[pl.ds(r0, nb), :]
        bb_ref[pl.ds(r0, nb), :] = h
        return h

    hl_ref[...] = lax.fori_loop(0, steps, lru_step, hl_ref[...])
    put(1, bb_ref[...] * _gelu(z_ref[:, 2 * w:3 * w]))

    npt = (SC_CONV - 1) * nb
    p = z_ref[:, 4 * w:5 * w] * z_ref[:, 5 * w:6 * w]
    pe_ref[npt:npt + rows, :] = p
    cv = scw_ref[SC_CONV - 1:SC_CONV, :] * p
    for k in range(SC_CONV - 1):
        cv = cv + scw_ref[k:k + 1, :] * pe_ref[k * nb:k * nb + rows, :]
    pe_ref[0:npt, :] = pe_ref[rows:rows + npt, :]
    put(2, z_ref[:, 3 * w:4 * w] * cv)


def _scan_mixers(z_tm, nb, s5b, s5c, a_re, a_im, d_skip, w_glu, conv_w, conv_b, wa, ba, wx, bx, lam, sc_w, *, steps):
    total, zc = z_tm.shape
    rows = nb * steps
    w = zc // 6
    ns = a_re.shape[1]
    full = lambda a: pl.BlockSpec(a.shape, lambda c: (0,) * a.ndim)
    params = [s5b, s5c, a_re, a_im, d_skip, w_glu, conv_w, conv_b, wa, ba, wx, bx, lam, sc_w]
    return pl.pallas_call(
        functools.partial(_scan_kernel, nb=nb, steps=steps),
        out_shape=jax.ShapeDtypeStruct((3 * w // LANES, total, LANES), F32),
        grid=(total // rows,),
        in_specs=[pl.BlockSpec((rows, zc), lambda c: (c, 0))] + [full(a) for a in params],
        out_specs=pl.BlockSpec((3 * w // LANES, rows, LANES), lambda c: (0, c, 0)),
        scratch_shapes=[
            pltpu.VMEM((rows, 2 * ns), F32),
            pltpu.VMEM((nb, ns), F32),
            pltpu.VMEM((nb, ns), F32),
            pltpu.VMEM((rows + (LRU_CONV - 1) * nb, w), F32),
            pltpu.VMEM((nb, w), F32),
            pltpu.VMEM((rows + (SC_CONV - 1) * nb, w), F32),
            pltpu.VMEM((rows, w), F32),
            pltpu.VMEM((rows, w), F32),
        ],
        compiler_params=_cparams(("arbitrary",)),
        name="scan_mixers",
    )(z_tm, *params)


def _compress_kernel(kv_ref, pek_ref, pev_ref, wk1_ref, wk1p_ref, wk2_ref, wv1_ref, wv1p_ref, wv2_ref, ko_ref, vo_ref):
    nblk = ko_ref.shape[1]
    acc = [None] * 4
    for l in range(CMP_STRIDE):
        xl = kv_ref[0, pl.ds(l, nblk, stride=CMP_STRIDE), :]
        terms = (_dot(xl, wk1p_ref[l], HI), _dot(xl, wk1p_ref[CMP_STRIDE + l], HI),
                 _dot(xl, wv1p_ref[l], HI), _dot(xl, wv1p_ref[CMP_STRIDE + l], HI))
        acc = [t if a is None else a + t for a, t in zip(acc, terms)]

    def phi(first, second, pe, w1_ref, w2_ref):
        hid = first + pltpu.roll(second, nblk - 1, 0) + _dot(pe, w1_ref[...], HI)
        return _dot(_gelu(hid), w2_ref[...], HI)

    ko_ref[0] = phi(acc[0], acc[1], pek_ref[...], wk1_ref, wk2_ref)
    vo_ref[0] = phi(acc[2], acc[3], pev_ref[...], wv1_ref, wv2_ref)


def _compress(kcvc, pe_k, pe_v, wk1, wk2, wv1, wv2):
    bsz, seq, kvw = kcvc.shape
    nblk = seq // CMP_STRIDE
    hd = wk2.shape[1]
    hidden = wk1.shape[1]
    full = lambda a: pl.BlockSpec(a.shape, lambda b: (0,) * a.ndim)
    zeros = jnp.zeros((L_CMP, HEAD_DIM, hidden), wk1.dtype)
    wk1p = jnp.concatenate([wk1.reshape(L_CMP, HEAD_DIM, hidden), zeros], axis=1)
    wv1p = jnp.concatenate([zeros, wv1.reshape(L_CMP, HEAD_DIM, hidden)], axis=1)
    params = [pe_k.reshape(1, -1), pe_v.reshape(1, -1), wk1, wk1p, wk2, wv1, wv1p, wv2]
    oblk = pl.BlockSpec((1, nblk, hd), lambda b: (b, 0, 0))
    return pl.pallas_call(
        _compress_kernel,
        out_shape=(jax.ShapeDtypeStruct((bsz, nblk, hd), F32),) * 2,
        grid=(bsz,),
        in_specs=[pl.BlockSpec((1, seq, kvw), lambda b: (b, 0, 0))] + [full(a) for a in params],
        out_specs=(oblk, oblk),
        compiler_params=_cparams(("arbitrary",)),
        name="nsa_compress",
    )(kcvc, *params)


def _split_bf16(a):
    hi = a.astype(BF16)
    return hi, (a - hi.astype(F32)).astype(BF16)


def _cmp_kernel(q_ref, kc_ref, vct_ref, ovt_ref, ocmp_ref, qa_ref, *, heads, tq):
    i = pl.program_id(1)
    q_t = (q_ref[0] * HEAD_DIM ** -0.5).T
    kc_hi, kc_lo = _split_bf16(kc_ref[0])
    vct = vct_ref[0]
    nblk = kc_hi.shape[0]
    t = i * tq + lax.broadcasted_iota(jnp.int32, (nblk, tq), 1)
    n_id = lax.broadcasted_iota(jnp.int32, (nblk, tq), 0)
    visible = (n_id * CMP_STRIDE + (L_CMP - 1)) <= t
    psum = jnp.zeros((nblk, tq), F32)
    for h in range(heads):
        q_hi, q_lo = _split_bf16(q_t[h * HEAD_DIM:(h + 1) * HEAD_DIM, :])
        s = _dot(kc_hi, q_hi) + (_dot(kc_hi, q_lo) + _dot(kc_lo, q_hi))
        s = jnp.where(visible, s, NEG_INF)
        m = jnp.max(s, axis=0, keepdims=True)
        e = jnp.where(visible, jnp.exp(s - m), 0.0)
        den = jnp.sum(e, axis=0, keepdims=True)
        p = e / jnp.where(den > 0.0, den, 1.0)
        psum = psum + p
        ocmp_ref[0, h * HEAD_DIM:(h + 1) * HEAD_DIM, :] = _dot(vct, p.astype(BF16))

    p_hi, p_lo = _split_bf16(psum)
    imp_t = _dot(ovt_ref[...], p_hi) + _dot(ovt_ref[...], p_lo)
    nslot = ovt_ref.shape[0]
    tt = i * tq + lax.broadcasted_iota(jnp.int32, (nslot, tq), 1)
    j = lax.broadcasted_iota(jnp.int32, (nslot, tq), 0)
    cur = tt // L_SEL
    forced = (j == 0) | (j == cur) | (j == cur - 1)
    future = (j * L_SEL) > tt
    imp_t = jnp.where(future, -1.0, jnp.where(forced, FORCED_SCORE, imp_t))
    nreal = qa_ref.shape[2] - HEAD_DIM
    cand = imp_t[0:nreal, :]
    jrow = lax.broadcasted_iota(jnp.int32, (nreal, tq), 0)
    bias_t = jnp.full((nreal, tq), NEG_INF, F32)
    for _ in range(TOP_N):
        best = jnp.max(cand, axis=0, keepdims=True)
        first = jnp.min(jnp.where(cand == best, jrow, nreal), axis=0, keepdims=True)
        hit = jrow == first
        bias_t = jnp.where(hit, 0.0, bias_t)
        cand = jnp.where(hit, -2.0, cand)
    for h in range(heads):
        qa_ref[0, h] = jnp.concatenate([q_t[h * HEAD_DIM:(h + 1) * HEAD_DIM, :] * LOG2E, bias_t], axis=0).astype(BF16)


def _cmp_select(q, k_cmp, v_cmp_t, overlap_t, *, heads, tq):
    bsz, seq, _ = q.shape
    qd = heads * HEAD_DIM
    nblk = k_cmp.shape[1]
    nsel = LANES - HEAD_DIM
    return pl.pallas_call(
        functools.partial(_cmp_kernel, heads=heads, tq=tq),
        out_shape=(
            jax.ShapeDtypeStruct((bsz, qd, seq), F32),
            jax.ShapeDtypeStruct((bsz, heads, HEAD_DIM + nsel, seq), BF16),
        ),
        grid=(bsz, seq // tq),
        in_specs=[
            pl.BlockSpec((1, tq, qd), lambda b, i: (b, i, 0)),
            pl.BlockSpec((1, nblk, HEAD_DIM), lambda b, i: (b, 0, 0)),
            pl.BlockSpec((1, HEAD_DIM, nblk), lambda b, i: (b, 0, 0)),
            pl.BlockSpec(overlap_t.shape, lambda b, i: (0, 0)),
        ],
        out_specs=(
            pl.BlockSpec((1, qd, tq), lambda b, i: (b, 0, i)),
            pl.BlockSpec((1, heads, HEAD_DIM + nsel, tq), lambda b, i: (b, 0, 0, i)),
        ),
        compiler_params=_cparams(("arbitrary", "arbitrary")),
        name="nsa_cmp_select",
    )(q, k_cmp, v_cmp_t, overlap_t)


def _attend_kernel(qa_ref, ka_ref, vst_ref, kwa_ref, vwt_ref, gn_ref, ocmp_ref, o_ref,
                   m_ref, acc_ref, s0_ref, s1_ref, *, heads, tq, tk, nwin):
    i = pl.program_id(1)
    cols = heads * tq
    qat = jnp.concatenate([qa_ref[0, h] for h in range(heads)], axis=1)
    t_row = i * tq + (lax.broadcasted_iota(jnp.int32, (1, cols), 1) % tq)

    m_ref[...] = jnp.full(m_ref.shape, NEG_INF, F32)
    acc_ref[...] = jnp.zeros(acc_ref.shape, F32)
    key_id = lax.broadcasted_iota(jnp.int32, (tk, cols), 0)
    n_full = (i * tq + 1) // tk
    s0_ref[...] = _dot(ka_ref[0, 0], qat)

    st = jnp.maximum(i + 1 - nwin, 0)
    kw = kwa_ref[0, pl.ds(st, nwin)].reshape(nwin * tq, kwa_ref.shape[3])
    s = _dot(kw, qat)
    dist = t_row - (st * tq + lax.broadcasted_iota(jnp.int32, (nwin * tq, cols), 0))
    s = jnp.where((dist >= 0) & (dist < WINDOW), s, NEG_INF)
    m = jnp.max(s, axis=0, keepdims=True)
    pb = jnp.exp2(s - m).astype(BF16)
    ow = _dot(vwt_ref[0, st], pb[0:tq, :])
    for j in range(1, nwin):
        ow = ow + _dot(vwt_ref[0, st + j], pb[j * tq:(j + 1) * tq, :])
    owin = ow[0:HEAD_DIM, :] / ow[HEAD_DIM:HEAD_DIM + 1, :]

    def softmax_tile(kt, s, masked):
        if masked:
            s = jnp.where(kt * tk + key_id <= t_row, s, NEG_INF)
        m_old = m_ref[...]
        m_new = jnp.maximum(m_old, jnp.max(s, axis=0, keepdims=True))
        p = jnp.exp2(s - m_new).astype(BF16)
        acc_ref[...] = jnp.exp2(m_old - m_new) * acc_ref[...] + _dot(vst_ref[0, kt], p)
        m_ref[...] = m_new

    def pair_body(j, carry):
        s1_ref[...] = _dot(ka_ref[0, 2 * j + 1], qat)
        softmax_tile(2 * j, s0_ref[...], False)
        s0_ref[...] = _dot(ka_ref[0, 2 * j + 2], qat)
        softmax_tile(2 * j + 1, s1_ref[...], False)
        return carry

    n_pair = n_full // 2
    lax.fori_loop(0, n_pair, pair_body, 0)
    s1_ref[...] = _dot(ka_ref[0, 2 * n_pair + 1], qat)
    softmax_tile(2 * n_pair, s0_ref[...], True)
    softmax_tile(2 * n_pair + 1, s1_ref[...], True)

    osel = acc_ref[0:HEAD_DIM, :] / acc_ref[HEAD_DIM:HEAD_DIM + 1, :]

    g_t = _sigmoid(gn_ref[0]).T
    outs = []
    for h in range(heads):
        cs = slice(h * tq, (h + 1) * tq)
        outs.append(g_t[3 * h:3 * h + 1, :] * ocmp_ref[0, h * HEAD_DIM:(h + 1) * HEAD_DIM, :]
                    + g_t[3 * h + 1:3 * h + 2, :] * osel[:, cs]
                    + g_t[3 * h + 2:3 * h + 3, :] * owin[:, cs])
    o_ref[0] = jnp.concatenate(outs, axis=0).T


def _attend(qa, ka, vst, kwa, vwt, z_nsa, gn_block, o_cmp_t, *, tq, tk):
    bsz, heads, ka_w, seq = qa.shape
    qd = heads * HEAD_DIM
    nwin = WINDOW // tq + 1
    return pl.pallas_call(
        functools.partial(_attend_kernel, heads=heads, tq=tq, tk=tk, nwin=nwin),
        out_shape=jax.ShapeDtypeStruct((bsz, seq, qd), F32),
        grid=(bsz, seq // tq),
        in_specs=[
            pl.BlockSpec((1, heads, ka_w, tq), lambda b, i: (b, 0, 0, i)),
            pl.BlockSpec((1,) + ka.shape[1:], lambda b, i: (b, 0, 0, 0)),
            pl.BlockSpec((1,) + vst.shape[1:], lambda b, i: (b, 0, 0, 0)),
            pl.BlockSpec((1,) + kwa.shape[1:], lambda b, i: (b, 0, 0, 0)),
            pl.BlockSpec((1,) + vwt.shape[1:], lambda b, i: (b, 0, 0, 0)),
            pl.BlockSpec((1, tq, LANES), lambda b, i: (b, i, gn_block)),
            pl.BlockSpec((1, qd, tq), lambda b, i: (b, 0, i)),
        ],
        out_specs=pl.BlockSpec((1, tq, qd), lambda b, i: (b, i, 0)),
        scratch_shapes=[
            pltpu.VMEM((1, heads * tq), F32),
            pltpu.VMEM((vst.shape[2], heads * tq), F32),
            pltpu.VMEM((tk, heads * tq), F32),
            pltpu.VMEM((tk, heads * tq), F32),
        ],
        compiler_params=_cparams(("arbitrary", "arbitrary")),
        name="nsa_attend",
    )(qa, ka, vst, kwa, vwt, z_nsa, o_cmp_t)


def _merge_kernel(ys_ref, yb_ref, gm_ref, x_ref, wb_ref, wo_ref, gate_ref, o_ref, m_ref):
    nb, tt, w = yb_ref.shape
    d = x_ref.shape[2]
    rows = nb * tt
    def slab(c):
        return jnp.concatenate([ys_ref[c, pl.ds(b, tt, stride=nb), :] for b in range(nb)], axis=0)

    ys = jnp.concatenate([slab(c) for c in range(ys_ref.shape[0])], axis=1)
    branches = (ys[:, 0:w], yb_ref[...].reshape(rows, w), ys[:, w:2 * w], ys[:, 2 * w:3 * w])
    ybs = [y.astype(BF16) for y in branches]
    for n0 in range(0, d, MXU_WIDTH):
        plain = None
        gated = None
        for mi, y in enumerate(ybs):
            p = _dot(y, wb_ref[mi, :, n0:n0 + MXU_WIDTH])
            gm = gm_ref[:, :, mi * d + n0:mi * d + n0 + MXU_WIDTH].reshape(rows, MXU_WIDTH)
            t = jnp.tanh(gm.astype(F32)) * p
            plain = p if plain is None else plain + p
            gated = t if gated is None else gated + t
        m_ref[:, n0:n0 + MXU_WIDTH] = (0.5 * (plain + gated)).astype(BF16)
    mix = _dot(m_ref[...], wo_ref[...]).reshape(nb, tt, d)
    o_ref[...] = x_ref[...] + gate_ref[...] * mix


def _merge(y_scan_tm, y_b, g_merge, x, w_branch, w_out, gate, *, tt):
    bsz, seq, d = x.shape
    w = y_b.shape[2]
    tile = lambda n: pl.BlockSpec((bsz, tt, n), lambda i: (0, i, 0))
    return pl.pallas_call(
        _merge_kernel,
        out_shape=jax.ShapeDtypeStruct((bsz, seq, d), F32),
        grid=(seq // tt,),
        in_specs=[
            pl.BlockSpec((3 * w // LANES, tt * bsz, LANES), lambda i: (0, i, 0)),
            tile(w),
            tile(N_BRANCH * d),
            tile(d),
            pl.BlockSpec(w_branch.shape, lambda i: (0, 0, 0)),
            pl.BlockSpec(w_out.shape, lambda i: (0, 0)),
            pl.BlockSpec((bsz, 1, d), lambda i: (0, 0, 0)),
        ],
        out_specs=tile(d),
        scratch_shapes=[pltpu.VMEM((tt * bsz, d), BF16)],
        compiler_params=_cparams(("arbitrary",)),
        name="merge_out",
    )(y_scan_tm, y_b, g_merge, x, w_branch, w_out, gate)


def _final_norm(out, fg_ref):
    var = jnp.mean(out * out, axis=-1, keepdims=True)
    return out * lax.rsqrt(var + RMS_EPS) * fg_ref[...]


def _ffn_kernel(x_ref, g_ref, sc_ref, sh_ref, gate_ref, wg_ref, wu_ref, wd_ref, *rest, final, fc):
    if final:
        fg_ref, o_ref, a_ref = rest
    else:
        o_ref, a_ref = rest
    hb = _norm_mod(x_ref[0], g_ref[...], sc_ref[0], sh_ref[0]).astype(BF16)
    for f0 in range(0, a_ref.shape[1], fc):
        a_ref[:, f0:f0 + fc] = (_silu(_dot(hb, wg_ref[:, f0:f0 + fc])) * _dot(hb, wu_ref[:, f0:f0 + fc])).astype(BF16)
    out = x_ref[0] + gate_ref[0] * _dot(a_ref[...], wd_ref[...])
    o_ref[0] = _final_norm(out, fg_ref) if final else out


def _ffn(x, g, scale, shift, gate, weights, *, final_g, tm):
    bsz, seq, d = x.shape
    row = lambda: pl.BlockSpec((1, 1, d), lambda b, i: (b, 0, 0))
    wg, wu, wd = weights
    ff = wg.shape[1]
    in_specs = [
        pl.BlockSpec((1, tm, d), lambda b, i: (b, i, 0)),
        pl.BlockSpec((1, d), lambda b, i: (0, 0)),
        row(), row(), row(),
        pl.BlockSpec((d, ff), lambda b, i: (0, 0)),
        pl.BlockSpec((d, ff), lambda b, i: (0, 0)),
        pl.BlockSpec((ff, d), lambda b, i: (0, 0)),
    ]
    args = [x, g.reshape(1, d), scale, shift, gate, wg, wu, wd]
    final = final_g is not None
    if final:
        in_specs.append(pl.BlockSpec((1, d), lambda b, i: (0, 0)))
        args.append(final_g.reshape(1, d))
    return pl.pallas_call(
        functools.partial(_ffn_kernel, final=final, fc=MXU_WIDTH),
        out_shape=jax.ShapeDtypeStruct((bsz, seq, d), F32),
        grid=(bsz, seq // tm),
        in_specs=in_specs,
        out_specs=pl.BlockSpec((1, tm, d), lambda b, i: (b, i, 0)),
        scratch_shapes=[pltpu.VMEM((tm, ff), BF16)],
        compiler_params=_cparams(("arbitrary", "arbitrary")),
        name="dense_ffn",
    )(*args)


def _route_kernel(x_ref, g_ref, sc_ref, sh_ref, rw_ref, rb_ref, h_ref, r_ref):
    h = _norm_mod(x_ref[0], g_ref[...], sc_ref[0], sh_ref[0])
    h_hi, h_lo = _split_bf16(h)
    nslab = h.shape[1] // LANES
    for c in range(nslab):
        h_ref[pl.ds(c, h.shape[0], stride=nslab), :] = h[:, c * LANES:(c + 1) * LANES]
    w_hi, w_lo = _split_bf16(rw_ref[...])
    logits = _dot(h_hi, w_hi) + (_dot(h_hi, w_lo) + _dot(h_lo, w_hi)) + rb_ref[...]
    lane = lax.broadcasted_iota(jnp.int32, logits.shape, 1)
    big = logits.shape[1]
    v1 = jnp.max(logits, axis=-1, keepdims=True)
    i1 = jnp.min(jnp.where(logits == v1, lane, big), axis=-1, keepdims=True)
    rest_l = jnp.where(lane == i1, NEG_INF * 4.0, logits)
    v2 = jnp.max(rest_l, axis=-1, keepdims=True)
    i2 = jnp.min(jnp.where(rest_l == v2, lane, big), axis=-1, keepdims=True)
    e2 = jnp.exp(v2 - v1)
    w1 = 1.0 / (1.0 + e2)
    w2 = e2 / (1.0 + e2)
    r_ref[0] = jnp.where(lane == 0, i1.astype(F32), jnp.where(lane == 1, i2.astype(F32), jnp.where(lane == 2, w1, w2)))


def _route(x, g, scale, shift, rw, rb, *, tm):
    bsz, seq, d = x.shape
    row = lambda: pl.BlockSpec((1, 1, d), lambda b, i: (b, 0, 0))
    return pl.pallas_call(
        _route_kernel,
        out_shape=(jax.ShapeDtypeStruct((bsz * seq * (d // LANES), LANES), F32), jax.ShapeDtypeStruct((bsz, seq, LANES), F32)),
        grid=(bsz, seq // tm),
        in_specs=[
            pl.BlockSpec((1, tm, d), lambda b, i: (b, i, 0)),
            pl.BlockSpec((1, d), lambda b, i: (0, 0)),
            row(), row(),
            pl.BlockSpec(rw.shape, lambda b, i: (0, 0)),
            pl.BlockSpec(rb.shape, lambda b, i: (0, 0)),
        ],
        out_specs=(pl.BlockSpec((tm * (d // LANES), LANES), lambda b, i: (b * (seq // tm) + i, 0)),
                   pl.BlockSpec((1, tm, LANES), lambda b, i: (b, i, 0))),
        compiler_params=_cparams(("arbitrary", "arbitrary")),
        name="moe_route",
    )(x, g.reshape(1, d), scale, shift, rw, rb)


def _gffn_kernel(te_ref, nu_ref, x_ref, wg_ref, wu_ref, wd_ref, o_ref):
    i = pl.program_id(0)

    @pl.when(i < nu_ref[0])
    def _():
        nslab = wg_ref.shape[1] // LANES
        tmg = o_ref.shape[0]
        hb = jnp.concatenate([x_ref[pl.ds(c, tmg, stride=nslab), :] for c in range(nslab)], axis=1).astype(BF16)
        a = (_silu(_dot(hb, wg_ref[0])) * _dot(hb, wu_ref[0])).astype(BF16)
        o_ref[...] = _dot(a, wd_ref[0]).astype(o_ref.dtype)

    @pl.when(i >= nu_ref[0])
    def _():
        o_ref[...] = jnp.zeros_like(o_ref)


def _grouped_ffn(xs, tile_expert, n_used, wg, wu, wd, *, tmg):
    _, d, fe = wg.shape
    nslab = d // LANES
    rows = xs.shape[0] // nslab
    return pl.pallas_call(
        _gffn_kernel,
        out_shape=jax.ShapeDtypeStruct((rows, d), BF16),
        grid_spec=pltpu.PrefetchScalarGridSpec(
            num_scalar_prefetch=2,
            grid=(rows // tmg,),
            in_specs=[
                pl.BlockSpec((tmg * nslab, LANES), lambda i, te, nu: (i, 0)),
                pl.BlockSpec((1, d, fe), lambda i, te, nu: (te[i], 0, 0)),
                pl.BlockSpec((1, d, fe), lambda i, te, nu: (te[i], 0, 0)),
                pl.BlockSpec((1, fe, d), lambda i, te, nu: (te[i], 0, 0)),
            ],
            out_specs=pl.BlockSpec((tmg, d), lambda i, te, nu: (i, 0)),
        ),
        compiler_params=_cparams(("arbitrary",)),
        name="moe_grouped_ffn",
    )(tile_expert, n_used, xs, wg, wu, wd)


def _combine_kernel(x_ref, gate_ref, y_ref, y2_ref, r_ref, *rest, final):
    if final:
        fg_ref, o_ref = rest
    else:
        (o_ref,) = rest
    r = r_ref[0]
    f = r[:, 2:3] * y_ref[0].astype(F32) + r[:, 3:4] * y2_ref[0].astype(F32)
    out = x_ref[0] + gate_ref[0] * f
    o_ref[0] = _final_norm(out, fg_ref) if final else out


def _combine(x, gate, y1, y2, route, *, final_g, tm):
    bsz, seq, d = x.shape
    tile = lambda: pl.BlockSpec((1, tm, d), lambda b, i: (b, i, 0))
    in_specs = [
        tile(),
        pl.BlockSpec((1, 1, d), lambda b, i: (b, 0, 0)),
        tile(), tile(),
        pl.BlockSpec((1, tm, LANES), lambda b, i: (b, i, 0)),
    ]
    args = [x, gate, y1, y2, route]
    final = final_g is not None
    if final:
        in_specs.append(pl.BlockSpec((1, d), lambda b, i: (0, 0)))
        args.append(final_g.reshape(1, d))
    return pl.pallas_call(
        functools.partial(_combine_kernel, final=final),
        out_shape=jax.ShapeDtypeStruct((bsz, seq, d), F32),
        grid=(bsz, seq // tm),
        in_specs=in_specs,
        out_specs=pl.BlockSpec((1, tm, d), lambda b, i: (b, i, 0)),
        compiler_params=_cparams(("arbitrary", "arbitrary")),
        name="moe_combine",
    )(*args)


def _src_kernel(te_ref, k0_ref, jlo_ref, jhi_ref, c_ref, o_ref, cnt_ref, *, nblk, tb, last):
    i = pl.program_id(0)
    tmg = o_ref.shape[0]
    base = te_ref[i] * nblk
    kcol = k0_ref[i] + lax.broadcasted_iota(jnp.int32, (tmg, 1), 0)
    cnt_ref[...] = jnp.zeros_like(cnt_ref)

    def blk(j, carry):
        hit = (c_ref[base + j] <= kcol).astype(jnp.int32)
        part = hit[:, 0:LANES]
        for q in range(1, tb // LANES):
            part = part + hit[:, q * LANES:(q + 1) * LANES]
        cnt_ref[...] += part
        return carry

    lax.fori_loop(jlo_ref[i], jhi_ref[i], blk, 0)
    total = jnp.sum(cnt_ref[...], axis=1, keepdims=True) + jlo_ref[i] * tb
    o_ref[...] = jnp.minimum(total, last)


def _moe(x, g, scale, shift, gate, rw, rb, wg, wu, wd, *, final_g, tm, tmg):
    bsz, seq, d = x.shape
    ne = wg.shape[0]
    tokens = bsz * seq
    tb = 512
    nblk = tokens // tb
    h, route = _route(x, g, scale, shift, rw, rb, tm=tm)
    ar = jnp.arange(ne, dtype=jnp.int32)[:, None]
    m1 = route[:, :, 0].astype(jnp.int32).reshape(1, tokens) == ar
    m2 = route[:, :, 1].astype(jnp.int32).reshape(1, tokens) == ar
    count = jnp.cumsum((m1 | m2).astype(jnp.int32), axis=1)
    padded = ((count[:, -1] + tmg - 1) // tmg) * tmg
    ends = jnp.cumsum(padded)
    offs = ends - padded
    row_of = count - 1 + offs[:, None]
    dest1 = jnp.sum(jnp.where(m1, row_of, 0), axis=0)
    dest2 = jnp.sum(jnp.where(m2, row_of, 0), axis=0)
    rows = tokens * 2 + ne * tmg
    ntiles = rows // tmg
    tile_start = jnp.arange(ntiles, dtype=jnp.int32) * tmg
    tile_expert = jnp.minimum(jnp.sum((tile_start[:, None] >= ends[None, :]).astype(jnp.int32), axis=1), ne - 1)
    n_used = (ends[-1] // tmg).astype(jnp.int32).reshape(1)
    k0 = tile_start - offs[tile_expert]
    cend = count[:, tb - 1::tb]
    cstart = jnp.concatenate([jnp.zeros((ne, 1), jnp.int32), cend[:, :-1]], axis=1)
    jlo = jnp.sum((cend[tile_expert] <= k0[:, None]).astype(jnp.int32), axis=1)
    jhi = jnp.sum((cstart[tile_expert] <= (k0 + tmg - 1)[:, None]).astype(jnp.int32), axis=1)
    src = pl.pallas_call(
        functools.partial(_src_kernel, nblk=nblk, tb=tb, last=tokens - 1),
        out_shape=jax.ShapeDtypeStruct((rows, 1), jnp.int32),
        grid_spec=pltpu.PrefetchScalarGridSpec(
            num_scalar_prefetch=4,
            grid=(ntiles,),
            in_specs=[pl.BlockSpec((ne * nblk, 1, tb), lambda i, *_: (0, 0, 0))],
            out_specs=pl.BlockSpec((tmg, 1), lambda i, *_: (i, 0)),
            scratch_shapes=[pltpu.VMEM((tmg, LANES), jnp.int32)],
        ),
        compiler_params=_cparams(("arbitrary",)),
        name="moe_sources",
    )(tile_expert, k0, jlo, jhi, count.reshape(ne * nblk, 1, tb))
    rows_of = lambda a, idx: a.at[idx].get(mode="promise_in_bounds")
    nslab = d // LANES
    xs = rows_of(h.reshape(tokens, nslab, LANES), src[:, 0]).reshape(rows * nslab, LANES)
    ys = _grouped_ffn(xs, tile_expert, n_used, wg, wu, wd, tmg=tmg)
    y1 = rows_of(ys, dest1).reshape(bsz, seq, d)
    y2 = rows_of(ys, dest2).reshape(bsz, seq, d)
    return _combine(x, gate, y1, y2, route, final_g=final_g, tm=tm)


def _block_diag(blocks):
    n, r, c = blocks.shape
    eye = jnp.eye(n, dtype=blocks.dtype)
    return (eye[:, None, :, None] * blocks[:, :, None, :]).reshape(n * r, n * c)


def _rope_tables(seq):
    inv = ROPE_THETA ** (-jnp.arange(0, HEAD_DIM, 2, dtype=F32) / HEAD_DIM)
    ang = jnp.arange(seq, dtype=F32)[:, None] * inv[None, :]
    cos, sin = jnp.cos(ang), jnp.sin(ang)
    reps = MXU_WIDTH // HEAD_DIM
    return jnp.concatenate([cos, cos] * reps, axis=1), jnp.concatenate([-sin, sin] * reps, axis=1)


def kernel(x, c, mod_w, mod_b, norm_mix_g, norm_ffn_g, w_in, s5_lambda_re, s5_lambda_im, s5_log_step, s5_b_re, s5_b_im, s5_c_re, s5_c_im, s5_d, s5_w_glu, nsa_pe_k, nsa_pe_v, nsa_cmp_k_w1, nsa_cmp_k_w2, nsa_cmp_v_w1, nsa_cmp_v_w2, lru_conv_w, lru_conv_b, lru_w_a, lru_b_a, lru_w_x, lru_b_x, lru_lambda, sc_conv_w, w_branch, w_out, ffn_w_gate, ffn_w_up, ffn_w_down, moe_router_w, moe_router_b, moe_w_gate, moe_w_up, moe_w_down, final_norm_g):
    bsz, seq, d = x.shape
    depth = mod_w.shape[0]
    w = d // N_BRANCH
    heads = w // HEAD_DIM
    groups = w // S5_GROUP
    assert bsz == SUBLANES, "time-major scan blocks hold one timestep per sublane tile"
    tm = min(512, seq)
    tq = 128
    steps = min(64, seq)
    nsel = seq // L_SEL
    assert TOP_N <= nsel <= LANES - HEAD_DIM and seq % tm == 0 and seq >= WINDOW + tq

    mod = _modulation(c, mod_w, mod_b)
    mod = mod.reshape(depth, bsz, 6, 1, d)

    sizes = (w, w, HEAD_DIM, HEAD_DIM, HEAD_DIM, HEAD_DIM, HEAD_DIM, HEAD_DIM, heads * 3, w, w, w, w, w, N_BRANCH * d)
    offs = np.concatenate([[0], np.cumsum(sizes)])
    col = lambda wmat_t, k: wmat_t[offs[k]:offs[k + 1], :]

    cos_t, sin_t = _rope_tables(seq)
    n_cmp = (seq - L_CMP) // CMP_STRIDE + 1
    nblk = seq // CMP_STRIDE
    cmp_start = np.arange(nblk) * CMP_STRIDE
    sel_start = np.arange(nsel) * L_SEL
    ovl = np.clip(np.minimum(cmp_start[:, None] + L_CMP, sel_start[None, :] + L_SEL)
                  - np.maximum(cmp_start[:, None], sel_start[None, :]), 0, None) / L_CMP
    ovl[n_cmp:] = 0.0
    overlap_t = jnp.asarray(np.pad(ovl, ((0, 0), (0, LANES - nsel))).T, BF16)

    w_in_t = jnp.swapaxes(w_in, 1, 2)

    for li in range(depth):
        sh1, sc1, gt1, sh2, sc2, gt2 = (mod[li, :, k] for k in range(6))
        wl = w_in_t[li]
        w_scan = jnp.concatenate([col(wl, 0), col(wl, 9), col(wl, 10), col(wl, 11), col(wl, 12), col(wl, 13)], axis=0).astype(BF16)
        gpad = jnp.zeros((LANES - heads * 3, d), wl.dtype)
        w_nsa = jnp.concatenate([col(wl, 1), col(wl, 8), gpad] + [col(wl, k) for k in (2, 3, 4, 7, 6, 5)], axis=0).astype(BF16)
        w_gm = (0.5 * col(wl, 14)).astype(BF16)
        g1 = norm_mix_g[li]
        z_scan, q_gn, kcvc, ka, kwa, vsw, g_merge = _project(x, g1, sc1, sh1, w_scan, w_nsa, w_gm, (cos_t, sin_t), tt=steps)

        a_re, a_im, bb_re, bb_im = _s5_params(s5_lambda_re[li], s5_lambda_im[li], s5_log_step[li], s5_b_re[li], s5_b_im[li])
        to_bd = lambda m: _block_diag(jnp.swapaxes(m.reshape(groups, S5_STATE, S5_GROUP), 1, 2))
        s5b = jnp.concatenate([to_bd(bb_re), to_bd(bb_im)], axis=1).astype(BF16)
        c_bd = lambda m: _block_diag(jnp.swapaxes(m, 1, 2))
        s5c = jnp.concatenate([c_bd(s5_c_re[li]), -c_bd(s5_c_im[li])], axis=0).astype(BF16)
        r1 = lambda a: a.reshape(1, -1)
        y_scan = _scan_mixers(
            z_scan, bsz, s5b, s5c, r1(a_re), r1(a_im), r1(s5_d[li]), s5_w_glu[li].astype(BF16),
            lru_conv_w[li], r1(lru_conv_b[li]), _block_diag(lru_w_a[li]).astype(BF16), r1(lru_b_a[li]),
            _block_diag(lru_w_x[li]).astype(BF16), r1(lru_b_x[li]), r1(lru_lambda[li]), sc_conv_w[li], steps=2 * steps)

        k_cmp, v_cmp = _compress(kcvc, nsa_pe_k[li], nsa_pe_v[li],
                                 nsa_cmp_k_w1[li], nsa_cmp_k_w2[li], nsa_cmp_v_w1[li], nsa_cmp_v_w2[li])
        o_cmp, qa = _cmp_select(q_gn, k_cmp, jnp.swapaxes(v_cmp, 1, 2).astype(BF16), overlap_t, heads=heads, tq=4 * tq)
        tk = min(512, seq)
        ones_pad = jnp.concatenate([jnp.ones((bsz, seq, 1), BF16), jnp.zeros((bsz, seq, SUBLANES - 1), BF16)], axis=2)
        tiles_t = lambda a, n: jnp.swapaxes(
            jnp.concatenate([a, ones_pad], axis=2).reshape(bsz, seq // n, n, HEAD_DIM + SUBLANES), 2, 3)
        y_b = _attend(qa, ka.reshape(bsz, seq // tk, tk, LANES), tiles_t(vsw[:, :, :HEAD_DIM], tk),
                      kwa.reshape(bsz, seq // tq, tq, LANES), tiles_t(vsw[:, :, HEAD_DIM:], tq),
                      q_gn, (heads * HEAD_DIM) // LANES, o_cmp, tq=tq, tk=tk)

        x = _merge(y_scan, y_b, g_merge, x, w_branch[li].astype(BF16), w_out[li].astype(BF16), gt1, tt=steps)

        final_g = final_norm_g if li == depth - 1 else None
        j = li // 2
        if li % 2 == 0:
            weights = tuple(_to_bf16(wt[j:j + 1])[0] for wt in (ffn_w_gate, ffn_w_up, ffn_w_down))
            x = _ffn(x, norm_ffn_g[li], sc2, sh2, gt2, weights, final_g=final_g, tm=tm)
        else:
            rw = jnp.pad(moe_router_w[j], ((0, 0), (0, LANES - N_EXPERTS)))
            rb = jnp.pad(moe_router_b[j], (0, LANES - N_EXPERTS), constant_values=NEG_INF).reshape(1, LANES)
            x = _moe(x, norm_ffn_g[li], sc2, sh2, gt2, rw, rb, _to_bf16(moe_w_gate[j]), _to_bf16(moe_w_up[j]),
                     _to_bf16(moe_w_down[j]), final_g=final_g, tm=tm, tmg=512)
    return x
```

```python
import functools
import math

import jax
import jax.numpy as jnp
import numpy as np
from jax import lax
from jax.experimental import pallas as pl
from jax.experimental.pallas import tpu as pltpu

F32 = jnp.float32
BF16 = jnp.bfloat16

N_BRANCH = 4
S5_GROUP = 16
S5_STATE = 64
HEAD_DIM = 64
L_CMP = 32
CMP_STRIDE = 16
L_SEL = 64
TOP_N = 16
WINDOW = 512
ROPE_THETA = 10000.0
FORCED_SCORE = 1.0e4
NEG_INF = -1.0e30
LRU_CONV = 4
LRU_C = 8.0
SC_CONV = 3
N_EXPERTS = 8
RMS_EPS = 1e-6
LOG2E = math.log2(math.e)

LANES = 128
SUBLANES = 8
MXU_WIDTH = 256
VMEM_LIMIT = 52 * 1024 * 1024

HI = lax.Precision.HIGHEST


def _cparams(sem):
    return pltpu.CompilerParams(dimension_semantics=sem, vmem_limit_bytes=VMEM_LIMIT)


def _dot(a, b, precision=None):
    return jnp.dot(a, b, preferred_element_type=F32, precision=precision)


def _dot_t(a, b, precision=None):
    return lax.dot_general(a, b, (((1,), (1,)), ((), ())), preferred_element_type=F32, precision=precision)


def _gelu(x):
    return 0.5 * x * (1.0 + jnp.tanh(math.sqrt(2.0 / math.pi) * (x + 0.044715 * (x * x * x))))


def _sigmoid(x):
    return 0.5 * jnp.tanh(0.5 * x) + 0.5


def _silu(x):
    hx = 0.5 * x
    return hx * jnp.tanh(hx) + hx


def _norm_mod(x, g, scale, shift):
    var = jnp.mean(x * x, axis=-1, keepdims=True)
    return (x * lax.rsqrt(var + RMS_EPS) * g) * (1.0 + scale) + shift


def _rot_half(z):
    n = z.shape[-1]
    lane = lax.broadcasted_iota(jnp.int32, z.shape, z.ndim - 1)
    first = (lane % HEAD_DIM) < (HEAD_DIM // 2)
    return jnp.where(first, pltpu.roll(z, n - HEAD_DIM // 2, z.ndim - 1), pltpu.roll(z, HEAD_DIM // 2, z.ndim - 1))


def _cast_kernel(x_ref, o_ref):
    o_ref[...] = x_ref[...].astype(o_ref.dtype)


def _to_bf16(a):
    n, r, c = a.shape
    return pl.pallas_call(
        _cast_kernel,
        out_shape=jax.ShapeDtypeStruct(a.shape, BF16),
        grid=(n,),
        in_specs=[pl.BlockSpec((1, r, c), lambda i: (i, 0, 0))],
        out_specs=pl.BlockSpec((1, r, c), lambda i: (i, 0, 0)),
        compiler_params=_cparams(("arbitrary",)),
        name="cast_bf16",
    )(a)


def _mod_kernel(c_ref, w_ref, b_ref, o_ref):
    c = c_ref[...]
    o_ref[0] = _dot(_silu(c), w_ref[0], HI) + b_ref[0]


def _modulation(c, mod_w, mod_b):
    depth, d, n = mod_w.shape
    bsz = c.shape[0]
    tn = 1024
    return pl.pallas_call(
        _mod_kernel,
        out_shape=jax.ShapeDtypeStruct((depth, bsz, n), F32),
        grid=(depth, n // tn),
        in_specs=[
            pl.BlockSpec((bsz, d), lambda l, j: (0, 0)),
            pl.BlockSpec((1, d, tn), lambda l, j: (l, 0, j)),
            pl.BlockSpec((1, 1, tn), lambda l, j: (l, 0, j)),
        ],
        out_specs=pl.BlockSpec((1, bsz, tn), lambda l, j: (l, 0, j)),
        compiler_params=_cparams(("arbitrary", "arbitrary")),
        name="modulation",
    )(c, mod_w, mod_b.reshape(depth, 1, n))


def _proj_kernel(x_ref, g_ref, sc_ref, sh_ref, ws_ref, wn_ref, wg_ref, cos_ref, sin_ref,
                 zs_ref, qg_ref, kcvc_ref, ka_ref, kwa_ref, vsw_ref, gm_ref, h_ref):
    nb, tt, _ = x_ref.shape
    rows = nb * tt
    nlc = h_ref.shape[0]
    parts = []
    for b in range(nb):
        hb = _norm_mod(x_ref[b], g_ref[...], sc_ref[b], sh_ref[b])
        for c in range(nlc):
            h_ref[c, pl.ds(b, tt, stride=nb), :] = hb[:, c * LANES:(c + 1) * LANES]
        parts.append(hb.astype(BF16))
    h_bt = jnp.concatenate(parts, axis=0)
    h_tb = jnp.concatenate([h_ref[c] for c in range(nlc)], axis=1).astype(BF16)
    for n0 in range(0, ws_ref.shape[0], 2 * MXU_WIDTH):
        zs_ref[:, n0:n0 + 2 * MXU_WIDTH] = _dot_t(h_tb, ws_ref[n0:n0 + 2 * MXU_WIDTH, :])
    cos = jnp.concatenate([cos_ref[...]] * nb, axis=0)
    sin = jnp.concatenate([sin_ref[...]] * nb, axis=0)
    group = lax.broadcasted_iota(jnp.int32, (rows, MXU_WIDTH), 1) // HEAD_DIM

    def nsa_chunk(c, roped_groups):
        z = _dot_t(h_bt, wn_ref[c * MXU_WIDTH:(c + 1) * MXU_WIDTH, :])
        if len(roped_groups) == MXU_WIDTH // HEAD_DIM:
            return z * cos + _rot_half(z) * sin
        roped = group == roped_groups[0]
        for gi in roped_groups[1:]:
            roped = roped | (group == gi)
        return jnp.where(roped, z * cos + _rot_half(z) * sin, z)

    def put(ref, val):
        ref[...] = val.astype(ref.dtype).reshape(ref.shape)

    qg_ref[:, :, 0:MXU_WIDTH] = nsa_chunk(0, (0, 1, 2, 3)).reshape(nb, tt, MXU_WIDTH)
    z1 = nsa_chunk(1, (2,))
    qg_ref[:, :, MXU_WIDTH:MXU_WIDTH + LANES] = z1[:, 0:LANES].reshape(nb, tt, LANES)
    put(kcvc_ref, z1[:, LANES:2 * LANES])
    z2 = nsa_chunk(2, (0, 2))
    ks_vw, kw_vs = z2[:, 0:LANES], z2[:, LANES:2 * LANES]
    lane = lax.broadcasted_iota(jnp.int32, (rows, LANES), 1)
    key_lanes = lane < HEAD_DIM
    t = pl.program_id(0) * tt + lax.broadcasted_iota(jnp.int32, (rows, LANES), 0) % tt
    block_onehot = (lane == HEAD_DIM + t // L_SEL).astype(F32)
    put(ka_ref, jnp.where(key_lanes, ks_vw, block_onehot))
    put(kwa_ref, jnp.where(key_lanes, kw_vs, 0.0))
    put(vsw_ref, jnp.where(key_lanes, pltpu.roll(kw_vs, HEAD_DIM, 1), ks_vw))
    for n0 in range(0, wg_ref.shape[0], 2 * MXU_WIDTH):
        z = _dot_t(h_bt, wg_ref[n0:n0 + 2 * MXU_WIDTH, :])
        gm_ref[:, :, n0:n0 + 2 * MXU_WIDTH] = z.astype(gm_ref.dtype).reshape(nb, tt, 2 * MXU_WIDTH)


def _project(x, g, scale, shift, w_scan, w_nsa, w_gm, rope_tabs, *, tt):
    bsz, seq, d = x.shape
    ns, nn, ng = w_scan.shape[0], w_nsa.shape[0], w_gm.shape[0]
    full = lambda a: pl.BlockSpec(a.shape, lambda i: (0,) * a.ndim)
    tok = lambda n, dt: jax.ShapeDtypeStruct((bsz, seq, n), dt)
    tok_spec = lambda n: pl.BlockSpec((bsz, tt, n), lambda i: (0, i, 0))
    return pl.pallas_call(
        _proj_kernel,
        out_shape=(
            jax.ShapeDtypeStruct((seq * bsz, ns), F32),
            tok(MXU_WIDTH + LANES, F32), tok(LANES, F32), tok(LANES, BF16), tok(LANES, BF16), tok(LANES, BF16),
            tok(ng, BF16),
        ),
        grid=(seq // tt,),
        in_specs=[
            pl.BlockSpec((bsz, tt, d), lambda i: (0, i, 0)),
            pl.BlockSpec((1, d), lambda i: (0, 0)),
            pl.BlockSpec((bsz, 1, d), lambda i: (0, 0, 0)),
            pl.BlockSpec((bsz, 1, d), lambda i: (0, 0, 0)),
            full(w_scan), full(w_nsa), full(w_gm),
            pl.BlockSpec((tt, MXU_WIDTH), lambda i: (i, 0)),
            pl.BlockSpec((tt, MXU_WIDTH), lambda i: (i, 0)),
        ],
        out_specs=(
            pl.BlockSpec((tt * bsz, ns), lambda i: (i, 0)),
            tok_spec(MXU_WIDTH + LANES), tok_spec(LANES), tok_spec(LANES), tok_spec(LANES), tok_spec(LANES),
            tok_spec(ng),
        ),
        scratch_shapes=[pltpu.VMEM((d // LANES, tt * bsz, LANES), F32)],
        compiler_params=_cparams(("arbitrary",)),
        name="proj_in",
    )(x, g.reshape(1, d), scale, shift, w_scan, w_nsa, w_gm, *rope_tabs)


def _s5_param_kernel(lre_ref, lim_ref, ls_ref, bre_ref, bim_ref, are_ref, aim_ref, bbre_ref, bbim_ref):
    lre = lre_ref[...]
    lim = lim_ref[...]
    step = jnp.exp(ls_ref[...])
    mag = jnp.exp(lre * step)
    ar = mag * jnp.cos(lim * step)
    ai = mag * jnp.sin(lim * step)
    xr = ar - 1.0
    den = lre * lre + lim * lim
    cr = (xr * lre + ai * lim) / den
    ci = (ai * lre - xr * lim) / den
    bre = bre_ref[...]
    bim = bim_ref[...]
    are_ref[...] = ar
    aim_ref[...] = ai
    bbre_ref[...] = cr * bre - ci * bim
    bbim_ref[...] = cr * bim + ci * bre


def _s5_params(lam_re, lam_im, log_step, b_re, b_im):
    g, n = lam_re.shape
    c = b_re.shape[-1]
    rows = g * n
    col = lambda a: a.reshape(rows, 1)
    ls = jnp.broadcast_to(log_step[:, None], (g, n))
    out = pl.pallas_call(
        _s5_param_kernel,
        out_shape=(
            jax.ShapeDtypeStruct((rows, 1), F32),
            jax.ShapeDtypeStruct((rows, 1), F32),
            jax.ShapeDtypeStruct((rows, c), F32),
            jax.ShapeDtypeStruct((rows, c), F32),
        ),
        name="s5_params",
    )(col(lam_re), col(lam_im), col(ls), b_re.reshape(rows, c), b_im.reshape(rows, c))
    return out


def _scan_kernel(z_ref, s5b_ref, s5c_ref, are_ref, aim_ref, d_ref, wglu_ref,
                 cw_ref, cb_ref, wa_ref, ba_ref, wx_ref, bx_ref, lam_ref, scw_ref,
                 o_ref,
                 bu_ref, hre_ref, him_ref, xe_ref, hl_ref, pe_ref, ab_ref, bb_ref, *, nb, steps):
    c = pl.program_id(0)
    rows = nb * steps
    w = d_ref.shape[1]
    ns = are_ref.shape[1]

    @pl.when(c == 0)
    def _():
        hre_ref[...] = jnp.zeros_like(hre_ref)
        him_ref[...] = jnp.zeros_like(him_ref)
        hl_ref[...] = jnp.zeros_like(hl_ref)
        xe_ref[0:(LRU_CONV - 1) * nb, :] = jnp.zeros(((LRU_CONV - 1) * nb, w), F32)
        pe_ref[0:(SC_CONV - 1) * nb, :] = jnp.zeros(((SC_CONV - 1) * nb, w), F32)

    u = z_ref[:, 0:w]
    bu_ref[...] = _dot(u.astype(BF16), s5b_ref[...])
    a_re = jnp.broadcast_to(are_ref[...], (nb, ns))
    a_im = jnp.broadcast_to(aim_ref[...], (nb, ns))

    def s5_step(t, carry):
        h_re, h_im = carry
        r0 = pl.multiple_of(t * nb, nb)
        b_re = bu_ref[pl.ds(r0, nb), 0:ns]
        b_im = bu_ref[pl.ds(r0, nb), ns:2 * ns]
        n_re = a_re * h_re - a_im * h_im + b_re
        n_im = a_re * h_im + a_im * h_re + b_im
        bu_ref[pl.ds(r0, nb), 0:ns] = n_re
        bu_ref[pl.ds(r0, nb), ns:2 * ns] = n_im
        return n_re, n_im

    h_re, h_im = lax.fori_loop(0, steps, s5_step, (hre_ref[...], him_ref[...]))
    hre_ref[...] = h_re
    him_ref[...] = h_im
    y = _dot(bu_ref[...].astype(BF16), s5c_ref[...]) + d_ref[...] * u
    zg = _gelu(y)
    def put(branch, val):
        for k in range(w // LANES):
            o_ref[branch * (w // LANES) + k] = val[:, k * LANES:(k + 1) * LANES]

    put(0, zg * _sigmoid(_dot(zg.astype(BF16), wglu_ref[...])))

    nt = (LRU_CONV - 1) * nb
    x = z_ref[:, w:2 * w]
    xe_ref[nt:nt + rows, :] = x
    xc = cb_ref[...] + cw_ref[LRU_CONV - 1:LRU_CONV, :] * x
    for k in range(LRU_CONV - 1):
        xc = xc + cw_ref[k:k + 1, :] * xe_ref[k * nb:k * nb + rows, :]
    xe_ref[0:nt, :] = xe_ref[rows:rows + nt, :]
    xcb = xc.astype(BF16)
    r = _sigmoid(_dot(xcb, wa_ref[...]) + ba_ref[...])
    gi = _sigmoid(_dot(xcb, wx_ref[...]) + bx_ref[...])
    nl = -lam_ref[...]
    softplus = jnp.maximum(nl, 0.0) + jnp.log(1.0 + jnp.exp(-jnp.abs(nl)))
    log_a = (-LRU_C) * r * softplus
    a = jnp.exp(log_a)
    ab_ref[...] = a
    bb_ref[...] = jnp.sqrt(1.0 - a * a) * (gi * xc)

    def lru_step(t, h):
        r0 = pl.multiple_of(t * nb, nb)
        h = ab_ref[pl.ds(r0, nb), :] * h + bb_ref[pl.ds(r0, nb), :]
        bb_ref[pl.ds(r0, nb), :] = h
        return h

    hl_ref[...] = lax.fori_loop(0, steps, lru_step, hl_ref[...])
    put(1, bb_ref[...] * _gelu(z_ref[:, 2 * w:3 * w]))

    npt = (SC_CONV - 1) * nb
    p = z_ref[:, 4 * w:5 * w] * z_ref[:, 5 * w:6 * w]
    pe_ref[npt:npt + rows, :] = p
    cv = scw_ref[SC_CONV - 1:SC_CONV, :] * p
    for k in range(SC_CONV - 1):
        cv = cv + scw_ref[k:k + 1, :] * pe_ref[k * nb:k * nb + rows, :]
    pe_ref[0:npt, :] = pe_ref[rows:rows + npt, :]
    put(2, z_ref[:, 3 * w:4 * w] * cv)


def _scan_mixers(z_tm, nb, s5b, s5c, a_re, a_im, d_skip, w_glu, conv_w, conv_b, wa, ba, wx, bx, lam, sc_w, *, steps):
    total, zc = z_tm.shape
    rows = nb * steps
    w = zc // 6
    ns = a_re.shape[1]
    full = lambda a: pl.BlockSpec(a.shape, lambda c: (0,) * a.ndim)
    params = [s5b, s5c, a_re, a_im, d_skip, w_glu, conv_w, conv_b, wa, ba, wx, bx, lam, sc_w]
    return pl.pallas_call(
        functools.partial(_scan_kernel, nb=nb, steps=steps),
        out_shape=jax.ShapeDtypeStruct((3 * w // LANES, total, LANES), F32),
        grid=(total // rows,),
        in_specs=[pl.BlockSpec((rows, zc), lambda c: (c, 0))] + [full(a) for a in params],
        out_specs=pl.BlockSpec((3 * w // LANES, rows, LANES), lambda c: (0, c, 0)),
        scratch_shapes=[
            pltpu.VMEM((rows, 2 * ns), F32),
            pltpu.VMEM((nb, ns), F32),
            pltpu.VMEM((nb, ns), F32),
            pltpu.VMEM((rows + (LRU_CONV - 1) * nb, w), F32),
            pltpu.VMEM((nb, w), F32),
            pltpu.VMEM((rows + (SC_CONV - 1) * nb, w), F32),
            pltpu.VMEM((rows, w), F32),
            pltpu.VMEM((rows, w), F32),
        ],
        compiler_params=_cparams(("arbitrary",)),
        name="scan_mixers",
    )(z_tm, *params)


def _compress_kernel(kv_ref, pek_ref, pev_ref, wk1_ref, w1p_ref, wk2_ref, wv1_ref, wv2_ref, ko_ref, vo_ref):
    nblk = ko_ref.shape[1]
    hidden = wk1_ref.shape[1]
    first = None
    second = None
    for l in range(CMP_STRIDE):
        xl = kv_ref[0, pl.ds(l, nblk, stride=CMP_STRIDE), :]
        t1 = _dot(xl, w1p_ref[l], HI)
        t2 = _dot(xl, w1p_ref[CMP_STRIDE + l], HI)
        first = t1 if first is None else first + t1
        second = t2 if second is None else second + t2
    both = first + pltpu.roll(second, nblk - 1, 0)

    def phi(part, pe, w1_ref, w2_ref):
        return _dot(_gelu(part + _dot(pe, w1_ref[...], HI)), w2_ref[...], HI)

    ko_ref[0] = phi(both[:, 0:hidden], pek_ref[...], wk1_ref, wk2_ref)
    vo_ref[0] = phi(both[:, hidden:2 * hidden], pev_ref[...], wv1_ref, wv2_ref)


def _compress(kcvc, pe_k, pe_v, wk1, wk2, wv1, wv2):
    bsz, seq, kvw = kcvc.shape
    nblk = seq // CMP_STRIDE
    hd = wk2.shape[1]
    hidden = wk1.shape[1]
    full = lambda a: pl.BlockSpec(a.shape, lambda b: (0,) * a.ndim)
    zeros = jnp.zeros((L_CMP, HEAD_DIM, hidden), wk1.dtype)
    w1p = jnp.concatenate([
        jnp.concatenate([wk1.reshape(L_CMP, HEAD_DIM, hidden), zeros], axis=2),
        jnp.concatenate([zeros, wv1.reshape(L_CMP, HEAD_DIM, hidden)], axis=2),
    ], axis=1)
    params = [pe_k.reshape(1, -1), pe_v.reshape(1, -1), wk1, w1p, wk2, wv1, wv2]
    oblk = pl.BlockSpec((1, nblk, hd), lambda b: (b, 0, 0))
    return pl.pallas_call(
        _compress_kernel,
        out_shape=(jax.ShapeDtypeStruct((bsz, nblk, hd), F32),) * 2,
        grid=(bsz,),
        in_specs=[pl.BlockSpec((1, seq, kvw), lambda b: (b, 0, 0))] + [full(a) for a in params],
        out_specs=(oblk, oblk),
        compiler_params=_cparams(("arbitrary",)),
        name="nsa_compress",
    )(kcvc, *params)


def _split_bf16(a):
    hi = a.astype(BF16)
    return hi, (a - hi.astype(F32)).astype(BF16)


def _cmp_kernel(q_ref, kc_ref, vct_ref, ovt_ref, ocmp_ref, qa_ref, *, heads, tq):
    i = pl.program_id(1)
    q_t = (q_ref[0] * HEAD_DIM ** -0.5).T
    kc_hi, kc_lo = _split_bf16(kc_ref[0])
    vct = vct_ref[0]
    nblk = kc_hi.shape[0]
    t = i * tq + lax.broadcasted_iota(jnp.int32, (nblk, tq), 1)
    n_id = lax.broadcasted_iota(jnp.int32, (nblk, tq), 0)
    visible = (n_id * CMP_STRIDE + (L_CMP - 1)) <= t
    psum = jnp.zeros((nblk, tq), F32)
    for h in range(heads):
        q_hi, q_lo = _split_bf16(q_t[h * HEAD_DIM:(h + 1) * HEAD_DIM, :])
        s = _dot(kc_hi, q_hi) + (_dot(kc_hi, q_lo) + _dot(kc_lo, q_hi))
        s = jnp.where(visible, s, NEG_INF)
        m = jnp.max(s, axis=0, keepdims=True)
        e = jnp.where(visible, jnp.exp(s - m), 0.0)
        den = jnp.sum(e, axis=0, keepdims=True)
        p = e / jnp.where(den > 0.0, den, 1.0)
        psum = psum + p
        ocmp_ref[0, h * HEAD_DIM:(h + 1) * HEAD_DIM, :] = _dot(vct, p.astype(BF16))

    p_hi, p_lo = _split_bf16(psum)
    imp_t = _dot(ovt_ref[...], p_hi) + _dot(ovt_ref[...], p_lo)
    nslot = ovt_ref.shape[0]
    tt = i * tq + lax.broadcasted_iota(jnp.int32, (nslot, tq), 1)
    j = lax.broadcasted_iota(jnp.int32, (nslot, tq), 0)
    cur = tt // L_SEL
    forced = (j == 0) | (j == cur) | (j == cur - 1)
    future = (j * L_SEL) > tt
    imp_t = jnp.where(future, -1.0, jnp.where(forced, FORCED_SCORE, imp_t))
    nreal = qa_ref.shape[2] - HEAD_DIM
    cand = imp_t[0:nreal, :]
    jrow = lax.broadcasted_iota(jnp.int32, (nreal, tq), 0)
    bias_t = jnp.full((nreal, tq), NEG_INF, F32)
    for _ in range(TOP_N):
        best = jnp.max(cand, axis=0, keepdims=True)
        first = jnp.min(jnp.where(cand == best, jrow, nreal), axis=0, keepdims=True)
        hit = jrow == first
        bias_t = jnp.where(hit, 0.0, bias_t)
        cand = jnp.where(hit, -2.0, cand)
    for h in range(heads):
        qa_ref[0, h] = jnp.concatenate([q_t[h * HEAD_DIM:(h + 1) * HEAD_DIM, :] * LOG2E, bias_t], axis=0).astype(BF16)


def _cmp_select(q, k_cmp, v_cmp_t, overlap_t, *, heads, tq):
    bsz, seq, _ = q.shape
    qd = heads * HEAD_DIM
    nblk = k_cmp.shape[1]
    nsel = LANES - HEAD_DIM
    return pl.pallas_call(
        functools.partial(_cmp_kernel, heads=heads, tq=tq),
        out_shape=(
            jax.ShapeDtypeStruct((bsz, qd, seq), F32),
            jax.ShapeDtypeStruct((bsz, heads, HEAD_DIM + nsel, seq), BF16),
        ),
        grid=(bsz, seq // tq),
        in_specs=[
            pl.BlockSpec((1, tq, qd), lambda b, i: (b, i, 0)),
            pl.BlockSpec((1, nblk, HEAD_DIM), lambda b, i: (b, 0, 0)),
            pl.BlockSpec((1, HEAD_DIM, nblk), lambda b, i: (b, 0, 0)),
            pl.BlockSpec(overlap_t.shape, lambda b, i: (0, 0)),
        ],
        out_specs=(
            pl.BlockSpec((1, qd, tq), lambda b, i: (b, 0, i)),
            pl.BlockSpec((1, heads, HEAD_DIM + nsel, tq), lambda b, i: (b, 0, 0, i)),
        ),
        compiler_params=_cparams(("arbitrary", "arbitrary")),
        name="nsa_cmp_select",
    )(q, k_cmp, v_cmp_t, overlap_t)


def _attend_kernel(qa_ref, ka_ref, vst_ref, kwa_ref, vwt_ref, gn_ref, ocmp_ref, o_ref,
                   m_ref, acc_ref, s0_ref, s1_ref, *, heads, tq, tk, nwin):
    i = pl.program_id(1)
    cols = heads * tq
    qat = jnp.concatenate([qa_ref[0, h] for h in range(heads)], axis=1)
    t_row = i * tq + (lax.broadcasted_iota(jnp.int32, (1, cols), 1) % tq)

    m_ref[...] = jnp.full(m_ref.shape, NEG_INF, F32)
    acc_ref[...] = jnp.zeros(acc_ref.shape, F32)
    key_id = lax.broadcasted_iota(jnp.int32, (tk, cols), 0)
    n_full = (i * tq + 1) // tk
    s0_ref[...] = _dot(ka_ref[0, 0], qat)

    st = jnp.maximum(i + 1 - nwin, 0)
    kw = kwa_ref[0, pl.ds(st, nwin)].reshape(nwin * tq, kwa_ref.shape[3])
    s = _dot(kw, qat)
    dist = t_row - (st * tq + lax.broadcasted_iota(jnp.int32, (nwin * tq, cols), 0))
    s = jnp.where((dist >= 0) & (dist < WINDOW), s, NEG_INF)
    m = jnp.max(s, axis=0, keepdims=True)
    pb = jnp.exp2(s - m).astype(BF16)
    ow = _dot(vwt_ref[0, st], pb[0:tq, :])
    for j in range(1, nwin):
        ow = ow + _dot(vwt_ref[0, st + j], pb[j * tq:(j + 1) * tq, :])
    owin = ow[0:HEAD_DIM, :] / ow[HEAD_DIM:HEAD_DIM + 1, :]

    def softmax_tile(kt, s, masked):
        if masked:
            s = jnp.where(kt * tk + key_id <= t_row, s, NEG_INF)
        m_old = m_ref[...]
        m_new = jnp.maximum(m_old, jnp.max(s, axis=0, keepdims=True))
        p = jnp.exp2(s - m_new).astype(BF16)
        acc_ref[...] = jnp.exp2(m_old - m_new) * acc_ref[...] + _dot(vst_ref[0, kt], p)
        m_ref[...] = m_new

    def pair_body(j, carry):
        s1_ref[...] = _dot(ka_ref[0, 2 * j + 1], qat)
        softmax_tile(2 * j, s0_ref[...], False)
        s0_ref[...] = _dot(ka_ref[0, 2 * j + 2], qat)
        softmax_tile(2 * j + 1, s1_ref[...], False)
        return carry

    n_pair = n_full // 2
    lax.fori_loop(0, n_pair, pair_body, 0)
    s1_ref[...] = _dot(ka_ref[0, 2 * n_pair + 1], qat)
    softmax_tile(2 * n_pair, s0_ref[...], True)
    softmax_tile(2 * n_pair + 1, s1_ref[...], True)

    osel = acc_ref[0:HEAD_DIM, :] / acc_ref[HEAD_DIM:HEAD_DIM + 1, :]

    g_t = _sigmoid(gn_ref[0]).T
    outs = []
    for h in range(heads):
        cs = slice(h * tq, (h + 1) * tq)
        outs.append(g_t[3 * h:3 * h + 1, :] * ocmp_ref[0, h * HEAD_DIM:(h + 1) * HEAD_DIM, :]
                    + g_t[3 * h + 1:3 * h + 2, :] * osel[:, cs]
                    + g_t[3 * h + 2:3 * h + 3, :] * owin[:, cs])
    o_ref[0] = jnp.concatenate(outs, axis=0).T


def _attend(qa, ka, vst, kwa, vwt, z_nsa, gn_block, o_cmp_t, *, tq, tk):
    bsz, heads, ka_w, seq = qa.shape
    qd = heads * HEAD_DIM
    nwin = WINDOW // tq + 1
    return pl.pallas_call(
        functools.partial(_attend_kernel, heads=heads, tq=tq, tk=tk, nwin=nwin),
        out_shape=jax.ShapeDtypeStruct((bsz, seq, qd), F32),
        grid=(bsz, seq // tq),
        in_specs=[
            pl.BlockSpec((1, heads, ka_w, tq), lambda b, i: (b, 0, 0, i)),
            pl.BlockSpec((1,) + ka.shape[1:], lambda b, i: (b, 0, 0, 0)),
            pl.BlockSpec((1,) + vst.shape[1:], lambda b, i: (b, 0, 0, 0)),
            pl.BlockSpec((1,) + kwa.shape[1:], lambda b, i: (b, 0, 0, 0)),
            pl.BlockSpec((1,) + vwt.shape[1:], lambda b, i: (b, 0, 0, 0)),
            pl.BlockSpec((1, tq, LANES), lambda b, i: (b, i, gn_block)),
            pl.BlockSpec((1, qd, tq), lambda b, i: (b, 0, i)),
        ],
        out_specs=pl.BlockSpec((1, tq, qd), lambda b, i: (b, i, 0)),
        scratch_shapes=[
            pltpu.VMEM((1, heads * tq), F32),
            pltpu.VMEM((vst.shape[2], heads * tq), F32),
            pltpu.VMEM((tk, heads * tq), F32),
            pltpu.VMEM((tk, heads * tq), F32),
        ],
        compiler_params=_cparams(("arbitrary", "arbitrary")),
        name="nsa_attend",
    )(qa, ka, vst, kwa, vwt, z_nsa, o_cmp_t)


def _merge_kernel(ys_ref, yb_ref, gm_ref, x_ref, wb_ref, wo_ref, gate_ref, o_ref, m_ref):
    nb, tt, w = yb_ref.shape
    d = x_ref.shape[2]
    rows = nb * tt
    def slab(c):
        return jnp.concatenate([ys_ref[c, pl.ds(b, tt, stride=nb), :] for b in range(nb)], axis=0)

    ys = jnp.concatenate([slab(c) for c in range(ys_ref.shape[0])], axis=1)
    branches = (ys[:, 0:w], yb_ref[...].reshape(rows, w), ys[:, w:2 * w], ys[:, 2 * w:3 * w])
    ybs = [y.astype(BF16) for y in branches]
    for n0 in range(0, d, MXU_WIDTH):
        plain = None
        gated = None
        for mi, y in enumerate(ybs):
            p = _dot(y, wb_ref[mi, :, n0:n0 + MXU_WIDTH])
            gm = gm_ref[:, :, mi * d + n0:mi * d + n0 + MXU_WIDTH].reshape(rows, MXU_WIDTH)
            t = jnp.tanh(gm.astype(F32)) * p
            plain = p if plain is None else plain + p
            gated = t if gated is None else gated + t
        m_ref[:, n0:n0 + MXU_WIDTH] = (0.5 * (plain + gated)).astype(BF16)
    mix = _dot(m_ref[...], wo_ref[...]).reshape(nb, tt, d)
    o_ref[...] = x_ref[...] + gate_ref[...] * mix


def _merge(y_scan_tm, y_b, g_merge, x, w_branch, w_out, gate, *, tt):
    bsz, seq, d = x.shape
    w = y_b.shape[2]
    tile = lambda n: pl.BlockSpec((bsz, tt, n), lambda i: (0, i, 0))
    return pl.pallas_call(
        _merge_kernel,
        out_shape=jax.ShapeDtypeStruct((bsz, seq, d), F32),
        grid=(seq // tt,),
        in_specs=[
            pl.BlockSpec((3 * w // LANES, tt * bsz, LANES), lambda i: (0, i, 0)),
            tile(w),
            tile(N_BRANCH * d),
            tile(d),
            pl.BlockSpec(w_branch.shape, lambda i: (0, 0, 0)),
            pl.BlockSpec(w_out.shape, lambda i: (0, 0)),
            pl.BlockSpec((bsz, 1, d), lambda i: (0, 0, 0)),
        ],
        out_specs=tile(d),
        scratch_shapes=[pltpu.VMEM((tt * bsz, d), BF16)],
        compiler_params=_cparams(("arbitrary",)),
        name="merge_out",
    )(y_scan_tm, y_b, g_merge, x, w_branch, w_out, gate)


def _final_norm(out, fg_ref):
    var = jnp.mean(out * out, axis=-1, keepdims=True)
    return out * lax.rsqrt(var + RMS_EPS) * fg_ref[...]


def _ffn_kernel(x_ref, g_ref, sc_ref, sh_ref, gate_ref, wg_ref, wu_ref, wd_ref, *rest, final, fc):
    if final:
        fg_ref, o_ref, a_ref = rest
    else:
        o_ref, a_ref = rest
    hb = _norm_mod(x_ref[0], g_ref[...], sc_ref[0], sh_ref[0]).astype(BF16)
    for f0 in range(0, a_ref.shape[1], fc):
        a_ref[:, f0:f0 + fc] = (_silu(_dot(hb, wg_ref[:, f0:f0 + fc])) * _dot(hb, wu_ref[:, f0:f0 + fc])).astype(BF16)
    out = x_ref[0] + gate_ref[0] * _dot(a_ref[...], wd_ref[...])
    o_ref[0] = _final_norm(out, fg_ref) if final else out


def _ffn(x, g, scale, shift, gate, weights, *, final_g, tm):
    bsz, seq, d = x.shape
    row = lambda: pl.BlockSpec((1, 1, d), lambda b, i: (b, 0, 0))
    wg, wu, wd = weights
    ff = wg.shape[1]
    in_specs = [
        pl.BlockSpec((1, tm, d), lambda b, i: (b, i, 0)),
        pl.BlockSpec((1, d), lambda b, i: (0, 0)),
        row(), row(), row(),
        pl.BlockSpec((d, ff), lambda b, i: (0, 0)),
        pl.BlockSpec((d, ff), lambda b, i: (0, 0)),
        pl.BlockSpec((ff, d), lambda b, i: (0, 0)),
    ]
    args = [x, g.reshape(1, d), scale, shift, gate, wg, wu, wd]
    final = final_g is not None
    if final:
        in_specs.append(pl.BlockSpec((1, d), lambda b, i: (0, 0)))
        args.append(final_g.reshape(1, d))
    return pl.pallas_call(
        functools.partial(_ffn_kernel, final=final, fc=MXU_WIDTH),
        out_shape=jax.ShapeDtypeStruct((bsz, seq, d), F32),
        grid=(bsz, seq // tm),
        in_specs=in_specs,
        out_specs=pl.BlockSpec((1, tm, d), lambda b, i: (b, i, 0)),
        scratch_shapes=[pltpu.VMEM((tm, ff), BF16)],
        compiler_params=_cparams(("arbitrary", "arbitrary")),
        name="dense_ffn",
    )(*args)


def _route_kernel(x_ref, g_ref, sc_ref, sh_ref, rw_ref, rb_ref, h_ref, r_ref):
    h = _norm_mod(x_ref[0], g_ref[...], sc_ref[0], sh_ref[0])
    h_hi, h_lo = _split_bf16(h)
    h_ref[0] = h_hi
    w_hi, w_lo = _split_bf16(rw_ref[...])
    logits = _dot(h_hi, w_hi) + (_dot(h_hi, w_lo) + _dot(h_lo, w_hi)) + rb_ref[...]
    lane = lax.broadcasted_iota(jnp.int32, logits.shape, 1)
    big = logits.shape[1]
    v1 = jnp.max(logits, axis=-1, keepdims=True)
    i1 = jnp.min(jnp.where(logits == v1, lane, big), axis=-1, keepdims=True)
    rest_l = jnp.where(lane == i1, NEG_INF * 4.0, logits)
    v2 = jnp.max(rest_l, axis=-1, keepdims=True)
    i2 = jnp.min(jnp.where(rest_l == v2, lane, big), axis=-1, keepdims=True)
    e2 = jnp.exp(v2 - v1)
    w1 = 1.0 / (1.0 + e2)
    w2 = e2 / (1.0 + e2)
    r_ref[0] = jnp.where(lane == 0, i1.astype(F32), jnp.where(lane == 1, i2.astype(F32), jnp.where(lane == 2, w1, w2)))


def _route(x, g, scale, shift, rw, rb, *, tm):
    bsz, seq, d = x.shape
    row = lambda: pl.BlockSpec((1, 1, d), lambda b, i: (b, 0, 0))
    return pl.pallas_call(
        _route_kernel,
        out_shape=(jax.ShapeDtypeStruct((bsz, seq, d), BF16), jax.ShapeDtypeStruct((bsz, seq, LANES), F32)),
        grid=(bsz, seq // tm),
        in_specs=[
            pl.BlockSpec((1, tm, d), lambda b, i: (b, i, 0)),
            pl.BlockSpec((1, d), lambda b, i: (0, 0)),
            row(), row(),
            pl.BlockSpec(rw.shape, lambda b, i: (0, 0)),
            pl.BlockSpec(rb.shape, lambda b, i: (0, 0)),
        ],
        out_specs=(pl.BlockSpec((1, tm, d), lambda b, i: (b, i, 0)), pl.BlockSpec((1, tm, LANES), lambda b, i: (b, i, 0))),
        compiler_params=_cparams(("arbitrary", "arbitrary")),
        name="moe_route",
    )(x, g.reshape(1, d), scale, shift, rw, rb)


def _gffn_kernel(te_ref, nu_ref, x_ref, wg_ref, wu_ref, wd_ref, o_ref):
    i = pl.program_id(0)

    @pl.when(i < nu_ref[0])
    def _():
        hb = x_ref[...]
        a = (_silu(_dot(hb, wg_ref[0])) * _dot(hb, wu_ref[0])).astype(BF16)
        o_ref[...] = _dot(a, wd_ref[0]).astype(o_ref.dtype)

    @pl.when(i >= nu_ref[0])
    def _():
        o_ref[...] = jnp.zeros_like(o_ref)


def _grouped_ffn(xs, tile_expert, n_used, wg, wu, wd, *, tmg):
    rows, d = xs.shape
    _, _, fe = wg.shape
    return pl.pallas_call(
        _gffn_kernel,
        out_shape=jax.ShapeDtypeStruct((rows, d), BF16),
        grid_spec=pltpu.PrefetchScalarGridSpec(
            num_scalar_prefetch=2,
            grid=(rows // tmg,),
            in_specs=[
                pl.BlockSpec((tmg, d), lambda i, te, nu: (i, 0)),
                pl.BlockSpec((1, d, fe), lambda i, te, nu: (te[i], 0, 0)),
                pl.BlockSpec((1, d, fe), lambda i, te, nu: (te[i], 0, 0)),
                pl.BlockSpec((1, fe, d), lambda i, te, nu: (te[i], 0, 0)),
            ],
            out_specs=pl.BlockSpec((tmg, d), lambda i, te, nu: (i, 0)),
        ),
        compiler_params=_cparams(("arbitrary",)),
        name="moe_grouped_ffn",
    )(tile_expert, n_used, xs, wg, wu, wd)


def _combine_kernel(x_ref, gate_ref, y_ref, y2_ref, r_ref, *rest, final):
    if final:
        fg_ref, o_ref = rest
    else:
        (o_ref,) = rest
    r = r_ref[0]
    f = r[:, 2:3] * y_ref[0].astype(F32) + r[:, 3:4] * y2_ref[0].astype(F32)
    out = x_ref[0] + gate_ref[0] * f
    o_ref[0] = _final_norm(out, fg_ref) if final else out


def _combine(x, gate, y1, y2, route, *, final_g, tm):
    bsz, seq, d = x.shape
    tile = lambda: pl.BlockSpec((1, tm, d), lambda b, i: (b, i, 0))
    in_specs = [
        tile(),
        pl.BlockSpec((1, 1, d), lambda b, i: (b, 0, 0)),
        tile(), tile(),
        pl.BlockSpec((1, tm, LANES), lambda b, i: (b, i, 0)),
    ]
    args = [x, gate, y1, y2, route]
    final = final_g is not None
    if final:
        in_specs.append(pl.BlockSpec((1, d), lambda b, i: (0, 0)))
        args.append(final_g.reshape(1, d))
    return pl.pallas_call(
        functools.partial(_combine_kernel, final=final),
        out_shape=jax.ShapeDtypeStruct((bsz, seq, d), F32),
        grid=(bsz, seq // tm),
        in_specs=in_specs,
        out_specs=pl.BlockSpec((1, tm, d), lambda b, i: (b, i, 0)),
        compiler_params=_cparams(("arbitrary", "arbitrary")),
        name="moe_combine",
    )(*args)


def _src_kernel(te_ref, k0_ref, jlo_ref, jhi_ref, c_ref, o_ref, cnt_ref, *, nblk, tb, last):
    i = pl.program_id(0)
    tmg = o_ref.shape[0]
    base = te_ref[i] * nblk
    kcol = k0_ref[i] + lax.broadcasted_iota(jnp.int32, (tmg, 1), 0)
    cnt_ref[...] = jnp.zeros_like(cnt_ref)

    def blk(j, carry):
        hit = (c_ref[base + j] <= kcol).astype(jnp.int32)
        part = hit[:, 0:LANES]
        for q in range(1, tb // LANES):
            part = part + hit[:, q * LANES:(q + 1) * LANES]
        cnt_ref[...] += part
        return carry

    lax.fori_loop(jlo_ref[i], jhi_ref[i], blk, 0)
    total = jnp.sum(cnt_ref[...], axis=1, keepdims=True) + jlo_ref[i] * tb
    o_ref[...] = jnp.minimum(total, last)


def _moe(x, g, scale, shift, gate, rw, rb, wg, wu, wd, *, final_g, tm, tmg):
    bsz, seq, d = x.shape
    ne = wg.shape[0]
    tokens = bsz * seq
    tb = 512
    nblk = tokens // tb
    h, route = _route(x, g, scale, shift, rw, rb, tm=tm)
    ar = jnp.arange(ne, dtype=jnp.int32)[:, None]
    m1 = route[:, :, 0].astype(jnp.int32).reshape(1, tokens) == ar
    m2 = route[:, :, 1].astype(jnp.int32).reshape(1, tokens) == ar
    count = jnp.cumsum((m1 | m2).astype(jnp.int32), axis=1)
    padded = ((count[:, -1] + tmg - 1) // tmg) * tmg
    ends = jnp.cumsum(padded)
    offs = ends - padded
    row_of = count - 1 + offs[:, None]
    dest1 = jnp.sum(jnp.where(m1, row_of, 0), axis=0)
    dest2 = jnp.sum(jnp.where(m2, row_of, 0), axis=0)
    rows = tokens * 2 + ne * tmg
    ntiles = rows // tmg
    tile_start = jnp.arange(ntiles, dtype=jnp.int32) * tmg
    tile_expert = jnp.minimum(jnp.sum((tile_start[:, None] >= ends[None, :]).astype(jnp.int32), axis=1), ne - 1)
    n_used = (ends[-1] // tmg).astype(jnp.int32).reshape(1)
    k0 = tile_start - offs[tile_expert]
    cend = count[:, tb - 1::tb]
    cstart = jnp.concatenate([jnp.zeros((ne, 1), jnp.int32), cend[:, :-1]], axis=1)
    jlo = jnp.sum((cend[tile_expert] <= k0[:, None]).astype(jnp.int32), axis=1)
    jhi = jnp.sum((cstart[tile_expert] <= (k0 + tmg - 1)[:, None]).astype(jnp.int32), axis=1)
    src = pl.pallas_call(
        functools.partial(_src_kernel, nblk=nblk, tb=tb, last=tokens - 1),
        out_shape=jax.ShapeDtypeStruct((rows, 1), jnp.int32),
        grid_spec=pltpu.PrefetchScalarGridSpec(
            num_scalar_prefetch=4,
            grid=(ntiles,),
            in_specs=[pl.BlockSpec((ne * nblk, 1, tb), lambda i, *_: (0, 0, 0))],
            out_specs=pl.BlockSpec((tmg, 1), lambda i, *_: (i, 0)),
            scratch_shapes=[pltpu.VMEM((tmg, LANES), jnp.int32)],
        ),
        compiler_params=_cparams(("arbitrary",)),
        name="moe_sources",
    )(tile_expert, k0, jlo, jhi, count.reshape(ne * nblk, 1, tb))
    rows_of = lambda a, idx: a.at[idx].get(mode="promise_in_bounds")
    xs = rows_of(h.reshape(tokens, d), src[:, 0])
    ys = _grouped_ffn(xs, tile_expert, n_used, wg, wu, wd, tmg=tmg)
    y12 = rows_of(ys, jnp.concatenate([dest1, dest2])).reshape(2, bsz, seq, d)
    return _combine(x, gate, y12[0], y12[1], route, final_g=final_g, tm=tm)


def _block_diag(blocks):
    n, r, c = blocks.shape
    eye = jnp.eye(n, dtype=blocks.dtype)
    return (eye[:, None, :, None] * blocks[:, :, None, :]).reshape(n * r, n * c)


def _rope_tables(seq):
    inv = ROPE_THETA ** (-jnp.arange(0, HEAD_DIM, 2, dtype=F32) / HEAD_DIM)
    ang = jnp.arange(seq, dtype=F32)[:, None] * inv[None, :]
    cos, sin = jnp.cos(ang), jnp.sin(ang)
    reps = MXU_WIDTH // HEAD_DIM
    return jnp.concatenate([cos, cos] * reps, axis=1), jnp.concatenate([-sin, sin] * reps, axis=1)


def kernel(x, c, mod_w, mod_b, norm_mix_g, norm_ffn_g, w_in, s5_lambda_re, s5_lambda_im, s5_log_step, s5_b_re, s5_b_im, s5_c_re, s5_c_im, s5_d, s5_w_glu, nsa_pe_k, nsa_pe_v, nsa_cmp_k_w1, nsa_cmp_k_w2, nsa_cmp_v_w1, nsa_cmp_v_w2, lru_conv_w, lru_conv_b, lru_w_a, lru_b_a, lru_w_x, lru_b_x, lru_lambda, sc_conv_w, w_branch, w_out, ffn_w_gate, ffn_w_up, ffn_w_down, moe_router_w, moe_router_b, moe_w_gate, moe_w_up, moe_w_down, final_norm_g):
    bsz, seq, d = x.shape
    depth = mod_w.shape[0]
    w = d // N_BRANCH
    heads = w // HEAD_DIM
    groups = w // S5_GROUP
    assert bsz == SUBLANES, "time-major scan blocks hold one timestep per sublane tile"
    tm = min(512, seq)
    tq = 128
    steps = min(64, seq)
    nsel = seq // L_SEL
    assert TOP_N <= nsel <= LANES - HEAD_DIM and seq % tm == 0 and seq >= WINDOW + tq

    mod = _modulation(c, mod_w, mod_b)
    mod = mod.reshape(depth, bsz, 6, 1, d)

    sizes = (w, w, HEAD_DIM, HEAD_DIM, HEAD_DIM, HEAD_DIM, HEAD_DIM, HEAD_DIM, heads * 3, w, w, w, w, w, N_BRANCH * d)
    offs = np.concatenate([[0], np.cumsum(sizes)])
    col = lambda wmat_t, k: wmat_t[offs[k]:offs[k + 1], :]

    cos_t, sin_t = _rope_tables(seq)
    n_cmp = (seq - L_CMP) // CMP_STRIDE + 1
    nblk = seq // CMP_STRIDE
    cmp_start = np.arange(nblk) * CMP_STRIDE
    sel_start = np.arange(nsel) * L_SEL
    ovl = np.clip(np.minimum(cmp_start[:, None] + L_CMP, sel_start[None, :] + L_SEL)
                  - np.maximum(cmp_start[:, None], sel_start[None, :]), 0, None) / L_CMP
    ovl[n_cmp:] = 0.0
    overlap_t = jnp.asarray(np.pad(ovl, ((0, 0), (0, LANES - nsel))).T, BF16)

    w_in_t = jnp.swapaxes(w_in, 1, 2)

    for li in range(depth):
        sh1, sc1, gt1, sh2, sc2, gt2 = (mod[li, :, k] for k in range(6))
        wl = w_in_t[li]
        w_scan = jnp.concatenate([col(wl, 0), col(wl, 9), col(wl, 10), col(wl, 11), col(wl, 12), col(wl, 13)], axis=0).astype(BF16)
        gpad = jnp.zeros((LANES - heads * 3, d), wl.dtype)
        w_nsa = jnp.concatenate([col(wl, 1), col(wl, 8), gpad] + [col(wl, k) for k in (2, 3, 4, 7, 6, 5)], axis=0).astype(BF16)
        w_gm = (0.5 * col(wl, 14)).astype(BF16)
        g1 = norm_mix_g[li]
        z_scan, q_gn, kcvc, ka, kwa, vsw, g_merge = _project(x, g1, sc1, sh1, w_scan, w_nsa, w_gm, (cos_t, sin_t), tt=steps)

        a_re, a_im, bb_re, bb_im = _s5_params(s5_lambda_re[li], s5_lambda_im[li], s5_log_step[li], s5_b_re[li], s5_b_im[li])
        to_bd = lambda m: _block_diag(jnp.swapaxes(m.reshape(groups, S5_STATE, S5_GROUP), 1, 2))
        s5b = jnp.concatenate([to_bd(bb_re), to_bd(bb_im)], axis=1).astype(BF16)
        c_bd = lambda m: _block_diag(jnp.swapaxes(m, 1, 2))
        s5c = jnp.concatenate([c_bd(s5_c_re[li]), -c_bd(s5_c_im[li])], axis=0).astype(BF16)
        r1 = lambda a: a.reshape(1, -1)
        y_scan = _scan_mixers(
            z_scan, bsz, s5b, s5c, r1(a_re), r1(a_im), r1(s5_d[li]), s5_w_glu[li].astype(BF16),
            lru_conv_w[li], r1(lru_conv_b[li]), _block_diag(lru_w_a[li]).astype(BF16), r1(lru_b_a[li]),
            _block_diag(lru_w_x[li]).astype(BF16), r1(lru_b_x[li]), r1(lru_lambda[li]), sc_conv_w[li], steps=2 * steps)

        k_cmp, v_cmp = _compress(kcvc, nsa_pe_k[li], nsa_pe_v[li],
                                 nsa_cmp_k_w1[li], nsa_cmp_k_w2[li], nsa_cmp_v_w1[li], nsa_cmp_v_w2[li])
        o_cmp, qa = _cmp_select(q_gn, k_cmp, jnp.swapaxes(v_cmp, 1, 2).astype(BF16), overlap_t, heads=heads, tq=4 * tq)
        tk = min(512, seq)
        ones_pad = jnp.concatenate([jnp.ones((bsz, seq, 1), BF16), jnp.zeros((bsz, seq, SUBLANES - 1), BF16)], axis=2)
        tiles_t = lambda a, n: jnp.swapaxes(
            jnp.concatenate([a, ones_pad], axis=2).reshape(bsz, seq // n, n, HEAD_DIM + SUBLANES), 2, 3)
        y_b = _attend(qa, ka.reshape(bsz, seq // tk, tk, LANES), tiles_t(vsw[:, :, :HEAD_DIM], tk),
                      kwa.reshape(bsz, seq // tq, tq, LANES), tiles_t(vsw[:, :, HEAD_DIM:], tq),
                      q_gn, (heads * HEAD_DIM) // LANES, o_cmp, tq=tq, tk=tk)

        x = _merge(y_scan, y_b, g_merge, x, w_branch[li].astype(BF16), w_out[li].astype(BF16), gt1, tt=steps)

        final_g = final_norm_g if li == depth - 1 else None
        j = li // 2
        if li % 2 == 0:
            weights = tuple(_to_bf16(wt[j:j + 1])[0] for wt in (ffn_w_gate, ffn_w_up, ffn_w_down))
            x = _ffn(x, norm_ffn_g[li], sc2, sh2, gt2, weights, final_g=final_g, tm=tm)
        else:
            rw = jnp.pad(moe_router_w[j], ((0, 0), (0, LANES - N_EXPERTS)))
            rb = jnp.pad(moe_router_b[j], (0, LANES - N_EXPERTS), constant_values=NEG_INF).reshape(1, LANES)
            x = _moe(x, norm_ffn_g[li], sc2, sh2, gt2, rw, rb, _to_bf16(moe_w_gate[j]), _to_bf16(moe_w_up[j]),
                     _to_bf16(moe_w_down[j]), final_g=final_g, tm=tm, tmg=512)
    return x
```

```python
import functools
import math

import jax
import jax.numpy as jnp
import numpy as np
from jax import lax
from jax.experimental import pallas as pl
from jax.experimental.pallas import tpu as pltpu

F32 = jnp.float32
BF16 = jnp.bfloat16

N_BRANCH = 4
S5_GROUP = 16
S5_STATE = 64
HEAD_DIM = 64
L_CMP = 32
CMP_STRIDE = 16
L_SEL = 64
TOP_N = 16
WINDOW = 512
ROPE_THETA = 10000.0
FORCED_SCORE = 1.0e4
NEG_INF = -1.0e30
LRU_CONV = 4
LRU_C = 8.0
SC_CONV = 3
N_EXPERTS = 8
RMS_EPS = 1e-6
LOG2E = math.log2(math.e)

LANES = 128
SUBLANES = 8
MXU_WIDTH = 256
VMEM_LIMIT = 52 * 1024 * 1024

HI = lax.Precision.HIGHEST


def _cparams(sem):
    return pltpu.CompilerParams(dimension_semantics=sem, vmem_limit_bytes=VMEM_LIMIT)


def _dot(a, b, precision=None):
    return jnp.dot(a, b, preferred_element_type=F32, precision=precision)


def _dot_t(a, b, precision=None):
    return lax.dot_general(a, b, (((1,), (1,)), ((), ())), preferred_element_type=F32, precision=precision)


def _gelu(x):
    return 0.5 * x * (1.0 + jnp.tanh(math.sqrt(2.0 / math.pi) * (x + 0.044715 * (x * x * x))))


def _sigmoid(x):
    return 0.5 * jnp.tanh(0.5 * x) + 0.5


def _silu(x):
    hx = 0.5 * x
    return hx * jnp.tanh(hx) + hx


def _norm_mod(x, g, scale, shift):
    var = jnp.mean(x * x, axis=-1, keepdims=True)
    return (x * lax.rsqrt(var + RMS_EPS) * g) * (1.0 + scale) + shift


def _rot_half(z):
    n = z.shape[-1]
    lane = lax.broadcasted_iota(jnp.int32, z.shape, z.ndim - 1)
    first = (lane % HEAD_DIM) < (HEAD_DIM // 2)
    return jnp.where(first, pltpu.roll(z, n - HEAD_DIM // 2, z.ndim - 1), pltpu.roll(z, HEAD_DIM // 2, z.ndim - 1))


def _cast_kernel(x_ref, o_ref):
    o_ref[...] = x_ref[...].astype(o_ref.dtype)


def _to_bf16(a):
    n, r, c = a.shape
    return pl.pallas_call(
        _cast_kernel,
        out_shape=jax.ShapeDtypeStruct(a.shape, BF16),
        grid=(n,),
        in_specs=[pl.BlockSpec((1, r, c), lambda i: (i, 0, 0))],
        out_specs=pl.BlockSpec((1, r, c), lambda i: (i, 0, 0)),
        compiler_params=_cparams(("arbitrary",)),
        name="cast_bf16",
    )(a)


def _mod_kernel(c_ref, w_ref, b_ref, o_ref):
    c = c_ref[...]
    o_ref[0] = _dot(_silu(c), w_ref[0], HI) + b_ref[0]


def _modulation(c, mod_w, mod_b):
    depth, d, n = mod_w.shape
    bsz = c.shape[0]
    tn = 1024
    return pl.pallas_call(
        _mod_kernel,
        out_shape=jax.ShapeDtypeStruct((depth, bsz, n), F32),
        grid=(depth, n // tn),
        in_specs=[
            pl.BlockSpec((bsz, d), lambda l, j: (0, 0)),
            pl.BlockSpec((1, d, tn), lambda l, j: (l, 0, j)),
            pl.BlockSpec((1, 1, tn), lambda l, j: (l, 0, j)),
        ],
        out_specs=pl.BlockSpec((1, bsz, tn), lambda l, j: (l, 0, j)),
        compiler_params=_cparams(("arbitrary", "arbitrary")),
        name="modulation",
    )(c, mod_w, mod_b.reshape(depth, 1, n))


def _proj_kernel(x_ref, g_ref, sc_ref, sh_ref, ws_ref, wn_ref, wg_ref, cos_ref, sin_ref,
                 zs_ref, qg_ref, kcvc_ref, ka_ref, kwa_ref, vsw_ref, gm_ref, h_ref):
    nb, tt, _ = x_ref.shape
    rows = nb * tt
    nlc = h_ref.shape[0]
    parts = []
    for b in range(nb):
        hb = _norm_mod(x_ref[b], g_ref[...], sc_ref[b], sh_ref[b])
        for c in range(nlc):
            h_ref[c, pl.ds(b, tt, stride=nb), :] = hb[:, c * LANES:(c + 1) * LANES]
        parts.append(hb.astype(BF16))
    h_bt = jnp.concatenate(parts, axis=0)
    h_tb = jnp.concatenate([h_ref[c] for c in range(nlc)], axis=1).astype(BF16)
    for n0 in range(0, ws_ref.shape[0], 2 * MXU_WIDTH):
        zs_ref[:, n0:n0 + 2 * MXU_WIDTH] = _dot_t(h_tb, ws_ref[n0:n0 + 2 * MXU_WIDTH, :])
    cos = jnp.concatenate([cos_ref[...]] * nb, axis=0)
    sin = jnp.concatenate([sin_ref[...]] * nb, axis=0)
    group = lax.broadcasted_iota(jnp.int32, (rows, MXU_WIDTH), 1) // HEAD_DIM

    def nsa_chunk(c, roped_groups):
        z = _dot_t(h_bt, wn_ref[c * MXU_WIDTH:(c + 1) * MXU_WIDTH, :])
        if len(roped_groups) == MXU_WIDTH // HEAD_DIM:
            return z * cos + _rot_half(z) * sin
        roped = group == roped_groups[0]
        for gi in roped_groups[1:]:
            roped = roped | (group == gi)
        return jnp.where(roped, z * cos + _rot_half(z) * sin, z)

    def put(ref, val):
        ref[...] = val.astype(ref.dtype).reshape(ref.shape)

    qg_ref[:, :, 0:MXU_WIDTH] = nsa_chunk(0, (0, 1, 2, 3)).reshape(nb, tt, MXU_WIDTH)
    z1 = nsa_chunk(1, (2,))
    qg_ref[:, :, MXU_WIDTH:MXU_WIDTH + LANES] = z1[:, 0:LANES].reshape(nb, tt, LANES)
    put(kcvc_ref, z1[:, LANES:2 * LANES])
    z2 = nsa_chunk(2, (0, 2))
    ks_vw, kw_vs = z2[:, 0:LANES], z2[:, LANES:2 * LANES]
    lane = lax.broadcasted_iota(jnp.int32, (rows, LANES), 1)
    key_lanes = lane < HEAD_DIM
    t = pl.program_id(0) * tt + lax.broadcasted_iota(jnp.int32, (rows, LANES), 0) % tt
    block_onehot = (lane == HEAD_DIM + t // L_SEL).astype(F32)
    put(ka_ref, jnp.where(key_lanes, ks_vw, block_onehot))
    put(kwa_ref, jnp.where(key_lanes, kw_vs, 0.0))
    put(vsw_ref, jnp.where(key_lanes, pltpu.roll(kw_vs, HEAD_DIM, 1), ks_vw))
    for n0 in range(0, wg_ref.shape[0], 2 * MXU_WIDTH):
        z = _dot_t(h_bt, wg_ref[n0:n0 + 2 * MXU_WIDTH, :])
        gm_ref[:, :, n0:n0 + 2 * MXU_WIDTH] = z.astype(gm_ref.dtype).reshape(nb, tt, 2 * MXU_WIDTH)


def _project(x, g, scale, shift, w_scan, w_nsa, w_gm, rope_tabs, *, tt):
    bsz, seq, d = x.shape
    ns, nn, ng = w_scan.shape[0], w_nsa.shape[0], w_gm.shape[0]
    full = lambda a: pl.BlockSpec(a.shape, lambda i: (0,) * a.ndim)
    tok = lambda n, dt: jax.ShapeDtypeStruct((bsz, seq, n), dt)
    tok_spec = lambda n: pl.BlockSpec((bsz, tt, n), lambda i: (0, i, 0))
    return pl.pallas_call(
        _proj_kernel,
        out_shape=(
            jax.ShapeDtypeStruct((seq * bsz, ns), F32),
            tok(MXU_WIDTH + LANES, F32), tok(LANES, F32), tok(LANES, BF16), tok(LANES, BF16), tok(LANES, BF16),
            tok(ng, BF16),
        ),
        grid=(seq // tt,),
        in_specs=[
            pl.BlockSpec((bsz, tt, d), lambda i: (0, i, 0)),
            pl.BlockSpec((1, d), lambda i: (0, 0)),
            pl.BlockSpec((bsz, 1, d), lambda i: (0, 0, 0)),
            pl.BlockSpec((bsz, 1, d), lambda i: (0, 0, 0)),
            full(w_scan), full(w_nsa), full(w_gm),
            pl.BlockSpec((tt, MXU_WIDTH), lambda i: (i, 0)),
            pl.BlockSpec((tt, MXU_WIDTH), lambda i: (i, 0)),
        ],
        out_specs=(
            pl.BlockSpec((tt * bsz, ns), lambda i: (i, 0)),
            tok_spec(MXU_WIDTH + LANES), tok_spec(LANES), tok_spec(LANES), tok_spec(LANES), tok_spec(LANES),
            tok_spec(ng),
        ),
        scratch_shapes=[pltpu.VMEM((d // LANES, tt * bsz, LANES), F32)],
        compiler_params=_cparams(("arbitrary",)),
        name="proj_in",
    )(x, g.reshape(1, d), scale, shift, w_scan, w_nsa, w_gm, *rope_tabs)


def _s5_param_kernel(lre_ref, lim_ref, ls_ref, bre_ref, bim_ref, are_ref, aim_ref, bbre_ref, bbim_ref):
    lre = lre_ref[...]
    lim = lim_ref[...]
    step = jnp.exp(ls_ref[...])
    mag = jnp.exp(lre * step)
    ar = mag * jnp.cos(lim * step)
    ai = mag * jnp.sin(lim * step)
    xr = ar - 1.0
    den = lre * lre + lim * lim
    cr = (xr * lre + ai * lim) / den
    ci = (ai * lre - xr * lim) / den
    bre = bre_ref[...]
    bim = bim_ref[...]
    are_ref[...] = ar
    aim_ref[...] = ai
    bbre_ref[...] = cr * bre - ci * bim
    bbim_ref[...] = cr * bim + ci * bre


def _s5_params(lam_re, lam_im, log_step, b_re, b_im):
    g, n = lam_re.shape
    c = b_re.shape[-1]
    rows = g * n
    col = lambda a: a.reshape(rows, 1)
    ls = jnp.broadcast_to(log_step[:, None], (g, n))
    out = pl.pallas_call(
        _s5_param_kernel,
        out_shape=(
            jax.ShapeDtypeStruct((rows, 1), F32),
            jax.ShapeDtypeStruct((rows, 1), F32),
            jax.ShapeDtypeStruct((rows, c), F32),
            jax.ShapeDtypeStruct((rows, c), F32),
        ),
        name="s5_params",
    )(col(lam_re), col(lam_im), col(ls), b_re.reshape(rows, c), b_im.reshape(rows, c))
    return out


def _scan_kernel(z_ref, s5b_ref, s5c_ref, are_ref, aim_ref, d_ref, wglu_ref,
                 cw_ref, cb_ref, wa_ref, ba_ref, wx_ref, bx_ref, lam_ref, scw_ref,
                 o_ref,
                 bu_ref, hre_ref, him_ref, xe_ref, hl_ref, pe_ref, ab_ref, bb_ref, *, nb, steps):
    c = pl.program_id(0)
    rows = nb * steps
    w = d_ref.shape[1]
    ns = are_ref.shape[1]

    @pl.when(c == 0)
    def _():
        hre_ref[...] = jnp.zeros_like(hre_ref)
        him_ref[...] = jnp.zeros_like(him_ref)
        hl_ref[...] = jnp.zeros_like(hl_ref)
        xe_ref[0:(LRU_CONV - 1) * nb, :] = jnp.zeros(((LRU_CONV - 1) * nb, w), F32)
        pe_ref[0:(SC_CONV - 1) * nb, :] = jnp.zeros(((SC_CONV - 1) * nb, w), F32)

    u = z_ref[:, 0:w]
    bu_ref[...] = _dot(u.astype(BF16), s5b_ref[...])
    a_re = jnp.broadcast_to(are_ref[...], (nb, ns))
    a_im = jnp.broadcast_to(aim_ref[...], (nb, ns))

    def s5_step(t, carry):
        h_re, h_im = carry
        r0 = pl.multiple_of(t * nb, nb)
        b_re = bu_ref[pl.ds(r0, nb), 0:ns]
        b_im = bu_ref[pl.ds(r0, nb), ns:2 * ns]
        n_re = a_re * h_re - a_im * h_im + b_re
        n_im = a_re * h_im + a_im * h_re + b_im
        bu_ref[pl.ds(r0, nb), 0:ns] = n_re
        bu_ref[pl.ds(r0, nb), ns:2 * ns] = n_im
        return n_re, n_im

    h_re, h_im = lax.fori_loop(0, steps, s5_step, (hre_ref[...], him_ref[...]))
    hre_ref[...] = h_re
    him_ref[...] = h_im
    y = _dot(bu_ref[...].astype(BF16), s5c_ref[...]) + d_ref[...] * u
    zg = _gelu(y)
    def put(branch, val):
        for k in range(w // LANES):
            o_ref[branch * (w // LANES) + k] = val[:, k * LANES:(k + 1) * LANES]

    put(0, zg * _sigmoid(_dot(zg.astype(BF16), wglu_ref[...])))

    nt = (LRU_CONV - 1) * nb
    x = z_ref[:, w:2 * w]
    xe_ref[nt:nt + rows, :] = x
    xc = cb_ref[...] + cw_ref[LRU_CONV - 1:LRU_CONV, :] * x
    for k in range(LRU_CONV - 1):
        xc = xc + cw_ref[k:k + 1, :] * xe_ref[k * nb:k * nb + rows, :]
    xe_ref[0:nt, :] = xe_ref[rows:rows + nt, :]
    xcb = xc.astype(BF16)
    r = _sigmoid(_dot(xcb, wa_ref[...]) + ba_ref[...])
    gi = _sigmoid(_dot(xcb, wx_ref[...]) + bx_ref[...])
    nl = -lam_ref[...]
    softplus = jnp.maximum(nl, 0.0) + jnp.log(1.0 + jnp.exp(-jnp.abs(nl)))
    log_a = (-LRU_C) * r * softplus
    a = jnp.exp(log_a)
    ab_ref[...] = a
    bb_ref[...] = jnp.sqrt(1.0 - a * a) * (gi * xc)

    def lru_step(t, h):
        r0 = pl.multiple_of(t * nb, nb)
        h = ab_ref[pl.ds(r0, nb), :] * h + bb_ref[pl.ds(r0, nb), :]
        bb_ref[pl.ds(r0, nb), :] = h
        return h

    hl_ref[...] = lax.fori_loop(0, steps, lru_step, hl_ref[...])
    put(1, bb_ref[...] * _gelu(z_ref[:, 2 * w:3 * w]))

    npt = (SC_CONV - 1) * nb
    p = z_ref[:, 4 * w:5 * w] * z_ref[:, 5 * w:6 * w]
    pe_ref[npt:npt + rows, :] = p
    cv = scw_ref[SC_CONV - 1:SC_CONV, :] * p
    for k in range(SC_CONV - 1):
        cv = cv + scw_ref[k:k + 1, :] * pe_ref[k * nb:k * nb + rows, :]
    pe_ref[0:npt, :] = pe_ref[rows:rows + npt, :]
    put(2, z_ref[:, 3 * w:4 * w] * cv)


def _scan_mixers(z_tm, nb, s5b, s5c, a_re, a_im, d_skip, w_glu, conv_w, conv_b, wa, ba, wx, bx, lam, sc_w, *, steps):
    total, zc = z_tm.shape
    rows = nb * steps
    w = zc // 6
    ns = a_re.shape[1]
    full = lambda a: pl.BlockSpec(a.shape, lambda c: (0,) * a.ndim)
    params = [s5b, s5c, a_re, a_im, d_skip, w_glu, conv_w, conv_b, wa, ba, wx, bx, lam, sc_w]
    return pl.pallas_call(
        functools.partial(_scan_kernel, nb=nb, steps=steps),
        out_shape=jax.ShapeDtypeStruct((3 * w // LANES, total, LANES), F32),
        grid=(total // rows,),
        in_specs=[pl.BlockSpec((rows, zc), lambda c: (c, 0))] + [full(a) for a in params],
        out_specs=pl.BlockSpec((3 * w // LANES, rows, LANES), lambda c: (0, c, 0)),
        scratch_shapes=[
            pltpu.VMEM((rows, 2 * ns), F32),
            pltpu.VMEM((nb, ns), F32),
            pltpu.VMEM((nb, ns), F32),
            pltpu.VMEM((rows + (LRU_CONV - 1) * nb, w), F32),
            pltpu.VMEM((nb, w), F32),
            pltpu.VMEM((rows + (SC_CONV - 1) * nb, w), F32),
            pltpu.VMEM((rows, w), F32),
            pltpu.VMEM((rows, w), F32),
        ],
        compiler_params=_cparams(("arbitrary",)),
        name="scan_mixers",
    )(z_tm, *params)


def _compress_kernel(kv_ref, pek_ref, pev_ref, wk1_ref, wk1p_ref, wk2_ref, wv1_ref, wv1p_ref, wv2_ref, ko_ref, vo_ref):
    nblk = ko_ref.shape[1]
    acc = [None] * 4
    for l in range(CMP_STRIDE):
        xl = kv_ref[0, pl.ds(l, nblk, stride=CMP_STRIDE), :]
        terms = (_dot(xl, wk1p_ref[l], HI), _dot(xl, wk1p_ref[CMP_STRIDE + l], HI),
                 _dot(xl, wv1p_ref[l], HI), _dot(xl, wv1p_ref[CMP_STRIDE + l], HI))
        acc = [t if a is None else a + t for a, t in zip(acc, terms)]

    def phi(first, second, pe, w1_ref, w2_ref):
        hid = first + pltpu.roll(second, nblk - 1, 0) + _dot(pe, w1_ref[...], HI)
        return _dot(_gelu(hid), w2_ref[...], HI)

    ko_ref[0] = phi(acc[0], acc[1], pek_ref[...], wk1_ref, wk2_ref)
    vo_ref[0] = phi(acc[2], acc[3], pev_ref[...], wv1_ref, wv2_ref)


def _compress(kcvc, pe_k, pe_v, wk1, wk2, wv1, wv2):
    bsz, seq, kvw = kcvc.shape
    nblk = seq // CMP_STRIDE
    hd = wk2.shape[1]
    hidden = wk1.shape[1]
    full = lambda a: pl.BlockSpec(a.shape, lambda b: (0,) * a.ndim)
    zeros = jnp.zeros((L_CMP, HEAD_DIM, hidden), wk1.dtype)
    wk1p = jnp.concatenate([wk1.reshape(L_CMP, HEAD_DIM, hidden), zeros], axis=1)
    wv1p = jnp.concatenate([zeros, wv1.reshape(L_CMP, HEAD_DIM, hidden)], axis=1)
    params = [pe_k.reshape(1, -1), pe_v.reshape(1, -1), wk1, wk1p, wk2, wv1, wv1p, wv2]
    oblk = pl.BlockSpec((1, nblk, hd), lambda b: (b, 0, 0))
    return pl.pallas_call(
        _compress_kernel,
        out_shape=(jax.ShapeDtypeStruct((bsz, nblk, hd), F32),) * 2,
        grid=(bsz,),
        in_specs=[pl.BlockSpec((1, seq, kvw), lambda b: (b, 0, 0))] + [full(a) for a in params],
        out_specs=(oblk, oblk),
        compiler_params=_cparams(("arbitrary",)),
        name="nsa_compress",
    )(kcvc, *params)


def _split_bf16(a):
    hi = a.astype(BF16)
    return hi, (a - hi.astype(F32)).astype(BF16)


def _cmp_kernel(q_ref, kc_ref, vct_ref, ovt_ref, ocmp_ref, qa_ref, *, heads, tq):
    i = pl.program_id(1)
    q_t = (q_ref[0] * HEAD_DIM ** -0.5).T
    kc_hi, kc_lo = _split_bf16(kc_ref[0])
    vct = vct_ref[0]
    nblk = kc_hi.shape[0]
    t = i * tq + lax.broadcasted_iota(jnp.int32, (nblk, tq), 1)
    n_id = lax.broadcasted_iota(jnp.int32, (nblk, tq), 0)
    visible = (n_id * CMP_STRIDE + (L_CMP - 1)) <= t
    psum = jnp.zeros((nblk, tq), F32)
    for h in range(heads):
        q_hi, q_lo = _split_bf16(q_t[h * HEAD_DIM:(h + 1) * HEAD_DIM, :])
        s = _dot(kc_hi, q_hi) + (_dot(kc_hi, q_lo) + _dot(kc_lo, q_hi))
        s = jnp.where(visible, s, NEG_INF)
        m = jnp.max(s, axis=0, keepdims=True)
        e = jnp.where(visible, jnp.exp(s - m), 0.0)
        den = jnp.sum(e, axis=0, keepdims=True)
        p = e / jnp.where(den > 0.0, den, 1.0)
        psum = psum + p
        ocmp_ref[0, h * HEAD_DIM:(h + 1) * HEAD_DIM, :] = _dot(vct, p.astype(BF16))

    p_hi, p_lo = _split_bf16(psum)
    imp_t = _dot(ovt_ref[...], p_hi) + _dot(ovt_ref[...], p_lo)
    nslot = ovt_ref.shape[0]
    tt = i * tq + lax.broadcasted_iota(jnp.int32, (nslot, tq), 1)
    j = lax.broadcasted_iota(jnp.int32, (nslot, tq), 0)
    cur = tt // L_SEL
    forced = (j == 0) | (j == cur) | (j == cur - 1)
    future = (j * L_SEL) > tt
    imp_t = jnp.where(future, -1.0, jnp.where(forced, FORCED_SCORE, imp_t))
    nreal = qa_ref.shape[2] - HEAD_DIM
    cand = imp_t[0:nreal, :]
    jrow = lax.broadcasted_iota(jnp.int32, (nreal, tq), 0)
    bias_t = jnp.full((nreal, tq), NEG_INF, F32)
    for _ in range(TOP_N):
        best = jnp.max(cand, axis=0, keepdims=True)
        first = jnp.min(jnp.where(cand == best, jrow, nreal), axis=0, keepdims=True)
        hit = jrow == first
        bias_t = jnp.where(hit, 0.0, bias_t)
        cand = jnp.where(hit, -2.0, cand)
    for h in range(heads):
        qa_ref[0, h] = jnp.concatenate([q_t[h * HEAD_DIM:(h + 1) * HEAD_DIM, :] * LOG2E, bias_t], axis=0).astype(BF16)


def _cmp_select(q, k_cmp, v_cmp_t, overlap_t, *, heads, tq):
    bsz, seq, _ = q.shape
    qd = heads * HEAD_DIM
    nblk = k_cmp.shape[1]
    nsel = LANES - HEAD_DIM
    return pl.pallas_call(
        functools.partial(_cmp_kernel, heads=heads, tq=tq),
        out_shape=(
            jax.ShapeDtypeStruct((bsz, qd, seq), F32),
            jax.ShapeDtypeStruct((bsz, heads, HEAD_DIM + nsel, seq), BF16),
        ),
        grid=(bsz, seq // tq),
        in_specs=[
            pl.BlockSpec((1, tq, qd), lambda b, i: (b, i, 0)),
            pl.BlockSpec((1, nblk, HEAD_DIM), lambda b, i: (b, 0, 0)),
            pl.BlockSpec((1, HEAD_DIM, nblk), lambda b, i: (b, 0, 0)),
            pl.BlockSpec(overlap_t.shape, lambda b, i: (0, 0)),
        ],
        out_specs=(
            pl.BlockSpec((1, qd, tq), lambda b, i: (b, 0, i)),
            pl.BlockSpec((1, heads, HEAD_DIM + nsel, tq), lambda b, i: (b, 0, 0, i)),
        ),
        compiler_params=_cparams(("arbitrary", "arbitrary")),
        name="nsa_cmp_select",
    )(q, k_cmp, v_cmp_t, overlap_t)


def _attend_kernel(qa_ref, ka_ref, vst_ref, kwa_ref, vwt_ref, gn_ref, ocmp_ref, o_ref,
                   m_ref, acc_ref, s0_ref, s1_ref, *, heads, tq, tk, nwin):
    i = pl.program_id(1)
    cols = heads * tq
    qat = jnp.concatenate([qa_ref[0, h] for h in range(heads)], axis=1)
    t_row = i * tq + (lax.broadcasted_iota(jnp.int32, (1, cols), 1) % tq)

    m_ref[...] = jnp.full(m_ref.shape, NEG_INF, F32)
    acc_ref[...] = jnp.zeros(acc_ref.shape, F32)
    key_id = lax.broadcasted_iota(jnp.int32, (tk, cols), 0)
    n_full = (i * tq + 1) // tk
    s0_ref[...] = _dot(ka_ref[0, 0], qat)

    st = jnp.maximum(i + 1 - nwin, 0)
    kw = kwa_ref[0, pl.ds(st, nwin)].reshape(nwin * tq, kwa_ref.shape[3])
    s = _dot(kw, qat)
    dist = t_row - (st * tq + lax.broadcasted_iota(jnp.int32, (nwin * tq, cols), 0))
    s = jnp.where((dist >= 0) & (dist < WINDOW), s, NEG_INF)
    m = jnp.max(s, axis=0, keepdims=True)
    pb = jnp.exp2(s - m).astype(BF16)
    ow = _dot(vwt_ref[0, st], pb[0:tq, :])
    for j in range(1, nwin):
        ow = ow + _dot(vwt_ref[0, st + j], pb[j * tq:(j + 1) * tq, :])
    owin = ow[0:HEAD_DIM, :] / ow[HEAD_DIM:HEAD_DIM + 1, :]

    def softmax_tile(kt, s, masked):
        if masked:
            s = jnp.where(kt * tk + key_id <= t_row, s, NEG_INF)
        m_old = m_ref[...]
        m_new = jnp.maximum(m_old, jnp.max(s, axis=0, keepdims=True))
        p = jnp.exp2(s - m_new).astype(BF16)
        acc_ref[...] = jnp.exp2(m_old - m_new) * acc_ref[...] + _dot(vst_ref[0, kt], p)
        m_ref[...] = m_new

    def pair_body(j, carry):
        s1_ref[...] = _dot(ka_ref[0, 2 * j + 1], qat)
        softmax_tile(2 * j, s0_ref[...], False)
        s0_ref[...] = _dot(ka_ref[0, 2 * j + 2], qat)
        softmax_tile(2 * j + 1, s1_ref[...], False)
        return carry

    n_pair = n_full // 2
    lax.fori_loop(0, n_pair, pair_body, 0)
    s1_ref[...] = _dot(ka_ref[0, 2 * n_pair + 1], qat)
    softmax_tile(2 * n_pair, s0_ref[...], True)
    softmax_tile(2 * n_pair + 1, s1_ref[...], True)

    osel = acc_ref[0:HEAD_DIM, :] / acc_ref[HEAD_DIM:HEAD_DIM + 1, :]

    g_t = _sigmoid(gn_ref[0]).T
    outs = []
    for h in range(heads):
        cs = slice(h * tq, (h + 1) * tq)
        outs.append(g_t[3 * h:3 * h + 1, :] * ocmp_ref[0, h * HEAD_DIM:(h + 1) * HEAD_DIM, :]
                    + g_t[3 * h + 1:3 * h + 2, :] * osel[:, cs]
                    + g_t[3 * h + 2:3 * h + 3, :] * owin[:, cs])
    o_ref[0] = jnp.concatenate(outs, axis=0).T


def _attend(qa, ka, vst, kwa, vwt, z_nsa, gn_block, o_cmp_t, *, tq, tk):
    bsz, heads, ka_w, seq = qa.shape
    qd = heads * HEAD_DIM
    nwin = WINDOW // tq + 1
    return pl.pallas_call(
        functools.partial(_attend_kernel, heads=heads, tq=tq, tk=tk, nwin=nwin),
        out_shape=jax.ShapeDtypeStruct((bsz, seq, qd), F32),
        grid=(bsz, seq // tq),
        in_specs=[
            pl.BlockSpec((1, heads, ka_w, tq), lambda b, i: (b, 0, 0, i)),
            pl.BlockSpec((1,) + ka.shape[1:], lambda b, i: (b, 0, 0, 0)),
            pl.BlockSpec((1,) + vst.shape[1:], lambda b, i: (b, 0, 0, 0)),
            pl.BlockSpec((1,) + kwa.shape[1:], lambda b, i: (b, 0, 0, 0)),
            pl.BlockSpec((1,) + vwt.shape[1:], lambda b, i: (b, 0, 0, 0)),
            pl.BlockSpec((1, tq, LANES), lambda b, i: (b, i, gn_block)),
            pl.BlockSpec((1, qd, tq), lambda b, i: (b, 0, i)),
        ],
        out_specs=pl.BlockSpec((1, tq, qd), lambda b, i: (b, i, 0)),
        scratch_shapes=[
            pltpu.VMEM((1, heads * tq), F32),
            pltpu.VMEM((vst.shape[2], heads * tq), F32),
            pltpu.VMEM((tk, heads * tq), F32),
            pltpu.VMEM((tk, heads * tq), F32),
        ],
        compiler_params=_cparams(("arbitrary", "arbitrary")),
        name="nsa_attend",
    )(qa, ka, vst, kwa, vwt, z_nsa, o_cmp_t)


def _merge_kernel(ys_ref, yb_ref, gm_ref, x_ref, wb_ref, wo_ref, gate_ref, o_ref, m_ref):
    nb, tt, w = yb_ref.shape
    d = x_ref.shape[2]
    rows = nb * tt
    def slab(c):
        return jnp.concatenate([ys_ref[c, pl.ds(b, tt, stride=nb), :] for b in range(nb)], axis=0)

    ys = jnp.concatenate([slab(c) for c in range(ys_ref.shape[0])], axis=1)
    branches = (ys[:, 0:w], yb_ref[...].reshape(rows, w), ys[:, w:2 * w], ys[:, 2 * w:3 * w])
    ybs = [y.astype(BF16) for y in branches]
    for n0 in range(0, d, MXU_WIDTH):
        plain = None
        gated = None
        for mi, y in enumerate(ybs):
            p = _dot(y, wb_ref[mi, :, n0:n0 + MXU_WIDTH])
            gm = gm_ref[:, :, mi * d + n0:mi * d + n0 + MXU_WIDTH].reshape(rows, MXU_WIDTH)
            t = jnp.tanh(gm.astype(F32)) * p
            plain = p if plain is None else plain + p
            gated = t if gated is None else gated + t
        m_ref[:, n0:n0 + MXU_WIDTH] = (0.5 * (plain + gated)).astype(BF16)
    mix = _dot(m_ref[...], wo_ref[...]).reshape(nb, tt, d)
    o_ref[...] = x_ref[...] + gate_ref[...] * mix


def _merge(y_scan_tm, y_b, g_merge, x, w_branch, w_out, gate, *, tt):
    bsz, seq, d = x.shape
    w = y_b.shape[2]
    tile = lambda n: pl.BlockSpec((bsz, tt, n), lambda i: (0, i, 0))
    return pl.pallas_call(
        _merge_kernel,
        out_shape=jax.ShapeDtypeStruct((bsz, seq, d), F32),
        grid=(seq // tt,),
        in_specs=[
            pl.BlockSpec((3 * w // LANES, tt * bsz, LANES), lambda i: (0, i, 0)),
            tile(w),
            tile(N_BRANCH * d),
            tile(d),
            pl.BlockSpec(w_branch.shape, lambda i: (0, 0, 0)),
            pl.BlockSpec(w_out.shape, lambda i: (0, 0)),
            pl.BlockSpec((bsz, 1, d), lambda i: (0, 0, 0)),
        ],
        out_specs=tile(d),
        scratch_shapes=[pltpu.VMEM((tt * bsz, d), BF16)],
        compiler_params=_cparams(("arbitrary",)),
        name="merge_out",
    )(y_scan_tm, y_b, g_merge, x, w_branch, w_out, gate)


def _final_norm(out, fg_ref):
    var = jnp.mean(out * out, axis=-1, keepdims=True)
    return out * lax.rsqrt(var + RMS_EPS) * fg_ref[...]


def _ffn_kernel(x_ref, g_ref, sc_ref, sh_ref, gate_ref, wg_ref, wu_ref, wd_ref, *rest, final, fc):
    if final:
        fg_ref, o_ref, a_ref = rest
    else:
        o_ref, a_ref = rest
    hb = _norm_mod(x_ref[0], g_ref[...], sc_ref[0], sh_ref[0]).astype(BF16)
    for f0 in range(0, a_ref.shape[1], fc):
        a_ref[:, f0:f0 + fc] = (_silu(_dot(hb, wg_ref[:, f0:f0 + fc])) * _dot(hb, wu_ref[:, f0:f0 + fc])).astype(BF16)
    out = x_ref[0] + gate_ref[0] * _dot(a_ref[...], wd_ref[...])
    o_ref[0] = _final_norm(out, fg_ref) if final else out


def _ffn(x, g, scale, shift, gate, weights, *, final_g, tm):
    bsz, seq, d = x.shape
    row = lambda: pl.BlockSpec((1, 1, d), lambda b, i: (b, 0, 0))
    wg, wu, wd = weights
    ff = wg.shape[1]
    in_specs = [
        pl.BlockSpec((1, tm, d), lambda b, i: (b, i, 0)),
        pl.BlockSpec((1, d), lambda b, i: (0, 0)),
        row(), row(), row(),
        pl.BlockSpec((d, ff), lambda b, i: (0, 0)),
        pl.BlockSpec((d, ff), lambda b, i: (0, 0)),
        pl.BlockSpec((ff, d), lambda b, i: (0, 0)),
    ]
    args = [x, g.reshape(1, d), scale, shift, gate, wg, wu, wd]
    final = final_g is not None
    if final:
        in_specs.append(pl.BlockSpec((1, d), lambda b, i: (0, 0)))
        args.append(final_g.reshape(1, d))
    return pl.pallas_call(
        functools.partial(_ffn_kernel, final=final, fc=MXU_WIDTH),
        out_shape=jax.ShapeDtypeStruct((bsz, seq, d), F32),
        grid=(bsz, seq // tm),
        in_specs=in_specs,
        out_specs=pl.BlockSpec((1, tm, d), lambda b, i: (b, i, 0)),
        scratch_shapes=[pltpu.VMEM((tm, ff), BF16)],
        compiler_params=_cparams(("arbitrary", "arbitrary")),
        name="dense_ffn",
    )(*args)


def _route_kernel(x_ref, g_ref, sc_ref, sh_ref, rw_ref, rb_ref, h_ref, r_ref):
    h = _norm_mod(x_ref[0], g_ref[...], sc_ref[0], sh_ref[0])
    h_hi, h_lo = _split_bf16(h)
    h_ref[0] = h_hi
    w_hi, w_lo = _split_bf16(rw_ref[...])
    logits = _dot(h_hi, w_hi) + (_dot(h_hi, w_lo) + _dot(h_lo, w_hi)) + rb_ref[...]
    lane = lax.broadcasted_iota(jnp.int32, logits.shape, 1)
    big = logits.shape[1]
    v1 = jnp.max(logits, axis=-1, keepdims=True)
    i1 = jnp.min(jnp.where(logits == v1, lane, big), axis=-1, keepdims=True)
    rest_l = jnp.where(lane == i1, NEG_INF * 4.0, logits)
    v2 = jnp.max(rest_l, axis=-1, keepdims=True)
    i2 = jnp.min(jnp.where(rest_l == v2, lane, big), axis=-1, keepdims=True)
    e2 = jnp.exp(v2 - v1)
    w1 = 1.0 / (1.0 + e2)
    w2 = e2 / (1.0 + e2)
    r_ref[0] = jnp.where(lane == 0, i1.astype(F32), jnp.where(lane == 1, i2.astype(F32), jnp.where(lane == 2, w1, w2)))


def _route(x, g, scale, shift, rw, rb, *, tm):
    bsz, seq, d = x.shape
    row = lambda: pl.BlockSpec((1, 1, d), lambda b, i: (b, 0, 0))
    return pl.pallas_call(
        _route_kernel,
        out_shape=(jax.ShapeDtypeStruct((bsz, seq, d), BF16), jax.ShapeDtypeStruct((bsz, seq, LANES), F32)),
        grid=(bsz, seq // tm),
        in_specs=[
            pl.BlockSpec((1, tm, d), lambda b, i: (b, i, 0)),
            pl.BlockSpec((1, d), lambda b, i: (0, 0)),
            row(), row(),
            pl.BlockSpec(rw.shape, lambda b, i: (0, 0)),
            pl.BlockSpec(rb.shape, lambda b, i: (0, 0)),
        ],
        out_specs=(pl.BlockSpec((1, tm, d), lambda b, i: (b, i, 0)), pl.BlockSpec((1, tm, LANES), lambda b, i: (b, i, 0))),
        compiler_params=_cparams(("arbitrary", "arbitrary")),
        name="moe_route",
    )(x, g.reshape(1, d), scale, shift, rw, rb)


def _gffn_kernel(te_ref, nu_ref, x_ref, wg_ref, wu_ref, wd_ref, o_ref):
    i = pl.program_id(0)

    @pl.when(i < nu_ref[0])
    def _():
        hb = x_ref[...]
        a = (_silu(_dot(hb, wg_ref[0])) * _dot(hb, wu_ref[0])).astype(BF16)
        o_ref[...] = _dot(a, wd_ref[0]).astype(o_ref.dtype)

    @pl.when(i >= nu_ref[0])
    def _():
        o_ref[...] = jnp.zeros_like(o_ref)


def _grouped_ffn(xs, tile_expert, n_used, wg, wu, wd, *, tmg):
    rows, d = xs.shape
    _, _, fe = wg.shape
    return pl.pallas_call(
        _gffn_kernel,
        out_shape=jax.ShapeDtypeStruct((rows, d), BF16),
        grid_spec=pltpu.PrefetchScalarGridSpec(
            num_scalar_prefetch=2,
            grid=(rows // tmg,),
            in_specs=[
                pl.BlockSpec((tmg, d), lambda i, te, nu: (i, 0)),
                pl.BlockSpec((1, d, fe), lambda i, te, nu: (te[i], 0, 0)),
                pl.BlockSpec((1, d, fe), lambda i, te, nu: (te[i], 0, 0)),
                pl.BlockSpec((1, fe, d), lambda i, te, nu: (te[i], 0, 0)),
            ],
            out_specs=pl.BlockSpec((tmg, d), lambda i, te, nu: (i, 0)),
        ),
        compiler_params=_cparams(("arbitrary",)),
        name="moe_grouped_ffn",
    )(tile_expert, n_used, xs, wg, wu, wd)


def _combine_kernel(x_ref, gate_ref, y_ref, y2_ref, r_ref, *rest, final):
    if final:
        fg_ref, o_ref = rest
    else:
        (o_ref,) = rest
    r = r_ref[0]
    f = r[:, 2:3] * y_ref[0].astype(F32) + r[:, 3:4] * y2_ref[0].astype(F32)
    out = x_ref[0] + gate_ref[0] * f
    o_ref[0] = _final_norm(out, fg_ref) if final else out


def _combine(x, gate, y1, y2, route, *, final_g, tm):
    bsz, seq, d = x.shape
    tile = lambda: pl.BlockSpec((1, tm, d), lambda b, i: (b, i, 0))
    in_specs = [
        tile(),
        pl.BlockSpec((1, 1, d), lambda b, i: (b, 0, 0)),
        tile(), tile(),
        pl.BlockSpec((1, tm, LANES), lambda b, i: (b, i, 0)),
    ]
    args = [x, gate, y1, y2, route]
    final = final_g is not None
    if final:
        in_specs.append(pl.BlockSpec((1, d), lambda b, i: (0, 0)))
        args.append(final_g.reshape(1, d))
    return pl.pallas_call(
        functools.partial(_combine_kernel, final=final),
        out_shape=jax.ShapeDtypeStruct((bsz, seq, d), F32),
        grid=(bsz, seq // tm),
        in_specs=in_specs,
        out_specs=pl.BlockSpec((1, tm, d), lambda b, i: (b, i, 0)),
        compiler_params=_cparams(("arbitrary", "arbitrary")),
        name="moe_combine",
    )(*args)


def _src_kernel(te_ref, k0_ref, jlo_ref, jhi_ref, c_ref, o_ref, cnt_ref, *, nblk, tb, last):
    i = pl.program_id(0)
    tmg = o_ref.shape[0]
    base = te_ref[i] * nblk
    kcol = k0_ref[i] + lax.broadcasted_iota(jnp.int32, (tmg, 1), 0)
    cnt_ref[...] = jnp.zeros_like(cnt_ref)

    def blk(j, carry):
        hit = (c_ref[base + j] <= kcol).astype(jnp.int32)
        part = hit[:, 0:LANES]
        for q in range(1, tb // LANES):
            part = part + hit[:, q * LANES:(q + 1) * LANES]
        cnt_ref[...] += part
        return carry

    lax.fori_loop(jlo_ref[i], jhi_ref[i], blk, 0)
    total = jnp.sum(cnt_ref[...], axis=1, keepdims=True) + jlo_ref[i] * tb
    o_ref[...] = jnp.minimum(total, last)


def _moe(x, g, scale, shift, gate, rw, rb, wg, wu, wd, *, final_g, tm, tmg):
    bsz, seq, d = x.shape
    ne = wg.shape[0]
    tokens = bsz * seq
    tb = 512
    nblk = tokens // tb
    h, route = _route(x, g, scale, shift, rw, rb, tm=tm)
    ar = jnp.arange(ne, dtype=jnp.int32)[:, None]
    m1 = route[:, :, 0].astype(jnp.int32).reshape(1, tokens) == ar
    m2 = route[:, :, 1].astype(jnp.int32).reshape(1, tokens) == ar
    count = jnp.cumsum((m1 | m2).astype(jnp.int32), axis=1)
    padded = ((count[:, -1] + tmg - 1) // tmg) * tmg
    ends = jnp.cumsum(padded)
    offs = ends - padded
    row_of = count - 1 + offs[:, None]
    dest1 = jnp.sum(jnp.where(m1, row_of, 0), axis=0)
    dest2 = jnp.sum(jnp.where(m2, row_of, 0), axis=0)
    rows = tokens * 2 + ne * tmg
    ntiles = rows // tmg
    tile_start = jnp.arange(ntiles, dtype=jnp.int32) * tmg
    tile_expert = jnp.minimum(jnp.sum((tile_start[:, None] >= ends[None, :]).astype(jnp.int32), axis=1), ne - 1)
    n_used = (ends[-1] // tmg).astype(jnp.int32).reshape(1)
    k0 = tile_start - offs[tile_expert]
    cend = count[:, tb - 1::tb]
    cstart = jnp.concatenate([jnp.zeros((ne, 1), jnp.int32), cend[:, :-1]], axis=1)
    jlo = jnp.sum((cend[tile_expert] <= k0[:, None]).astype(jnp.int32), axis=1)
    jhi = jnp.sum((cstart[tile_expert] <= (k0 + tmg - 1)[:, None]).astype(jnp.int32), axis=1)
    src = pl.pallas_call(
        functools.partial(_src_kernel, nblk=nblk, tb=tb, last=tokens - 1),
        out_shape=jax.ShapeDtypeStruct((rows, 1), jnp.int32),
        grid_spec=pltpu.PrefetchScalarGridSpec(
            num_scalar_prefetch=4,
            grid=(ntiles,),
            in_specs=[pl.BlockSpec((ne * nblk, 1, tb), lambda i, *_: (0, 0, 0))],
            out_specs=pl.BlockSpec((tmg, 1), lambda i, *_: (i, 0)),
            scratch_shapes=[pltpu.VMEM((tmg, LANES), jnp.int32)],
        ),
        compiler_params=_cparams(("arbitrary",)),
        name="moe_sources",
    )(tile_expert, k0, jlo, jhi, count.reshape(ne * nblk, 1, tb))
    rows_of = lambda a, idx: a.at[idx].get(mode="promise_in_bounds")
    xs = rows_of(h.reshape(tokens, d), src[:, 0])
    ys = _grouped_ffn(xs, tile_expert, n_used, wg, wu, wd, tmg=tmg)
    y1 = rows_of(ys, dest1).reshape(bsz, seq, d)
    y2 = rows_of(ys, dest2).reshape(bsz, seq, d)
    return _combine(x, gate, y1, y2, route, final_g=final_g, tm=tm)


def _block_diag(blocks):
    n, r, c = blocks.shape
    eye = jnp.eye(n, dtype=blocks.dtype)
    return (eye[:, None, :, None] * blocks[:, :, None, :]).reshape(n * r, n * c)


def _rope_tables(seq):
    inv = ROPE_THETA ** (-jnp.arange(0, HEAD_DIM, 2, dtype=F32) / HEAD_DIM)
    ang = jnp.arange(seq, dtype=F32)[:, None] * inv[None, :]
    cos, sin = jnp.cos(ang), jnp.sin(ang)
    reps = MXU_WIDTH // HEAD_DIM
    return jnp.concatenate([cos, cos] * reps, axis=1), jnp.concatenate([-sin, sin] * reps, axis=1)


def kernel(x, c, mod_w, mod_b, norm_mix_g, norm_ffn_g, w_in, s5_lambda_re, s5_lambda_im, s5_log_step, s5_b_re, s5_b_im, s5_c_re, s5_c_im, s5_d, s5_w_glu, nsa_pe_k, nsa_pe_v, nsa_cmp_k_w1, nsa_cmp_k_w2, nsa_cmp_v_w1, nsa_cmp_v_w2, lru_conv_w, lru_conv_b, lru_w_a, lru_b_a, lru_w_x, lru_b_x, lru_lambda, sc_conv_w, w_branch, w_out, ffn_w_gate, ffn_w_up, ffn_w_down, moe_router_w, moe_router_b, moe_w_gate, moe_w_up, moe_w_down, final_norm_g):
    bsz, seq, d = x.shape
    depth = mod_w.shape[0]
    w = d // N_BRANCH
    heads = w // HEAD_DIM
    groups = w // S5_GROUP
    assert bsz == SUBLANES, "time-major scan blocks hold one timestep per sublane tile"
    tm = min(512, seq)
    tq = 128
    steps = min(64, seq)
    nsel = seq // L_SEL
    assert TOP_N <= nsel <= LANES - HEAD_DIM and seq % tm == 0 and seq >= WINDOW + tq

    mod = _modulation(c, mod_w, mod_b)
    mod = mod.reshape(depth, bsz, 6, 1, d)

    sizes = (w, w, HEAD_DIM, HEAD_DIM, HEAD_DIM, HEAD_DIM, HEAD_DIM, HEAD_DIM, heads * 3, w, w, w, w, w, N_BRANCH * d)
    offs = np.concatenate([[0], np.cumsum(sizes)])
    col = lambda wmat_t, k: wmat_t[offs[k]:offs[k + 1], :]

    cos_t, sin_t = _rope_tables(seq)
    n_cmp = (seq - L_CMP) // CMP_STRIDE + 1
    nblk = seq // CMP_STRIDE
    cmp_start = np.arange(nblk) * CMP_STRIDE
    sel_start = np.arange(nsel) * L_SEL
    ovl = np.clip(np.minimum(cmp_start[:, None] + L_CMP, sel_start[None, :] + L_SEL)
                  - np.maximum(cmp_start[:, None], sel_start[None, :]), 0, None) / L_CMP
    ovl[n_cmp:] = 0.0
    overlap_t = jnp.asarray(np.pad(ovl, ((0, 0), (0, LANES - nsel))).T, BF16)

    w_in_t = jnp.swapaxes(w_in, 1, 2)

    for li in range(depth):
        sh1, sc1, gt1, sh2, sc2, gt2 = (mod[li, :, k] for k in range(6))
        wl = w_in_t[li]
        w_scan = jnp.concatenate([col(wl, 0), col(wl, 9), col(wl, 10), col(wl, 11), col(wl, 12), col(wl, 13)], axis=0).astype(BF16)
        gpad = jnp.zeros((LANES - heads * 3, d), wl.dtype)
        w_nsa = jnp.concatenate([col(wl, 1), col(wl, 8), gpad] + [col(wl, k) for k in (2, 3, 4, 7, 6, 5)], axis=0).astype(BF16)
        w_gm = (0.5 * col(wl, 14)).astype(BF16)
        g1 = norm_mix_g[li]
        z_scan, q_gn, kcvc, ka, kwa, vsw, g_merge = _project(x, g1, sc1, sh1, w_scan, w_nsa, w_gm, (cos_t, sin_t), tt=steps)

        a_re, a_im, bb_re, bb_im = _s5_params(s5_lambda_re[li], s5_lambda_im[li], s5_log_step[li], s5_b_re[li], s5_b_im[li])
        to_bd = lambda m: _block_diag(jnp.swapaxes(m.reshape(groups, S5_STATE, S5_GROUP), 1, 2))
        s5b = jnp.concatenate([to_bd(bb_re), to_bd(bb_im)], axis=1).astype(BF16)
        c_bd = lambda m: _block_diag(jnp.swapaxes(m, 1, 2))
        s5c = jnp.concatenate([c_bd(s5_c_re[li]), -c_bd(s5_c_im[li])], axis=0).astype(BF16)
        r1 = lambda a: a.reshape(1, -1)
        y_scan = _scan_mixers(
            z_scan, bsz, s5b, s5c, r1(a_re), r1(a_im), r1(s5_d[li]), s5_w_glu[li].astype(BF16),
            lru_conv_w[li], r1(lru_conv_b[li]), _block_diag(lru_w_a[li]).astype(BF16), r1(lru_b_a[li]),
            _block_diag(lru_w_x[li]).astype(BF16), r1(lru_b_x[li]), r1(lru_lambda[li]), sc_conv_w[li], steps=2 * steps)

        k_cmp, v_cmp = _compress(kcvc, nsa_pe_k[li], nsa_pe_v[li],
                                 nsa_cmp_k_w1[li], nsa_cmp_k_w2[li], nsa_cmp_v_w1[li], nsa_cmp_v_w2[li])
        o_cmp, qa = _cmp_select(q_gn, k_cmp, jnp.swapaxes(v_cmp, 1, 2).astype(BF16), overlap_t, heads=heads, tq=8 * tq)
        tk = min(512, seq)
        ones_pad = jnp.concatenate([jnp.ones((bsz, seq, 1), BF16), jnp.zeros((bsz, seq, SUBLANES - 1), BF16)], axis=2)
        tiles_t = lambda a, n: jnp.swapaxes(
            jnp.concatenate([a, ones_pad], axis=2).reshape(bsz, seq // n, n, HEAD_DIM + SUBLANES), 2, 3)
        y_b = _attend(qa, ka.reshape(bsz, seq // tk, tk, LANES), tiles_t(vsw[:, :, :HEAD_DIM], tk),
                      kwa.reshape(bsz, seq // tq, tq, LANES), tiles_t(vsw[:, :, HEAD_DIM:], tq),
                      q_gn, (heads * HEAD_DIM) // LANES, o_cmp, tq=tq, tk=tk)

        x = _merge(y_scan, y_b, g_merge, x, w_branch[li].astype(BF16), w_out[li].astype(BF16), gt1, tt=steps)

        final_g = final_norm_g if li == depth - 1 else None
        j = li // 2
        if li % 2 == 0:
            weights = tuple(_to_bf16(wt[j:j + 1])[0] for wt in (ffn_w_gate, ffn_w_up, ffn_w_down))
            x = _ffn(x, norm_ffn_g[li], sc2, sh2, gt2, weights, final_g=final_g, tm=tm)
        else:
            rw = jnp.pad(moe_router_w[j], ((0, 0), (0, LANES - N_EXPERTS)))
            rb = jnp.pad(moe_router_b[j], (0, LANES - N_EXPERTS), constant_values=NEG_INF).reshape(1, LANES)
            x = _moe(x, norm_ffn_g[li], sc2, sh2, gt2, rw, rb, _to_bf16(moe_w_gate[j]), _to_bf16(moe_w_up[j]),
                     _to_bf16(moe_w_down[j]), final_g=final_g, tm=tm, tmg=512)
    return x
```
